```python
import math
import jax, jax.numpy as jnp
from jax import lax
import numpy as np

D_MODEL = 1024
BATCH = 8
SEQ = 2048
DEPTH = 1

HEAD_DIM = 64
SWA_Q_HEADS = 8
SWA_KV_HEADS = 2
SWA_WINDOW = 128
MOBA_Q_HEADS = 8
MOBA_KV_HEADS = 2
MOBA_BLOCK = 256
MOBA_TOPK = 3
MOBA_Q_CHUNK = 32
N_HEADS = SWA_Q_HEADS + MOBA_Q_HEADS
REL_BUCKETS = 32
REL_MAX_DIST = 128
N_GROUPS = 4
EXPERTS_PER_GROUP = 8
N_EXPERTS = N_GROUPS * EXPERTS_PER_GROUP
EXPERT_TOPK = 2
D_EXPERT = 512
DISPATCH_BLOCK = 256
PLE_DIM = 256
EPS = 1e-6
NEG = -1e30

SWA_Q_W = SWA_Q_HEADS * HEAD_DIM
SWA_KV_W = SWA_KV_HEADS * HEAD_DIM
MOBA_Q_W = MOBA_Q_HEADS * HEAD_DIM
MOBA_KV_W = MOBA_KV_HEADS * HEAD_DIM
MIX_W = SWA_Q_W + MOBA_Q_W
IN_W = SWA_Q_W + 2 * SWA_KV_W + MOBA_Q_W + 2 * MOBA_KV_W
SPLITS = [SWA_Q_W, SWA_Q_W + SWA_KV_W, SWA_Q_W + 2 * SWA_KV_W,
          SWA_Q_W + 2 * SWA_KV_W + MOBA_Q_W, SWA_Q_W + 2 * SWA_KV_W + MOBA_Q_W + MOBA_KV_W]

kernel_name = 'hymba_swa_sink_moba_hiermoe_ple'


def rms_norm(x, g):
    xf = x.astype(jnp.float32)
    y = xf * lax.rsqrt(jnp.mean(xf * xf, axis=-1, keepdims=True) + EPS)
    return (y * g.astype(jnp.float32)).astype(x.dtype)


def rel_bucket(dist):
    n = jnp.maximum(dist, 0)
    exact = REL_BUCKETS // 2
    nf = jnp.maximum(n, 1).astype(jnp.float32)
    large = exact + (jnp.log(nf / exact) / math.log(REL_MAX_DIST / exact)
                     * (REL_BUCKETS - exact)).astype(jnp.int32)
    return jnp.where(n < exact, n, jnp.minimum(large, REL_BUCKETS - 1))


def swa_attention(q, k, v, sinks, rel_tbl):
    B, S = q.shape[0], q.shape[1]
    W = SWA_WINDOW
    NB = S // W
    G = SWA_KV_HEADS
    R = SWA_Q_HEADS // G
    qb = q.reshape(B, NB, W, G, R, HEAD_DIM)
    pad = ((0, 0), (W, 0), (0, 0), (0, 0))
    kp = jnp.pad(k, pad).reshape(B, NB + 1, W, G, HEAD_DIM)
    vp = jnp.pad(v, pad).reshape(B, NB + 1, W, G, HEAD_DIM)
    kband = jnp.concatenate([kp[:, :-1], kp[:, 1:]], axis=2)
    vband = jnp.concatenate([vp[:, :-1], vp[:, 1:]], axis=2)
    qi = jnp.arange(W)[:, None]
    kj = jnp.arange(2 * W)[None, :]
    dist = qi + W - kj
    band = (dist >= 0) & (dist < W)
    key_pos = (jnp.arange(NB)[:, None, None] - 1) * W + kj[None]
    mask = band[None] & (key_pos >= 0)
    bias = rel_tbl.astype(jnp.float32)[rel_bucket(dist)]
    bias = bias.transpose(2, 0, 1).reshape(G, R, W, 2 * W)
    s = jnp.einsum('bnqgrd,bnkgd->bngrqk', qb, kband).astype(jnp.float32) * HEAD_DIM ** -0.5 + bias
    s = jnp.where(mask[None, :, None, None], s, NEG)
    sink = sinks.astype(jnp.float32).reshape(G, R)[None, None, :, :, None, None]
    m = jnp.maximum(jnp.max(s, axis=-1, keepdims=True), sink)
    e = jnp.exp(s - m)
    prob = e / (jnp.sum(e, axis=-1, keepdims=True) + jnp.exp(sink - m))
    o = jnp.einsum('bngrqk,bnkgd->bnqgrd', prob.astype(v.dtype), vband)
    return o.reshape(B, S, SWA_Q_W)


def moba_attention(q, k, v, rel_tbl):
    B, S = q.shape[0], q.shape[1]
    G = MOBA_KV_HEADS
    R = MOBA_Q_HEADS // G
    BS = MOBA_BLOCK
    C = MOBA_Q_CHUNK
    NBK = -(-S // BS)
    S_pad = NBK * BS
    K = min(MOBA_TOPK, NBK)
    NC = S // C
    scale = HEAD_DIM ** -0.5
    qh = q.reshape(B, S, G, R, HEAD_DIM).transpose(0, 2, 3, 1, 4)
    pad = ((0, 0), (0, S_pad - S), (0, 0), (0, 0))
    kT = jnp.pad(k, pad).transpose(0, 2, 1, 3)
    vT = jnp.pad(v, pad).transpose(0, 2, 1, 3)
    kblk = kT.reshape(B, G, NBK, BS, HEAD_DIM)
    vblk = vT.reshape(B, G, NBK, BS, HEAD_DIM)
    kmean = jnp.mean(kblk.astype(jnp.float32), axis=3)
    gate = jnp.einsum('bgrsd,bgnd->bgrsn', qh.astype(jnp.float32), kmean)
    fully_past = jnp.arange(NBK)[None, :] < (jnp.arange(S) // BS)[:, None]
    gate = jnp.where(fully_past, gate, NEG)
    _, sel = lax.top_k(gate, K)
    tbl = rel_tbl.astype(jnp.float32).T.reshape(G, R, REL_BUCKETS)
    bi = jnp.arange(B)[:, None, None, None, None]
    gi = jnp.arange(G)[None, :, None, None, None]
    gi6 = jnp.arange(G)[None, :, None, None, None, None]
    ri6 = jnp.arange(R)[None, None, :, None, None, None]

    def chunk(args):
        c, qc, selc = args
        q0 = c * C
        own_blk = q0 // BS
        qpos = q0 + jnp.arange(C)
        kg = kblk[bi, gi, selc]
        vg = vblk[bi, gi, selc]
        s_past = jnp.einsum('bgrcd,bgrcksd->bgrcks', qc, kg).astype(jnp.float32) * scale
        kpos = selc[..., None] * BS + jnp.arange(BS)
        s_past = s_past + tbl[gi6, ri6, rel_bucket(qpos[:, None, None] - kpos)]
        valid = jnp.arange(K) < jnp.minimum(K, own_blk)
        s_past = jnp.where(valid[:, None], s_past, NEG)
        own_start = own_blk * BS
        k_own = lax.dynamic_slice_in_dim(kT, own_start, BS, axis=2)
        v_own = lax.dynamic_slice_in_dim(vT, own_start, BS, axis=2)
        s_own = jnp.einsum('bgrcd,bgsd->bgrcs', qc, k_own).astype(jnp.float32) * scale
        d_own = qpos[:, None] - (own_start + jnp.arange(BS))[None, :]
        s_own = s_own + tbl[:, :, rel_bucket(d_own)]
        s_own = jnp.where(d_own >= 0, s_own, NEG)
        s_all = jnp.concatenate([s_past.reshape(B, G, R, C, K * BS), s_own], axis=-1)
        prob = jax.nn.softmax(s_all, axis=-1).astype(v.dtype)
        p_past = prob[..., :K * BS].reshape(B, G, R, C, K, BS)
        return (jnp.einsum('bgrcks,bgrcksd->bgrcd', p_past, vg)
                + jnp.einsum('bgrcs,bgsd->bgrcd', prob[..., K * BS:], v_own))

    q_chunks = qh.reshape(B, G, R, NC, C, HEAD_DIM).transpose(3, 0, 1, 2, 4, 5)
    sel_chunks = sel.reshape(B, G, R, NC, C, K).transpose(3, 0, 1, 2, 4, 5)
    o = lax.map(chunk, (jnp.arange(NC), q_chunks, sel_chunks))
    return o.transpose(1, 0, 4, 2, 3, 5).reshape(B, S, MOBA_Q_W)


def hier_moe(h, w_rg, b_rg, w_re, b_re, w_g, w_u, w_d):
    T, D = h.shape
    A = T * EXPERT_TOPK
    M = DISPATCH_BLOCK
    NBLK = -(-A // M) + N_EXPERTS
    P = NBLK * M
    glog = (h @ w_rg).astype(jnp.float32) + b_rg.astype(jnp.float32)
    gprob = jax.nn.softmax(glog, axis=-1)
    gsel = jnp.argmax(glog, axis=-1)
    gw = jnp.take_along_axis(gprob, gsel[:, None], axis=1)
    elog = ((h @ w_re).astype(jnp.float32) + b_re.astype(jnp.float32)).reshape(T, N_GROUPS, EXPERTS_PER_GROUP)
    elog = jnp.take_along_axis(elog, gsel[:, None, None], axis=1)[:, 0]
    top_v, top_i = lax.top_k(elog, EXPERT_TOPK)
    w = gw * jax.nn.softmax(top_v, axis=-1)
    eid = (gsel[:, None] * EXPERTS_PER_GROUP + top_i).reshape(A)
    wflat = w.reshape(A)
    counts = jnp.bincount(eid, length=N_EXPERTS)
    start = jnp.cumsum(counts) - counts
    padded = (counts + M - 1) // M * M
    pad_end = jnp.cumsum(padded)
    pad_start = pad_end - padded
    order = jnp.argsort(eid)
    es = eid[order]
    tok = order // EXPERT_TOPK
    dest = pad_start[es] + (jnp.arange(A) - start[es])
    x_buf = jnp.zeros((P, D), h.dtype).at[dest].set(h[tok])
    block_expert = jnp.minimum(jnp.searchsorted(pad_end, jnp.arange(NBLK) * M, side='right'), N_EXPERTS - 1)

    def run_block(args):
        xb, e = args
        return (jax.nn.silu(xb @ w_g[e]) * (xb @ w_u[e])) @ w_d[e]

    out_buf = lax.map(run_block, (x_buf.reshape(NBLK, M, D), block_expert)).reshape(P, D)
    return jnp.zeros((T, D), h.dtype).at[tok].add(out_buf[dest] * wflat[order][:, None].astype(h.dtype))


def setup_inputs(seed: int = 0) -> dict:
    key = jax.random.key(seed)
    ks = jax.random.split(key, 26)
    f = jnp.float32
    def nrm(k, shape, scale):
        return jax.random.normal(k, shape, f) * scale
    def gain(k, shape):
        return 1.0 + 0.05 * jax.random.normal(k, shape, f)
    L = DEPTH
    return {
        'x': nrm(ks[0], (BATCH, SEQ, D_MODEL), 1.0),
        'p': nrm(ks[1], (DEPTH, BATCH, SEQ, PLE_DIM), 1.0),
        'rel_bias': nrm(ks[2], (REL_BUCKETS, N_HEADS), 0.2),
        'attn_norm': gain(ks[3], (L, D_MODEL)),
        'w_in': nrm(ks[4], (L, D_MODEL, IN_W), D_MODEL ** -0.5),
        'swa_q_norm': gain(ks[5], (L, HEAD_DIM)),
        'swa_k_norm': gain(ks[6], (L, HEAD_DIM)),
        'swa_sinks': nrm(ks[7], (L, SWA_Q_HEADS), 0.5),
        'moba_q_norm': gain(ks[8], (L, HEAD_DIM)),
        'moba_k_norm': gain(ks[9], (L, HEAD_DIM)),
        'swa_out_norm': gain(ks[10], (L, SWA_Q_W)),
        'moba_out_norm': gain(ks[11], (L, MOBA_Q_W)),
        'w_out': nrm(ks[12], (L, MIX_W, D_MODEL), MIX_W ** -0.5),
        'ffn_norm': gain(ks[13], (L, D_MODEL)),
        'w_router_group': nrm(ks[14], (L, D_MODEL, N_GROUPS), D_MODEL ** -0.5),
        'b_router_group': nrm(ks[15], (L, N_GROUPS), 0.01),
        'w_router_expert': nrm(ks[16], (L, D_MODEL, N_EXPERTS), D_MODEL ** -0.5),
        'b_router_expert': nrm(ks[17], (L, N_EXPERTS), 0.01),
        'w_exp_gate': nrm(ks[18], (L, N_EXPERTS, D_MODEL, D_EXPERT), D_MODEL ** -0.5),
        'w_exp_up': nrm(ks[19], (L, N_EXPERTS, D_MODEL, D_EXPERT), D_MODEL ** -0.5),
        'w_exp_down': nrm(ks[20], (L, N_EXPERTS, D_EXPERT, D_MODEL), D_EXPERT ** -0.5),
        'ple_norm': gain(ks[21], (L, D_MODEL)),
        'w_ple_gate': nrm(ks[22], (L, D_MODEL, D_MODEL), D_MODEL ** -0.5),
        'w_ple_proj': nrm(ks[23], (L, PLE_DIM, D_MODEL), PLE_DIM ** -0.5),
        'ple_out_norm': gain(ks[24], (L, D_MODEL)),
    }


def reference(x, p, rel_bias, attn_norm, w_in, swa_q_norm, swa_k_norm, swa_sinks,
              moba_q_norm, moba_k_norm, swa_out_norm, moba_out_norm, w_out, ffn_norm,
              w_router_group, b_router_group, w_router_expert, b_router_expert,
              w_exp_gate, w_exp_up, w_exp_down, ple_norm, w_ple_gate, w_ple_proj, ple_out_norm):
    B, S, D = x.shape
    swa_tbl = rel_bias[:, :SWA_Q_HEADS]
    moba_tbl = rel_bias[:, SWA_Q_HEADS:]
    for i in range(DEPTH):
        h = rms_norm(x, attn_norm[i])
        qa, ka, va, qb, kb, vb = jnp.split(h @ w_in[i], SPLITS, axis=-1)
        qa = rms_norm(qa.reshape(B, S, SWA_Q_HEADS, HEAD_DIM), swa_q_norm[i])
        ka = rms_norm(ka.reshape(B, S, SWA_KV_HEADS, HEAD_DIM), swa_k_norm[i])
        va = va.reshape(B, S, SWA_KV_HEADS, HEAD_DIM)
        qb = rms_norm(qb.reshape(B, S, MOBA_Q_HEADS, HEAD_DIM), moba_q_norm[i])
        kb = rms_norm(kb.reshape(B, S, MOBA_KV_HEADS, HEAD_DIM), moba_k_norm[i])
        vb = vb.reshape(B, S, MOBA_KV_HEADS, HEAD_DIM)
        o_a = swa_attention(qa, ka, va, swa_sinks[i], swa_tbl)
        o_b = moba_attention(qb, kb, vb, moba_tbl)
        mix = jnp.concatenate([rms_norm(o_a, swa_out_norm[i]), rms_norm(o_b, moba_out_norm[i])], axis=-1)
        x = x + mix @ w_out[i]
        hf = rms_norm(x, ffn_norm[i]).reshape(B * S, D)
        x = x + hier_moe(hf, w_router_group[i], b_router_group[i], w_router_expert[i], b_router_expert[i],
                         w_exp_gate[i], w_exp_up[i], w_exp_down[i]).reshape(B, S, D)
        gate = jax.nn.sigmoid(rms_norm(x, ple_norm[i]) @ w_ple_gate[i])
        x = x + gate * rms_norm(p[i] @ w_ple_proj[i], ple_out_norm[i])
    return x
```

```python
import functools
import math

import numpy as np
import jax
import jax.numpy as jnp
from jax import lax
from jax.experimental import pallas as pl
from jax.experimental.pallas import tpu as pltpu

F32 = jnp.float32
BF16 = jnp.bfloat16
I32 = jnp.int32

HEAD_DIM = 64
SWA_Q_HEADS = 8
SWA_KV_HEADS = 2
SWA_WINDOW = 128
MOBA_Q_HEADS = 8
MOBA_KV_HEADS = 2
MOBA_BLOCK = 256
MOBA_TOPK = 3
N_HEADS = SWA_Q_HEADS + MOBA_Q_HEADS
REL_BUCKETS = 32
REL_MAX_DIST = 128
N_GROUPS = 4
EXPERTS_PER_GROUP = 8
N_EXPERTS = N_GROUPS * EXPERTS_PER_GROUP
EXPERT_TOPK = 2
DISPATCH_BLOCK = 256
EPS = 1e-6
NEG = -1e30

LANES = 128
GROUP_HEADS = 4
GROUP_W = GROUP_HEADS * HEAD_DIM
ROUTER_ROWS = 128
EXPERT_ROW0 = 8
TOKEN_TILE = 512
PLE_TILE = 256
VMEM_LIMIT = 52 * 1024 * 1024


def _dot_nt(a, b):
    return lax.dot_general(a, b, (((1,), (1,)), ((), ())), preferred_element_type=F32)


def _rmsnorm(x, g):
    ms = jnp.mean(x * x, axis=-1, keepdims=True)
    return x * lax.rsqrt(ms + EPS) * g


def _rel_bucket_np(dist):
    n = np.maximum(dist, 0)
    exact = REL_BUCKETS // 2
    nf = np.maximum(n, 1).astype(np.float32)
    large = exact + (np.log(nf / exact) / math.log(REL_MAX_DIST / exact)
                     * (REL_BUCKETS - exact)).astype(np.int32)
    return np.where(n < exact, n, np.minimum(large, REL_BUCKETS - 1)).astype(np.int32)


def _band_buckets(block):
    qi = np.arange(block)[:, None]
    kj = np.arange(2 * block)[None, :]
    return _rel_bucket_np(qi + block - kj)


def _bias_from_buckets(bkt, tbl_ref, head):
    acc = jnp.zeros(bkt.shape, F32)
    for j in range(REL_BUCKETS):
        acc = jnp.where(bkt == j, tbl_ref[j * N_HEADS + head], acc)
    return acc


def _qkv_kernel(x_ref, g_ref, w_ref, hg_ref, qa_ref, ka_ref, va_ref, qb_ref, kb_ref, vb_ref):
    h = _rmsnorm(x_ref[...], g_ref[...])
    acc = jnp.dot(h.astype(BF16), w_ref[...], preferred_element_type=F32)
    lo = lax.broadcasted_iota(I32, (1, LANES), 1) < HEAD_DIM

    def head_normed(c, scale):
        blk = acc[:, LANES * c:LANES * (c + 1)]
        sq = blk * blk
        s_lo = jnp.sum(jnp.where(lo, sq, 0.0), axis=-1, keepdims=True)
        s_hi = jnp.sum(jnp.where(lo, 0.0, sq), axis=-1, keepdims=True)
        inv = jnp.where(lo, lax.rsqrt(s_lo / HEAD_DIM + EPS), lax.rsqrt(s_hi / HEAD_DIM + EPS))
        return blk * inv * hg_ref[:, LANES * c:LANES * (c + 1)] * scale

    scale = HEAD_DIM ** -0.5
    for c in range(4):
        qa_ref[:, LANES * c:LANES * (c + 1)] = head_normed(c, scale).astype(BF16)
        qb_ref[:, LANES * c:LANES * (c + 1)] = head_normed(6 + c, scale).astype(BF16)
    ka_ref[...] = head_normed(4, 1.0).astype(BF16)
    va_ref[...] = acc[:, LANES * 5:LANES * 6].astype(BF16)
    kb_ref[...] = head_normed(10, 1.0).astype(BF16)
    vb_ref[...] = acc[:, LANES * 11:LANES * 12].astype(BF16)


def _qkv_call(x2d, attn_g, w_in, head_gain):
    T, D = x2d.shape
    tm = TOKEN_TILE
    tok = lambda w: pl.BlockSpec((tm, w), lambda i: (i, 0))
    full = lambda a: pl.BlockSpec(a.shape, lambda i: (0, 0))
    out_w = (SWA_Q_HEADS * HEAD_DIM, LANES, LANES, MOBA_Q_HEADS * HEAD_DIM, LANES, LANES)
    return pl.pallas_call(
        _qkv_kernel,
        grid=(T // tm,),
        in_specs=[tok(D), full(attn_g), full(w_in), full(head_gain)],
        out_specs=[tok(w) for w in out_w],
        out_shape=[jax.ShapeDtypeStruct((T, w), BF16) for w in out_w],
        compiler_params=pltpu.CompilerParams(
            dimension_semantics=("arbitrary",), vmem_limit_bytes=VMEM_LIMIT),
        name="qkv",
    )(x2d, attn_g, w_in, head_gain)


def _swa_kernel(tbl_ref, sink_ref, bkt_ref, q_ref, k_ref, v_ref, o_ref, bias_scr, kg_scr, vg_scr):
    W = SWA_WINDOW
    R = GROUP_HEADS
    S = q_ref.shape[0]

    @pl.when(pl.program_id(0) == 0)
    def _init_bias():
        bkt = bkt_ref[...]
        for h in range(SWA_Q_HEADS):
            bias_scr[h] = _bias_from_buckets(bkt, tbl_ref, h)

    qi = lax.broadcasted_iota(I32, (W, 2 * W), 0)
    kj = lax.broadcasted_iota(I32, (W, 2 * W), 1)
    dist = qi + W - kj
    band = (dist >= 0) & (dist < W)
    own_part = kj >= W
    head_id = lax.broadcasted_iota(I32, (R, 1, 1), 0)

    for g in range(SWA_KV_HEADS):
        kg_scr[...] = k_ref[:, HEAD_DIM * g:HEAD_DIM * (g + 1)]
        vg_scr[...] = v_ref[:, HEAD_DIM * g:HEAD_DIM * (g + 1)]
        sink = jnp.zeros((R, 1, 1), F32)
        for r in range(R):
            sink = jnp.where(head_id == r, sink_ref[R * g + r], sink)
        bias = bias_scr[R * g:R * (g + 1)]

        def body(i, carry, g=g, sink=sink, bias=bias):
            r0 = pl.multiple_of(i * W, W)
            p0 = pl.multiple_of(jnp.maximum(i - 1, 0) * W, W)
            qblk = q_ref[pl.ds(r0, W), GROUP_W * g:GROUP_W * (g + 1)]
            q4 = jnp.concatenate([qblk[:, HEAD_DIM * r:HEAD_DIM * (r + 1)] for r in range(R)], axis=0)
            kband = jnp.concatenate([kg_scr[pl.ds(p0, W), :], kg_scr[pl.ds(r0, W), :]], axis=0)
            vband = jnp.concatenate([vg_scr[pl.ds(p0, W), :], vg_scr[pl.ds(r0, W), :]], axis=0)
            s = _dot_nt(q4, kband).reshape(R, W, 2 * W) + bias
            mask = band & (own_part | (i > 0))
            s = jnp.where(mask[None], s, NEG)
            m = jnp.maximum(jnp.max(s, axis=-1, keepdims=True), sink)
            e = jnp.exp(s - m)
            den = jnp.sum(e, axis=-1, keepdims=True) + jnp.exp(sink - m)
            p = (e / den).astype(BF16).reshape(R * W, 2 * W)
            o = jnp.dot(p, vband, preferred_element_type=F32)
            o_ref[pl.ds(r0, W), GROUP_W * g:GROUP_W * (g + 1)] = jnp.concatenate(
                [o[W * r:W * (r + 1)] for r in range(R)], axis=1)
            return carry

        lax.fori_loop(0, S // W, body, 0)


def _swa_call(tbl_flat, sinks, qa, ka, va, B, S):
    W = SWA_WINDOW
    bkt = jnp.asarray(_band_buckets(W))
    smem = pl.BlockSpec(memory_space=pltpu.SMEM)
    return pl.pallas_call(
        _swa_kernel,
        grid=(B,),
        in_specs=[smem, smem,
                  pl.BlockSpec(bkt.shape, lambda b: (0, 0)),
                  pl.BlockSpec((S, qa.shape[1]), lambda b: (b, 0)),
                  pl.BlockSpec((S, LANES), lambda b: (b, 0)),
                  pl.BlockSpec((S, LANES), lambda b: (b, 0))],
        out_specs=pl.BlockSpec((S, qa.shape[1]), lambda b: (b, 0)),
        out_shape=jax.ShapeDtypeStruct(qa.shape, F32),
        scratch_shapes=[pltpu.VMEM((SWA_Q_HEADS, W, 2 * W), F32),
                        pltpu.VMEM((S, HEAD_DIM), BF16),
                        pltpu.VMEM((S, HEAD_DIM), BF16)],
        compiler_params=pltpu.CompilerParams(
            dimension_semantics=("arbitrary",), vmem_limit_bytes=VMEM_LIMIT),
        name="swa",
    )(tbl_flat, sinks, bkt, qa, ka, va)


def _moba_kernel(tbl_ref, bkt_ref, q_ref, k_ref, v_ref, o_ref,
                 bias_scr, kg_scr, vg_scr, km_scr, q4_scr, sel_scr, m_scr, l_scr, acc_scr):
    BS = MOBA_BLOCK
    R = GROUP_HEADS
    S = q_ref.shape[0]
    NBK = S // BS
    ROWS = R * BS

    @pl.when(pl.program_id(0) == 0)
    def _init_bias():
        bkt = bkt_ref[...]
        for h in range(MOBA_Q_HEADS):
            g, r = divmod(h, R)
            band = _bias_from_buckets(bkt, tbl_ref, SWA_Q_HEADS + h)
            far = tbl_ref[(REL_BUCKETS - 1) * N_HEADS + SWA_Q_HEADS + h]
            bias_scr[g, 0, r] = jnp.full((BS, BS), far, F32)
            bias_scr[g, 1, r] = band[:, :BS]
            bias_scr[g, 2, r] = band[:, BS:]

    lane = lax.broadcasted_iota(I32, (1, LANES), 1)
    causal = (lax.broadcasted_iota(I32, (BS, BS), 0) >= lax.broadcasted_iota(I32, (BS, BS), 1))[None]

    for g in range(MOBA_KV_HEADS):
        kg_scr[...] = k_ref[:, HEAD_DIM * g:HEAD_DIM * (g + 1)]
        vg_scr[...] = v_ref[:, HEAD_DIM * g:HEAD_DIM * (g + 1)]
        kmean = jnp.sum(kg_scr[...].astype(F32).reshape(NBK, BS, HEAD_DIM), axis=1) / BS
        kmean = jnp.concatenate([kmean, jnp.zeros((LANES - NBK, HEAD_DIM), F32)], axis=0)
        km_hi = kmean.astype(BF16)
        km_scr[0] = km_hi
        km_scr[1] = (kmean - km_hi.astype(F32)).astype(BF16)

        def qblock(mi, carry, g=g):
            r0 = pl.multiple_of(mi * BS, BS)
            qblk = q_ref[pl.ds(r0, BS), GROUP_W * g:GROUP_W * (g + 1)]
            q4 = jnp.concatenate([qblk[:, HEAD_DIM * r:HEAD_DIM * (r + 1)] for r in range(R)], axis=0)
            q4_scr[...] = q4
            gate = _dot_nt(q4, km_scr[0]) + _dot_nt(q4, km_scr[1])
            past = lane < mi
            gm = jnp.where(past, gate, NEG)
            cnt = jnp.zeros((ROWS, LANES), F32)
            for n in range(NBK - 1):
                col = gm[:, n:n + 1]
                beats = (gm > col) | ((gm == col) & (lane < n))
                c = jnp.sum(jnp.where(beats, 1.0, 0.0), axis=-1, keepdims=True)
                cnt = jnp.where(lane == n, c, cnt)
            sel_scr[...] = jnp.where((past & (cnt < MOBA_TOPK)) | (lane == mi), 1.0, 0.0)
            m_scr[...] = jnp.full(m_scr.shape, NEG, F32)
            l_scr[...] = jnp.zeros(l_scr.shape, F32)
            acc_scr[...] = jnp.zeros(acc_scr.shape, F32)

            def kvblock(j, c2):
                n = mi - j
                k0 = pl.multiple_of(n * BS, BS)
                case = jnp.where(j == 0, 2, jnp.where(j == 1, 1, 0))
                s = _dot_nt(q4_scr[...], kg_scr[pl.ds(k0, BS), :]).reshape(R, BS, BS) + bias_scr[g, case]
                selcol = jnp.sum(jnp.where(lane == n, sel_scr[...], 0.0), axis=-1, keepdims=True)
                allowed = (selcol.reshape(R, BS, 1) > 0.5) & (causal | (j > 0))
                s = jnp.where(allowed, s, NEG).reshape(ROWS, BS)
                m_prev = m_scr[...]
                m_new = jnp.maximum(m_prev, jnp.max(s, axis=-1, keepdims=True))
                alpha = jnp.exp(m_prev - m_new)
                e = jnp.exp(s - m_new)
                l_scr[...] = alpha * l_scr[...] + jnp.sum(e, axis=-1, keepdims=True)
                acc_scr[...] = alpha * acc_scr[...] + jnp.dot(
                    e.astype(BF16), vg_scr[pl.ds(k0, BS), :], preferred_element_type=F32)
                m_scr[...] = m_new
                return c2

            lax.fori_loop(0, mi + 1, kvblock, 0)
            o = acc_scr[...] / l_scr[...]
            o_ref[pl.ds(r0, BS), GROUP_W * g:GROUP_W * (g + 1)] = jnp.concatenate(
                [o[BS * r:BS * (r + 1)] for r in range(R)], axis=1)
            return carry

        lax.fori_loop(0, NBK, qblock, 0)


def _moba_call(tbl_flat, qb, kb, vb, B, S):
    BS = MOBA_BLOCK
    R = GROUP_HEADS
    bkt = jnp.asarray(_band_buckets(BS))
    smem = pl.BlockSpec(memory_space=pltpu.SMEM)
    return pl.pallas_call(
        _moba_kernel,
        grid=(B,),
        in_specs=[smem,
                  pl.BlockSpec(bkt.shape, lambda b: (0, 0)),
                  pl.BlockSpec((S, qb.shape[1]), lambda b: (b, 0)),
                  pl.BlockSpec((S, LANES), lambda b: (b, 0)),
                  pl.BlockSpec((S, LANES), lambda b: (b, 0))],
        out_specs=pl.BlockSpec((S, qb.shape[1]), lambda b: (b, 0)),
        out_shape=jax.ShapeDtypeStruct(qb.shape, F32),
        scratch_shapes=[pltpu.VMEM((MOBA_KV_HEADS, 3, R, BS, BS), F32),
                        pltpu.VMEM((S, HEAD_DIM), BF16),
                        pltpu.VMEM((S, HEAD_DIM), BF16),
                        pltpu.VMEM((2, LANES, HEAD_DIM), BF16),
                        pltpu.VMEM((R * BS, HEAD_DIM), BF16),
                        pltpu.VMEM((R * BS, LANES), F32),
                        pltpu.VMEM((R * BS, 1), F32),
                        pltpu.VMEM((R * BS, 1), F32),
                        pltpu.VMEM((R * BS, HEAD_DIM), F32)],
        compiler_params=pltpu.CompilerParams(
            dimension_semantics=("arbitrary",), vmem_limit_bytes=VMEM_LIMIT),
        name="moba",
    )(tbl_flat, bkt, qb, kb, vb)


def _outproj_kernel(oa_ref, ob_ref, x_ref, ga_ref, gb_ref, woa_ref, wob_ref, fg_ref,
                    wrh_ref, wrl_ref, rb_ref, x1_ref, hf_ref, eid_ref, wt_ref):
    na = _rmsnorm(oa_ref[...], ga_ref[...]).astype(BF16)
    nb = _rmsnorm(ob_ref[...], gb_ref[...]).astype(BF16)
    y = (jnp.dot(na, woa_ref[...], preferred_element_type=F32)
         + jnp.dot(nb, wob_ref[...], preferred_element_type=F32))
    x1 = x_ref[...] + y
    x1_ref[...] = x1
    hf = _rmsnorm(x1, fg_ref[...])
    hf_ref[...] = hf

    hf_hi = hf.astype(BF16)
    hf_lo = (hf - hf_hi.astype(F32)).astype(BF16)
    lt = (_dot_nt(wrh_ref[...], hf_hi) + _dot_nt(wrh_ref[...], hf_lo)
          + _dot_nt(wrl_ref[...], hf_hi) + rb_ref[...])
    gl = [lt[j:j + 1, :] for j in range(N_GROUPS)]
    best = gl[0]
    gsel = jnp.zeros(best.shape, I32)
    for j in range(1, N_GROUPS):
        better = gl[j] > best
        gsel = jnp.where(better, j, gsel)
        best = jnp.where(better, gl[j], best)
    gsum = jnp.zeros(best.shape, F32)
    for j in range(N_GROUPS):
        gsum = gsum + jnp.exp(gl[j] - best)
    gw = 1.0 / gsum
    E = EXPERTS_PER_GROUP
    es = lt[EXPERT_ROW0:EXPERT_ROW0 + E, :]
    for j in range(1, N_GROUPS):
        es = jnp.where(gsel == j, lt[EXPERT_ROW0 + E * j:EXPERT_ROW0 + E * (j + 1), :], es)
    row = lax.broadcasted_iota(I32, es.shape, 0)
    v1 = jnp.max(es, axis=0, keepdims=True)
    i1 = jnp.min(jnp.where(es == v1, row, E), axis=0, keepdims=True)
    es2 = jnp.where(row == i1, -jnp.inf, es)
    v2 = jnp.max(es2, axis=0, keepdims=True)
    i2 = jnp.min(jnp.where(es2 == v2, row, E), axis=0, keepdims=True)
    e2 = jnp.exp(v2 - v1)
    den = 1.0 + e2
    eid_ref[...] = jnp.concatenate([gsel * E + i1, gsel * E + i2], axis=0)
    wt_ref[...] = jnp.concatenate([gw * (1.0 / den), gw * (e2 / den)], axis=0)


def _outproj_call(oa, ob, x2d, ga, gb, woa, wob, fg, wrh, wrl, rb):
    T, D = x2d.shape
    tm = TOKEN_TILE
    tok = lambda w: pl.BlockSpec((tm, w), lambda i: (i, 0))
    full = lambda a: pl.BlockSpec(a.shape, lambda i: (0, 0))
    col = pl.BlockSpec((EXPERT_TOPK, tm), lambda i: (0, i))
    return pl.pallas_call(
        _outproj_kernel,
        grid=(T // tm,),
        in_specs=[tok(oa.shape[1]), tok(ob.shape[1]), tok(D), full(ga), full(gb), full(woa), full(wob),
                  full(fg), full(wrh), full(wrl), full(rb)],
        out_specs=[tok(D), tok(D), col, col],
        out_shape=[jax.ShapeDtypeStruct((T, D), F32), jax.ShapeDtypeStruct((T, D), F32),
                   jax.ShapeDtypeStruct((EXPERT_TOPK, T), I32), jax.ShapeDtypeStruct((EXPERT_TOPK, T), F32)],
        compiler_params=pltpu.CompilerParams(
            dimension_semantics=("arbitrary",), vmem_limit_bytes=VMEM_LIMIT),
        name="outproj",
    )(oa, ob, x2d, ga, gb, woa, wob, fg, wrh, wrl, rb)


def _row_copy(src_hbm, row, dst, slot, sem):
    return pltpu.make_async_copy(src_hbm.at[pl.ds(row, 1)], dst.at[pl.ds(slot, 1)], sem)


def _moe_kernel(be_ref, nu_ref, idx_ref, hf_hbm, wg_ref, wu_ref, wd_ref, o_ref, xbuf, sem):
    i = pl.program_id(0)
    M = xbuf.shape[0]

    @pl.when(i < nu_ref[0])
    def _compute():
        def issue(r, c):
            _row_copy(hf_hbm, idx_ref[0, 0, r], xbuf, r, sem).start()
            return c
        lax.fori_loop(0, M, issue, 0)

        def drain(r, c):
            _row_copy(hf_hbm, 0, xbuf, r, sem).wait()
            return c
        lax.fori_loop(0, M, drain, 0)

        xb = xbuf[...].astype(BF16)
        a = jnp.dot(xb, wg_ref[0].astype(BF16), preferred_element_type=F32)
        u = jnp.dot(xb, wu_ref[0].astype(BF16), preferred_element_type=F32)
        hmid = (a * jax.nn.sigmoid(a)) * u
        o_ref[...] = jnp.dot(hmid.astype(BF16), wd_ref[0].astype(BF16), preferred_element_type=F32)

    @pl.when(i >= nu_ref[0])
    def _unused():
        o_ref[...] = jnp.zeros(o_ref.shape, F32)


def _moe_call(block_expert, n_used, src_tok, hf, w_g, w_u, w_d):
    T, D = hf.shape
    M = DISPATCH_BLOCK
    nblk = block_expert.shape[0]
    DE = w_g.shape[2]
    grid_spec = pltpu.PrefetchScalarGridSpec(
        num_scalar_prefetch=2,
        grid=(nblk,),
        in_specs=[pl.BlockSpec((1, 1, M), lambda i, be, nu: (i, 0, 0), memory_space=pltpu.SMEM),
                  pl.BlockSpec(memory_space=pl.ANY),
                  pl.BlockSpec((1, D, DE), lambda i, be, nu: (be[i], 0, 0)),
                  pl.BlockSpec((1, D, DE), lambda i, be, nu: (be[i], 0, 0)),
                  pl.BlockSpec((1, DE, D), lambda i, be, nu: (be[i], 0, 0))],
        out_specs=pl.BlockSpec((M, D), lambda i, be, nu: (i, 0)),
        scratch_shapes=[pltpu.VMEM((M, D), F32), pltpu.SemaphoreType.DMA],
    )
    return pl.pallas_call(
        _moe_kernel,
        grid_spec=grid_spec,
        out_shape=jax.ShapeDtypeStruct((nblk * M, D), F32),
        compiler_params=pltpu.CompilerParams(
            dimension_semantics=("arbitrary",), vmem_limit_bytes=VMEM_LIMIT),
        name="moe",
    )(block_expert, n_used, src_tok.reshape(nblk, 1, M), hf, w_g, w_u, w_d)


def _ple_kernel(idx_ref, obuf_hbm, x1_ref, wt_ref, p_ref, pg_ref, wgate_ref, wproj_ref, pog_ref,
                out_ref, gbuf, sem):
    tm = x1_ref.shape[0]

    def issue(r, c):
        _row_copy(obuf_hbm, idx_ref[0, 0, r], gbuf, r, sem).start()
        return c
    lax.fori_loop(0, EXPERT_TOPK * tm, issue, 0)

    proj = _rmsnorm(jnp.dot(p_ref[...].astype(BF16), wproj_ref[...], preferred_element_type=F32),
                    pog_ref[...])

    def drain(r, c):
        _row_copy(obuf_hbm, 0, gbuf, r, sem).wait()
        return c
    lax.fori_loop(0, EXPERT_TOPK * tm, drain, 0)

    moe = gbuf[0:tm, :] * wt_ref[:, 0:1] + gbuf[tm:2 * tm, :] * wt_ref[:, 1:2]
    x2 = x1_ref[...] + moe
    hn = _rmsnorm(x2, pg_ref[...]).astype(BF16)
    gate = jax.nn.sigmoid(jnp.dot(hn, wgate_ref[...], preferred_element_type=F32))
    out_ref[...] = x2 + gate * proj


def _ple_call(pos_tiles, out_buf, x1, wt_cols, p2d, pg, wgate, wproj, pog):
    T, D = x1.shape
    tm = PLE_TILE
    tok = lambda w: pl.BlockSpec((tm, w), lambda i: (i, 0))
    full = lambda a: pl.BlockSpec(a.shape, lambda i: (0, 0))
    return pl.pallas_call(
        _ple_kernel,
        grid=(T // tm,),
        in_specs=[pl.BlockSpec((1, 1, EXPERT_TOPK * tm), lambda i: (i, 0, 0), memory_space=pltpu.SMEM),
                  pl.BlockSpec(memory_space=pl.ANY),
                  tok(D), tok(EXPERT_TOPK), tok(p2d.shape[1]), full(pg), full(wgate), full(wproj), full(pog)],
        out_specs=tok(D),
        out_shape=jax.ShapeDtypeStruct((T, D), F32),
        scratch_shapes=[pltpu.VMEM((EXPERT_TOPK * tm, D), F32), pltpu.SemaphoreType.DMA],
        compiler_params=pltpu.CompilerParams(
            dimension_semantics=("arbitrary",), vmem_limit_bytes=VMEM_LIMIT),
        name="ple",
    )(pos_tiles, out_buf, x1, wt_cols, p2d, pg, wgate, wproj, pog)


def _dispatch_plan(eid, T):
    A = EXPERT_TOPK * T
    M = DISPATCH_BLOCK
    nblk = -(-A // M) + N_EXPERTS
    e_flat = eid.reshape(A)
    tok_flat = jnp.tile(jnp.arange(T, dtype=I32), EXPERT_TOPK)
    counts = jnp.bincount(e_flat, length=N_EXPERTS).astype(I32)
    start = jnp.cumsum(counts) - counts
    padded = (counts + M - 1) // M * M
    pad_end = jnp.cumsum(padded)
    pad_start = pad_end - padded
    order = jnp.argsort(e_flat).astype(I32)
    es = e_flat[order]
    dest_sorted = pad_start[es] + (jnp.arange(A, dtype=I32) - start[es])
    src_tok = jnp.zeros((nblk * M,), I32).at[dest_sorted].set(tok_flat[order])
    dest = jnp.zeros((A,), I32).at[order].set(dest_sorted)
    n_used = (pad_end[-1] // M).astype(I32)
    blk = jnp.arange(nblk, dtype=I32)
    be = jnp.minimum(jnp.searchsorted(pad_end, blk * M, side='right'), N_EXPERTS - 1).astype(I32)
    be = jnp.where(blk < n_used, be, be[jnp.maximum(n_used - 1, 0)])
    return be, n_used.reshape(1), src_tok, dest.reshape(EXPERT_TOPK, T)


def _layer(x2d, p2d, rel_bias, attn_norm, w_in, swa_q_norm, swa_k_norm, swa_sinks, moba_q_norm,
           moba_k_norm, swa_out_norm, moba_out_norm, w_out, ffn_norm, w_rg, b_rg, w_re, b_re,
           w_g, w_u, w_d, ple_norm, w_ple_gate, w_ple_proj, ple_out_norm, B, S):
    T, D = x2d.shape
    row = lambda v: v.reshape(1, -1).astype(F32)
    ones = jnp.ones((SWA_KV_HEADS * HEAD_DIM,), F32)
    head_gain = jnp.concatenate([
        jnp.tile(swa_q_norm, SWA_Q_HEADS), jnp.tile(swa_k_norm, SWA_KV_HEADS), ones,
        jnp.tile(moba_q_norm, MOBA_Q_HEADS), jnp.tile(moba_k_norm, MOBA_KV_HEADS), ones])
    qa, ka, va, qb, kb, vb = _qkv_call(x2d, row(attn_norm), w_in.astype(BF16), row(head_gain))

    tbl_flat = rel_bias.astype(F32).reshape(-1)
    oa = _swa_call(tbl_flat, swa_sinks.astype(F32), qa, ka, va, B, S)
    ob = _moba_call(tbl_flat, qb, kb, vb, B, S)

    wr = jnp.zeros((ROUTER_ROWS, D), F32)
    wr = wr.at[:N_GROUPS].set(w_rg.T).at[EXPERT_ROW0:EXPERT_ROW0 + N_EXPERTS].set(w_re.T)
    wr_hi = wr.astype(BF16)
    wr_lo = (wr - wr_hi.astype(F32)).astype(BF16)
    rb = jnp.zeros((ROUTER_ROWS, 1), F32)
    rb = rb.at[:N_GROUPS, 0].set(b_rg).at[EXPERT_ROW0:EXPERT_ROW0 + N_EXPERTS, 0].set(b_re)
    wo = w_out.astype(BF16)
    na_w = SWA_Q_HEADS * HEAD_DIM
    x1, hf, eid, wts = _outproj_call(oa, ob, x2d, row(swa_out_norm), row(moba_out_norm),
                                     wo[:na_w], wo[na_w:], row(ffn_norm), wr_hi, wr_lo, rb)

    be, n_used, src_tok, dest = _dispatch_plan(eid, T)
    out_buf = _moe_call(be, n_used, src_tok, hf, w_g, w_u, w_d)

    tm = PLE_TILE
    pos_tiles = dest.reshape(EXPERT_TOPK, T // tm, tm).transpose(1, 0, 2).reshape(T // tm, 1, EXPERT_TOPK * tm)
    return _ple_call(pos_tiles, out_buf, x1, wts.T, p2d, row(ple_norm), w_ple_gate.astype(BF16),
                     w_ple_proj.astype(BF16), row(ple_out_norm))


def kernel(x, p, rel_bias, attn_norm, w_in, swa_q_norm, swa_k_norm, swa_sinks, moba_q_norm, moba_k_norm,
           swa_out_norm, moba_out_norm, w_out, ffn_norm, w_router_group, b_router_group, w_router_expert,
           b_router_expert, w_exp_gate, w_exp_up, w_exp_down, ple_norm, w_ple_gate, w_ple_proj, ple_out_norm):
    B, S, D = x.shape
    x2d = x.reshape(B * S, D)
    for i in range(p.shape[0]):
        x2d = _layer(x2d, p[i].reshape(B * S, -1), rel_bias, attn_norm[i], w_in[i], swa_q_norm[i],
                     swa_k_norm[i], swa_sinks[i], moba_q_norm[i], moba_k_norm[i], swa_out_norm[i],
                     moba_out_norm[i], w_out[i], ffn_norm[i], w_router_group[i], b_router_group[i],
                     w_router_expert[i], b_router_expert[i], w_exp_gate[i], w_exp_up[i], w_exp_down[i],
                     ple_norm[i], w_ple_gate[i], w_ple_proj[i], ple_out_norm[i], B, S)
    return x2d.reshape(B, S, D)
```

```python
import math

import numpy as np
import jax
import jax.numpy as jnp
from jax import lax
from jax.experimental import pallas as pl
from jax.experimental.pallas import tpu as pltpu

F32 = jnp.float32
BF16 = jnp.bfloat16
I32 = jnp.int32

HEAD_DIM = 64
SWA_Q_HEADS = 8
SWA_KV_HEADS = 2
SWA_WINDOW = 128
MOBA_Q_HEADS = 8
MOBA_KV_HEADS = 2
MOBA_BLOCK = 256
MOBA_TOPK = 3
N_HEADS = SWA_Q_HEADS + MOBA_Q_HEADS
REL_BUCKETS = 32
REL_MAX_DIST = 128
N_GROUPS = 4
EXPERTS_PER_GROUP = 8
N_EXPERTS = N_GROUPS * EXPERTS_PER_GROUP
EXPERT_TOPK = 2
DISPATCH_BLOCK = 256
EPS = 1e-6
NEG = -1e30

LANES = 128
BF16_SUBLANES = 16
GROUP_HEADS = 4
GROUP_W = GROUP_HEADS * HEAD_DIM
ROUTER_ROWS = 128
EXPERT_ROW0 = 8
TOKEN_TILE = 512
PLE_TILE = 256
VMEM_LIMIT = 52 * 1024 * 1024


def _dot_nt(a, b):
    return lax.dot_general(a, b, (((1,), (1,)), ((), ())), preferred_element_type=F32)


def _rmsnorm(x, g):
    ms = jnp.mean(x * x, axis=-1, keepdims=True)
    return x * lax.rsqrt(ms + EPS) * g


def _rel_bucket_np(dist):
    n = np.maximum(dist, 0)
    exact = REL_BUCKETS // 2
    nf = np.maximum(n, 1).astype(np.float32)
    large = exact + (np.log(nf / exact) / math.log(REL_MAX_DIST / exact)
                     * (REL_BUCKETS - exact)).astype(np.int32)
    return np.where(n < exact, n, np.minimum(large, REL_BUCKETS - 1)).astype(np.int32)


def _band_buckets(block):
    qi = np.arange(block)[:, None]
    kj = np.arange(2 * block)[None, :]
    return _rel_bucket_np(qi + block - kj)


def _bias_from_buckets(bkt, tbl_ref, head):
    acc = jnp.zeros(bkt.shape, F32)
    for j in range(REL_BUCKETS):
        acc = jnp.where(bkt == j, tbl_ref[j * N_HEADS + head], acc)
    return acc


def _qkv_kernel(x_ref, g_ref, w_ref, wvt_ref, hg_ref, qa_ref, ka_ref, va_ref, qb_ref, kb_ref, vbt_ref):
    h = _rmsnorm(x_ref[...], g_ref[...]).astype(BF16)
    acc = jnp.dot(h, w_ref[...], preferred_element_type=F32)
    lo = lax.broadcasted_iota(I32, (1, LANES), 1) < HEAD_DIM

    def head_normed(c, scale):
        blk = acc[:, LANES * c:LANES * (c + 1)]
        sq = blk * blk
        s_lo = jnp.sum(jnp.where(lo, sq, 0.0), axis=-1, keepdims=True)
        s_hi = jnp.sum(jnp.where(lo, 0.0, sq), axis=-1, keepdims=True)
        inv = jnp.where(lo, lax.rsqrt(s_lo / HEAD_DIM + EPS), lax.rsqrt(s_hi / HEAD_DIM + EPS))
        return blk * inv * hg_ref[:, LANES * c:LANES * (c + 1)] * scale

    scale = HEAD_DIM ** -0.5
    for c in range(4):
        qa_ref[:, LANES * c:LANES * (c + 1)] = head_normed(c, scale).astype(BF16)
        qb_ref[:, LANES * c:LANES * (c + 1)] = head_normed(6 + c, scale).astype(BF16)
    ka_ref[...] = head_normed(4, 1.0).astype(BF16)
    va_ref[...] = acc[:, LANES * 5:LANES * 6].astype(BF16)
    kb_ref[...] = head_normed(10, 1.0).astype(BF16)
    vbt = _dot_nt(wvt_ref[...], h).astype(BF16)
    for j in range(vbt_ref.shape[0]):
        vbt_ref[j] = vbt[:, MOBA_BLOCK * j:MOBA_BLOCK * (j + 1)]


def _qkv_call(x2d, attn_g, w_in, w_vbt, head_gain):
    T, D = x2d.shape
    tm = TOKEN_TILE
    tok = lambda w: pl.BlockSpec((tm, w), lambda i: (i, 0))
    full = lambda a: pl.BlockSpec(a.shape, lambda i: (0, 0))
    out_w = (SWA_Q_HEADS * HEAD_DIM, LANES, LANES, MOBA_Q_HEADS * HEAD_DIM, LANES)
    nb = tm // MOBA_BLOCK
    return pl.pallas_call(
        _qkv_kernel,
        grid=(T // tm,),
        in_specs=[tok(D), full(attn_g), full(w_in), full(w_vbt), full(head_gain)],
        out_specs=[tok(w) for w in out_w] + [pl.BlockSpec((nb, LANES, MOBA_BLOCK), lambda i: (i, 0, 0))],
        out_shape=[jax.ShapeDtypeStruct((T, w), BF16) for w in out_w]
        + [jax.ShapeDtypeStruct((T // MOBA_BLOCK, LANES, MOBA_BLOCK), BF16)],
        compiler_params=pltpu.CompilerParams(
            dimension_semantics=("arbitrary",), vmem_limit_bytes=VMEM_LIMIT),
        name="qkv",
    )(x2d, attn_g, w_in, w_vbt, head_gain)


def _swa_kernel(tbl_ref, sink_ref, bkt_ref, q_ref, k_ref, v_ref, o_ref, bias_scr, kg_scr, vg_scr):
    W = SWA_WINDOW
    R = GROUP_HEADS
    S = q_ref.shape[0]

    @pl.when(pl.program_id(0) == 0)
    def _init_bias():
        bkt = bkt_ref[...]
        for h in range(SWA_Q_HEADS):
            bias_scr[h] = _bias_from_buckets(bkt, tbl_ref, h)

    qi = lax.broadcasted_iota(I32, (W, 2 * W), 0)
    kj = lax.broadcasted_iota(I32, (W, 2 * W), 1)
    dist = qi + W - kj
    band = (dist >= 0) & (dist < W)
    own_part = kj >= W
    head_id = lax.broadcasted_iota(I32, (R, 1, 1), 0)

    for g in range(SWA_KV_HEADS):
        kg_scr[...] = k_ref[:, HEAD_DIM * g:HEAD_DIM * (g + 1)]
        vg_scr[...] = v_ref[:, HEAD_DIM * g:HEAD_DIM * (g + 1)]
        sink = jnp.zeros((R, 1, 1), F32)
        for r in range(R):
            sink = jnp.where(head_id == r, sink_ref[R * g + r], sink)
        bias = bias_scr[R * g:R * (g + 1)]

        def body(i, carry, g=g, sink=sink, bias=bias):
            r0 = pl.multiple_of(i * W, W)
            p0 = pl.multiple_of(jnp.maximum(i - 1, 0) * W, W)
            qblk = q_ref[pl.ds(r0, W), GROUP_W * g:GROUP_W * (g + 1)]
            q4 = jnp.concatenate([qblk[:, HEAD_DIM * r:HEAD_DIM * (r + 1)] for r in range(R)], axis=0)
            kband = jnp.concatenate([kg_scr[pl.ds(p0, W), :], kg_scr[pl.ds(r0, W), :]], axis=0)
            vband = jnp.concatenate([vg_scr[pl.ds(p0, W), :], vg_scr[pl.ds(r0, W), :]], axis=0)
            s = _dot_nt(q4, kband).reshape(R, W, 2 * W) + bias
            mask = band & (own_part | (i > 0))
            s = jnp.where(mask[None], s, NEG)
            m = jnp.maximum(jnp.max(s, axis=-1, keepdims=True), sink)
            e = jnp.exp(s - m)
            den = jnp.sum(e, axis=-1, keepdims=True) + jnp.exp(sink - m)
            p = (e / den).astype(BF16).reshape(R * W, 2 * W)
            o = jnp.dot(p, vband, preferred_element_type=F32)
            o_ref[pl.ds(r0, W), GROUP_W * g:GROUP_W * (g + 1)] = jnp.concatenate(
                [o[W * r:W * (r + 1)] for r in range(R)], axis=1)
            return carry

        lax.fori_loop(0, S // W, body, 0)


def _swa_call(tbl_flat, sinks, qa, ka, va, B, S):
    W = SWA_WINDOW
    bkt = jnp.asarray(_band_buckets(W))
    smem = pl.BlockSpec(memory_space=pltpu.SMEM)
    return pl.pallas_call(
        _swa_kernel,
        grid=(B,),
        in_specs=[smem, smem,
                  pl.BlockSpec(bkt.shape, lambda b: (0, 0)),
                  pl.BlockSpec((S, qa.shape[1]), lambda b: (b, 0)),
                  pl.BlockSpec((S, LANES), lambda b: (b, 0)),
                  pl.BlockSpec((S, LANES), lambda b: (b, 0))],
        out_specs=pl.BlockSpec((S, qa.shape[1]), lambda b: (b, 0)),
        out_shape=jax.ShapeDtypeStruct(qa.shape, F32),
        scratch_shapes=[pltpu.VMEM((SWA_Q_HEADS, W, 2 * W), F32),
                        pltpu.VMEM((S, HEAD_DIM), BF16),
                        pltpu.VMEM((S, HEAD_DIM), BF16)],
        compiler_params=pltpu.CompilerParams(
            dimension_semantics=("arbitrary",), vmem_limit_bytes=VMEM_LIMIT),
        name="swa",
    )(tbl_flat, sinks, bkt, qa, ka, va)


def _moba_kernel(tbl_ref, bkt_ref, q_ref, k_ref, vt_ref, o_ref,
                 bias_scr, kg_scr, km_scr, q4_scr, sel_scr, m_scr, l_scr, acc_scr):
    BS = MOBA_BLOCK
    R = GROUP_HEADS
    S = q_ref.shape[0]
    NBK = S // BS
    ROWS = R * BS

    @pl.when(pl.program_id(0) == 0)
    def _init_bias():
        bkt = bkt_ref[...]
        for h in range(MOBA_Q_HEADS):
            g, r = divmod(h, R)
            band = _bias_from_buckets(bkt, tbl_ref, SWA_Q_HEADS + h)
            far = tbl_ref[(REL_BUCKETS - 1) * N_HEADS + SWA_Q_HEADS + h]
            bias_scr[g, 0, :, BS * r:BS * (r + 1)] = jnp.full((BS, BS), far, F32)
            bias_scr[g, 1, :, BS * r:BS * (r + 1)] = band[:BS]
            bias_scr[g, 2, :, BS * r:BS * (r + 1)] = band[BS:]

    blk_row = lax.broadcasted_iota(I32, (BF16_SUBLANES, 1), 0)

    for g in range(MOBA_KV_HEADS):
        kg_scr[...] = k_ref[:, HEAD_DIM * g:HEAD_DIM * (g + 1)]
        kmean = jnp.sum(kg_scr[...].astype(F32).reshape(NBK, BS, HEAD_DIM), axis=1) / BS
        kmean = jnp.concatenate([kmean, jnp.zeros((BF16_SUBLANES - NBK, HEAD_DIM), F32)], axis=0)
        km_hi = kmean.astype(BF16)
        km_scr[0] = km_hi
        km_scr[1] = (kmean - km_hi.astype(F32)).astype(BF16)

        def qblock(mi, carry, g=g):
            r0 = pl.multiple_of(mi * BS, BS)
            qblk = q_ref[pl.ds(r0, BS), GROUP_W * g:GROUP_W * (g + 1)]
            q4 = jnp.concatenate([qblk[:, HEAD_DIM * r:HEAD_DIM * (r + 1)] for r in range(R)], axis=0)
            q4_scr[...] = q4
            gate = _dot_nt(km_scr[0], q4) + _dot_nt(km_scr[1], q4)
            past = blk_row < mi
            gm = jnp.where(past, gate, NEG)
            cnt = jnp.zeros(gm.shape, F32)
            for n in range(NBK - 1):
                col = gm[n:n + 1, :]
                beats = (gm > col) | ((gm == col) & (blk_row < n))
                c = jnp.sum(jnp.where(beats, 1.0, 0.0), axis=0, keepdims=True)
                cnt = jnp.where(blk_row == n, c, cnt)
            sel_scr[...] = jnp.where(past & (cnt < MOBA_TOPK), 1.0, 0.0)

            def scores(n, case, allowed):
                k0 = pl.multiple_of(n * BS, BS)
                s = _dot_nt(kg_scr[pl.ds(k0, BS), :], q4_scr[...]) + bias_scr[g, case]
                return jnp.where(allowed, s, NEG)

            def values_t(n):
                return vt_ref[n, HEAD_DIM * g:HEAD_DIM * (g + 1), :]

            key_i = lax.broadcasted_iota(I32, (BS, ROWS), 0)
            qry_i = lax.broadcasted_iota(I32, (BS, ROWS), 1) & (BS - 1)
            s = scores(mi, 2, key_i <= qry_i)
            m0 = jnp.max(s, axis=0, keepdims=True)
            e = jnp.exp(s - m0)
            m_scr[...] = m0
            l_scr[...] = jnp.sum(e, axis=0, keepdims=True)
            acc_scr[...] = jnp.dot(values_t(mi), e.astype(BF16), preferred_element_type=F32)

            def kvblock(j, c2):
                n = mi - j
                s = scores(n, jnp.where(j == 1, 1, 0), sel_scr[pl.ds(n, 1), :] > 0.5)
                m_prev = m_scr[...]
                m_new = jnp.maximum(m_prev, jnp.max(s, axis=0, keepdims=True))
                alpha = jnp.exp(m_prev - m_new)
                e = jnp.exp(s - m_new)
                l_scr[...] = alpha * l_scr[...] + jnp.sum(e, axis=0, keepdims=True)
                acc_scr[...] = alpha * acc_scr[...] + jnp.dot(
                    values_t(n), e.astype(BF16), preferred_element_type=F32)
                m_scr[...] = m_new
                return c2

            lax.fori_loop(1, mi + 1, kvblock, 0)
            ot = acc_scr[...] / l_scr[...]
            for pr in range(R // 2):
                pair = ot[:, 2 * BS * pr:2 * BS * (pr + 1)]
                stacked = jnp.concatenate([pair[:, :BS], pair[:, BS:]], axis=0)
                c0 = GROUP_W * g + LANES * pr
                o_ref[pl.ds(r0, BS), c0:c0 + LANES] = stacked.T
            return carry

        lax.fori_loop(0, NBK, qblock, 0)


def _moba_call(tbl_flat, qb, kb, vbt, B, S):
    BS = MOBA_BLOCK
    R = GROUP_HEADS
    NBK = S // BS
    bkt = jnp.asarray(np.ascontiguousarray(_band_buckets(BS).T))
    smem = pl.BlockSpec(memory_space=pltpu.SMEM)
    return pl.pallas_call(
        _moba_kernel,
        grid=(B,),
        in_specs=[smem,
                  pl.BlockSpec(bkt.shape, lambda b: (0, 0)),
                  pl.BlockSpec((S, qb.shape[1]), lambda b: (b, 0)),
                  pl.BlockSpec((S, LANES), lambda b: (b, 0)),
                  pl.BlockSpec((NBK, LANES, BS), lambda b: (b, 0, 0))],
        out_specs=pl.BlockSpec((S, qb.shape[1]), lambda b: (b, 0)),
        out_shape=jax.ShapeDtypeStruct(qb.shape, F32),
        scratch_shapes=[pltpu.VMEM((MOBA_KV_HEADS, 3, BS, R * BS), F32),
                        pltpu.VMEM((S, HEAD_DIM), BF16),
                        pltpu.VMEM((2, BF16_SUBLANES, HEAD_DIM), BF16),
                        pltpu.VMEM((R * BS, HEAD_DIM), BF16),
                        pltpu.VMEM((BF16_SUBLANES, R * BS), F32),
                        pltpu.VMEM((1, R * BS), F32),
                        pltpu.VMEM((1, R * BS), F32),
                        pltpu.VMEM((HEAD_DIM, R * BS), F32)],
        compiler_params=pltpu.CompilerParams(
            dimension_semantics=("arbitrary",), vmem_limit_bytes=VMEM_LIMIT),
        name="moba",
    )(tbl_flat, bkt, qb, kb, vbt)


def _outproj_kernel(oa_ref, ob_ref, x_ref, ga_ref, gb_ref, woa_ref, wob_ref, fg_ref,
                    wrh_ref, wrl_ref, rb_ref, x1_ref, hf_ref, eid_ref, wt_ref):
    na = _rmsnorm(oa_ref[...], ga_ref[...]).astype(BF16)
    nb = _rmsnorm(ob_ref[...], gb_ref[...]).astype(BF16)
    y = (jnp.dot(na, woa_ref[...], preferred_element_type=F32)
         + jnp.dot(nb, wob_ref[...], preferred_element_type=F32))
    x1 = x_ref[...] + y
    x1_ref[...] = x1
    hf = _rmsnorm(x1, fg_ref[...])
    hf_ref[...] = hf

    hf_hi = hf.astype(BF16)
    hf_lo = (hf - hf_hi.astype(F32)).astype(BF16)
    lt = (_dot_nt(wrh_ref[...], hf_hi) + _dot_nt(wrh_ref[...], hf_lo)
          + _dot_nt(wrl_ref[...], hf_hi) + rb_ref[...])
    gl = [lt[j:j + 1, :] for j in range(N_GROUPS)]
    best = gl[0]
    gsel = jnp.zeros(best.shape, I32)
    for j in range(1, N_GROUPS):
        better = gl[j] > best
        gsel = jnp.where(better, j, gsel)
        best = jnp.where(better, gl[j], best)
    gsum = jnp.zeros(best.shape, F32)
    for j in range(N_GROUPS):
        gsum = gsum + jnp.exp(gl[j] - best)
    gw = 1.0 / gsum
    E = EXPERTS_PER_GROUP
    es = lt[EXPERT_ROW0:EXPERT_ROW0 + E, :]
    for j in range(1, N_GROUPS):
        es = jnp.where(gsel == j, lt[EXPERT_ROW0 + E * j:EXPERT_ROW0 + E * (j + 1), :], es)
    row = lax.broadcasted_iota(I32, es.shape, 0)
    v1 = jnp.max(es, axis=0, keepdims=True)
    i1 = jnp.min(jnp.where(es == v1, row, E), axis=0, keepdims=True)
    es2 = jnp.where(row == i1, -jnp.inf, es)
    v2 = jnp.max(es2, axis=0, keepdims=True)
    i2 = jnp.min(jnp.where(es2 == v2, row, E), axis=0, keepdims=True)
    e2 = jnp.exp(v2 - v1)
    den = 1.0 + e2
    eid_ref[...] = jnp.concatenate([gsel * E + i1, gsel * E + i2], axis=0)
    wt_ref[...] = jnp.concatenate([gw * (1.0 / den), gw * (e2 / den)], axis=0)


def _outproj_call(oa, ob, x2d, ga, gb, woa, wob, fg, wrh, wrl, rb):
    T, D = x2d.shape
    tm = TOKEN_TILE
    tok = lambda w: pl.BlockSpec((tm, w), lambda i: (i, 0))
    full = lambda a: pl.BlockSpec(a.shape, lambda i: (0, 0))
    col = pl.BlockSpec((EXPERT_TOPK, tm), lambda i: (0, i))
    return pl.pallas_call(
        _outproj_kernel,
        grid=(T // tm,),
        in_specs=[tok(oa.shape[1]), tok(ob.shape[1]), tok(D), full(ga), full(gb), full(woa), full(wob),
                  full(fg), full(wrh), full(wrl), full(rb)],
        out_specs=[tok(D), tok(D), col, col],
        out_shape=[jax.ShapeDtypeStruct((T, D), F32), jax.ShapeDtypeStruct((T, D), F32),
                   jax.ShapeDtypeStruct((EXPERT_TOPK, T), I32), jax.ShapeDtypeStruct((EXPERT_TOPK, T), F32)],
        compiler_params=pltpu.CompilerParams(
            dimension_semantics=("arbitrary",), vmem_limit_bytes=VMEM_LIMIT),
        name="outproj",
    )(oa, ob, x2d, ga, gb, woa, wob, fg, wrh, wrl, rb)


def _moe_kernel(be_ref, nu_ref, src_ref, srcn_ref, dst_ref, hf_hbm, wg_ref, wu_ref, wd_ref, out_hbm,
                xbuf, obuf, wgb, wub, wdb, gsem, ssem):
    i = pl.program_id(0)
    n_used = nu_ref[0]
    M = xbuf.shape[1]
    slot = i % 2

    def start_gather(idx_ref, buf_slot):
        for r in range(M):
            pltpu.make_async_copy(hf_hbm.at[pl.ds(idx_ref[0, 0, r], 1)],
                                  xbuf.at[buf_slot, pl.ds(r, 1)], gsem.at[buf_slot]).start()

    def wait_rows(buf, buf_slot, sem):
        pltpu.make_async_copy(hf_hbm.at[pl.ds(0, M)], buf.at[buf_slot], sem.at[buf_slot]).wait()

    @pl.when(i < n_used)
    def _compute():
        @pl.when(i == 0)
        def _first_gather():
            start_gather(src_ref, 0)
            obuf[1] = jnp.zeros(obuf.shape[1:], F32)
            spare = pltpu.make_async_copy(obuf.at[1], out_hbm.at[pl.ds(out_hbm.shape[0] - M, M)], ssem.at[1])
            spare.start()
            spare.wait()

        @pl.when((i == 0) | (be_ref[i] != be_ref[jnp.maximum(i - 1, 0)]))
        def _cast_weights():
            wgb[...] = wg_ref[0].astype(BF16)
            wub[...] = wu_ref[0].astype(BF16)
            wdb[...] = wd_ref[0].astype(BF16)

        wait_rows(xbuf, slot, gsem)

        @pl.when(i >= 2)
        def _reuse_obuf():
            wait_rows(obuf, slot, ssem)

        start_gather(srcn_ref, 1 - slot)
        xb = xbuf[slot].astype(BF16)
        a = jnp.dot(xb, wgb[...], preferred_element_type=F32)
        u = jnp.dot(xb, wub[...], preferred_element_type=F32)
        hmid = (a * jax.nn.sigmoid(a)) * u
        obuf[slot] = jnp.dot(hmid.astype(BF16), wdb[...], preferred_element_type=F32)
        for r in range(M):
            pltpu.make_async_copy(obuf.at[slot, pl.ds(r, 1)],
                                  out_hbm.at[pl.ds(dst_ref[0, 0, r], 1)], ssem.at[slot]).start()

        @pl.when(i == n_used - 1)
        def _drain():
            wait_rows(xbuf, 1 - slot, gsem)
            wait_rows(obuf, slot, ssem)

            @pl.when(i >= 1)
            def _drain_prev():
                wait_rows(obuf, 1 - slot, ssem)


def _moe_call(block_expert, n_used, src_tok, dst_row, hf, w_g, w_u, w_d):
    T, D = hf.shape
    M = DISPATCH_BLOCK
    nblk = block_expert.shape[0]
    DE = w_g.shape[2]
    idx_spec = lambda f: pl.BlockSpec((1, 1, M), f, memory_space=pltpu.SMEM)
    cur = lambda i, be, nu: (i, 0, 0)
    nxt = lambda i, be, nu: (jnp.minimum(i + 1, nblk - 1), 0, 0)
    wspec = lambda a, b: pl.BlockSpec((1, a, b), lambda i, be, nu: (be[i], 0, 0))
    grid_spec = pltpu.PrefetchScalarGridSpec(
        num_scalar_prefetch=2,
        grid=(nblk,),
        in_specs=[idx_spec(cur), idx_spec(nxt), idx_spec(cur),
                  pl.BlockSpec(memory_space=pl.ANY),
                  wspec(D, DE), wspec(D, DE), wspec(DE, D)],
        out_specs=pl.BlockSpec(memory_space=pl.ANY),
        scratch_shapes=[pltpu.VMEM((2, M, D), F32), pltpu.VMEM((2, M, D), F32),
                        pltpu.VMEM((D, DE), BF16), pltpu.VMEM((D, DE), BF16), pltpu.VMEM((DE, D), BF16),
                        pltpu.SemaphoreType.DMA((2,)), pltpu.SemaphoreType.DMA((2,))],
    )
    src3 = src_tok.reshape(nblk, 1, M)
    return pl.pallas_call(
        _moe_kernel,
        grid_spec=grid_spec,
        out_shape=jax.ShapeDtypeStruct((EXPERT_TOPK * T + M, D), F32),
        compiler_params=pltpu.CompilerParams(
            dimension_semantics=("arbitrary",), vmem_limit_bytes=VMEM_LIMIT),
        name="moe",
    )(block_expert, n_used, src3, src3, dst_row.reshape(nblk, 1, M), hf, w_g, w_u, w_d)


def _ple_kernel(x1_ref, m0_ref, m1_ref, wt_ref, p_ref, pg_ref, wgate_ref, wproj_ref, pog_ref, out_ref):
    proj = _rmsnorm(jnp.dot(p_ref[...].astype(BF16), wproj_ref[...], preferred_element_type=F32),
                    pog_ref[...])
    moe = m0_ref[...] * wt_ref[:, 0:1] + m1_ref[...] * wt_ref[:, 1:2]
    x2 = x1_ref[...] + moe
    hn = _rmsnorm(x2, pg_ref[...]).astype(BF16)
    gate = jax.nn.sigmoid(jnp.dot(hn, wgate_ref[...], preferred_element_type=F32))
    out_ref[...] = x2 + gate * proj


def _ple_call(moe_rows, x1, wt_cols, p2d, pg, wgate, wproj, pog):
    T, D = x1.shape
    tm = PLE_TILE
    tok = lambda w: pl.BlockSpec((tm, w), lambda i: (i, 0))
    full = lambda a: pl.BlockSpec(a.shape, lambda i: (0, 0))
    second = pl.BlockSpec((tm, D), lambda i: (T // tm + i, 0))
    return pl.pallas_call(
        _ple_kernel,
        grid=(T // tm,),
        in_specs=[tok(D), tok(D), second, tok(EXPERT_TOPK), tok(p2d.shape[1]),
                  full(pg), full(wgate), full(wproj), full(pog)],
        out_specs=tok(D),
        out_shape=jax.ShapeDtypeStruct((T, D), F32),
        compiler_params=pltpu.CompilerParams(
            dimension_semantics=("arbitrary",), vmem_limit_bytes=VMEM_LIMIT),
        name="ple",
    )(x1, moe_rows, moe_rows, wt_cols, p2d, pg, wgate, wproj, pog)


def _dispatch_plan(eid, T):
    A = EXPERT_TOPK * T
    M = DISPATCH_BLOCK
    nblk = A // M + N_EXPERTS
    e_flat = eid.reshape(A)
    experts = jnp.arange(N_EXPERTS, dtype=I32)
    counts = jnp.sum((e_flat[:, None] == experts[None, :]).astype(I32), axis=0)
    start = jnp.cumsum(counts) - counts
    padded = (counts + M - 1) // M * M
    pad_end = jnp.cumsum(padded)
    pad_start = pad_end - padded
    order = jnp.argsort(e_flat).astype(I32)
    n_used = pad_end[-1] // M
    blk = jnp.arange(nblk, dtype=I32)
    be = jnp.sum((pad_end[None, :] <= (blk * M)[:, None]).astype(I32), axis=1)
    be = jnp.minimum(be, N_EXPERTS - 1)
    be = jnp.where(blk < n_used, be, be[jnp.maximum(n_used - 1, 0)])
    slot = jnp.arange(nblk * M, dtype=I32)
    e_s = jnp.repeat(be, M)
    off = slot - pad_start[e_s]
    valid = (slot < n_used * M) & (off < counts[e_s])
    a_s = order[jnp.clip(start[e_s] + off, 0, A - 1)]
    src_tok = jnp.where(valid, jnp.where(a_s >= T, a_s - T, a_s), 0)
    dst_row = jnp.where(valid, a_s, A + slot % M)
    return be, n_used.reshape(1).astype(I32), src_tok, dst_row


def _layer(x2d, p2d, rel_bias, attn_norm, w_in, swa_q_norm, swa_k_norm, swa_sinks, moba_q_norm,
           moba_k_norm, swa_out_norm, moba_out_norm, w_out, ffn_norm, w_rg, b_rg, w_re, b_re,
           w_g, w_u, w_d, ple_norm, w_ple_gate, w_ple_proj, ple_out_norm, B, S):
    T, D = x2d.shape
    row = lambda v: v.reshape(1, -1).astype(F32)
    ones = jnp.ones((SWA_KV_HEADS * HEAD_DIM,), F32)
    head_gain = jnp.concatenate([
        jnp.tile(swa_q_norm, SWA_Q_HEADS), jnp.tile(swa_k_norm, SWA_KV_HEADS), ones,
        jnp.tile(moba_q_norm, MOBA_Q_HEADS), jnp.tile(moba_k_norm, MOBA_KV_HEADS), ones])
    w_in_b = w_in.astype(BF16)
    w_vbt = w_in_b[:, w_in.shape[1] - MOBA_KV_HEADS * HEAD_DIM:].T
    qa, ka, va, qb, kb, vbt = _qkv_call(x2d, row(attn_norm), w_in_b, w_vbt, row(head_gain))

    tbl_flat = rel_bias.astype(F32).reshape(-1)
    oa = _swa_call(tbl_flat, swa_sinks.astype(F32), qa, ka, va, B, S)
    ob = _moba_call(tbl_flat, qb, kb, vbt, B, S)

    pad_rows = lambda n: jnp.zeros((n, D), F32)
    wr = jnp.concatenate([w_rg.T, pad_rows(EXPERT_ROW0 - N_GROUPS), w_re.T,
                          pad_rows(ROUTER_ROWS - EXPERT_ROW0 - N_EXPERTS)], axis=0)
    wr_hi = wr.astype(BF16)
    wr_lo = (wr - wr_hi.astype(F32)).astype(BF16)
    rb = jnp.concatenate([b_rg, jnp.zeros((EXPERT_ROW0 - N_GROUPS,), F32), b_re,
                          jnp.zeros((ROUTER_ROWS - EXPERT_ROW0 - N_EXPERTS,), F32)]).reshape(ROUTER_ROWS, 1)
    wo = w_out.astype(BF16)
    na_w = SWA_Q_HEADS * HEAD_DIM
    x1, hf, eid, wts = _outproj_call(oa, ob, x2d, row(swa_out_norm), row(moba_out_norm),
                                     wo[:na_w], wo[na_w:], row(ffn_norm), wr_hi, wr_lo, rb)

    be, n_used, src_tok, dst_row = _dispatch_plan(eid, T)
    moe_rows = _moe_call(be, n_used, src_tok, dst_row, hf, w_g, w_u, w_d)
    return _ple_call(moe_rows, x1, wts.T, p2d, row(ple_norm), w_ple_gate.astype(BF16),
                     w_ple_proj.astype(BF16), row(ple_out_norm))


def kernel(x, p, rel_bias, attn_norm, w_in, swa_q_norm, swa_k_norm, swa_sinks, moba_q_norm, moba_k_norm,
           swa_out_norm, moba_out_norm, w_out, ffn_norm, w_router_group, b_router_group, w_router_expert,
           b_router_expert, w_exp_gate, w_exp_up, w_exp_down, ple_norm, w_ple_gate, w_ple_proj, ple_out_norm):
    B, S, D = x.shape
    x2d = x.reshape(B * S, D)
    for i in range(p.shape[0]):
        x2d = _layer(x2d, p[i].reshape(B * S, -1), rel_bias, attn_norm[i], w_in[i], swa_q_norm[i],
                     swa_k_norm[i], swa_sinks[i], moba_q_norm[i], moba_k_norm[i], swa_out_norm[i],
                     moba_out_norm[i], w_out[i], ffn_norm[i], w_router_group[i], b_router_group[i],
                     w_router_expert[i], b_router_expert[i], w_exp_gate[i], w_exp_up[i], w_exp_down[i],
                     ple_norm[i], w_ple_gate[i], w_ple_proj[i], ple_out_norm[i], B, S)
    return x2d.reshape(B, S, D)
```

```python
import math

import numpy as np
import jax
import jax.numpy as jnp
from jax import lax
from jax.experimental import pallas as pl
from jax.experimental.pallas import tpu as pltpu

F32 = jnp.float32
BF16 = jnp.bfloat16
I32 = jnp.int32

HEAD_DIM = 64
SWA_Q_HEADS = 8
SWA_KV_HEADS = 2
SWA_WINDOW = 128
MOBA_Q_HEADS = 8
MOBA_KV_HEADS = 2
MOBA_BLOCK = 256
MOBA_TOPK = 3
N_HEADS = SWA_Q_HEADS + MOBA_Q_HEADS
REL_BUCKETS = 32
REL_MAX_DIST = 128
N_GROUPS = 4
EXPERTS_PER_GROUP = 8
N_EXPERTS = N_GROUPS * EXPERTS_PER_GROUP
EXPERT_TOPK = 2
DISPATCH_BLOCK = 256
EPS = 1e-6
NEG = -1e30

DMA_PRIORITIES = 2
LANES = 128
BF16_SUBLANES = 16
GROUP_HEADS = 4
GROUP_W = GROUP_HEADS * HEAD_DIM
ROUTER_ROWS = 128
EXPERT_ROW0 = 8
TOKEN_TILE = 512
PLE_TILE = 256
VMEM_LIMIT = 52 * 1024 * 1024


def _dot_nt(a, b):
    return lax.dot_general(a, b, (((1,), (1,)), ((), ())), preferred_element_type=F32)


def _rmsnorm(x, g):
    ms = jnp.mean(x * x, axis=-1, keepdims=True)
    return x * lax.rsqrt(ms + EPS) * g


def _rel_bucket_np(dist):
    n = np.maximum(dist, 0)
    exact = REL_BUCKETS // 2
    nf = np.maximum(n, 1).astype(np.float32)
    large = exact + (np.log(nf / exact) / math.log(REL_MAX_DIST / exact)
                     * (REL_BUCKETS - exact)).astype(np.int32)
    return np.where(n < exact, n, np.minimum(large, REL_BUCKETS - 1)).astype(np.int32)


def _band_buckets(block):
    qi = np.arange(block)[:, None]
    kj = np.arange(2 * block)[None, :]
    return _rel_bucket_np(qi + block - kj)


def _bias_from_buckets(bkt, tbl_ref, head):
    acc = jnp.zeros(bkt.shape, F32)
    for j in range(REL_BUCKETS):
        acc = jnp.where(bkt == j, tbl_ref[j * N_HEADS + head], acc)
    return acc


def _qkv_kernel(x_ref, g_ref, w_ref, wvt_ref, hg_ref, qa_ref, ka_ref, qb_ref, kb_ref, vat_ref, vbt_ref):
    h = _rmsnorm(x_ref[...], g_ref[...]).astype(BF16)
    acc = jnp.dot(h, w_ref[...], preferred_element_type=F32)
    lo = lax.broadcasted_iota(I32, (1, LANES), 1) < HEAD_DIM

    def head_normed(c, scale):
        blk = acc[:, LANES * c:LANES * (c + 1)]
        sq = blk * blk
        s_lo = jnp.sum(jnp.where(lo, sq, 0.0), axis=-1, keepdims=True)
        s_hi = jnp.sum(jnp.where(lo, 0.0, sq), axis=-1, keepdims=True)
        inv = jnp.where(lo, lax.rsqrt(s_lo / HEAD_DIM + EPS), lax.rsqrt(s_hi / HEAD_DIM + EPS))
        return blk * inv * hg_ref[:, LANES * c:LANES * (c + 1)] * scale

    scale = HEAD_DIM ** -0.5
    for c in range(4):
        qa_ref[:, LANES * c:LANES * (c + 1)] = head_normed(c, scale).astype(BF16)
        qb_ref[:, LANES * c:LANES * (c + 1)] = head_normed(5 + c, scale).astype(BF16)
    ka_ref[...] = head_normed(4, 1.0).astype(BF16)
    kb_ref[...] = head_normed(9, 1.0).astype(BF16)
    vt = _dot_nt(wvt_ref[...], h).astype(BF16)
    for j in range(vat_ref.shape[0]):
        vat_ref[j] = vt[:LANES, SWA_WINDOW * j:SWA_WINDOW * (j + 1)]
    for j in range(vbt_ref.shape[0]):
        vbt_ref[j] = vt[LANES:, MOBA_BLOCK * j:MOBA_BLOCK * (j + 1)]


def _qkv_call(x2d, attn_g, w_qk, w_vt, head_gain):
    T, D = x2d.shape
    tm = TOKEN_TILE
    tok = lambda w: pl.BlockSpec((tm, w), lambda i: (i, 0))
    full = lambda a: pl.BlockSpec(a.shape, lambda i: (0, 0))
    out_w = (SWA_Q_HEADS * HEAD_DIM, LANES, MOBA_Q_HEADS * HEAD_DIM, LANES)
    slabs = lambda blk: pl.BlockSpec((tm // blk, LANES, blk), lambda i: (i, 0, 0))
    slab_shape = lambda blk: jax.ShapeDtypeStruct((T // blk, LANES, blk), BF16)
    return pl.pallas_call(
        _qkv_kernel,
        grid=(T // tm,),
        in_specs=[tok(D), full(attn_g), full(w_qk), full(w_vt), full(head_gain)],
        out_specs=[tok(w) for w in out_w] + [slabs(SWA_WINDOW), slabs(MOBA_BLOCK)],
        out_shape=[jax.ShapeDtypeStruct((T, w), BF16) for w in out_w]
        + [slab_shape(SWA_WINDOW), slab_shape(MOBA_BLOCK)],
        compiler_params=pltpu.CompilerParams(
            dimension_semantics=("arbitrary",), vmem_limit_bytes=VMEM_LIMIT),
        name="qkv",
    )(x2d, attn_g, w_qk, w_vt, head_gain)


def _swa_kernel(tbl_ref, sink_ref, bkt_ref, q_ref, k_ref, vt_ref, o_ref, bias_scr, kg_scr):
    W = SWA_WINDOW
    R = GROUP_HEADS
    G = SWA_KV_HEADS
    S = q_ref.shape[0]

    @pl.when(pl.program_id(0) == 0)
    def _init_bias():
        bkt = bkt_ref[...]
        for h in range(SWA_Q_HEADS):
            g, r = divmod(h, R)
            bias_scr[g, :, W * r:W * (r + 1)] = _bias_from_buckets(bkt, tbl_ref, h)

    key_i = lax.broadcasted_iota(I32, (2 * W, R * W), 0)
    qry_i = lax.broadcasted_iota(I32, (2 * W, R * W), 1) & (W - 1)
    dist = qry_i + W - key_i
    band = (dist >= 0) & (dist < W)
    own_part = key_i >= W
    head_of_lane = lax.broadcasted_iota(I32, (1, R * W), 1) // W
    sinks = []
    for g in range(G):
        kg_scr[g] = k_ref[:, HEAD_DIM * g:HEAD_DIM * (g + 1)]
        row = jnp.zeros((1, R * W), F32)
        for r in range(R):
            row = jnp.where(head_of_lane == r, sink_ref[R * g + r], row)
        sinks.append(row)

    def body(i, carry):
        r0 = pl.multiple_of(i * W, W)
        prev = jnp.maximum(i - 1, 0)
        p0 = pl.multiple_of(prev * W, W)
        scores = []
        for g in range(G):
            qblk = q_ref[pl.ds(r0, W), GROUP_W * g:GROUP_W * (g + 1)]
            q4 = jnp.concatenate([qblk[:, HEAD_DIM * r:HEAD_DIM * (r + 1)] for r in range(R)], axis=0)
            kband = jnp.concatenate([kg_scr[g, pl.ds(p0, W), :], kg_scr[g, pl.ds(r0, W), :]], axis=0)
            scores.append(_dot_nt(kband, q4))
        mask = band & (own_part | (i > 0))
        for g in range(G):
            s = jnp.where(mask, scores[g] + bias_scr[g], NEG)
            m = jnp.maximum(jnp.max(s, axis=0, keepdims=True), sinks[g])
            e = jnp.exp(s - m)
            den = jnp.sum(e, axis=0, keepdims=True) + jnp.exp(sinks[g] - m)
            feat = slice(HEAD_DIM * g, HEAD_DIM * (g + 1))
            vtband = jnp.concatenate([vt_ref[prev, feat, :], vt_ref[i, feat, :]], axis=1)
            ot = jnp.dot(vtband, e.astype(BF16), preferred_element_type=F32) / den
            for pr in range(R // 2):
                pair = ot[:, 2 * W * pr:2 * W * (pr + 1)]
                c0 = GROUP_W * g + LANES * pr
                o_ref[pl.ds(r0, W), c0:c0 + LANES] = jnp.concatenate([pair[:, :W], pair[:, W:]], axis=0).T
        return carry

    lax.fori_loop(0, S // W, body, 0)


def _swa_call(tbl_flat, sinks, qa, ka, vat, B, S):
    W = SWA_WINDOW
    bkt = jnp.asarray(np.ascontiguousarray(_band_buckets(W).T))
    smem = pl.BlockSpec(memory_space=pltpu.SMEM)
    return pl.pallas_call(
        _swa_kernel,
        grid=(B,),
        in_specs=[smem, smem,
                  pl.BlockSpec(bkt.shape, lambda b: (0, 0)),
                  pl.BlockSpec((S, qa.shape[1]), lambda b: (b, 0)),
                  pl.BlockSpec((S, LANES), lambda b: (b, 0)),
                  pl.BlockSpec((S // W, LANES, W), lambda b: (b, 0, 0))],
        out_specs=pl.BlockSpec((S, qa.shape[1]), lambda b: (b, 0)),
        out_shape=jax.ShapeDtypeStruct(qa.shape, F32),
        scratch_shapes=[pltpu.VMEM((SWA_KV_HEADS, 2 * W, GROUP_HEADS * W), F32),
                        pltpu.VMEM((SWA_KV_HEADS, S, HEAD_DIM), BF16)],
        compiler_params=pltpu.CompilerParams(
            dimension_semantics=("arbitrary",), vmem_limit_bytes=VMEM_LIMIT),
        name="swa",
    )(tbl_flat, sinks, bkt, qa, ka, vat)


def _moba_kernel(tbl_ref, bkt_ref, q_ref, k_ref, vt_ref, o_ref,
                 bias_scr, kg_scr, km_scr, q4_scr, sel_scr, m_scr, l_scr, acc_scr):
    BS = MOBA_BLOCK
    R = GROUP_HEADS
    G = MOBA_KV_HEADS
    S = q_ref.shape[0]
    NBK = S // BS
    ROWS = R * BS

    @pl.when(pl.program_id(0) == 0)
    def _init_bias():
        bkt = bkt_ref[...]
        for h in range(MOBA_Q_HEADS):
            g, r = divmod(h, R)
            band = _bias_from_buckets(bkt, tbl_ref, SWA_Q_HEADS + h)
            far = tbl_ref[(REL_BUCKETS - 1) * N_HEADS + SWA_Q_HEADS + h]
            bias_scr[g, 0, :, BS * r:BS * (r + 1)] = jnp.full((BS, BS), far, F32)
            bias_scr[g, 1, :, BS * r:BS * (r + 1)] = band[:BS]
            bias_scr[g, 2, :, BS * r:BS * (r + 1)] = band[BS:]

    blk_row = lax.broadcasted_iota(I32, (BF16_SUBLANES, 1), 0)
    for g in range(G):
        kg_scr[g] = k_ref[:, HEAD_DIM * g:HEAD_DIM * (g + 1)]
        kmean = jnp.sum(kg_scr[g].astype(F32).reshape(NBK, BS, HEAD_DIM), axis=1) / BS
        kmean = jnp.concatenate([kmean, jnp.zeros((BF16_SUBLANES - NBK, HEAD_DIM), F32)], axis=0)
        km_hi = kmean.astype(BF16)
        km_scr[g, 0] = km_hi
        km_scr[g, 1] = (kmean - km_hi.astype(F32)).astype(BF16)

    def scores(n):
        k0 = pl.multiple_of(n * BS, BS)
        return [_dot_nt(kg_scr[g, pl.ds(k0, BS), :], q4_scr[g]) for g in range(G)]

    def values_t(n, g):
        return vt_ref[n, HEAD_DIM * g:HEAD_DIM * (g + 1), :]

    def qblock(mi, carry):
        r0 = pl.multiple_of(mi * BS, BS)
        past = blk_row < mi
        for g in range(G):
            qblk = q_ref[pl.ds(r0, BS), GROUP_W * g:GROUP_W * (g + 1)]
            q4 = jnp.concatenate([qblk[:, HEAD_DIM * r:HEAD_DIM * (r + 1)] for r in range(R)], axis=0)
            q4_scr[g] = q4
            gate = _dot_nt(km_scr[g, 0], q4) + _dot_nt(km_scr[g, 1], q4)
            gm = jnp.where(past, gate, NEG)
            cnt = jnp.zeros(gm.shape, F32)
            for n in range(NBK - 1):
                col = gm[n:n + 1, :]
                beats = (gm > col) | ((gm == col) & (blk_row < n))
                c = jnp.sum(jnp.where(beats, 1.0, 0.0), axis=0, keepdims=True)
                cnt = jnp.where(blk_row == n, c, cnt)
            sel_scr[g] = jnp.where(past & (cnt < MOBA_TOPK), 1.0, 0.0)

        key_i = lax.broadcasted_iota(I32, (BS, ROWS), 0)
        qry_i = lax.broadcasted_iota(I32, (BS, ROWS), 1) & (BS - 1)
        causal = key_i <= qry_i
        sc = scores(mi)
        for g in range(G):
            s = jnp.where(causal, sc[g] + bias_scr[g, 2], NEG)
            m0 = jnp.max(s, axis=0, keepdims=True)
            e = jnp.exp(s - m0)
            m_scr[g] = m0
            l_scr[g] = jnp.sum(e, axis=0, keepdims=True)
            acc_scr[g] = jnp.dot(values_t(mi, g), e.astype(BF16), preferred_element_type=F32)

        def kvblock(j, c2):
            n = mi - j
            case = jnp.where(j == 1, 1, 0)
            sc = scores(n)
            for g in range(G):
                s = jnp.where(sel_scr[g, pl.ds(n, 1), :] > 0.5, sc[g] + bias_scr[g, case], NEG)
                m_prev = m_scr[g]
                m_new = jnp.maximum(m_prev, jnp.max(s, axis=0, keepdims=True))
                alpha = jnp.exp(m_prev - m_new)
                e = jnp.exp(s - m_new)
                l_scr[g] = alpha * l_scr[g] + jnp.sum(e, axis=0, keepdims=True)
                acc_scr[g] = alpha * acc_scr[g] + jnp.dot(
                    values_t(n, g), e.astype(BF16), preferred_element_type=F32)
                m_scr[g] = m_new
            return c2

        lax.fori_loop(1, mi + 1, kvblock, 0)
        for g in range(G):
            ot = acc_scr[g] / l_scr[g]
            for pr in range(R // 2):
                pair = ot[:, 2 * BS * pr:2 * BS * (pr + 1)]
                c0 = GROUP_W * g + LANES * pr
                o_ref[pl.ds(r0, BS), c0:c0 + LANES] = jnp.concatenate([pair[:, :BS], pair[:, BS:]], axis=0).T
        return carry

    lax.fori_loop(0, NBK, qblock, 0)


def _moba_call(tbl_flat, qb, kb, vbt, B, S):
    BS = MOBA_BLOCK
    R = GROUP_HEADS
    G = MOBA_KV_HEADS
    NBK = S // BS
    bkt = jnp.asarray(np.ascontiguousarray(_band_buckets(BS).T))
    smem = pl.BlockSpec(memory_space=pltpu.SMEM)
    return pl.pallas_call(
        _moba_kernel,
        grid=(B,),
        in_specs=[smem,
                  pl.BlockSpec(bkt.shape, lambda b: (0, 0)),
                  pl.BlockSpec((S, qb.shape[1]), lambda b: (b, 0)),
                  pl.BlockSpec((S, LANES), lambda b: (b, 0)),
                  pl.BlockSpec((NBK, LANES, BS), lambda b: (b, 0, 0))],
        out_specs=pl.BlockSpec((S, qb.shape[1]), lambda b: (b, 0)),
        out_shape=jax.ShapeDtypeStruct(qb.shape, F32),
        scratch_shapes=[pltpu.VMEM((G, 3, BS, R * BS), F32),
                        pltpu.VMEM((G, S, HEAD_DIM), BF16),
                        pltpu.VMEM((G, 2, BF16_SUBLANES, HEAD_DIM), BF16),
                        pltpu.VMEM((G, R * BS, HEAD_DIM), BF16),
                        pltpu.VMEM((G, BF16_SUBLANES, R * BS), F32),
                        pltpu.VMEM((G, 1, R * BS), F32),
                        pltpu.VMEM((G, 1, R * BS), F32),
                        pltpu.VMEM((G, HEAD_DIM, R * BS), F32)],
        compiler_params=pltpu.CompilerParams(
            dimension_semantics=("arbitrary",), vmem_limit_bytes=VMEM_LIMIT),
        name="moba",
    )(tbl_flat, bkt, qb, kb, vbt)


def _outproj_kernel(oa_ref, ob_ref, x_ref, ga_ref, gb_ref, woa_ref, wob_ref, fg_ref,
                    wrh_ref, wrl_ref, rb_ref, x1_ref, hf_ref, eid_ref, wt_ref):
    na = _rmsnorm(oa_ref[...], ga_ref[...]).astype(BF16)
    nb = _rmsnorm(ob_ref[...], gb_ref[...]).astype(BF16)
    y = (jnp.dot(na, woa_ref[...], preferred_element_type=F32)
         + jnp.dot(nb, wob_ref[...], preferred_element_type=F32))
    x1 = x_ref[...] + y
    x1_ref[...] = x1
    hf = _rmsnorm(x1, fg_ref[...])
    hf_ref[...] = hf

    hf_hi = hf.astype(BF16)
    hf_lo = (hf - hf_hi.astype(F32)).astype(BF16)
    lt = (_dot_nt(wrh_ref[...], hf_hi) + _dot_nt(wrh_ref[...], hf_lo)
          + _dot_nt(wrl_ref[...], hf_hi) + rb_ref[...])
    gl = [lt[j:j + 1, :] for j in range(N_GROUPS)]
    best = gl[0]
    gsel = jnp.zeros(best.shape, I32)
    for j in range(1, N_GROUPS):
        better = gl[j] > best
        gsel = jnp.where(better, j, gsel)
        best = jnp.where(better, gl[j], best)
    gsum = jnp.zeros(best.shape, F32)
    for j in range(N_GROUPS):
        gsum = gsum + jnp.exp(gl[j] - best)
    gw = 1.0 / gsum
    E = EXPERTS_PER_GROUP
    es = lt[EXPERT_ROW0:EXPERT_ROW0 + E, :]
    for j in range(1, N_GROUPS):
        es = jnp.where(gsel == j, lt[EXPERT_ROW0 + E * j:EXPERT_ROW0 + E * (j + 1), :], es)
    row = lax.broadcasted_iota(I32, es.shape, 0)
    v1 = jnp.max(es, axis=0, keepdims=True)
    i1 = jnp.min(jnp.where(es == v1, row, E), axis=0, keepdims=True)
    es2 = jnp.where(row == i1, -jnp.inf, es)
    v2 = jnp.max(es2, axis=0, keepdims=True)
    i2 = jnp.min(jnp.where(es2 == v2, row, E), axis=0, keepdims=True)
    e2 = jnp.exp(v2 - v1)
    den = 1.0 + e2
    eid_ref[...] = jnp.concatenate([gsel * E + i1, gsel * E + i2], axis=0)
    wt_ref[...] = jnp.concatenate([gw * (1.0 / den), gw * (e2 / den)], axis=0)


def _outproj_call(oa, ob, x2d, ga, gb, woa, wob, fg, wrh, wrl, rb):
    T, D = x2d.shape
    tm = TOKEN_TILE
    tok = lambda w: pl.BlockSpec((tm, w), lambda i: (i, 0))
    full = lambda a: pl.BlockSpec(a.shape, lambda i: (0, 0))
    col = pl.BlockSpec((EXPERT_TOPK, tm), lambda i: (0, i))
    return pl.pallas_call(
        _outproj_kernel,
        grid=(T // tm,),
        in_specs=[tok(oa.shape[1]), tok(ob.shape[1]), tok(D), full(ga), full(gb), full(woa), full(wob),
                  full(fg), full(wrh), full(wrl), full(rb)],
        out_specs=[tok(D), tok(D), col, col],
        out_shape=[jax.ShapeDtypeStruct((T, D), F32), jax.ShapeDtypeStruct((T, D), F32),
                   jax.ShapeDtypeStruct((EXPERT_TOPK, T), I32), jax.ShapeDtypeStruct((EXPERT_TOPK, T), F32)],
        compiler_params=pltpu.CompilerParams(
            dimension_semantics=("arbitrary",), vmem_limit_bytes=VMEM_LIMIT),
        name="outproj",
    )(oa, ob, x2d, ga, gb, woa, wob, fg, wrh, wrl, rb)


def _moe_kernel(be_ref, nu_ref, src_ref, srcn_ref, dst_ref, hf_hbm, wg_ref, wu_ref, wd_ref, out_hbm,
                xbuf, obuf, wgb, wub, wdb, gsem, ssem):
    i = pl.program_id(0)
    n_used = nu_ref[0]
    M = xbuf.shape[1]
    slot = i % 2

    def start_gather(idx_ref, buf_slot):
        for r in range(M):
            pltpu.make_async_copy(hf_hbm.at[pl.ds(idx_ref[0, 0, r], 1)], xbuf.at[buf_slot, pl.ds(r, 1)],
                                  gsem.at[buf_slot]).start(priority=r % DMA_PRIORITIES)

    def wait_rows(buf, buf_slot, sem):
        pltpu.make_async_copy(hf_hbm.at[pl.ds(0, M)], buf.at[buf_slot], sem.at[buf_slot]).wait()

    @pl.when(i < n_used)
    def _compute():
        @pl.when(i == 0)
        def _first_gather():
            start_gather(src_ref, 0)
            obuf[1] = jnp.zeros(obuf.shape[1:], F32)
            spare = pltpu.make_async_copy(obuf.at[1], out_hbm.at[pl.ds(out_hbm.shape[0] - M, M)], ssem.at[1])
            spare.start()
            spare.wait()

        @pl.when((i == 0) | (be_ref[i] != be_ref[jnp.maximum(i - 1, 0)]))
        def _cast_weights():
            wgb[...] = wg_ref[0].astype(BF16)
            wub[...] = wu_ref[0].astype(BF16)
            wdb[...] = wd_ref[0].astype(BF16)

        wait_rows(xbuf, slot, gsem)

        @pl.when(i >= 2)
        def _reuse_obuf():
            wait_rows(obuf, slot, ssem)

        start_gather(srcn_ref, 1 - slot)
        xb = xbuf[slot].astype(BF16)
        a = jnp.dot(xb, wgb[...], preferred_element_type=F32)
        u = jnp.dot(xb, wub[...], preferred_element_type=F32)
        hmid = (a * jax.nn.sigmoid(a)) * u
        obuf[slot] = jnp.dot(hmid.astype(BF16), wdb[...], preferred_element_type=F32)
        for r in range(M):
            pltpu.make_async_copy(obuf.at[slot, pl.ds(r, 1)], out_hbm.at[pl.ds(dst_ref[0, 0, r], 1)],
                                  ssem.at[slot]).start(priority=r % DMA_PRIORITIES)

        @pl.when(i == n_used - 1)
        def _drain():
            wait_rows(xbuf, 1 - slot, gsem)
            wait_rows(obuf, slot, ssem)

            @pl.when(i >= 1)
            def _drain_prev():
                wait_rows(obuf, 1 - slot, ssem)


def _moe_call(block_expert, n_used, src_tok, dst_row, hf, w_g, w_u, w_d):
    T, D = hf.shape
    M = DISPATCH_BLOCK
    nblk = block_expert.shape[0]
    DE = w_g.shape[2]
    idx_spec = lambda f: pl.BlockSpec((1, 1, M), f, memory_space=pltpu.SMEM)
    cur = lambda i, be, nu: (i, 0, 0)
    nxt = lambda i, be, nu: (jnp.minimum(i + 1, nblk - 1), 0, 0)
    wspec = lambda a, b: pl.BlockSpec((1, a, b), lambda i, be, nu: (be[i], 0, 0))
    grid_spec = pltpu.PrefetchScalarGridSpec(
        num_scalar_prefetch=2,
        grid=(nblk,),
        in_specs=[idx_spec(cur), idx_spec(nxt), idx_spec(cur),
                  pl.BlockSpec(memory_space=pl.ANY),
                  wspec(D, DE), wspec(D, DE), wspec(DE, D)],
        out_specs=pl.BlockSpec(memory_space=pl.ANY),
        scratch_shapes=[pltpu.VMEM((2, M, D), F32), pltpu.VMEM((2, M, D), F32),
                        pltpu.VMEM((D, DE), BF16), pltpu.VMEM((D, DE), BF16), pltpu.VMEM((DE, D), BF16),
                        pltpu.SemaphoreType.DMA((2,)), pltpu.SemaphoreType.DMA((2,))],
    )
    src3 = src_tok.reshape(nblk, 1, M)
    return pl.pallas_call(
        _moe_kernel,
        grid_spec=grid_spec,
        out_shape=jax.ShapeDtypeStruct((EXPERT_TOPK * T + M, D), F32),
        compiler_params=pltpu.CompilerParams(
            dimension_semantics=("arbitrary",), vmem_limit_bytes=VMEM_LIMIT),
        name="moe",
    )(block_expert, n_used, src3, src3, dst_row.reshape(nblk, 1, M), hf, w_g, w_u, w_d)


def _ple_kernel(x1_ref, m0_ref, m1_ref, wt_ref, p_ref, pg_ref, wgate_ref, wproj_ref, pog_ref, out_ref):
    proj = _rmsnorm(jnp.dot(p_ref[...].astype(BF16), wproj_ref[...], preferred_element_type=F32),
                    pog_ref[...])
    moe = m0_ref[...] * wt_ref[:, 0:1] + m1_ref[...] * wt_ref[:, 1:2]
    x2 = x1_ref[...] + moe
    hn = _rmsnorm(x2, pg_ref[...]).astype(BF16)
    gate = jax.nn.sigmoid(jnp.dot(hn, wgate_ref[...], preferred_element_type=F32))
    out_ref[...] = x2 + gate * proj


def _ple_call(moe_rows, x1, wt_cols, p2d, pg, wgate, wproj, pog):
    T, D = x1.shape
    tm = PLE_TILE
    tok = lambda w: pl.BlockSpec((tm, w), lambda i: (i, 0))
    full = lambda a: pl.BlockSpec(a.shape, lambda i: (0, 0))
    second = pl.BlockSpec((tm, D), lambda i: (T // tm + i, 0))
    return pl.pallas_call(
        _ple_kernel,
        grid=(T // tm,),
        in_specs=[tok(D), tok(D), second, tok(EXPERT_TOPK), tok(p2d.shape[1]),
                  full(pg), full(wgate), full(wproj), full(pog)],
        out_specs=tok(D),
        out_shape=jax.ShapeDtypeStruct((T, D), F32),
        compiler_params=pltpu.CompilerParams(
            dimension_semantics=("arbitrary",), vmem_limit_bytes=VMEM_LIMIT),
        name="ple",
    )(x1, moe_rows, moe_rows, wt_cols, p2d, pg, wgate, wproj, pog)


def _dispatch_plan(eid, T):
    A = EXPERT_TOPK * T
    M = DISPATCH_BLOCK
    nblk = A // M + N_EXPERTS
    e_flat = eid.reshape(A)
    experts = jnp.arange(N_EXPERTS, dtype=I32)
    counts = jnp.sum((e_flat[:, None] == experts[None, :]).astype(I32), axis=0)
    start = jnp.cumsum(counts) - counts
    padded = (counts + M - 1) // M * M
    pad_end = jnp.cumsum(padded)
    pad_start = pad_end - padded
    order = jnp.argsort(e_flat).astype(I32)
    n_used = pad_end[-1] // M
    blk = jnp.arange(nblk, dtype=I32)
    be = jnp.sum((pad_end[None, :] <= (blk * M)[:, None]).astype(I32), axis=1)
    be = jnp.minimum(be, N_EXPERTS - 1)
    be = jnp.where(blk < n_used, be, be[jnp.maximum(n_used - 1, 0)])
    row = jnp.arange(M, dtype=I32)[None, :]
    off = (blk * M - pad_start[be])[:, None] + row
    valid = (blk < n_used)[:, None] & (off < counts[be][:, None])
    a_s = order[jnp.clip(start[be][:, None] + off, 0, A - 1)]
    src_tok = jnp.where(valid, jnp.where(a_s >= T, a_s - T, a_s), 0)
    dst_row = jnp.where(valid, a_s, A + row)
    return be, n_used.reshape(1).astype(I32), src_tok, dst_row


def _layer(x2d, p2d, rel_bias, attn_norm, w_in, swa_q_norm, swa_k_norm, swa_sinks, moba_q_norm,
           moba_k_norm, swa_out_norm, moba_out_norm, w_out, ffn_norm, w_rg, b_rg, w_re, b_re,
           w_g, w_u, w_d, ple_norm, w_ple_gate, w_ple_proj, ple_out_norm, B, S):
    T, D = x2d.shape
    row = lambda v: v.reshape(1, -1).astype(F32)
    head_gain = jnp.concatenate([
        jnp.tile(swa_q_norm, SWA_Q_HEADS), jnp.tile(swa_k_norm, SWA_KV_HEADS),
        jnp.tile(moba_q_norm, MOBA_Q_HEADS), jnp.tile(moba_k_norm, MOBA_KV_HEADS)])
    w_in_b = w_in.astype(BF16)
    qa_w, kv_w = SWA_Q_HEADS * HEAD_DIM, SWA_KV_HEADS * HEAD_DIM
    qb_w = MOBA_Q_HEADS * HEAD_DIM
    c_va = qa_w + kv_w
    c_qb = c_va + kv_w
    c_vb = c_qb + qb_w + kv_w
    w_qk = jnp.concatenate([w_in_b[:, :c_va], w_in_b[:, c_qb:c_vb]], axis=1)
    w_vt = jnp.concatenate([w_in_b[:, c_va:c_qb], w_in_b[:, c_vb:]], axis=1).T
    qa, ka, qb, kb, vat, vbt = _qkv_call(x2d, row(attn_norm), w_qk, w_vt, row(head_gain))

    tbl_flat = rel_bias.astype(F32).reshape(-1)
    oa = _swa_call(tbl_flat, swa_sinks.astype(F32), qa, ka, vat, B, S)
    ob = _moba_call(tbl_flat, qb, kb, vbt, B, S)

    pad_rows = lambda n: jnp.zeros((n, D), F32)
    wr = jnp.concatenate([w_rg.T, pad_rows(EXPERT_ROW0 - N_GROUPS), w_re.T,
                          pad_rows(ROUTER_ROWS - EXPERT_ROW0 - N_EXPERTS)], axis=0)
    wr_hi = wr.astype(BF16)
    wr_lo = (wr - wr_hi.astype(F32)).astype(BF16)
    rb = jnp.concatenate([b_rg, jnp.zeros((EXPERT_ROW0 - N_GROUPS,), F32), b_re,
                          jnp.zeros((ROUTER_ROWS - EXPERT_ROW0 - N_EXPERTS,), F32)]).reshape(ROUTER_ROWS, 1)
    wo = w_out.astype(BF16)
    na_w = SWA_Q_HEADS * HEAD_DIM
    x1, hf, eid, wts = _outproj_call(oa, ob, x2d, row(swa_out_norm), row(moba_out_norm),
                                     wo[:na_w], wo[na_w:], row(ffn_norm), wr_hi, wr_lo, rb)

    be, n_used, src_tok, dst_row = _dispatch_plan(eid, T)
    moe_rows = _moe_call(be, n_used, src_tok, dst_row, hf, w_g, w_u, w_d)
    return _ple_call(moe_rows, x1, wts.T, p2d, row(ple_norm), w_ple_gate.astype(BF16),
                     w_ple_proj.astype(BF16), row(ple_out_norm))


def kernel(x, p, rel_bias, attn_norm, w_in, swa_q_norm, swa_k_norm, swa_sinks, moba_q_norm, moba_k_norm,
           swa_out_norm, moba_out_norm, w_out, ffn_norm, w_router_group, b_router_group, w_router_expert,
           b_router_expert, w_exp_gate, w_exp_up, w_exp_down, ple_norm, w_ple_gate, w_ple_proj, ple_out_norm):
    B, S, D = x.shape
    x2d = x.reshape(B * S, D)
    for i in range(p.shape[0]):
        x2d = _layer(x2d, p[i].reshape(B * S, -1), rel_bias, attn_norm[i], w_in[i], swa_q_norm[i],
                     swa_k_norm[i], swa_sinks[i], moba_q_norm[i], moba_k_norm[i], swa_out_norm[i],
                     moba_out_norm[i], w_out[i], ffn_norm[i], w_router_group[i], b_router_group[i],
                     w_router_expert[i], b_router_expert[i], w_exp_gate[i], w_exp_up[i], w_exp_down[i],
                     ple_norm[i], w_ple_gate[i], w_ple_proj[i], ple_out_norm[i], B, S)
    return x2d.reshape(B, S, D)
```

```python
import math

import numpy as np
import jax
import jax.numpy as jnp
from jax import lax
from jax.experimental import pallas as pl
from jax.experimental.pallas import tpu as pltpu

F32 = jnp.float32
BF16 = jnp.bfloat16
I32 = jnp.int32

HEAD_DIM = 64
SWA_Q_HEADS = 8
SWA_KV_HEADS = 2
SWA_WINDOW = 128
MOBA_Q_HEADS = 8
MOBA_KV_HEADS = 2
MOBA_BLOCK = 256
MOBA_TOPK = 3
N_HEADS = SWA_Q_HEADS + MOBA_Q_HEADS
REL_BUCKETS = 32
REL_MAX_DIST = 128
N_GROUPS = 4
EXPERTS_PER_GROUP = 8
N_EXPERTS = N_GROUPS * EXPERTS_PER_GROUP
EXPERT_TOPK = 2
DISPATCH_BLOCK = 256
EPS = 1e-6
NEG = -1e30

DMA_PRIORITIES = 2
LANES = 128
ROW_TILE = 8
BF16_SUBLANES = 16
GROUP_HEADS = 4
GROUP_W = GROUP_HEADS * HEAD_DIM
ROUTER_ROWS = 128
EXPERT_ROW0 = 8
TOKEN_TILE = 512
PLE_TILE = 256
VMEM_LIMIT = 52 * 1024 * 1024


def _dot_nt(a, b):
    return lax.dot_general(a, b, (((1,), (1,)), ((), ())), preferred_element_type=F32)


def _rmsnorm(x, g):
    ms = jnp.mean(x * x, axis=-1, keepdims=True)
    return x * lax.rsqrt(ms + EPS) * g


def _to_token_tiles(ref, x):
    m = x.shape[0]
    for j in range(x.shape[1] // LANES):
        ref[pl.ds(j, m, stride=ROW_TILE), :] = x[:, LANES * j:LANES * (j + 1)]


def _from_token_tiles(ref, m):
    return jnp.concatenate([ref[pl.ds(j, m, stride=ROW_TILE), :] for j in range(ref.shape[0] // m)], axis=1)


def _rel_bucket_np(dist):
    n = np.maximum(dist, 0)
    exact = REL_BUCKETS // 2
    nf = np.maximum(n, 1).astype(np.float32)
    large = exact + (np.log(nf / exact) / math.log(REL_MAX_DIST / exact)
                     * (REL_BUCKETS - exact)).astype(np.int32)
    return np.where(n < exact, n, np.minimum(large, REL_BUCKETS - 1)).astype(np.int32)


def _band_buckets(block):
    qi = np.arange(block)[:, None]
    kj = np.arange(2 * block)[None, :]
    return _rel_bucket_np(qi + block - kj)


def _bias_from_buckets(bkt, tbl_ref, head):
    acc = jnp.zeros(bkt.shape, F32)
    for j in range(REL_BUCKETS):
        acc = jnp.where(bkt == j, tbl_ref[j * N_HEADS + head], acc)
    return acc


def _qkv_kernel(x_ref, g_ref, w_ref, wvt_ref, hg_ref, qa_ref, ka_ref, qb_ref, kb_ref, vat_ref, vbt_ref):
    h = _rmsnorm(x_ref[...], g_ref[...]).astype(BF16)
    acc = jnp.dot(h, w_ref[...], preferred_element_type=F32)
    lo = lax.broadcasted_iota(I32, (1, LANES), 1) < HEAD_DIM

    def head_normed(c, scale):
        blk = acc[:, LANES * c:LANES * (c + 1)]
        sq = blk * blk
        s_lo = jnp.sum(jnp.where(lo, sq, 0.0), axis=-1, keepdims=True)
        s_hi = jnp.sum(jnp.where(lo, 0.0, sq), axis=-1, keepdims=True)
        inv = jnp.where(lo, lax.rsqrt(s_lo / HEAD_DIM + EPS), lax.rsqrt(s_hi / HEAD_DIM + EPS))
        return blk * inv * hg_ref[:, LANES * c:LANES * (c + 1)] * scale

    scale = HEAD_DIM ** -0.5
    for c in range(4):
        qa_ref[:, LANES * c:LANES * (c + 1)] = head_normed(c, scale).astype(BF16)
        qb_ref[:, LANES * c:LANES * (c + 1)] = head_normed(5 + c, scale).astype(BF16)
    ka_ref[...] = head_normed(4, 1.0).astype(BF16)
    kb_ref[...] = head_normed(9, 1.0).astype(BF16)
    vt = _dot_nt(wvt_ref[...], h).astype(BF16)
    for j in range(vat_ref.shape[0]):
        vat_ref[j] = vt[:LANES, SWA_WINDOW * j:SWA_WINDOW * (j + 1)]
    for j in range(vbt_ref.shape[0]):
        vbt_ref[j] = vt[LANES:, MOBA_BLOCK * j:MOBA_BLOCK * (j + 1)]


def _qkv_call(x2d, attn_g, w_qk, w_vt, head_gain):
    T, D = x2d.shape
    tm = TOKEN_TILE
    tok = lambda w: pl.BlockSpec((tm, w), lambda i: (i, 0))
    full = lambda a: pl.BlockSpec(a.shape, lambda i: (0, 0))
    out_w = (SWA_Q_HEADS * HEAD_DIM, LANES, MOBA_Q_HEADS * HEAD_DIM, LANES)
    slabs = lambda blk: pl.BlockSpec((tm // blk, LANES, blk), lambda i: (i, 0, 0))
    slab_shape = lambda blk: jax.ShapeDtypeStruct((T // blk, LANES, blk), BF16)
    return pl.pallas_call(
        _qkv_kernel,
        grid=(T // tm,),
        in_specs=[tok(D), full(attn_g), full(w_qk), full(w_vt), full(head_gain)],
        out_specs=[tok(w) for w in out_w] + [slabs(SWA_WINDOW), slabs(MOBA_BLOCK)],
        out_shape=[jax.ShapeDtypeStruct((T, w), BF16) for w in out_w]
        + [slab_shape(SWA_WINDOW), slab_shape(MOBA_BLOCK)],
        compiler_params=pltpu.CompilerParams(
            dimension_semantics=("arbitrary",), vmem_limit_bytes=VMEM_LIMIT),
        name="qkv",
    )(x2d, attn_g, w_qk, w_vt, head_gain)


def _swa_kernel(tbl_ref, sink_ref, bkt_ref, q_ref, k_ref, vt_ref, o_ref, bias_scr, kg_scr):
    W = SWA_WINDOW
    R = GROUP_HEADS
    G = SWA_KV_HEADS
    S = q_ref.shape[0]

    @pl.when(pl.program_id(0) == 0)
    def _init_bias():
        bkt = bkt_ref[...]
        for h in range(SWA_Q_HEADS):
            g, r = divmod(h, R)
            bias_scr[g, :, W * r:W * (r + 1)] = _bias_from_buckets(bkt, tbl_ref, h)

    key_i = lax.broadcasted_iota(I32, (2 * W, R * W), 0)
    qry_i = lax.broadcasted_iota(I32, (2 * W, R * W), 1) & (W - 1)
    dist = qry_i + W - key_i
    band = (dist >= 0) & (dist < W)
    own_part = key_i >= W
    head_of_lane = lax.broadcasted_iota(I32, (1, R * W), 1) // W
    sinks = []
    for g in range(G):
        kg_scr[g] = k_ref[:, HEAD_DIM * g:HEAD_DIM * (g + 1)]
        row = jnp.zeros((1, R * W), F32)
        for r in range(R):
            row = jnp.where(head_of_lane == r, sink_ref[R * g + r], row)
        sinks.append(row)

    def body(i, carry):
        r0 = pl.multiple_of(i * W, W)
        prev = jnp.maximum(i - 1, 0)
        p0 = pl.multiple_of(prev * W, W)
        scores = []
        for g in range(G):
            qblk = q_ref[pl.ds(r0, W), GROUP_W * g:GROUP_W * (g + 1)]
            q4 = jnp.concatenate([qblk[:, HEAD_DIM * r:HEAD_DIM * (r + 1)] for r in range(R)], axis=0)
            kband = jnp.concatenate([kg_scr[g, pl.ds(p0, W), :], kg_scr[g, pl.ds(r0, W), :]], axis=0)
            scores.append(_dot_nt(kband, q4))
        mask = band & (own_part | (i > 0))
        for g in range(G):
            s = jnp.where(mask, scores[g] + bias_scr[g], NEG)
            m = jnp.maximum(jnp.max(s, axis=0, keepdims=True), sinks[g])
            e = jnp.exp(s - m)
            den = jnp.sum(e, axis=0, keepdims=True) + jnp.exp(sinks[g] - m)
            feat = slice(HEAD_DIM * g, HEAD_DIM * (g + 1))
            vtband = jnp.concatenate([vt_ref[prev, feat, :], vt_ref[i, feat, :]], axis=1)
            ot = jnp.dot(vtband, e.astype(BF16), preferred_element_type=F32) / den
            for pr in range(R // 2):
                pair = ot[:, 2 * W * pr:2 * W * (pr + 1)]
                c0 = GROUP_W * g + LANES * pr
                o_ref[pl.ds(r0, W), c0:c0 + LANES] = jnp.concatenate([pair[:, :W], pair[:, W:]], axis=0).T
        return carry

    lax.fori_loop(0, S // W, body, 0)


def _swa_call(tbl_flat, sinks, qa, ka, vat, B, S):
    W = SWA_WINDOW
    bkt = jnp.asarray(np.ascontiguousarray(_band_buckets(W).T))
    smem = pl.BlockSpec(memory_space=pltpu.SMEM)
    return pl.pallas_call(
        _swa_kernel,
        grid=(B,),
        in_specs=[smem, smem,
                  pl.BlockSpec(bkt.shape, lambda b: (0, 0)),
                  pl.BlockSpec((S, qa.shape[1]), lambda b: (b, 0)),
                  pl.BlockSpec((S, LANES), lambda b: (b, 0)),
                  pl.BlockSpec((S // W, LANES, W), lambda b: (b, 0, 0))],
        out_specs=pl.BlockSpec((S, qa.shape[1]), lambda b: (b, 0)),
        out_shape=jax.ShapeDtypeStruct(qa.shape, F32),
        scratch_shapes=[pltpu.VMEM((SWA_KV_HEADS, 2 * W, GROUP_HEADS * W), F32),
                        pltpu.VMEM((SWA_KV_HEADS, S, HEAD_DIM), BF16)],
        compiler_params=pltpu.CompilerParams(
            dimension_semantics=("arbitrary",), vmem_limit_bytes=VMEM_LIMIT),
        name="swa",
    )(tbl_flat, sinks, bkt, qa, ka, vat)


def _moba_kernel(tbl_ref, bkt_ref, q_ref, k_ref, vt_ref, o_ref,
                 bias_scr, kg_scr, km_scr, q4_scr, sel_scr, m_scr, l_scr, acc_scr):
    BS = MOBA_BLOCK
    R = GROUP_HEADS
    G = MOBA_KV_HEADS
    S = q_ref.shape[0]
    NBK = S // BS
    ROWS = R * BS

    @pl.when(pl.program_id(0) == 0)
    def _init_bias():
        bkt = bkt_ref[...]
        for h in range(MOBA_Q_HEADS):
            g, r = divmod(h, R)
            band = _bias_from_buckets(bkt, tbl_ref, SWA_Q_HEADS + h)
            far = tbl_ref[(REL_BUCKETS - 1) * N_HEADS + SWA_Q_HEADS + h]
            bias_scr[g, 0, :, BS * r:BS * (r + 1)] = jnp.full((BS, BS), far, F32)
            bias_scr[g, 1, :, BS * r:BS * (r + 1)] = band[:BS]
            bias_scr[g, 2, :, BS * r:BS * (r + 1)] = band[BS:]

    blk_row = lax.broadcasted_iota(I32, (BF16_SUBLANES, 1), 0)
    for g in range(G):
        kg_scr[g] = k_ref[:, HEAD_DIM * g:HEAD_DIM * (g + 1)]
        kmean = jnp.sum(kg_scr[g].astype(F32).reshape(NBK, BS, HEAD_DIM), axis=1) / BS
        kmean = jnp.concatenate([kmean, jnp.zeros((BF16_SUBLANES - NBK, HEAD_DIM), F32)], axis=0)
        km_hi = kmean.astype(BF16)
        km_scr[g, 0] = km_hi
        km_scr[g, 1] = (kmean - km_hi.astype(F32)).astype(BF16)

    def scores(n):
        k0 = pl.multiple_of(n * BS, BS)
        return [_dot_nt(kg_scr[g, pl.ds(k0, BS), :], q4_scr[g]) for g in range(G)]

    def values_t(n, g):
        return vt_ref[n, HEAD_DIM * g:HEAD_DIM * (g + 1), :]

    def qblock(mi, carry):
        r0 = pl.multiple_of(mi * BS, BS)
        past = blk_row < mi
        for g in range(G):
            qblk = q_ref[pl.ds(r0, BS), GROUP_W * g:GROUP_W * (g + 1)]
            q4 = jnp.concatenate([qblk[:, HEAD_DIM * r:HEAD_DIM * (r + 1)] for r in range(R)], axis=0)
            q4_scr[g] = q4
            gate = _dot_nt(km_scr[g, 0], q4) + _dot_nt(km_scr[g, 1], q4)
            gm = jnp.where(past, gate, NEG)
            cnt = jnp.zeros(gm.shape, F32)
            for n in range(NBK - 1):
                col = gm[n:n + 1, :]
                beats = (gm > col) | ((gm == col) & (blk_row < n))
                c = jnp.sum(jnp.where(beats, 1.0, 0.0), axis=0, keepdims=True)
                cnt = jnp.where(blk_row == n, c, cnt)
            sel_scr[g] = jnp.where(past & (cnt < MOBA_TOPK), 1.0, 0.0)

        key_i = lax.broadcasted_iota(I32, (BS, ROWS), 0)
        qry_i = lax.broadcasted_iota(I32, (BS, ROWS), 1) & (BS - 1)
        causal = key_i <= qry_i
        sc = scores(mi)
        for g in range(G):
            s = jnp.where(causal, sc[g] + bias_scr[g, 2], NEG)
            m0 = jnp.max(s, axis=0, keepdims=True)
            e = jnp.exp(s - m0)
            m_scr[g] = m0
            l_scr[g] = jnp.sum(e, axis=0, keepdims=True)
            acc_scr[g] = jnp.dot(values_t(mi, g), e.astype(BF16), preferred_element_type=F32)

        def kvblock(j, c2):
            n = mi - j
            case = jnp.where(j == 1, 1, 0)
            sc = scores(n)
            for g in range(G):
                s = jnp.where(sel_scr[g, pl.ds(n, 1), :] > 0.5, sc[g] + bias_scr[g, case], NEG)
                m_prev = m_scr[g]
                m_new = jnp.maximum(m_prev, jnp.max(s, axis=0, keepdims=True))
                alpha = jnp.exp(m_prev - m_new)
                e = jnp.exp(s - m_new)
                l_scr[g] = alpha * l_scr[g] + jnp.sum(e, axis=0, keepdims=True)
                acc_scr[g] = alpha * acc_scr[g] + jnp.dot(
                    values_t(n, g), e.astype(BF16), preferred_element_type=F32)
                m_scr[g] = m_new
            return c2

        lax.fori_loop(1, mi + 1, kvblock, 0)
        for g in range(G):
            ot = acc_scr[g] / l_scr[g]
            for pr in range(R // 2):
                pair = ot[:, 2 * BS * pr:2 * BS * (pr + 1)]
                c0 = GROUP_W * g + LANES * pr
                o_ref[pl.ds(r0, BS), c0:c0 + LANES] = jnp.concatenate([pair[:, :BS], pair[:, BS:]], axis=0).T
        return carry

    lax.fori_loop(0, NBK, qblock, 0)


def _moba_call(tbl_flat, qb, kb, vbt, B, S):
    BS = MOBA_BLOCK
    R = GROUP_HEADS
    G = MOBA_KV_HEADS
    NBK = S // BS
    bkt = jnp.asarray(np.ascontiguousarray(_band_buckets(BS).T))
    smem = pl.BlockSpec(memory_space=pltpu.SMEM)
    return pl.pallas_call(
        _moba_kernel,
        grid=(B,),
        in_specs=[smem,
                  pl.BlockSpec(bkt.shape, lambda b: (0, 0)),
                  pl.BlockSpec((S, qb.shape[1]), lambda b: (b, 0)),
                  pl.BlockSpec((S, LANES), lambda b: (b, 0)),
                  pl.BlockSpec((NBK, LANES, BS), lambda b: (b, 0, 0))],
        out_specs=pl.BlockSpec((S, qb.shape[1]), lambda b: (b, 0)),
        out_shape=jax.ShapeDtypeStruct(qb.shape, F32),
        scratch_shapes=[pltpu.VMEM((G, 3, BS, R * BS), F32),
                        pltpu.VMEM((G, S, HEAD_DIM), BF16),
                        pltpu.VMEM((G, 2, BF16_SUBLANES, HEAD_DIM), BF16),
                        pltpu.VMEM((G, R * BS, HEAD_DIM), BF16),
                        pltpu.VMEM((G, BF16_SUBLANES, R * BS), F32),
                        pltpu.VMEM((G, 1, R * BS), F32),
                        pltpu.VMEM((G, 1, R * BS), F32),
                        pltpu.VMEM((G, HEAD_DIM, R * BS), F32)],
        compiler_params=pltpu.CompilerParams(
            dimension_semantics=("arbitrary",), vmem_limit_bytes=VMEM_LIMIT),
        name="moba",
    )(tbl_flat, bkt, qb, kb, vbt)


def _outproj_kernel(oa_ref, ob_ref, x_ref, ga_ref, gb_ref, woa_ref, wob_ref, fg_ref,
                    wrh_ref, wrl_ref, rb_ref, x1_ref, hf_ref, eid_ref, wt_ref):
    na = _rmsnorm(oa_ref[...], ga_ref[...]).astype(BF16)
    nb = _rmsnorm(ob_ref[...], gb_ref[...]).astype(BF16)
    y = (jnp.dot(na, woa_ref[...], preferred_element_type=F32)
         + jnp.dot(nb, wob_ref[...], preferred_element_type=F32))
    x1 = x_ref[...] + y
    x1_ref[...] = x1
    hf = _rmsnorm(x1, fg_ref[...])
    _to_token_tiles(hf_ref, hf)

    hf_hi = hf.astype(BF16)
    hf_lo = (hf - hf_hi.astype(F32)).astype(BF16)
    lt = (_dot_nt(wrh_ref[...], hf_hi) + _dot_nt(wrh_ref[...], hf_lo)
          + _dot_nt(wrl_ref[...], hf_hi) + rb_ref[...])
    gl = [lt[j:j + 1, :] for j in range(N_GROUPS)]
    best = gl[0]
    gsel = jnp.zeros(best.shape, I32)
    for j in range(1, N_GROUPS):
        better = gl[j] > best
        gsel = jnp.where(better, j, gsel)
        best = jnp.where(better, gl[j], best)
    gsum = jnp.zeros(best.shape, F32)
    for j in range(N_GROUPS):
        gsum = gsum + jnp.exp(gl[j] - best)
    gw = 1.0 / gsum
    E = EXPERTS_PER_GROUP
    es = lt[EXPERT_ROW0:EXPERT_ROW0 + E, :]
    for j in range(1, N_GROUPS):
        es = jnp.where(gsel == j, lt[EXPERT_ROW0 + E * j:EXPERT_ROW0 + E * (j + 1), :], es)
    row = lax.broadcasted_iota(I32, es.shape, 0)
    v1 = jnp.max(es, axis=0, keepdims=True)
    i1 = jnp.min(jnp.where(es == v1, row, E), axis=0, keepdims=True)
    es2 = jnp.where(row == i1, -jnp.inf, es)
    v2 = jnp.max(es2, axis=0, keepdims=True)
    i2 = jnp.min(jnp.where(es2 == v2, row, E), axis=0, keepdims=True)
    e2 = jnp.exp(v2 - v1)
    den = 1.0 + e2
    eid_ref[...] = jnp.concatenate([gsel * E + i1, gsel * E + i2], axis=0)
    wt_ref[...] = jnp.concatenate([gw * (1.0 / den), gw * (e2 / den)], axis=0)


def _outproj_call(oa, ob, x2d, ga, gb, woa, wob, fg, wrh, wrl, rb):
    T, D = x2d.shape
    tm = TOKEN_TILE
    tok = lambda w: pl.BlockSpec((tm, w), lambda i: (i, 0))
    full = lambda a: pl.BlockSpec(a.shape, lambda i: (0, 0))
    col = pl.BlockSpec((EXPERT_TOPK, tm), lambda i: (0, i))
    return pl.pallas_call(
        _outproj_kernel,
        grid=(T // tm,),
        in_specs=[tok(oa.shape[1]), tok(ob.shape[1]), tok(D), full(ga), full(gb), full(woa), full(wob),
                  full(fg), full(wrh), full(wrl), full(rb)],
        out_specs=[tok(D), pl.BlockSpec((tm * ROW_TILE, LANES), lambda i: (i, 0)), col, col],
        out_shape=[jax.ShapeDtypeStruct((T, D), F32), jax.ShapeDtypeStruct((T * ROW_TILE, LANES), F32),
                   jax.ShapeDtypeStruct((EXPERT_TOPK, T), I32), jax.ShapeDtypeStruct((EXPERT_TOPK, T), F32)],
        compiler_params=pltpu.CompilerParams(
            dimension_semantics=("arbitrary",), vmem_limit_bytes=VMEM_LIMIT),
        name="outproj",
    )(oa, ob, x2d, ga, gb, woa, wob, fg, wrh, wrl, rb)


def _moe_kernel(be_ref, nu_ref, src_ref, srcn_ref, dst_ref, hf_hbm, wg_ref, wu_ref, wd_ref, out_hbm,
                xbuf, obuf, wgb, wub, wdb, gsem, ssem):
    i = pl.program_id(0)
    n_used = nu_ref[0]
    M = xbuf.shape[1] // ROW_TILE
    slot = i % 2

    def start_gather(idx_ref, buf_slot):
        for r in range(M):
            tok_row = pl.multiple_of(idx_ref[0, 0, r] * ROW_TILE, ROW_TILE)
            pltpu.make_async_copy(hf_hbm.at[pl.ds(tok_row, ROW_TILE)],
                                  xbuf.at[buf_slot, pl.ds(r * ROW_TILE, ROW_TILE)],
                                  gsem.at[buf_slot]).start(priority=r % DMA_PRIORITIES)

    def wait_rows(buf, buf_slot, sem):
        pltpu.make_async_copy(hf_hbm.at[pl.ds(0, M * ROW_TILE)], buf.at[buf_slot], sem.at[buf_slot]).wait()

    @pl.when(i < n_used)
    def _compute():
        @pl.when(i == 0)
        def _first_gather():
            start_gather(src_ref, 0)
            obuf[1] = jnp.zeros(obuf.shape[1:], F32)
            spare = pltpu.make_async_copy(
                obuf.at[1], out_hbm.at[pl.ds(out_hbm.shape[0] - M * ROW_TILE, M * ROW_TILE)], ssem.at[1])
            spare.start()
            spare.wait()

        @pl.when((i == 0) | (be_ref[i] != be_ref[jnp.maximum(i - 1, 0)]))
        def _cast_weights():
            wgb[...] = wg_ref[0].astype(BF16)
            wub[...] = wu_ref[0].astype(BF16)
            wdb[...] = wd_ref[0].astype(BF16)

        wait_rows(xbuf, slot, gsem)

        @pl.when(i >= 2)
        def _reuse_obuf():
            wait_rows(obuf, slot, ssem)

        start_gather(srcn_ref, 1 - slot)
        xb = _from_token_tiles(xbuf.at[slot], M).astype(BF16)
        a = jnp.dot(xb, wgb[...], preferred_element_type=F32)
        u = jnp.dot(xb, wub[...], preferred_element_type=F32)
        hmid = (a * jax.nn.sigmoid(a)) * u
        _to_token_tiles(obuf.at[slot], jnp.dot(hmid.astype(BF16), wdb[...], preferred_element_type=F32))
        for r in range(M):
            dst_row = pl.multiple_of(dst_ref[0, 0, r] * ROW_TILE, ROW_TILE)
            pltpu.make_async_copy(obuf.at[slot, pl.ds(r * ROW_TILE, ROW_TILE)],
                                  out_hbm.at[pl.ds(dst_row, ROW_TILE)],
                                  ssem.at[slot]).start(priority=r % DMA_PRIORITIES)

        @pl.when(i == n_used - 1)
        def _drain():
            wait_rows(xbuf, 1 - slot, gsem)
            wait_rows(obuf, slot, ssem)

            @pl.when(i >= 1)
            def _drain_prev():
                wait_rows(obuf, 1 - slot, ssem)


def _moe_call(block_expert, n_used, src_tok, dst_row, hf, w_g, w_u, w_d):
    T = hf.shape[0] // ROW_TILE
    M = DISPATCH_BLOCK
    nblk = block_expert.shape[0]
    D, DE = w_g.shape[1], w_g.shape[2]
    assert D == ROW_TILE * LANES and hf.shape[1] == LANES
    idx_spec = lambda f: pl.BlockSpec((1, 1, M), f, memory_space=pltpu.SMEM)
    cur = lambda i, be, nu: (i, 0, 0)
    nxt = lambda i, be, nu: (jnp.minimum(i + 1, nblk - 1), 0, 0)
    wspec = lambda a, b: pl.BlockSpec((1, a, b), lambda i, be, nu: (be[i], 0, 0))
    grid_spec = pltpu.PrefetchScalarGridSpec(
        num_scalar_prefetch=2,
        grid=(nblk,),
        in_specs=[idx_spec(cur), idx_spec(nxt), idx_spec(cur),
                  pl.BlockSpec(memory_space=pl.ANY),
                  wspec(D, DE), wspec(D, DE), wspec(DE, D)],
        out_specs=pl.BlockSpec(memory_space=pl.ANY),
        scratch_shapes=[pltpu.VMEM((2, M * ROW_TILE, LANES), F32), pltpu.VMEM((2, M * ROW_TILE, LANES), F32),
                        pltpu.VMEM((D, DE), BF16), pltpu.VMEM((D, DE), BF16), pltpu.VMEM((DE, D), BF16),
                        pltpu.SemaphoreType.DMA((2,)), pltpu.SemaphoreType.DMA((2,))],
    )
    src3 = src_tok.reshape(nblk, 1, M)
    return pl.pallas_call(
        _moe_kernel,
        grid_spec=grid_spec,
        out_shape=jax.ShapeDtypeStruct(((EXPERT_TOPK * T + M) * ROW_TILE, LANES), F32),
        compiler_params=pltpu.CompilerParams(
            dimension_semantics=("arbitrary",), vmem_limit_bytes=VMEM_LIMIT),
        name="moe",
    )(block_expert, n_used, src3, src3, dst_row.reshape(nblk, 1, M), hf, w_g, w_u, w_d)


def _ple_kernel(x1_ref, m0_ref, m1_ref, wt_ref, p_ref, pg_ref, wgate_ref, wproj_ref, pog_ref, out_ref):
    proj = _rmsnorm(jnp.dot(p_ref[...].astype(BF16), wproj_ref[...], preferred_element_type=F32),
                    pog_ref[...])
    tm = x1_ref.shape[0]
    moe = (_from_token_tiles(m0_ref, tm) * wt_ref[:, 0:1] + _from_token_tiles(m1_ref, tm) * wt_ref[:, 1:2])
    x2 = x1_ref[...] + moe
    hn = _rmsnorm(x2, pg_ref[...]).astype(BF16)
    gate = jax.nn.sigmoid(jnp.dot(hn, wgate_ref[...], preferred_element_type=F32))
    out_ref[...] = x2 + gate * proj


def _ple_call(moe_rows, x1, wt_cols, p2d, pg, wgate, wproj, pog):
    T, D = x1.shape
    tm = PLE_TILE
    tok = lambda w: pl.BlockSpec((tm, w), lambda i: (i, 0))
    full = lambda a: pl.BlockSpec(a.shape, lambda i: (0, 0))
    first = pl.BlockSpec((tm * ROW_TILE, LANES), lambda i: (i, 0))
    second = pl.BlockSpec((tm * ROW_TILE, LANES), lambda i: (T // tm + i, 0))
    return pl.pallas_call(
        _ple_kernel,
        grid=(T // tm,),
        in_specs=[tok(D), first, second, tok(EXPERT_TOPK), tok(p2d.shape[1]),
                  full(pg), full(wgate), full(wproj), full(pog)],
        out_specs=tok(D),
        out_shape=jax.ShapeDtypeStruct((T, D), F32),
        compiler_params=pltpu.CompilerParams(
            dimension_semantics=("arbitrary",), vmem_limit_bytes=VMEM_LIMIT),
        name="ple",
    )(x1, moe_rows, moe_rows, wt_cols, p2d, pg, wgate, wproj, pog)


def _dispatch_plan(eid, T):
    A = EXPERT_TOPK * T
    M = DISPATCH_BLOCK
    nblk = A // M + N_EXPERTS
    e_flat = eid.reshape(A)
    experts = jnp.arange(N_EXPERTS, dtype=I32)
    counts = jnp.sum((e_flat[:, None] == experts[None, :]).astype(I32), axis=0)
    start = jnp.cumsum(counts) - counts
    padded = (counts + M - 1) // M * M
    pad_end = jnp.cumsum(padded)
    pad_start = pad_end - padded
    order = jnp.argsort(e_flat).astype(I32)
    n_used = pad_end[-1] // M
    blk = jnp.arange(nblk, dtype=I32)
    be = jnp.sum((pad_end[None, :] <= (blk * M)[:, None]).astype(I32), axis=1)
    be = jnp.minimum(be, N_EXPERTS - 1)
    be = jnp.where(blk < n_used, be, be[jnp.maximum(n_used - 1, 0)])
    row = jnp.arange(M, dtype=I32)[None, :]
    off = (blk * M - pad_start[be])[:, None] + row
    valid = (blk < n_used)[:, None] & (off < counts[be][:, None])
    a_s = order[jnp.clip(start[be][:, None] + off, 0, A - 1)]
    src_tok = jnp.where(valid, jnp.where(a_s >= T, a_s - T, a_s), 0)
    dst_row = jnp.where(valid, a_s, A + row)
    return be, n_used.reshape(1).astype(I32), src_tok, dst_row


def _layer(x2d, p2d, rel_bias, attn_norm, w_in, swa_q_norm, swa_k_norm, swa_sinks, moba_q_norm,
           moba_k_norm, swa_out_norm, moba_out_norm, w_out, ffn_norm, w_rg, b_rg, w_re, b_re,
           w_g, w_u, w_d, ple_norm, w_ple_gate, w_ple_proj, ple_out_norm, B, S):
    T, D = x2d.shape
    row = lambda v: v.reshape(1, -1).astype(F32)
    head_gain = jnp.concatenate([
        jnp.tile(swa_q_norm, SWA_Q_HEADS), jnp.tile(swa_k_norm, SWA_KV_HEADS),
        jnp.tile(moba_q_norm, MOBA_Q_HEADS), jnp.tile(moba_k_norm, MOBA_KV_HEADS)])
    w_in_b = w_in.astype(BF16)
    qa_w, kv_w = SWA_Q_HEADS * HEAD_DIM, SWA_KV_HEADS * HEAD_DIM
    qb_w = MOBA_Q_HEADS * HEAD_DIM
    c_va = qa_w + kv_w
    c_qb = c_va + kv_w
    c_vb = c_qb + qb_w + kv_w
    w_qk = jnp.concatenate([w_in_b[:, :c_va], w_in_b[:, c_qb:c_vb]], axis=1)
    w_vt = jnp.concatenate([w_in_b[:, c_va:c_qb], w_in_b[:, c_vb:]], axis=1).T
    qa, ka, qb, kb, vat, vbt = _qkv_call(x2d, row(attn_norm), w_qk, w_vt, row(head_gain))

    tbl_flat = rel_bias.astype(F32).reshape(-1)
    oa = _swa_call(tbl_flat, swa_sinks.astype(F32), qa, ka, vat, B, S)
    ob = _moba_call(tbl_flat, qb, kb, vbt, B, S)

    pad_rows = lambda n: jnp.zeros((n, D), F32)
    wr = jnp.concatenate([w_rg.T, pad_rows(EXPERT_ROW0 - N_GROUPS), w_re.T,
                          pad_rows(ROUTER_ROWS - EXPERT_ROW0 - N_EXPERTS)], axis=0)
    wr_hi = wr.astype(BF16)
    wr_lo = (wr - wr_hi.astype(F32)).astype(BF16)
    rb = jnp.concatenate([b_rg, jnp.zeros((EXPERT_ROW0 - N_GROUPS,), F32), b_re,
                          jnp.zeros((ROUTER_ROWS - EXPERT_ROW0 - N_EXPERTS,), F32)]).reshape(ROUTER_ROWS, 1)
    wo = w_out.astype(BF16)
    na_w = SWA_Q_HEADS * HEAD_DIM
    x1, hf, eid, wts = _outproj_call(oa, ob, x2d, row(swa_out_norm), row(moba_out_norm),
                                     wo[:na_w], wo[na_w:], row(ffn_norm), wr_hi, wr_lo, rb)

    be, n_used, src_tok, dst_row = _dispatch_plan(eid, T)
    moe_rows = _moe_call(be, n_used, src_tok, dst_row, hf, w_g, w_u, w_d)
    return _ple_call(moe_rows, x1, wts.T, p2d, row(ple_norm), w_ple_gate.astype(BF16),
                     w_ple_proj.astype(BF16), row(ple_out_norm))


def kernel(x, p, rel_bias, attn_norm, w_in, swa_q_norm, swa_k_norm, swa_sinks, moba_q_norm, moba_k_norm,
           swa_out_norm, moba_out_norm, w_out, ffn_norm, w_router_group, b_router_group, w_router_expert,
           b_router_expert, w_exp_gate, w_exp_up, w_exp_down, ple_norm, w_ple_gate, w_ple_proj, ple_out_norm):
    B, S, D = x.shape
    x2d = x.reshape(B * S, D)
    for i in range(p.shape[0]):
        x2d = _layer(x2d, p[i].reshape(B * S, -1), rel_bias, attn_norm[i], w_in[i], swa_q_norm[i],
                     swa_k_norm[i], swa_sinks[i], moba_q_norm[i], moba_k_norm[i], swa_out_norm[i],
                     moba_out_norm[i], w_out[i], ffn_norm[i], w_router_group[i], b_router_group[i],
                     w_router_expert[i], b_router_expert[i], w_exp_gate[i], w_exp_up[i], w_exp_down[i],
                     ple_norm[i], w_ple_gate[i], w_ple_proj[i], ple_out_norm[i], B, S)
    return x2d.reshape(B, S, D)
```

```python
import math

import numpy as np
import jax
import jax.numpy as jnp
from jax import lax
from jax.experimental import pallas as pl
from jax.experimental.pallas import tpu as pltpu

F32 = jnp.float32
BF16 = jnp.bfloat16
I32 = jnp.int32

HEAD_DIM = 64
SWA_Q_HEADS = 8
SWA_KV_HEADS = 2
SWA_WINDOW = 128
MOBA_Q_HEADS = 8
MOBA_KV_HEADS = 2
MOBA_BLOCK = 256
MOBA_TOPK = 3
N_HEADS = SWA_Q_HEADS + MOBA_Q_HEADS
REL_BUCKETS = 32
REL_MAX_DIST = 128
N_GROUPS = 4
EXPERTS_PER_GROUP = 8
N_EXPERTS = N_GROUPS * EXPERTS_PER_GROUP
EXPERT_TOPK = 2
DISPATCH_BLOCK = 512
EPS = 1e-6
NEG = -1e30

DMA_PRIORITIES = 2
LANES = 128
ROW_TILE = 8
BF16_SUBLANES = 16
GROUP_HEADS = 4
GROUP_W = GROUP_HEADS * HEAD_DIM
ROUTER_ROWS = 128
EXPERT_ROW0 = 8
TOKEN_TILE = 512
PLE_TILE = 512
VMEM_LIMIT = 52 * 1024 * 1024


def _dot_nt(a, b):
    return lax.dot_general(a, b, (((1,), (1,)), ((), ())), preferred_element_type=F32)


def _rmsnorm(x, g):
    ms = jnp.mean(x * x, axis=-1, keepdims=True)
    return x * lax.rsqrt(ms + EPS) * g


def _to_token_tiles(ref, x):
    m = x.shape[0]
    for j in range(x.shape[1] // LANES):
        ref[pl.ds(j, m, stride=ROW_TILE), :] = x[:, LANES * j:LANES * (j + 1)]


def _from_token_tiles(ref, m):
    return jnp.concatenate([ref[pl.ds(j, m, stride=ROW_TILE), :] for j in range(ref.shape[0] // m)], axis=1)


def _rel_bucket_np(dist):
    n = np.maximum(dist, 0)
    exact = REL_BUCKETS // 2
    nf = np.maximum(n, 1).astype(np.float32)
    large = exact + (np.log(nf / exact) / math.log(REL_MAX_DIST / exact)
                     * (REL_BUCKETS - exact)).astype(np.int32)
    return np.where(n < exact, n, np.minimum(large, REL_BUCKETS - 1)).astype(np.int32)


def _band_buckets(block):
    qi = np.arange(block)[:, None]
    kj = np.arange(2 * block)[None, :]
    return _rel_bucket_np(qi + block - kj)


def _bias_from_buckets(bkt, tbl_ref, head):
    acc = jnp.zeros(bkt.shape, F32)
    for j in range(REL_BUCKETS):
        acc = jnp.where(bkt == j, tbl_ref[j * N_HEADS + head], acc)
    return acc


def _qkv_kernel(x_ref, g_ref, w_ref, wvt_ref, hg_ref, qa_ref, ka_ref, qb_ref, kb_ref, vat_ref, vbt_ref):
    h = _rmsnorm(x_ref[...], g_ref[...]).astype(BF16)
    acc = jnp.dot(h, w_ref[...], preferred_element_type=F32)
    lo = lax.broadcasted_iota(I32, (1, LANES), 1) < HEAD_DIM

    def head_normed(c, scale):
        blk = acc[:, LANES * c:LANES * (c + 1)]
        sq = blk * blk
        s_lo = jnp.sum(jnp.where(lo, sq, 0.0), axis=-1, keepdims=True)
        s_hi = jnp.sum(jnp.where(lo, 0.0, sq), axis=-1, keepdims=True)
        inv = jnp.where(lo, lax.rsqrt(s_lo / HEAD_DIM + EPS), lax.rsqrt(s_hi / HEAD_DIM + EPS))
        return blk * inv * hg_ref[:, LANES * c:LANES * (c + 1)] * scale

    scale = HEAD_DIM ** -0.5
    for c in range(4):
        qa_ref[:, LANES * c:LANES * (c + 1)] = head_normed(c, scale).astype(BF16)
        qb_ref[:, LANES * c:LANES * (c + 1)] = head_normed(5 + c, scale).astype(BF16)
    ka_ref[...] = head_normed(4, 1.0).astype(BF16)
    kb_ref[...] = head_normed(9, 1.0).astype(BF16)
    vt = _dot_nt(wvt_ref[...], h).astype(BF16)
    for j in range(vat_ref.shape[0]):
        vat_ref[j] = vt[:LANES, SWA_WINDOW * j:SWA_WINDOW * (j + 1)]
    for j in range(vbt_ref.shape[0]):
        vbt_ref[j] = vt[LANES:, MOBA_BLOCK * j:MOBA_BLOCK * (j + 1)]


def _qkv_call(x2d, attn_g, w_qk, w_vt, head_gain):
    T, D = x2d.shape
    tm = TOKEN_TILE
    tok = lambda w: pl.BlockSpec((tm, w), lambda i: (i, 0))
    full = lambda a: pl.BlockSpec(a.shape, lambda i: (0, 0))
    out_w = (SWA_Q_HEADS * HEAD_DIM, LANES, MOBA_Q_HEADS * HEAD_DIM, LANES)
    slabs = lambda blk: pl.BlockSpec((tm // blk, LANES, blk), lambda i: (i, 0, 0))
    slab_shape = lambda blk: jax.ShapeDtypeStruct((T // blk, LANES, blk), BF16)
    return pl.pallas_call(
        _qkv_kernel,
        grid=(T // tm,),
        in_specs=[tok(D), full(attn_g), full(w_qk), full(w_vt), full(head_gain)],
        out_specs=[tok(w) for w in out_w] + [slabs(SWA_WINDOW), slabs(MOBA_BLOCK)],
        out_shape=[jax.ShapeDtypeStruct((T, w), BF16) for w in out_w]
        + [slab_shape(SWA_WINDOW), slab_shape(MOBA_BLOCK)],
        compiler_params=pltpu.CompilerParams(
            dimension_semantics=("arbitrary",), vmem_limit_bytes=VMEM_LIMIT),
        name="qkv",
    )(x2d, attn_g, w_qk, w_vt, head_gain)


def _swa_kernel(tbl_ref, sink_ref, bkt_ref, q_ref, k_ref, vt_ref, o_ref, bias_scr, kg_scr):
    W = SWA_WINDOW
    R = GROUP_HEADS
    G = SWA_KV_HEADS
    S = q_ref.shape[0]

    @pl.when(pl.program_id(0) == 0)
    def _init_bias():
        bkt = bkt_ref[...]
        for h in range(SWA_Q_HEADS):
            g, r = divmod(h, R)
            bias_scr[g, :, W * r:W * (r + 1)] = _bias_from_buckets(bkt, tbl_ref, h)

    key_i = lax.broadcasted_iota(I32, (2 * W, R * W), 0)
    qry_i = lax.broadcasted_iota(I32, (2 * W, R * W), 1) & (W - 1)
    dist = qry_i + W - key_i
    band = (dist >= 0) & (dist < W)
    own_part = key_i >= W
    head_of_lane = lax.broadcasted_iota(I32, (1, R * W), 1) // W
    sinks = []
    for g in range(G):
        kg_scr[g] = k_ref[:, HEAD_DIM * g:HEAD_DIM * (g + 1)]
        row = jnp.zeros((1, R * W), F32)
        for r in range(R):
            row = jnp.where(head_of_lane == r, sink_ref[R * g + r], row)
        sinks.append(row)

    def body(i, carry):
        r0 = pl.multiple_of(i * W, W)
        prev = jnp.maximum(i - 1, 0)
        p0 = pl.multiple_of(prev * W, W)
        scores = []
        for g in range(G):
            qblk = q_ref[pl.ds(r0, W), GROUP_W * g:GROUP_W * (g + 1)]
            q4 = jnp.concatenate([qblk[:, HEAD_DIM * r:HEAD_DIM * (r + 1)] for r in range(R)], axis=0)
            kband = jnp.concatenate([kg_scr[g, pl.ds(p0, W), :], kg_scr[g, pl.ds(r0, W), :]], axis=0)
            scores.append(_dot_nt(kband, q4))
        mask = band & (own_part | (i > 0))
        for g in range(G):
            s = jnp.where(mask, scores[g] + bias_scr[g], NEG)
            m = jnp.maximum(jnp.max(s, axis=0, keepdims=True), sinks[g])
            e = jnp.exp(s - m)
            den = jnp.sum(e, axis=0, keepdims=True) + jnp.exp(sinks[g] - m)
            feat = slice(HEAD_DIM * g, HEAD_DIM * (g + 1))
            vtband = jnp.concatenate([vt_ref[prev, feat, :], vt_ref[i, feat, :]], axis=1)
            ot = jnp.dot(vtband, e.astype(BF16), preferred_element_type=F32) / den
            for pr in range(R // 2):
                pair = ot[:, 2 * W * pr:2 * W * (pr + 1)]
                c0 = GROUP_W * g + LANES * pr
                o_ref[pl.ds(r0, W), c0:c0 + LANES] = jnp.concatenate([pair[:, :W], pair[:, W:]], axis=0).T
        return carry

    lax.fori_loop(0, S // W, body, 0)


def _swa_call(tbl_flat, sinks, qa, ka, vat, B, S):
    W = SWA_WINDOW
    bkt = jnp.asarray(np.ascontiguousarray(_band_buckets(W).T))
    smem = pl.BlockSpec(memory_space=pltpu.SMEM)
    return pl.pallas_call(
        _swa_kernel,
        grid=(B,),
        in_specs=[smem, smem,
                  pl.BlockSpec(bkt.shape, lambda b: (0, 0)),
                  pl.BlockSpec((S, qa.shape[1]), lambda b: (b, 0)),
                  pl.BlockSpec((S, LANES), lambda b: (b, 0)),
                  pl.BlockSpec((S // W, LANES, W), lambda b: (b, 0, 0))],
        out_specs=pl.BlockSpec((S, qa.shape[1]), lambda b: (b, 0)),
        out_shape=jax.ShapeDtypeStruct(qa.shape, F32),
        scratch_shapes=[pltpu.VMEM((SWA_KV_HEADS, 2 * W, GROUP_HEADS * W), F32),
                        pltpu.VMEM((SWA_KV_HEADS, S, HEAD_DIM), BF16)],
        compiler_params=pltpu.CompilerParams(
            dimension_semantics=("arbitrary",), vmem_limit_bytes=VMEM_LIMIT),
        name="swa",
    )(tbl_flat, sinks, bkt, qa, ka, vat)


def _moba_kernel(tbl_ref, bkt_ref, q_ref, k_ref, vt_ref, o_ref,
                 bias_scr, kg_scr, km_scr, q4_scr, sel_scr, m_scr, l_scr, acc_scr):
    BS = MOBA_BLOCK
    R = GROUP_HEADS
    G = MOBA_KV_HEADS
    S = q_ref.shape[0]
    NBK = S // BS
    ROWS = R * BS

    @pl.when(pl.program_id(0) == 0)
    def _init_bias():
        bkt = bkt_ref[...]
        for h in range(MOBA_Q_HEADS):
            g, r = divmod(h, R)
            band = _bias_from_buckets(bkt, tbl_ref, SWA_Q_HEADS + h)
            far = tbl_ref[(REL_BUCKETS - 1) * N_HEADS + SWA_Q_HEADS + h]
            bias_scr[g, 0, :, BS * r:BS * (r + 1)] = jnp.full((BS, BS), far, F32)
            bias_scr[g, 1, :, BS * r:BS * (r + 1)] = band[:BS]
            bias_scr[g, 2, :, BS * r:BS * (r + 1)] = band[BS:]

    blk_row = lax.broadcasted_iota(I32, (BF16_SUBLANES, 1), 0)
    for g in range(G):
        kg_scr[g] = k_ref[:, HEAD_DIM * g:HEAD_DIM * (g + 1)]
        kmean = jnp.sum(kg_scr[g].astype(F32).reshape(NBK, BS, HEAD_DIM), axis=1) / BS
        kmean = jnp.concatenate([kmean, jnp.zeros((BF16_SUBLANES - NBK, HEAD_DIM), F32)], axis=0)
        km_hi = kmean.astype(BF16)
        km_scr[g, 0] = km_hi
        km_scr[g, 1] = (kmean - km_hi.astype(F32)).astype(BF16)

    def scores(n):
        k0 = pl.multiple_of(n * BS, BS)
        return [_dot_nt(kg_scr[g, pl.ds(k0, BS), :], q4_scr[g]) for g in range(G)]

    def values_t(n, g):
        return vt_ref[n, HEAD_DIM * g:HEAD_DIM * (g + 1), :]

    def qblock(mi, carry):
        r0 = pl.multiple_of(mi * BS, BS)
        past = blk_row < mi
        for g in range(G):
            qblk = q_ref[pl.ds(r0, BS), GROUP_W * g:GROUP_W * (g + 1)]
            q4 = jnp.concatenate([qblk[:, HEAD_DIM * r:HEAD_DIM * (r + 1)] for r in range(R)], axis=0)
            q4_scr[g] = q4
            gate = _dot_nt(km_scr[g, 0], q4) + _dot_nt(km_scr[g, 1], q4)
            gm = jnp.where(past, gate, NEG)
            cnt = jnp.zeros(gm.shape, F32)
            for n in range(NBK - 1):
                col = gm[n:n + 1, :]
                beats = (gm > col) | ((gm == col) & (blk_row < n))
                c = jnp.sum(jnp.where(beats, 1.0, 0.0), axis=0, keepdims=True)
                cnt = jnp.where(blk_row == n, c, cnt)
            sel_scr[g] = jnp.where(past & (cnt < MOBA_TOPK), 1.0, 0.0)

        key_i = lax.broadcasted_iota(I32, (BS, ROWS), 0)
        qry_i = lax.broadcasted_iota(I32, (BS, ROWS), 1) & (BS - 1)
        causal = key_i <= qry_i
        sc = scores(mi)
        for g in range(G):
            s = jnp.where(causal, sc[g] + bias_scr[g, 2], NEG)
            m0 = jnp.max(s, axis=0, keepdims=True)
            e = jnp.exp(s - m0)
            m_scr[g] = m0
            l_scr[g] = jnp.sum(e, axis=0, keepdims=True)
            acc_scr[g] = jnp.dot(values_t(mi, g), e.astype(BF16), preferred_element_type=F32)

        def kvblock(j, c2):
            n = mi - j
            case = jnp.where(j == 1, 1, 0)
            sc = scores(n)
            for g in range(G):
                s = jnp.where(sel_scr[g, pl.ds(n, 1), :] > 0.5, sc[g] + bias_scr[g, case], NEG)
                m_prev = m_scr[g]
                m_new = jnp.maximum(m_prev, jnp.max(s, axis=0, keepdims=True))
                alpha = jnp.exp(m_prev - m_new)
                e = jnp.exp(s - m_new)
                l_scr[g] = alpha * l_scr[g] + jnp.sum(e, axis=0, keepdims=True)
                acc_scr[g] = alpha * acc_scr[g] + jnp.dot(
                    values_t(n, g), e.astype(BF16), preferred_element_type=F32)
                m_scr[g] = m_new
            return c2

        lax.fori_loop(1, mi + 1, kvblock, 0)
        for g in range(G):
            ot = acc_scr[g] / l_scr[g]
            for pr in range(R // 2):
                pair = ot[:, 2 * BS * pr:2 * BS * (pr + 1)]
                c0 = GROUP_W * g + LANES * pr
                o_ref[pl.ds(r0, BS), c0:c0 + LANES] = jnp.concatenate([pair[:, :BS], pair[:, BS:]], axis=0).T
        return carry

    lax.fori_loop(0, NBK, qblock, 0)


def _moba_call(tbl_flat, qb, kb, vbt, B, S):
    BS = MOBA_BLOCK
    R = GROUP_HEADS
    G = MOBA_KV_HEADS
    NBK = S // BS
    bkt = jnp.asarray(np.ascontiguousarray(_band_buckets(BS).T))
    smem = pl.BlockSpec(memory_space=pltpu.SMEM)
    return pl.pallas_call(
        _moba_kernel,
        grid=(B,),
        in_specs=[smem,
                  pl.BlockSpec(bkt.shape, lambda b: (0, 0)),
                  pl.BlockSpec((S, qb.shape[1]), lambda b: (b, 0)),
                  pl.BlockSpec((S, LANES), lambda b: (b, 0)),
                  pl.BlockSpec((NBK, LANES, BS), lambda b: (b, 0, 0))],
        out_specs=pl.BlockSpec((S, qb.shape[1]), lambda b: (b, 0)),
        out_shape=jax.ShapeDtypeStruct(qb.shape, F32),
        scratch_shapes=[pltpu.VMEM((G, 3, BS, R * BS), F32),
                        pltpu.VMEM((G, S, HEAD_DIM), BF16),
                        pltpu.VMEM((G, 2, BF16_SUBLANES, HEAD_DIM), BF16),
                        pltpu.VMEM((G, R * BS, HEAD_DIM), BF16),
                        pltpu.VMEM((G, BF16_SUBLANES, R * BS), F32),
                        pltpu.VMEM((G, 1, R * BS), F32),
                        pltpu.VMEM((G, 1, R * BS), F32),
                        pltpu.VMEM((G, HEAD_DIM, R * BS), F32)],
        compiler_params=pltpu.CompilerParams(
            dimension_semantics=("arbitrary",), vmem_limit_bytes=VMEM_LIMIT),
        name="moba",
    )(tbl_flat, bkt, qb, kb, vbt)


def _outproj_kernel(oa_ref, ob_ref, x_ref, ga_ref, gb_ref, woa_ref, wob_ref, fg_ref,
                    wrh_ref, wrl_ref, rb_ref, x1_ref, hf_ref, eid_ref, wt_ref):
    na = _rmsnorm(oa_ref[...], ga_ref[...]).astype(BF16)
    nb = _rmsnorm(ob_ref[...], gb_ref[...]).astype(BF16)
    y = (jnp.dot(na, woa_ref[...], preferred_element_type=F32)
         + jnp.dot(nb, wob_ref[...], preferred_element_type=F32))
    x1 = x_ref[...] + y
    x1_ref[...] = x1
    hf = _rmsnorm(x1, fg_ref[...])
    _to_token_tiles(hf_ref, hf)

    hf_hi = hf.astype(BF16)
    hf_lo = (hf - hf_hi.astype(F32)).astype(BF16)
    lt = (_dot_nt(wrh_ref[...], hf_hi) + _dot_nt(wrh_ref[...], hf_lo)
          + _dot_nt(wrl_ref[...], hf_hi) + rb_ref[...])
    gl = [lt[j:j + 1, :] for j in range(N_GROUPS)]
    best = gl[0]
    gsel = jnp.zeros(best.shape, I32)
    for j in range(1, N_GROUPS):
        better = gl[j] > best
        gsel = jnp.where(better, j, gsel)
        best = jnp.where(better, gl[j], best)
    gsum = jnp.zeros(best.shape, F32)
    for j in range(N_GROUPS):
        gsum = gsum + jnp.exp(gl[j] - best)
    gw = 1.0 / gsum
    E = EXPERTS_PER_GROUP
    es = lt[EXPERT_ROW0:EXPERT_ROW0 + E, :]
    for j in range(1, N_GROUPS):
        es = jnp.where(gsel == j, lt[EXPERT_ROW0 + E * j:EXPERT_ROW0 + E * (j + 1), :], es)
    row = lax.broadcasted_iota(I32, es.shape, 0)
    v1 = jnp.max(es, axis=0, keepdims=True)
    i1 = jnp.min(jnp.where(es == v1, row, E), axis=0, keepdims=True)
    es2 = jnp.where(row == i1, -jnp.inf, es)
    v2 = jnp.max(es2, axis=0, keepdims=True)
    i2 = jnp.min(jnp.where(es2 == v2, row, E), axis=0, keepdims=True)
    e2 = jnp.exp(v2 - v1)
    den = 1.0 + e2
    eid_ref[...] = jnp.concatenate([gsel * E + i1, gsel * E + i2], axis=0)
    wt_ref[...] = jnp.concatenate([gw * (1.0 / den), gw * (e2 / den)], axis=0)


def _outproj_call(oa, ob, x2d, ga, gb, woa, wob, fg, wrh, wrl, rb):
    T, D = x2d.shape
    tm = TOKEN_TILE
    tok = lambda w: pl.BlockSpec((tm, w), lambda i: (i, 0))
    full = lambda a: pl.BlockSpec(a.shape, lambda i: (0, 0))
    col = pl.BlockSpec((EXPERT_TOPK, tm), lambda i: (0, i))
    return pl.pallas_call(
        _outproj_kernel,
        grid=(T // tm,),
        in_specs=[tok(oa.shape[1]), tok(ob.shape[1]), tok(D), full(ga), full(gb), full(woa), full(wob),
                  full(fg), full(wrh), full(wrl), full(rb)],
        out_specs=[tok(D), pl.BlockSpec((tm * ROW_TILE, LANES), lambda i: (i, 0)), col, col],
        out_shape=[jax.ShapeDtypeStruct((T, D), F32), jax.ShapeDtypeStruct((T * ROW_TILE, LANES), F32),
                   jax.ShapeDtypeStruct((EXPERT_TOPK, T), I32), jax.ShapeDtypeStruct((EXPERT_TOPK, T), F32)],
        compiler_params=pltpu.CompilerParams(
            dimension_semantics=("arbitrary",), vmem_limit_bytes=VMEM_LIMIT),
        name="outproj",
    )(oa, ob, x2d, ga, gb, woa, wob, fg, wrh, wrl, rb)


def _moe_kernel(be_ref, nu_ref, src_ref, srcn_ref, dst_ref, hf_hbm, wg_ref, wu_ref, wd_ref, out_hbm,
                xbuf, obuf, wgb, wub, wdb, gsem, ssem):
    i = pl.program_id(0)
    n_used = nu_ref[0]
    M = xbuf.shape[1] // ROW_TILE
    slot = i % 2

    def start_gather(idx_ref, buf_slot):
        for r in range(M):
            tok_row = pl.multiple_of(idx_ref[0, 0, r] * ROW_TILE, ROW_TILE)
            pltpu.make_async_copy(hf_hbm.at[pl.ds(tok_row, ROW_TILE)],
                                  xbuf.at[buf_slot, pl.ds(r * ROW_TILE, ROW_TILE)],
                                  gsem.at[buf_slot]).start(priority=r % DMA_PRIORITIES)

    def wait_rows(buf, buf_slot, sem):
        pltpu.make_async_copy(hf_hbm.at[pl.ds(0, M * ROW_TILE)], buf.at[buf_slot], sem.at[buf_slot]).wait()

    @pl.when(i < n_used)
    def _compute():
        @pl.when(i == 0)
        def _first_gather():
            start_gather(src_ref, 0)
            obuf[1] = jnp.zeros(obuf.shape[1:], F32)
            spare = pltpu.make_async_copy(
                obuf.at[1], out_hbm.at[pl.ds(out_hbm.shape[0] - M * ROW_TILE, M * ROW_TILE)], ssem.at[1])
            spare.start()
            spare.wait()

        @pl.when((i == 0) | (be_ref[i] != be_ref[jnp.maximum(i - 1, 0)]))
        def _cast_weights():
            wgb[...] = wg_ref[0].astype(BF16)
            wub[...] = wu_ref[0].astype(BF16)
            wdb[...] = wd_ref[0].astype(BF16)

        wait_rows(xbuf, slot, gsem)

        @pl.when(i >= 2)
        def _reuse_obuf():
            wait_rows(obuf, slot, ssem)

        start_gather(srcn_ref, 1 - slot)
        xb = _from_token_tiles(xbuf.at[slot], M).astype(BF16)
        a = jnp.dot(xb, wgb[...], preferred_element_type=F32)
        u = jnp.dot(xb, wub[...], preferred_element_type=F32)
        hmid = (a * jax.nn.sigmoid(a)) * u
        _to_token_tiles(obuf.at[slot], jnp.dot(hmid.astype(BF16), wdb[...], preferred_element_type=F32))
        for r in range(M):
            dst_row = pl.multiple_of(dst_ref[0, 0, r] * ROW_TILE, ROW_TILE)
            pltpu.make_async_copy(obuf.at[slot, pl.ds(r * ROW_TILE, ROW_TILE)],
                                  out_hbm.at[pl.ds(dst_row, ROW_TILE)],
                                  ssem.at[slot]).start(priority=r % DMA_PRIORITIES)

        @pl.when(i == n_used - 1)
        def _drain():
            wait_rows(xbuf, 1 - slot, gsem)
            wait_rows(obuf, slot, ssem)

            @pl.when(i >= 1)
            def _drain_prev():
                wait_rows(obuf, 1 - slot, ssem)


def _moe_call(block_expert, n_used, src_tok, dst_row, hf, w_g, w_u, w_d):
    T = hf.shape[0] // ROW_TILE
    M = DISPATCH_BLOCK
    nblk = block_expert.shape[0]
    D, DE = w_g.shape[1], w_g.shape[2]
    assert D == ROW_TILE * LANES and hf.shape[1] == LANES
    idx_spec = lambda f: pl.BlockSpec((1, 1, M), f, memory_space=pltpu.SMEM)
    cur = lambda i, be, nu: (i, 0, 0)
    nxt = lambda i, be, nu: (jnp.minimum(i + 1, nblk - 1), 0, 0)
    wspec = lambda a, b: pl.BlockSpec((1, a, b), lambda i, be, nu: (be[i], 0, 0))
    grid_spec = pltpu.PrefetchScalarGridSpec(
        num_scalar_prefetch=2,
        grid=(nblk,),
        in_specs=[idx_spec(cur), idx_spec(nxt), idx_spec(cur),
                  pl.BlockSpec(memory_space=pl.ANY),
                  wspec(D, DE), wspec(D, DE), wspec(DE, D)],
        out_specs=pl.BlockSpec(memory_space=pl.ANY),
        scratch_shapes=[pltpu.VMEM((2, M * ROW_TILE, LANES), F32), pltpu.VMEM((2, M * ROW_TILE, LANES), F32),
                        pltpu.VMEM((D, DE), BF16), pltpu.VMEM((D, DE), BF16), pltpu.VMEM((DE, D), BF16),
                        pltpu.SemaphoreType.DMA((2,)), pltpu.SemaphoreType.DMA((2,))],
    )
    src3 = src_tok.reshape(nblk, 1, M)
    return pl.pallas_call(
        _moe_kernel,
        grid_spec=grid_spec,
        out_shape=jax.ShapeDtypeStruct(((EXPERT_TOPK * T + M) * ROW_TILE, LANES), F32),
        compiler_params=pltpu.CompilerParams(
            dimension_semantics=("arbitrary",), vmem_limit_bytes=VMEM_LIMIT),
        name="moe",
    )(block_expert, n_used, src3, src3, dst_row.reshape(nblk, 1, M), hf, w_g, w_u, w_d)


def _ple_kernel(x1_ref, m0_ref, m1_ref, wt_ref, p_ref, pg_ref, wgate_ref, wproj_ref, pog_ref, out_ref):
    proj = _rmsnorm(jnp.dot(p_ref[...].astype(BF16), wproj_ref[...], preferred_element_type=F32),
                    pog_ref[...])
    tm = x1_ref.shape[0]
    moe = (_from_token_tiles(m0_ref, tm) * wt_ref[:, 0:1] + _from_token_tiles(m1_ref, tm) * wt_ref[:, 1:2])
    x2 = x1_ref[...] + moe
    hn = _rmsnorm(x2, pg_ref[...]).astype(BF16)
    gate = jax.nn.sigmoid(jnp.dot(hn, wgate_ref[...], preferred_element_type=F32))
    out_ref[...] = x2 + gate * proj


def _ple_call(moe_rows, x1, wt_cols, p2d, pg, wgate, wproj, pog):
    T, D = x1.shape
    tm = PLE_TILE
    tok = lambda w: pl.BlockSpec((tm, w), lambda i: (i, 0))
    full = lambda a: pl.BlockSpec(a.shape, lambda i: (0, 0))
    first = pl.BlockSpec((tm * ROW_TILE, LANES), lambda i: (i, 0))
    second = pl.BlockSpec((tm * ROW_TILE, LANES), lambda i: (T // tm + i, 0))
    return pl.pallas_call(
        _ple_kernel,
        grid=(T // tm,),
        in_specs=[tok(D), first, second, tok(EXPERT_TOPK), tok(p2d.shape[1]),
                  full(pg), full(wgate), full(wproj), full(pog)],
        out_specs=tok(D),
        out_shape=jax.ShapeDtypeStruct((T, D), F32),
        compiler_params=pltpu.CompilerParams(
            dimension_semantics=("arbitrary",), vmem_limit_bytes=VMEM_LIMIT),
        name="ple",
    )(x1, moe_rows, moe_rows, wt_cols, p2d, pg, wgate, wproj, pog)


def _dispatch_plan(eid, T):
    A = EXPERT_TOPK * T
    M = DISPATCH_BLOCK
    nblk = A // M + N_EXPERTS
    e_flat = eid.reshape(A)
    experts = jnp.arange(N_EXPERTS, dtype=I32)
    counts = jnp.sum((e_flat[:, None] == experts[None, :]).astype(I32), axis=0)
    start = jnp.cumsum(counts) - counts
    padded = (counts + M - 1) // M * M
    pad_end = jnp.cumsum(padded)
    pad_start = pad_end - padded
    order = jnp.argsort(e_flat).astype(I32)
    n_used = pad_end[-1] // M
    blk = jnp.arange(nblk, dtype=I32)
    be = jnp.sum((pad_end[None, :] <= (blk * M)[:, None]).astype(I32), axis=1)
    be = jnp.minimum(be, N_EXPERTS - 1)
    be = jnp.where(blk < n_used, be, be[jnp.maximum(n_used - 1, 0)])
    row = jnp.arange(M, dtype=I32)[None, :]
    off = (blk * M - pad_start[be])[:, None] + row
    valid = (blk < n_used)[:, None] & (off < counts[be][:, None])
    a_s = lax.optimization_barrier(order[jnp.clip(start[be][:, None] + off, 0, A - 1)])
    src_tok = jnp.where(valid, jnp.where(a_s >= T, a_s - T, a_s), 0)
    dst_row = jnp.where(valid, a_s, A + row)
    return be, n_used.reshape(1).astype(I32), src_tok, dst_row


def _layer(x2d, p2d, rel_bias, attn_norm, w_in, swa_q_norm, swa_k_norm, swa_sinks, moba_q_norm,
           moba_k_norm, swa_out_norm, moba_out_norm, w_out, ffn_norm, w_rg, b_rg, w_re, b_re,
           w_g, w_u, w_d, ple_norm, w_ple_gate, w_ple_proj, ple_out_norm, B, S):
    T, D = x2d.shape
    row = lambda v: v.reshape(1, -1).astype(F32)
    head_gain = jnp.concatenate([
        jnp.tile(swa_q_norm, SWA_Q_HEADS), jnp.tile(swa_k_norm, SWA_KV_HEADS),
        jnp.tile(moba_q_norm, MOBA_Q_HEADS), jnp.tile(moba_k_norm, MOBA_KV_HEADS)])
    w_in_b = w_in.astype(BF16)
    qa_w, kv_w = SWA_Q_HEADS * HEAD_DIM, SWA_KV_HEADS * HEAD_DIM
    qb_w = MOBA_Q_HEADS * HEAD_DIM
    c_va = qa_w + kv_w
    c_qb = c_va + kv_w
    c_vb = c_qb + qb_w + kv_w
    w_qk = jnp.concatenate([w_in_b[:, :c_va], w_in_b[:, c_qb:c_vb]], axis=1)
    w_vt = jnp.concatenate([w_in_b[:, c_va:c_qb], w_in_b[:, c_vb:]], axis=1).T
    qa, ka, qb, kb, vat, vbt = _qkv_call(x2d, row(attn_norm), w_qk, w_vt, row(head_gain))

    tbl_flat = rel_bias.astype(F32).reshape(-1)
    oa = _swa_call(tbl_flat, swa_sinks.astype(F32), qa, ka, vat, B, S)
    ob = _moba_call(tbl_flat, qb, kb, vbt, B, S)

    pad_rows = lambda n: jnp.zeros((n, D), F32)
    wr = jnp.concatenate([w_rg.T, pad_rows(EXPERT_ROW0 - N_GROUPS), w_re.T,
                          pad_rows(ROUTER_ROWS - EXPERT_ROW0 - N_EXPERTS)], axis=0)
    wr_hi = wr.astype(BF16)
    wr_lo = (wr - wr_hi.astype(F32)).astype(BF16)
    rb = jnp.concatenate([b_rg, jnp.zeros((EXPERT_ROW0 - N_GROUPS,), F32), b_re,
                          jnp.zeros((ROUTER_ROWS - EXPERT_ROW0 - N_EXPERTS,), F32)]).reshape(ROUTER_ROWS, 1)
    wo = w_out.astype(BF16)
    na_w = SWA_Q_HEADS * HEAD_DIM
    x1, hf, eid, wts = _outproj_call(oa, ob, x2d, row(swa_out_norm), row(moba_out_norm),
                                     wo[:na_w], wo[na_w:], row(ffn_norm), wr_hi, wr_lo, rb)

    be, n_used, src_tok, dst_row = _dispatch_plan(eid, T)
    moe_rows = _moe_call(be, n_used, src_tok, dst_row, hf, w_g, w_u, w_d)
    return _ple_call(moe_rows, x1, wts.T, p2d, row(ple_norm), w_ple_gate.astype(BF16),
                     w_ple_proj.astype(BF16), row(ple_out_norm))


def kernel(x, p, rel_bias, attn_norm, w_in, swa_q_norm, swa_k_norm, swa_sinks, moba_q_norm, moba_k_norm,
           swa_out_norm, moba_out_norm, w_out, ffn_norm, w_router_group, b_router_group, w_router_expert,
           b_router_expert, w_exp_gate, w_exp_up, w_exp_down, ple_norm, w_ple_gate, w_ple_proj, ple_out_norm):
    B, S, D = x.shape
    x2d = x.reshape(B * S, D)
    for i in range(p.shape[0]):
        x2d = _layer(x2d, p[i].reshape(B * S, -1), rel_bias, attn_norm[i], w_in[i], swa_q_norm[i],
                     swa_k_norm[i], swa_sinks[i], moba_q_norm[i], moba_k_norm[i], swa_out_norm[i],
                     moba_out_norm[i], w_out[i], ffn_norm[i], w_router_group[i], b_router_group[i],
                     w_router_expert[i], b_router_expert[i], w_exp_gate[i], w_exp_up[i], w_exp_down[i],
                     ple_norm[i], w_ple_gate[i], w_ple_proj[i], ple_out_norm[i], B, S)
    return x2d.reshape(B, S, D)
```

```python
import math

import numpy as np
import jax
import jax.numpy as jnp
from jax import lax
from jax.experimental import pallas as pl
from jax.experimental.pallas import tpu as pltpu

F32 = jnp.float32
BF16 = jnp.bfloat16
I32 = jnp.int32

HEAD_DIM = 64
SWA_Q_HEADS = 8
SWA_KV_HEADS = 2
SWA_WINDOW = 128
MOBA_Q_HEADS = 8
MOBA_KV_HEADS = 2
MOBA_BLOCK = 256
MOBA_TOPK = 3
N_HEADS = SWA_Q_HEADS + MOBA_Q_HEADS
REL_BUCKETS = 32
REL_MAX_DIST = 128
N_GROUPS = 4
EXPERTS_PER_GROUP = 8
N_EXPERTS = N_GROUPS * EXPERTS_PER_GROUP
EXPERT_TOPK = 2
DISPATCH_BLOCK = 256
EPS = 1e-6
NEG = -1e30

DMA_PRIORITIES = 2
LANES = 128
ROW_TILE = 8
BF16_SUBLANES = 16
GROUP_HEADS = 4
GROUP_W = GROUP_HEADS * HEAD_DIM
ROUTER_ROWS = 128
EXPERT_ROW0 = 8
TOKEN_TILE = 512
PLE_TILE = 512
VMEM_LIMIT = 52 * 1024 * 1024


def _dot_nt(a, b):
    return lax.dot_general(a, b, (((1,), (1,)), ((), ())), preferred_element_type=F32)


def _rmsnorm(x, g):
    ms = jnp.mean(x * x, axis=-1, keepdims=True)
    return x * lax.rsqrt(ms + EPS) * g


def _to_token_tiles(ref, x):
    m = x.shape[0]
    for j in range(x.shape[1] // LANES):
        ref[pl.ds(j, m, stride=ROW_TILE), :] = x[:, LANES * j:LANES * (j + 1)]


def _from_token_tiles(ref, m):
    return jnp.concatenate([ref[pl.ds(j, m, stride=ROW_TILE), :] for j in range(ref.shape[0] // m)], axis=1)


def _rel_bucket_np(dist):
    n = np.maximum(dist, 0)
    exact = REL_BUCKETS // 2
    nf = np.maximum(n, 1).astype(np.float32)
    large = exact + (np.log(nf / exact) / math.log(REL_MAX_DIST / exact)
                     * (REL_BUCKETS - exact)).astype(np.int32)
    return np.where(n < exact, n, np.minimum(large, REL_BUCKETS - 1)).astype(np.int32)


def _band_buckets(block):
    qi = np.arange(block)[:, None]
    kj = np.arange(2 * block)[None, :]
    return _rel_bucket_np(qi + block - kj)


def _bias_from_buckets(bkt, tbl_ref, head):
    acc = jnp.zeros(bkt.shape, F32)
    for j in range(REL_BUCKETS):
        acc = jnp.where(bkt == j, tbl_ref[j * N_HEADS + head], acc)
    return acc


def _qkv_kernel(x_ref, g_ref, w_ref, wvt_ref, hg_ref, qa_ref, ka_ref, qb_ref, kb_ref, vat_ref, vbt_ref):
    h = _rmsnorm(x_ref[...], g_ref[...]).astype(BF16)
    acc = jnp.dot(h, w_ref[...], preferred_element_type=F32)
    lo = lax.broadcasted_iota(I32, (1, LANES), 1) < HEAD_DIM

    def head_normed(c, scale):
        blk = acc[:, LANES * c:LANES * (c + 1)]
        sq = blk * blk
        s_lo = jnp.sum(jnp.where(lo, sq, 0.0), axis=-1, keepdims=True)
        s_hi = jnp.sum(jnp.where(lo, 0.0, sq), axis=-1, keepdims=True)
        inv = jnp.where(lo, lax.rsqrt(s_lo / HEAD_DIM + EPS), lax.rsqrt(s_hi / HEAD_DIM + EPS))
        return blk * inv * hg_ref[:, LANES * c:LANES * (c + 1)] * scale

    scale = HEAD_DIM ** -0.5
    for c in range(4):
        qa_ref[:, LANES * c:LANES * (c + 1)] = head_normed(c, scale).astype(BF16)
        qb_ref[:, LANES * c:LANES * (c + 1)] = head_normed(5 + c, scale).astype(BF16)
    ka_ref[...] = head_normed(4, 1.0).astype(BF16)
    kb_ref[...] = head_normed(9, 1.0).astype(BF16)
    vt = _dot_nt(wvt_ref[...], h).astype(BF16)
    for j in range(vat_ref.shape[0]):
        vat_ref[j] = vt[:LANES, SWA_WINDOW * j:SWA_WINDOW * (j + 1)]
    for j in range(vbt_ref.shape[0]):
        vbt_ref[j] = vt[LANES:, MOBA_BLOCK * j:MOBA_BLOCK * (j + 1)]


def _qkv_call(x2d, attn_g, w_qk, w_vt, head_gain):
    T, D = x2d.shape
    tm = TOKEN_TILE
    tok = lambda w: pl.BlockSpec((tm, w), lambda i: (i, 0))
    full = lambda a: pl.BlockSpec(a.shape, lambda i: (0, 0))
    out_w = (SWA_Q_HEADS * HEAD_DIM, LANES, MOBA_Q_HEADS * HEAD_DIM, LANES)
    slabs = lambda blk: pl.BlockSpec((tm // blk, LANES, blk), lambda i: (i, 0, 0))
    slab_shape = lambda blk: jax.ShapeDtypeStruct((T // blk, LANES, blk), BF16)
    return pl.pallas_call(
        _qkv_kernel,
        grid=(T // tm,),
        in_specs=[tok(D), full(attn_g), full(w_qk), full(w_vt), full(head_gain)],
        out_specs=[tok(w) for w in out_w] + [slabs(SWA_WINDOW), slabs(MOBA_BLOCK)],
        out_shape=[jax.ShapeDtypeStruct((T, w), BF16) for w in out_w]
        + [slab_shape(SWA_WINDOW), slab_shape(MOBA_BLOCK)],
        compiler_params=pltpu.CompilerParams(
            dimension_semantics=("arbitrary",), vmem_limit_bytes=VMEM_LIMIT),
        name="qkv",
    )(x2d, attn_g, w_qk, w_vt, head_gain)


def _swa_kernel(tbl_ref, sink_ref, bkt_ref, q_ref, k_ref, vt_ref, o_ref, bias_scr, kg_scr):
    W = SWA_WINDOW
    R = GROUP_HEADS
    G = SWA_KV_HEADS
    S = q_ref.shape[0]

    @pl.when(pl.program_id(0) == 0)
    def _init_bias():
        bkt = bkt_ref[...]
        for h in range(SWA_Q_HEADS):
            g, r = divmod(h, R)
            bias_scr[g, :, W * r:W * (r + 1)] = _bias_from_buckets(bkt, tbl_ref, h)

    key_i = lax.broadcasted_iota(I32, (2 * W, R * W), 0)
    qry_i = lax.broadcasted_iota(I32, (2 * W, R * W), 1) & (W - 1)
    dist = qry_i + W - key_i
    band = (dist >= 0) & (dist < W)
    own_part = key_i >= W
    head_of_lane = lax.broadcasted_iota(I32, (1, R * W), 1) // W
    sinks = []
    for g in range(G):
        kg_scr[g] = k_ref[:, HEAD_DIM * g:HEAD_DIM * (g + 1)]
        row = jnp.zeros((1, R * W), F32)
        for r in range(R):
            row = jnp.where(head_of_lane == r, sink_ref[R * g + r], row)
        sinks.append(row)

    def body(i, carry):
        r0 = pl.multiple_of(i * W, W)
        prev = jnp.maximum(i - 1, 0)
        p0 = pl.multiple_of(prev * W, W)
        scores = []
        for g in range(G):
            qblk = q_ref[pl.ds(r0, W), GROUP_W * g:GROUP_W * (g + 1)]
            q4 = jnp.concatenate([qblk[:, HEAD_DIM * r:HEAD_DIM * (r + 1)] for r in range(R)], axis=0)
            kband = jnp.concatenate([kg_scr[g, pl.ds(p0, W), :], kg_scr[g, pl.ds(r0, W), :]], axis=0)
            scores.append(_dot_nt(kband, q4))
        mask = band & (own_part | (i > 0))
        for g in range(G):
            s = jnp.where(mask, scores[g] + bias_scr[g], NEG)
            m = jnp.maximum(jnp.max(s, axis=0, keepdims=True), sinks[g])
            e = jnp.exp(s - m)
            den = jnp.sum(e, axis=0, keepdims=True) + jnp.exp(sinks[g] - m)
            feat = slice(HEAD_DIM * g, HEAD_DIM * (g + 1))
            vtband = jnp.concatenate([vt_ref[prev, feat, :], vt_ref[i, feat, :]], axis=1)
            ot = jnp.dot(vtband, e.astype(BF16), preferred_element_type=F32) / den
            for pr in range(R // 2):
                pair = ot[:, 2 * W * pr:2 * W * (pr + 1)]
                c0 = GROUP_W * g + LANES * pr
                o_ref[pl.ds(r0, W), c0:c0 + LANES] = jnp.concatenate([pair[:, :W], pair[:, W:]], axis=0).T
        return carry

    lax.fori_loop(0, S // W, body, 0)


def _swa_call(tbl_flat, sinks, qa, ka, vat, B, S):
    W = SWA_WINDOW
    bkt = jnp.asarray(np.ascontiguousarray(_band_buckets(W).T))
    smem = pl.BlockSpec(memory_space=pltpu.SMEM)
    return pl.pallas_call(
        _swa_kernel,
        grid=(B,),
        in_specs=[smem, smem,
                  pl.BlockSpec(bkt.shape, lambda b: (0, 0)),
                  pl.BlockSpec((S, qa.shape[1]), lambda b: (b, 0)),
                  pl.BlockSpec((S, LANES), lambda b: (b, 0)),
                  pl.BlockSpec((S // W, LANES, W), lambda b: (b, 0, 0))],
        out_specs=pl.BlockSpec((S, qa.shape[1]), lambda b: (b, 0)),
        out_shape=jax.ShapeDtypeStruct(qa.shape, F32),
        scratch_shapes=[pltpu.VMEM((SWA_KV_HEADS, 2 * W, GROUP_HEADS * W), F32),
                        pltpu.VMEM((SWA_KV_HEADS, S, HEAD_DIM), BF16)],
        compiler_params=pltpu.CompilerParams(
            dimension_semantics=("arbitrary",), vmem_limit_bytes=VMEM_LIMIT),
        name="swa",
    )(tbl_flat, sinks, bkt, qa, ka, vat)


def _moba_kernel(tbl_ref, bkt_ref, q_ref, k_ref, vt_ref, o_ref,
                 bias_scr, kg_scr, km_scr, q4_scr, sel_scr, m_scr, l_scr, acc_scr):
    BS = MOBA_BLOCK
    R = GROUP_HEADS
    G = MOBA_KV_HEADS
    S = q_ref.shape[0]
    NBK = S // BS
    ROWS = R * BS

    @pl.when(pl.program_id(0) == 0)
    def _init_bias():
        bkt = bkt_ref[...]
        for h in range(MOBA_Q_HEADS):
            g, r = divmod(h, R)
            band = _bias_from_buckets(bkt, tbl_ref, SWA_Q_HEADS + h)
            far = tbl_ref[(REL_BUCKETS - 1) * N_HEADS + SWA_Q_HEADS + h]
            bias_scr[g, 0, :, BS * r:BS * (r + 1)] = jnp.full((BS, BS), far, F32)
            bias_scr[g, 1, :, BS * r:BS * (r + 1)] = band[:BS]
            bias_scr[g, 2, :, BS * r:BS * (r + 1)] = band[BS:]

    blk_row = lax.broadcasted_iota(I32, (BF16_SUBLANES, 1), 0)
    for g in range(G):
        kg_scr[g] = k_ref[:, HEAD_DIM * g:HEAD_DIM * (g + 1)]
        kmean = jnp.sum(kg_scr[g].astype(F32).reshape(NBK, BS, HEAD_DIM), axis=1) / BS
        kmean = jnp.concatenate([kmean, jnp.zeros((BF16_SUBLANES - NBK, HEAD_DIM), F32)], axis=0)
        km_hi = kmean.astype(BF16)
        km_scr[g, 0] = km_hi
        km_scr[g, 1] = (kmean - km_hi.astype(F32)).astype(BF16)

    def scores(n):
        k0 = pl.multiple_of(n * BS, BS)
        return [_dot_nt(kg_scr[g, pl.ds(k0, BS), :], q4_scr[g]) for g in range(G)]

    def values_t(n, g):
        return vt_ref[n, HEAD_DIM * g:HEAD_DIM * (g + 1), :]

    def qblock(mi, carry):
        r0 = pl.multiple_of(mi * BS, BS)
        past = blk_row < mi
        for g in range(G):
            qblk = q_ref[pl.ds(r0, BS), GROUP_W * g:GROUP_W * (g + 1)]
            q4 = jnp.concatenate([qblk[:, HEAD_DIM * r:HEAD_DIM * (r + 1)] for r in range(R)], axis=0)
            q4_scr[g] = q4
            gate = _dot_nt(km_scr[g, 0], q4) + _dot_nt(km_scr[g, 1], q4)
            gm = jnp.where(past, gate, NEG)
            cnt = jnp.zeros(gm.shape, F32)
            for n in range(NBK - 1):
                col = gm[n:n + 1, :]
                beats = (gm > col) | ((gm == col) & (blk_row < n))
                c = jnp.sum(jnp.where(beats, 1.0, 0.0), axis=0, keepdims=True)
                cnt = jnp.where(blk_row == n, c, cnt)
            sel_scr[g] = jnp.where(past & (cnt < MOBA_TOPK), 1.0, 0.0)

        key_i = lax.broadcasted_iota(I32, (BS, ROWS), 0)
        qry_i = lax.broadcasted_iota(I32, (BS, ROWS), 1) & (BS - 1)
        causal = key_i <= qry_i
        sc = scores(mi)
        for g in range(G):
            s = jnp.where(causal, sc[g] + bias_scr[g, 2], NEG)
            m0 = jnp.max(s, axis=0, keepdims=True)
            e = jnp.exp(s - m0)
            m_scr[g] = m0
            l_scr[g] = jnp.sum(e, axis=0, keepdims=True)
            acc_scr[g] = jnp.dot(values_t(mi, g), e.astype(BF16), preferred_element_type=F32)

        def kvblock(j, c2):
            n = mi - j
            case = jnp.where(j == 1, 1, 0)
            sc = scores(n)
            for g in range(G):
                s = jnp.where(sel_scr[g, pl.ds(n, 1), :] > 0.5, sc[g] + bias_scr[g, case], NEG)
                m_prev = m_scr[g]
                m_new = jnp.maximum(m_prev, jnp.max(s, axis=0, keepdims=True))
                alpha = jnp.exp(m_prev - m_new)
                e = jnp.exp(s - m_new)
                l_scr[g] = alpha * l_scr[g] + jnp.sum(e, axis=0, keepdims=True)
                acc_scr[g] = alpha * acc_scr[g] + jnp.dot(
                    values_t(n, g), e.astype(BF16), preferred_element_type=F32)
                m_scr[g] = m_new
            return c2

        lax.fori_loop(1, mi + 1, kvblock, 0)
        for g in range(G):
            ot = acc_scr[g] / l_scr[g]
            for pr in range(R // 2):
                pair = ot[:, 2 * BS * pr:2 * BS * (pr + 1)]
                c0 = GROUP_W * g + LANES * pr
                o_ref[pl.ds(r0, BS), c0:c0 + LANES] = jnp.concatenate([pair[:, :BS], pair[:, BS:]], axis=0).T
        return carry

    lax.fori_loop(0, NBK, qblock, 0)


def _moba_call(tbl_flat, qb, kb, vbt, B, S):
    BS = MOBA_BLOCK
    R = GROUP_HEADS
    G = MOBA_KV_HEADS
    NBK = S // BS
    bkt = jnp.asarray(np.ascontiguousarray(_band_buckets(BS).T))
    smem = pl.BlockSpec(memory_space=pltpu.SMEM)
    return pl.pallas_call(
        _moba_kernel,
        grid=(B,),
        in_specs=[smem,
                  pl.BlockSpec(bkt.shape, lambda b: (0, 0)),
                  pl.BlockSpec((S, qb.shape[1]), lambda b: (b, 0)),
                  pl.BlockSpec((S, LANES), lambda b: (b, 0)),
                  pl.BlockSpec((NBK, LANES, BS), lambda b: (b, 0, 0))],
        out_specs=pl.BlockSpec((S, qb.shape[1]), lambda b: (b, 0)),
        out_shape=jax.ShapeDtypeStruct(qb.shape, F32),
        scratch_shapes=[pltpu.VMEM((G, 3, BS, R * BS), F32),
                        pltpu.VMEM((G, S, HEAD_DIM), BF16),
                        pltpu.VMEM((G, 2, BF16_SUBLANES, HEAD_DIM), BF16),
                        pltpu.VMEM((G, R * BS, HEAD_DIM), BF16),
                        pltpu.VMEM((G, BF16_SUBLANES, R * BS), F32),
                        pltpu.VMEM((G, 1, R * BS), F32),
                        pltpu.VMEM((G, 1, R * BS), F32),
                        pltpu.VMEM((G, HEAD_DIM, R * BS), F32)],
        compiler_params=pltpu.CompilerParams(
            dimension_semantics=("arbitrary",), vmem_limit_bytes=VMEM_LIMIT),
        name="moba",
    )(tbl_flat, bkt, qb, kb, vbt)


def _outproj_kernel(oa_ref, ob_ref, x_ref, ga_ref, gb_ref, woa_ref, wob_ref, fg_ref,
                    wrh_ref, wrl_ref, rb_ref, x1_ref, hf_ref, eid_ref, wt_ref):
    na = _rmsnorm(oa_ref[...], ga_ref[...]).astype(BF16)
    nb = _rmsnorm(ob_ref[...], gb_ref[...]).astype(BF16)
    y = (jnp.dot(na, woa_ref[...], preferred_element_type=F32)
         + jnp.dot(nb, wob_ref[...], preferred_element_type=F32))
    x1 = x_ref[...] + y
    x1_ref[...] = x1
    hf = _rmsnorm(x1, fg_ref[...])
    _to_token_tiles(hf_ref, hf)

    hf_hi = hf.astype(BF16)
    hf_lo = (hf - hf_hi.astype(F32)).astype(BF16)
    lt = (_dot_nt(wrh_ref[...], hf_hi) + _dot_nt(wrh_ref[...], hf_lo)
          + _dot_nt(wrl_ref[...], hf_hi) + rb_ref[...])
    gl = [lt[j:j + 1, :] for j in range(N_GROUPS)]
    best = gl[0]
    gsel = jnp.zeros(best.shape, I32)
    for j in range(1, N_GROUPS):
        better = gl[j] > best
        gsel = jnp.where(better, j, gsel)
        best = jnp.where(better, gl[j], best)
    gsum = jnp.zeros(best.shape, F32)
    for j in range(N_GROUPS):
        gsum = gsum + jnp.exp(gl[j] - best)
    gw = 1.0 / gsum
    E = EXPERTS_PER_GROUP
    es = lt[EXPERT_ROW0:EXPERT_ROW0 + E, :]
    for j in range(1, N_GROUPS):
        es = jnp.where(gsel == j, lt[EXPERT_ROW0 + E * j:EXPERT_ROW0 + E * (j + 1), :], es)
    row = lax.broadcasted_iota(I32, es.shape, 0)
    v1 = jnp.max(es, axis=0, keepdims=True)
    i1 = jnp.min(jnp.where(es == v1, row, E), axis=0, keepdims=True)
    es2 = jnp.where(row == i1, -jnp.inf, es)
    v2 = jnp.max(es2, axis=0, keepdims=True)
    i2 = jnp.min(jnp.where(es2 == v2, row, E), axis=0, keepdims=True)
    e2 = jnp.exp(v2 - v1)
    den = 1.0 + e2
    eid_ref[...] = jnp.concatenate([gsel * E + i1, gsel * E + i2], axis=0)
    wt_ref[...] = jnp.concatenate([gw * (1.0 / den), gw * (e2 / den)], axis=0)


def _outproj_call(oa, ob, x2d, ga, gb, woa, wob, fg, wrh, wrl, rb):
    T, D = x2d.shape
    tm = TOKEN_TILE
    tok = lambda w: pl.BlockSpec((tm, w), lambda i: (i, 0))
    full = lambda a: pl.BlockSpec(a.shape, lambda i: (0, 0))
    col = pl.BlockSpec((EXPERT_TOPK, tm), lambda i: (0, i))
    return pl.pallas_call(
        _outproj_kernel,
        grid=(T // tm,),
        in_specs=[tok(oa.shape[1]), tok(ob.shape[1]), tok(D), full(ga), full(gb), full(woa), full(wob),
                  full(fg), full(wrh), full(wrl), full(rb)],
        out_specs=[tok(D), pl.BlockSpec((tm * ROW_TILE, LANES), lambda i: (i, 0)), col, col],
        out_shape=[jax.ShapeDtypeStruct((T, D), F32), jax.ShapeDtypeStruct((T * ROW_TILE, LANES), F32),
                   jax.ShapeDtypeStruct((EXPERT_TOPK, T), I32), jax.ShapeDtypeStruct((EXPERT_TOPK, T), F32)],
        compiler_params=pltpu.CompilerParams(
            dimension_semantics=("arbitrary",), vmem_limit_bytes=VMEM_LIMIT),
        name="outproj",
    )(oa, ob, x2d, ga, gb, woa, wob, fg, wrh, wrl, rb)


MOE_PHASES = 4


def _moe_kernel(be_ref, nu_ref, src_ref, srcn_ref, dst_ref, dstp_ref, hf_hbm, wg_ref, wu_ref, wd_ref, out_hbm,
                xbuf, obuf, wgb, wub, wdb, xb_scr, a_scr, u_scr, h_scr, gsem, ssem):
    i = pl.program_id(0)
    n_used = nu_ref[0]
    M = xbuf.shape[1] // ROW_TILE
    slot = i % 2
    share = M // MOE_PHASES

    def gather_rows(idx_ref, buf_slot, rows):
        for r in rows:
            tok_row = pl.multiple_of(idx_ref[0, 0, r] * ROW_TILE, ROW_TILE)
            pltpu.make_async_copy(hf_hbm.at[pl.ds(tok_row, ROW_TILE)],
                                  xbuf.at[buf_slot, pl.ds(r * ROW_TILE, ROW_TILE)],
                                  gsem.at[buf_slot]).start(priority=r % DMA_PRIORITIES)

    def scatter_rows(idx_ref, buf_slot, rows):
        for r in rows:
            dst_row = pl.multiple_of(idx_ref[0, 0, r] * ROW_TILE, ROW_TILE)
            pltpu.make_async_copy(obuf.at[buf_slot, pl.ds(r * ROW_TILE, ROW_TILE)],
                                  out_hbm.at[pl.ds(dst_row, ROW_TILE)],
                                  ssem.at[buf_slot]).start(priority=r % DMA_PRIORITIES)

    def wait_rows(buf, buf_slot, sem):
        pltpu.make_async_copy(hf_hbm.at[pl.ds(0, M * ROW_TILE)], buf.at[buf_slot], sem.at[buf_slot]).wait()

    def copies_after_piece(ph):
        rows = range(share * ph, share * (ph + 1))
        gather_rows(srcn_ref, 1 - slot, rows)

        @pl.when(i >= 1)
        def _prev_scatter():
            scatter_rows(dstp_ref, 1 - slot, rows)

    @pl.when(i < n_used)
    def _compute():
        @pl.when(i == 0)
        def _first_gather():
            gather_rows(src_ref, 0, range(M))
            obuf[1] = jnp.zeros(obuf.shape[1:], F32)
            spare = pltpu.make_async_copy(
                obuf.at[1], out_hbm.at[pl.ds(out_hbm.shape[0] - M * ROW_TILE, M * ROW_TILE)], ssem.at[1])
            spare.start()
            spare.wait()

        @pl.when((i == 0) | (be_ref[i] != be_ref[jnp.maximum(i - 1, 0)]))
        def _cast_weights():
            wgb[...] = wg_ref[0].astype(BF16)
            wub[...] = wu_ref[0].astype(BF16)
            wdb[...] = wd_ref[0].astype(BF16)

        wait_rows(xbuf, slot, gsem)

        @pl.when(i >= 2)
        def _reuse_obuf():
            wait_rows(obuf, slot, ssem)

        xb_scr[...] = _from_token_tiles(xbuf.at[slot], M).astype(BF16)
        a_scr[...] = jnp.dot(xb_scr[...], wgb[...], preferred_element_type=F32)
        copies_after_piece(0)
        u_scr[...] = jnp.dot(xb_scr[...], wub[...], preferred_element_type=F32)
        copies_after_piece(1)
        a = a_scr[...]
        h_scr[...] = ((a * jax.nn.sigmoid(a)) * u_scr[...]).astype(BF16)
        copies_after_piece(2)
        _to_token_tiles(obuf.at[slot], jnp.dot(h_scr[...], wdb[...], preferred_element_type=F32))
        copies_after_piece(3)

        @pl.when(i == n_used - 1)
        def _drain():
            scatter_rows(dst_ref, slot, range(M))
            wait_rows(xbuf, 1 - slot, gsem)
            wait_rows(obuf, slot, ssem)

            @pl.when(i >= 1)
            def _drain_prev():
                wait_rows(obuf, 1 - slot, ssem)


def _moe_call(block_expert, n_used, src_tok, dst_row, hf, w_g, w_u, w_d):
    T = hf.shape[0] // ROW_TILE
    M = DISPATCH_BLOCK
    nblk = block_expert.shape[0]
    D, DE = w_g.shape[1], w_g.shape[2]
    assert D == ROW_TILE * LANES and hf.shape[1] == LANES and M % MOE_PHASES == 0
    idx_spec = lambda f: pl.BlockSpec((1, 1, M), f, memory_space=pltpu.SMEM)
    cur = lambda i, be, nu: (i, 0, 0)
    nxt = lambda i, be, nu: (jnp.minimum(i + 1, nblk - 1), 0, 0)
    prv = lambda i, be, nu: (jnp.maximum(i - 1, 0), 0, 0)
    wspec = lambda a, b: pl.BlockSpec((1, a, b), lambda i, be, nu: (be[i], 0, 0))
    grid_spec = pltpu.PrefetchScalarGridSpec(
        num_scalar_prefetch=2,
        grid=(nblk,),
        in_specs=[idx_spec(cur), idx_spec(nxt), idx_spec(cur), idx_spec(prv),
                  pl.BlockSpec(memory_space=pl.ANY),
                  wspec(D, DE), wspec(D, DE), wspec(DE, D)],
        out_specs=pl.BlockSpec(memory_space=pl.ANY),
        scratch_shapes=[pltpu.VMEM((2, M * ROW_TILE, LANES), F32), pltpu.VMEM((2, M * ROW_TILE, LANES), F32),
                        pltpu.VMEM((D, DE), BF16), pltpu.VMEM((D, DE), BF16), pltpu.VMEM((DE, D), BF16),
                        pltpu.VMEM((M, D), BF16), pltpu.VMEM((M, DE), F32), pltpu.VMEM((M, DE), F32),
                        pltpu.VMEM((M, DE), BF16),
                        pltpu.SemaphoreType.DMA((2,)), pltpu.SemaphoreType.DMA((2,))],
    )
    src3 = src_tok.reshape(nblk, 1, M)
    dst3 = dst_row.reshape(nblk, 1, M)
    return pl.pallas_call(
        _moe_kernel,
        grid_spec=grid_spec,
        out_shape=jax.ShapeDtypeStruct(((EXPERT_TOPK * T + M) * ROW_TILE, LANES), F32),
        compiler_params=pltpu.CompilerParams(
            dimension_semantics=("arbitrary",), vmem_limit_bytes=VMEM_LIMIT),
        name="moe",
    )(block_expert, n_used, src3, src3, dst3, dst3, hf, w_g, w_u, w_d)


def _ple_kernel(x1_ref, m0_ref, m1_ref, wt_ref, p_ref, pg_ref, wgate_ref, wproj_ref, pog_ref, out_ref):
    proj = _rmsnorm(jnp.dot(p_ref[...].astype(BF16), wproj_ref[...], preferred_element_type=F32),
                    pog_ref[...])
    tm = x1_ref.shape[0]
    moe = (_from_token_tiles(m0_ref, tm) * wt_ref[:, 0:1] + _from_token_tiles(m1_ref, tm) * wt_ref[:, 1:2])
    x2 = x1_ref[...] + moe
    hn = _rmsnorm(x2, pg_ref[...]).astype(BF16)
    gate = jax.nn.sigmoid(jnp.dot(hn, wgate_ref[...], preferred_element_type=F32))
    out_ref[...] = x2 + gate * proj


def _ple_call(moe_rows, x1, wt_cols, p2d, pg, wgate, wproj, pog):
    T, D = x1.shape
    tm = PLE_TILE
    tok = lambda w: pl.BlockSpec((tm, w), lambda i: (i, 0))
    full = lambda a: pl.BlockSpec(a.shape, lambda i: (0, 0))
    first = pl.BlockSpec((tm * ROW_TILE, LANES), lambda i: (i, 0))
    second = pl.BlockSpec((tm * ROW_TILE, LANES), lambda i: (T // tm + i, 0))
    return pl.pallas_call(
        _ple_kernel,
        grid=(T // tm,),
        in_specs=[tok(D), first, second, tok(EXPERT_TOPK), tok(p2d.shape[1]),
                  full(pg), full(wgate), full(wproj), full(pog)],
        out_specs=tok(D),
        out_shape=jax.ShapeDtypeStruct((T, D), F32),
        compiler_params=pltpu.CompilerParams(
            dimension_semantics=("arbitrary",), vmem_limit_bytes=VMEM_LIMIT),
        name="ple",
    )(x1, moe_rows, moe_rows, wt_cols, p2d, pg, wgate, wproj, pog)


def _dispatch_plan(eid, T):
    A = EXPERT_TOPK * T
    M = DISPATCH_BLOCK
    nblk = A // M + N_EXPERTS
    e_flat = eid.reshape(A)
    experts = jnp.arange(N_EXPERTS, dtype=I32)
    counts = jnp.sum((e_flat[:, None] == experts[None, :]).astype(I32), axis=0)
    padded = (counts + M - 1) // M * M
    n_used = jnp.sum(padded) // M
    row = jnp.arange(M, dtype=I32)[None, :]
    pad_keys = jnp.where(row < (padded - counts)[:, None], 2 * experts[:, None] + 1, 2 * N_EXPERTS)
    keys = jnp.concatenate([2 * e_flat, pad_keys.reshape(-1)])
    vals = jnp.concatenate([jnp.arange(A, dtype=I32), jnp.full((N_EXPERTS * M,), -1, I32)])
    keys, vals = lax.sort((keys, vals), num_keys=1, is_stable=True)
    keys = keys.reshape(nblk, M)
    a_s = vals.reshape(nblk, M)
    blk = jnp.arange(nblk, dtype=I32)
    be = jnp.minimum(keys[:, 0] // 2, N_EXPERTS - 1)
    be = jnp.where(blk < n_used, be, be[jnp.maximum(n_used - 1, 0)])
    valid = a_s >= 0
    src_tok = jnp.where(valid, jnp.where(a_s >= T, a_s - T, a_s), 0)
    dst_row = jnp.where(valid, a_s, A + row)
    return be, n_used.reshape(1).astype(I32), src_tok, dst_row


def _layer(x2d, p2d, rel_bias, attn_norm, w_in, swa_q_norm, swa_k_norm, swa_sinks, moba_q_norm,
           moba_k_norm, swa_out_norm, moba_out_norm, w_out, ffn_norm, w_rg, b_rg, w_re, b_re,
           w_g, w_u, w_d, ple_norm, w_ple_gate, w_ple_proj, ple_out_norm, B, S):
    T, D = x2d.shape
    row = lambda v: v.reshape(1, -1).astype(F32)
    head_gain = jnp.concatenate([
        jnp.tile(swa_q_norm, SWA_Q_HEADS), jnp.tile(swa_k_norm, SWA_KV_HEADS),
        jnp.tile(moba_q_norm, MOBA_Q_HEADS), jnp.tile(moba_k_norm, MOBA_KV_HEADS)])
    w_in_b = w_in.astype(BF16)
    qa_w, kv_w = SWA_Q_HEADS * HEAD_DIM, SWA_KV_HEADS * HEAD_DIM
    qb_w = MOBA_Q_HEADS * HEAD_DIM
    c_va = qa_w + kv_w
    c_qb = c_va + kv_w
    c_vb = c_qb + qb_w + kv_w
    w_qk = jnp.concatenate([w_in_b[:, :c_va], w_in_b[:, c_qb:c_vb]], axis=1)
    w_vt = jnp.concatenate([w_in_b[:, c_va:c_qb], w_in_b[:, c_vb:]], axis=1).T
    qa, ka, qb, kb, vat, vbt = _qkv_call(x2d, row(attn_norm), w_qk, w_vt, row(head_gain))

    tbl_flat = rel_bias.astype(F32).reshape(-1)
    oa = _swa_call(tbl_flat, swa_sinks.astype(F32), qa, ka, vat, B, S)
    ob = _moba_call(tbl_flat, qb, kb, vbt, B, S)

    pad_rows = lambda n: jnp.zeros((n, D), F32)
    wr = jnp.concatenate([w_rg.T, pad_rows(EXPERT_ROW0 - N_GROUPS), w_re.T,
                          pad_rows(ROUTER_ROWS - EXPERT_ROW0 - N_EXPERTS)], axis=0)
    wr_hi = wr.astype(BF16)
    wr_lo = (wr - wr_hi.astype(F32)).astype(BF16)
    rb = jnp.concatenate([b_rg, jnp.zeros((EXPERT_ROW0 - N_GROUPS,), F32), b_re,
                          jnp.zeros((ROUTER_ROWS - EXPERT_ROW0 - N_EXPERTS,), F32)]).reshape(ROUTER_ROWS, 1)
    wo = w_out.astype(BF16)
    na_w = SWA_Q_HEADS * HEAD_DIM
    x1, hf, eid, wts = _outproj_call(oa, ob, x2d, row(swa_out_norm), row(moba_out_norm),
                                     wo[:na_w], wo[na_w:], row(ffn_norm), wr_hi, wr_lo, rb)

    be, n_used, src_tok, dst_row = _dispatch_plan(eid, T)
    moe_rows = _moe_call(be, n_used, src_tok, dst_row, hf, w_g, w_u, w_d)
    return _ple_call(moe_rows, x1, wts.T, p2d, row(ple_norm), w_ple_gate.astype(BF16),
                     w_ple_proj.astype(BF16), row(ple_out_norm))


def kernel(x, p, rel_bias, attn_norm, w_in, swa_q_norm, swa_k_norm, swa_sinks, moba_q_norm, moba_k_norm,
           swa_out_norm, moba_out_norm, w_out, ffn_norm, w_router_group, b_router_group, w_router_expert,
           b_router_expert, w_exp_gate, w_exp_up, w_exp_down, ple_norm, w_ple_gate, w_ple_proj, ple_out_norm):
    B, S, D = x.shape
    x2d = x.reshape(B * S, D)
    for i in range(p.shape[0]):
        x2d = _layer(x2d, p[i].reshape(B * S, -1), rel_bias, attn_norm[i], w_in[i], swa_q_norm[i],
                     swa_k_norm[i], swa_sinks[i], moba_q_norm[i], moba_k_norm[i], swa_out_norm[i],
                     moba_out_norm[i], w_out[i], ffn_norm[i], w_router_group[i], b_router_group[i],
                     w_router_expert[i], b_router_expert[i], w_exp_gate[i], w_exp_up[i], w_exp_down[i],
                     ple_norm[i], w_ple_gate[i], w_ple_proj[i], ple_out_norm[i], B, S)
    return x2d.reshape(B, S, D)
```

```python
import math

import numpy as np
import jax
import jax.numpy as jnp
from jax import lax
from jax.experimental import pallas as pl
from jax.experimental.pallas import tpu as pltpu

F32 = jnp.float32
BF16 = jnp.bfloat16
I32 = jnp.int32

HEAD_DIM = 64
SWA_Q_HEADS = 8
SWA_KV_HEADS = 2
SWA_WINDOW = 128
MOBA_Q_HEADS = 8
MOBA_KV_HEADS = 2
MOBA_BLOCK = 256
MOBA_TOPK = 3
N_HEADS = SWA_Q_HEADS + MOBA_Q_HEADS
REL_BUCKETS = 32
REL_MAX_DIST = 128
N_GROUPS = 4
EXPERTS_PER_GROUP = 8
N_EXPERTS = N_GROUPS * EXPERTS_PER_GROUP
EXPERT_TOPK = 2
DISPATCH_BLOCK = 256
EPS = 1e-6
NEG = -1e30

DMA_PRIORITIES = 2
LANES = 128
ROW_TILE = 8
BF16_SUBLANES = 16
GROUP_HEADS = 4
GROUP_W = GROUP_HEADS * HEAD_DIM
ROUTER_ROWS = 128
EXPERT_ROW0 = 8
TOKEN_TILE = 512
PLE_TILE = 512
VMEM_LIMIT = 52 * 1024 * 1024


def _dot_nt(a, b):
    return lax.dot_general(a, b, (((1,), (1,)), ((), ())), preferred_element_type=F32)


def _rmsnorm(x, g):
    ms = jnp.mean(x * x, axis=-1, keepdims=True)
    return x * lax.rsqrt(ms + EPS) * g


def _to_token_tiles(ref, x):
    m = x.shape[0]
    for j in range(x.shape[1] // LANES):
        ref[pl.ds(j, m, stride=ROW_TILE), :] = x[:, LANES * j:LANES * (j + 1)]


def _from_token_tiles(ref, m):
    return jnp.concatenate([ref[pl.ds(j, m, stride=ROW_TILE), :] for j in range(ref.shape[0] // m)], axis=1)


def _rel_bucket_np(dist):
    n = np.maximum(dist, 0)
    exact = REL_BUCKETS // 2
    nf = np.maximum(n, 1).astype(np.float32)
    large = exact + (np.log(nf / exact) / math.log(REL_MAX_DIST / exact)
                     * (REL_BUCKETS - exact)).astype(np.int32)
    return np.where(n < exact, n, np.minimum(large, REL_BUCKETS - 1)).astype(np.int32)


def _band_buckets(block):
    qi = np.arange(block)[:, None]
    kj = np.arange(2 * block)[None, :]
    return _rel_bucket_np(qi + block - kj)


def _bias_from_buckets(bkt, tbl_ref, head):
    acc = jnp.zeros(bkt.shape, F32)
    for j in range(REL_BUCKETS):
        acc = jnp.where(bkt == j, tbl_ref[j * N_HEADS + head], acc)
    return acc


def _qkv_kernel(x_ref, g_ref, w_ref, wvt_ref, hg_ref, qa_ref, ka_ref, qb_ref, kb_ref, vat_ref, vbt_ref):
    h = _rmsnorm(x_ref[...], g_ref[...]).astype(BF16)
    acc = jnp.dot(h, w_ref[...], preferred_element_type=F32)
    lo = lax.broadcasted_iota(I32, (1, LANES), 1) < HEAD_DIM

    def head_normed(c, scale):
        blk = acc[:, LANES * c:LANES * (c + 1)]
        sq = blk * blk
        s_lo = jnp.sum(jnp.where(lo, sq, 0.0), axis=-1, keepdims=True)
        s_hi = jnp.sum(jnp.where(lo, 0.0, sq), axis=-1, keepdims=True)
        inv = jnp.where(lo, lax.rsqrt(s_lo / HEAD_DIM + EPS), lax.rsqrt(s_hi / HEAD_DIM + EPS))
        return blk * inv * hg_ref[:, LANES * c:LANES * (c + 1)] * scale

    scale = HEAD_DIM ** -0.5
    for c in range(4):
        qa_ref[:, LANES * c:LANES * (c + 1)] = head_normed(c, scale).astype(BF16)
        qb_ref[:, LANES * c:LANES * (c + 1)] = head_normed(5 + c, scale).astype(BF16)
    ka_ref[...] = head_normed(4, 1.0).astype(BF16)
    kb_ref[...] = head_normed(9, 1.0).astype(BF16)
    vt = _dot_nt(wvt_ref[...], h).astype(BF16)
    for j in range(vat_ref.shape[0]):
        vat_ref[j] = vt[:LANES, SWA_WINDOW * j:SWA_WINDOW * (j + 1)]
    for j in range(vbt_ref.shape[0]):
        vbt_ref[j] = vt[LANES:, MOBA_BLOCK * j:MOBA_BLOCK * (j + 1)]


def _qkv_call(x2d, attn_g, w_qk, w_vt, head_gain):
    T, D = x2d.shape
    tm = TOKEN_TILE
    tok = lambda w: pl.BlockSpec((tm, w), lambda i: (i, 0))
    full = lambda a: pl.BlockSpec(a.shape, lambda i: (0, 0))
    out_w = (SWA_Q_HEADS * HEAD_DIM, LANES, MOBA_Q_HEADS * HEAD_DIM, LANES)
    slabs = lambda blk: pl.BlockSpec((tm // blk, LANES, blk), lambda i: (i, 0, 0))
    slab_shape = lambda blk: jax.ShapeDtypeStruct((T // blk, LANES, blk), BF16)
    return pl.pallas_call(
        _qkv_kernel,
        grid=(T // tm,),
        in_specs=[tok(D), full(attn_g), full(w_qk), full(w_vt), full(head_gain)],
        out_specs=[tok(w) for w in out_w] + [slabs(SWA_WINDOW), slabs(MOBA_BLOCK)],
        out_shape=[jax.ShapeDtypeStruct((T, w), BF16) for w in out_w]
        + [slab_shape(SWA_WINDOW), slab_shape(MOBA_BLOCK)],
        compiler_params=pltpu.CompilerParams(
            dimension_semantics=("arbitrary",), vmem_limit_bytes=VMEM_LIMIT),
        name="qkv",
    )(x2d, attn_g, w_qk, w_vt, head_gain)


def _swa_kernel(tbl_ref, sink_ref, bkt_ref, q_ref, k_ref, vt_ref, o_ref, bias_scr, kg_scr):
    W = SWA_WINDOW
    R = GROUP_HEADS
    G = SWA_KV_HEADS
    S = q_ref.shape[0]

    @pl.when(pl.program_id(0) == 0)
    def _init_bias():
        bkt = bkt_ref[...]
        for h in range(SWA_Q_HEADS):
            g, r = divmod(h, R)
            bias_scr[g, :, W * r:W * (r + 1)] = _bias_from_buckets(bkt, tbl_ref, h)

    key_i = lax.broadcasted_iota(I32, (2 * W, R * W), 0)
    qry_i = lax.broadcasted_iota(I32, (2 * W, R * W), 1) & (W - 1)
    dist = qry_i + W - key_i
    band = (dist >= 0) & (dist < W)
    own_part = key_i >= W
    head_of_lane = lax.broadcasted_iota(I32, (1, R * W), 1) // W
    sinks = []
    for g in range(G):
        kg_scr[g] = k_ref[:, HEAD_DIM * g:HEAD_DIM * (g + 1)]
        row = jnp.zeros((1, R * W), F32)
        for r in range(R):
            row = jnp.where(head_of_lane == r, sink_ref[R * g + r], row)
        sinks.append(row)

    def body(i, carry):
        r0 = pl.multiple_of(i * W, W)
        prev = jnp.maximum(i - 1, 0)
        p0 = pl.multiple_of(prev * W, W)
        scores = []
        for g in range(G):
            qblk = q_ref[pl.ds(r0, W), GROUP_W * g:GROUP_W * (g + 1)]
            q4 = jnp.concatenate([qblk[:, HEAD_DIM * r:HEAD_DIM * (r + 1)] for r in range(R)], axis=0)
            kband = jnp.concatenate([kg_scr[g, pl.ds(p0, W), :], kg_scr[g, pl.ds(r0, W), :]], axis=0)
            scores.append(_dot_nt(kband, q4))
        mask = band & (own_part | (i > 0))
        for g in range(G):
            s = jnp.where(mask, scores[g] + bias_scr[g], NEG)
            m = jnp.maximum(jnp.max(s, axis=0, keepdims=True), sinks[g])
            e = jnp.exp(s - m)
            den = jnp.sum(e, axis=0, keepdims=True) + jnp.exp(sinks[g] - m)
            feat = slice(HEAD_DIM * g, HEAD_DIM * (g + 1))
            vtband = jnp.concatenate([vt_ref[prev, feat, :], vt_ref[i, feat, :]], axis=1)
            ot = jnp.dot(vtband, e.astype(BF16), preferred_element_type=F32) / den
            for pr in range(R // 2):
                pair = ot[:, 2 * W * pr:2 * W * (pr + 1)]
                c0 = GROUP_W * g + LANES * pr
                o_ref[pl.ds(r0, W), c0:c0 + LANES] = jnp.concatenate([pair[:, :W], pair[:, W:]], axis=0).T
        return carry

    lax.fori_loop(0, S // W, body, 0)


def _swa_call(tbl_flat, sinks, qa, ka, vat, B, S):
    W = SWA_WINDOW
    bkt = jnp.asarray(np.ascontiguousarray(_band_buckets(W).T))
    smem = pl.BlockSpec(memory_space=pltpu.SMEM)
    return pl.pallas_call(
        _swa_kernel,
        grid=(B,),
        in_specs=[smem, smem,
                  pl.BlockSpec(bkt.shape, lambda b: (0, 0)),
                  pl.BlockSpec((S, qa.shape[1]), lambda b: (b, 0)),
                  pl.BlockSpec((S, LANES), lambda b: (b, 0)),
                  pl.BlockSpec((S // W, LANES, W), lambda b: (b, 0, 0))],
        out_specs=pl.BlockSpec((S, qa.shape[1]), lambda b: (b, 0)),
        out_shape=jax.ShapeDtypeStruct(qa.shape, F32),
        scratch_shapes=[pltpu.VMEM((SWA_KV_HEADS, 2 * W, GROUP_HEADS * W), F32),
                        pltpu.VMEM((SWA_KV_HEADS, S, HEAD_DIM), BF16)],
        compiler_params=pltpu.CompilerParams(
            dimension_semantics=("arbitrary",), vmem_limit_bytes=VMEM_LIMIT),
        name="swa",
    )(tbl_flat, sinks, bkt, qa, ka, vat)


def _moba_kernel(tbl_ref, bkt_ref, q_ref, k_ref, vt_ref, o_ref,
                 bias_scr, kg_scr, km_scr, q4_scr, sel_scr, m_scr, l_scr, acc_scr):
    BS = MOBA_BLOCK
    R = GROUP_HEADS
    G = MOBA_KV_HEADS
    S = q_ref.shape[0]
    NBK = S // BS
    ROWS = R * BS

    @pl.when(pl.program_id(0) == 0)
    def _init_bias():
        bkt = bkt_ref[...]
        for h in range(MOBA_Q_HEADS):
            g, r = divmod(h, R)
            band = _bias_from_buckets(bkt, tbl_ref, SWA_Q_HEADS + h)
            far = tbl_ref[(REL_BUCKETS - 1) * N_HEADS + SWA_Q_HEADS + h]
            bias_scr[g, 0, :, BS * r:BS * (r + 1)] = jnp.full((BS, BS), far, F32)
            bias_scr[g, 1, :, BS * r:BS * (r + 1)] = band[:BS]
            bias_scr[g, 2, :, BS * r:BS * (r + 1)] = band[BS:]

    blk_row = lax.broadcasted_iota(I32, (BF16_SUBLANES, 1), 0)
    for g in range(G):
        kg_scr[g] = k_ref[:, HEAD_DIM * g:HEAD_DIM * (g + 1)]
        kmean = jnp.sum(kg_scr[g].astype(F32).reshape(NBK, BS, HEAD_DIM), axis=1) / BS
        kmean = jnp.concatenate([kmean, jnp.zeros((BF16_SUBLANES - NBK, HEAD_DIM), F32)], axis=0)
        km_hi = kmean.astype(BF16)
        km_scr[g, 0] = km_hi
        km_scr[g, 1] = (kmean - km_hi.astype(F32)).astype(BF16)

    def scores(n):
        k0 = pl.multiple_of(n * BS, BS)
        return [_dot_nt(kg_scr[g, pl.ds(k0, BS), :], q4_scr[g]) for g in range(G)]

    def values_t(n, g):
        return vt_ref[n, HEAD_DIM * g:HEAD_DIM * (g + 1), :]

    def qblock(mi, carry):
        r0 = pl.multiple_of(mi * BS, BS)
        past = blk_row < mi
        for g in range(G):
            qblk = q_ref[pl.ds(r0, BS), GROUP_W * g:GROUP_W * (g + 1)]
            q4 = jnp.concatenate([qblk[:, HEAD_DIM * r:HEAD_DIM * (r + 1)] for r in range(R)], axis=0)
            q4_scr[g] = q4
            gate = _dot_nt(km_scr[g, 0], q4) + _dot_nt(km_scr[g, 1], q4)
            gm = jnp.where(past, gate, NEG)
            cnt = jnp.zeros(gm.shape, F32)
            for n in range(NBK - 1):
                col = gm[n:n + 1, :]
                beats = (gm > col) | ((gm == col) & (blk_row < n))
                c = jnp.sum(jnp.where(beats, 1.0, 0.0), axis=0, keepdims=True)
                cnt = jnp.where(blk_row == n, c, cnt)
            sel_scr[g] = jnp.where(past & (cnt < MOBA_TOPK), 1.0, 0.0)

        key_i = lax.broadcasted_iota(I32, (BS, ROWS), 0)
        qry_i = lax.broadcasted_iota(I32, (BS, ROWS), 1) & (BS - 1)
        causal = key_i <= qry_i
        sc = scores(mi)
        for g in range(G):
            s = jnp.where(causal, sc[g] + bias_scr[g, 2], NEG)
            m0 = jnp.max(s, axis=0, keepdims=True)
            e = jnp.exp(s - m0)
            m_scr[g] = m0
            l_scr[g] = jnp.sum(e, axis=0, keepdims=True)
            acc_scr[g] = jnp.dot(values_t(mi, g), e.astype(BF16), preferred_element_type=F32)

        def kvblock(j, c2):
            n = mi - j
            case = jnp.where(j == 1, 1, 0)
            sc = scores(n)
            for g in range(G):
                s = jnp.where(sel_scr[g, pl.ds(n, 1), :] > 0.5, sc[g] + bias_scr[g, case], NEG)
                m_prev = m_scr[g]
                m_new = jnp.maximum(m_prev, jnp.max(s, axis=0, keepdims=True))
                alpha = jnp.exp(m_prev - m_new)
                e = jnp.exp(s - m_new)
                l_scr[g] = alpha * l_scr[g] + jnp.sum(e, axis=0, keepdims=True)
                acc_scr[g] = alpha * acc_scr[g] + jnp.dot(
                    values_t(n, g), e.astype(BF16), preferred_element_type=F32)
                m_scr[g] = m_new
            return c2

        lax.fori_loop(1, mi + 1, kvblock, 0)
        for g in range(G):
            ot = acc_scr[g] / l_scr[g]
            for pr in range(R // 2):
                pair = ot[:, 2 * BS * pr:2 * BS * (pr + 1)]
                c0 = GROUP_W * g + LANES * pr
                o_ref[pl.ds(r0, BS), c0:c0 + LANES] = jnp.concatenate([pair[:, :BS], pair[:, BS:]], axis=0).T
        return carry

    lax.fori_loop(0, NBK, qblock, 0)


def _moba_call(tbl_flat, qb, kb, vbt, B, S):
    BS = MOBA_BLOCK
    R = GROUP_HEADS
    G = MOBA_KV_HEADS
    NBK = S // BS
    bkt = jnp.asarray(np.ascontiguousarray(_band_buckets(BS).T))
    smem = pl.BlockSpec(memory_space=pltpu.SMEM)
    return pl.pallas_call(
        _moba_kernel,
        grid=(B,),
        in_specs=[smem,
                  pl.BlockSpec(bkt.shape, lambda b: (0, 0)),
                  pl.BlockSpec((S, qb.shape[1]), lambda b: (b, 0)),
                  pl.BlockSpec((S, LANES), lambda b: (b, 0)),
                  pl.BlockSpec((NBK, LANES, BS), lambda b: (b, 0, 0))],
        out_specs=pl.BlockSpec((S, qb.shape[1]), lambda b: (b, 0)),
        out_shape=jax.ShapeDtypeStruct(qb.shape, F32),
        scratch_shapes=[pltpu.VMEM((G, 3, BS, R * BS), F32),
                        pltpu.VMEM((G, S, HEAD_DIM), BF16),
                        pltpu.VMEM((G, 2, BF16_SUBLANES, HEAD_DIM), BF16),
                        pltpu.VMEM((G, R * BS, HEAD_DIM), BF16),
                        pltpu.VMEM((G, BF16_SUBLANES, R * BS), F32),
                        pltpu.VMEM((G, 1, R * BS), F32),
                        pltpu.VMEM((G, 1, R * BS), F32),
                        pltpu.VMEM((G, HEAD_DIM, R * BS), F32)],
        compiler_params=pltpu.CompilerParams(
            dimension_semantics=("arbitrary",), vmem_limit_bytes=VMEM_LIMIT),
        name="moba",
    )(tbl_flat, bkt, qb, kb, vbt)


def _outproj_kernel(oa_ref, ob_ref, x_ref, ga_ref, gb_ref, woa_ref, wob_ref, fg_ref,
                    wrh_ref, wrl_ref, rb_ref, x1_ref, hf_ref, eid_ref, wt_ref):
    na = _rmsnorm(oa_ref[...], ga_ref[...]).astype(BF16)
    nb = _rmsnorm(ob_ref[...], gb_ref[...]).astype(BF16)
    y = (jnp.dot(na, woa_ref[...], preferred_element_type=F32)
         + jnp.dot(nb, wob_ref[...], preferred_element_type=F32))
    x1 = x_ref[...] + y
    x1_ref[...] = x1
    hf = _rmsnorm(x1, fg_ref[...])
    _to_token_tiles(hf_ref, hf)

    hf_hi = hf.astype(BF16)
    hf_lo = (hf - hf_hi.astype(F32)).astype(BF16)
    lt = (_dot_nt(wrh_ref[...], hf_hi) + _dot_nt(wrh_ref[...], hf_lo)
          + _dot_nt(wrl_ref[...], hf_hi) + rb_ref[...])
    gl = [lt[j:j + 1, :] for j in range(N_GROUPS)]
    best = gl[0]
    gsel = jnp.zeros(best.shape, I32)
    for j in range(1, N_GROUPS):
        better = gl[j] > best
        gsel = jnp.where(better, j, gsel)
        best = jnp.where(better, gl[j], best)
    gsum = jnp.zeros(best.shape, F32)
    for j in range(N_GROUPS):
        gsum = gsum + jnp.exp(gl[j] - best)
    gw = 1.0 / gsum
    E = EXPERTS_PER_GROUP
    es = lt[EXPERT_ROW0:EXPERT_ROW0 + E, :]
    for j in range(1, N_GROUPS):
        es = jnp.where(gsel == j, lt[EXPERT_ROW0 + E * j:EXPERT_ROW0 + E * (j + 1), :], es)
    row = lax.broadcasted_iota(I32, es.shape, 0)
    v1 = jnp.max(es, axis=0, keepdims=True)
    i1 = jnp.min(jnp.where(es == v1, row, E), axis=0, keepdims=True)
    es2 = jnp.where(row == i1, -jnp.inf, es)
    v2 = jnp.max(es2, axis=0, keepdims=True)
    i2 = jnp.min(jnp.where(es2 == v2, row, E), axis=0, keepdims=True)
    e2 = jnp.exp(v2 - v1)
    den = 1.0 + e2
    eid_ref[...] = jnp.concatenate([gsel * E + i1, gsel * E + i2], axis=0)
    wt_ref[...] = jnp.concatenate([gw * (1.0 / den), gw * (e2 / den)], axis=0)


def _outproj_call(oa, ob, x2d, ga, gb, woa, wob, fg, wrh, wrl, rb):
    T, D = x2d.shape
    tm = TOKEN_TILE
    tok = lambda w: pl.BlockSpec((tm, w), lambda i: (i, 0))
    full = lambda a: pl.BlockSpec(a.shape, lambda i: (0, 0))
    col = pl.BlockSpec((EXPERT_TOPK, tm), lambda i: (0, i))
    return pl.pallas_call(
        _outproj_kernel,
        grid=(T // tm,),
        in_specs=[tok(oa.shape[1]), tok(ob.shape[1]), tok(D), full(ga), full(gb), full(woa), full(wob),
                  full(fg), full(wrh), full(wrl), full(rb)],
        out_specs=[tok(D), pl.BlockSpec((tm * ROW_TILE, LANES), lambda i: (i, 0)), col, col],
        out_shape=[jax.ShapeDtypeStruct((T, D), F32), jax.ShapeDtypeStruct((T * ROW_TILE, LANES), F32),
                   jax.ShapeDtypeStruct((EXPERT_TOPK, T), I32), jax.ShapeDtypeStruct((EXPERT_TOPK, T), F32)],
        compiler_params=pltpu.CompilerParams(
            dimension_semantics=("arbitrary",), vmem_limit_bytes=VMEM_LIMIT),
        name="outproj",
    )(oa, ob, x2d, ga, gb, woa, wob, fg, wrh, wrl, rb)


def _moe_kernel(be_ref, nu_ref, src_ref, srcn_ref, dst_ref, hf_hbm, wg_ref, wu_ref, wd_ref, out_hbm,
                xbuf, obuf, wgb, wub, wdb, gsem, ssem):
    i = pl.program_id(0)
    n_used = nu_ref[0]
    M = xbuf.shape[1] // ROW_TILE
    slot = i % 2

    def gather_rows(idx_ref, buf_slot):
        for r in range(M):
            tok_row = pl.multiple_of(idx_ref[0, 0, r] * ROW_TILE, ROW_TILE)
            pltpu.make_async_copy(hf_hbm.at[pl.ds(tok_row, ROW_TILE)],
                                  xbuf.at[buf_slot, pl.ds(r * ROW_TILE, ROW_TILE)],
                                  gsem.at[buf_slot]).start(priority=r % DMA_PRIORITIES)

    def scatter_rows(idx_ref, buf_slot):
        for r in range(M):
            dst_row = pl.multiple_of(idx_ref[0, 0, r] * ROW_TILE, ROW_TILE)
            pltpu.make_async_copy(obuf.at[buf_slot, pl.ds(r * ROW_TILE, ROW_TILE)],
                                  out_hbm.at[pl.ds(dst_row, ROW_TILE)],
                                  ssem.at[buf_slot]).start(priority=r % DMA_PRIORITIES)

    def wait_rows(buf, buf_slot, sem):
        pltpu.make_async_copy(hf_hbm.at[pl.ds(0, M * ROW_TILE)], buf.at[buf_slot], sem.at[buf_slot]).wait()

    @pl.when(i < n_used)
    def _compute():
        @pl.when(i == 0)
        def _first_gather():
            gather_rows(src_ref, 0)
            obuf[1] = jnp.zeros(obuf.shape[1:], F32)
            spare = pltpu.make_async_copy(
                obuf.at[1], out_hbm.at[pl.ds(out_hbm.shape[0] - M * ROW_TILE, M * ROW_TILE)], ssem.at[1])
            spare.start()
            spare.wait()

        @pl.when((i == 0) | (be_ref[i] != be_ref[jnp.maximum(i - 1, 0)]))
        def _cast_weights():
            wgb[...] = wg_ref[0].astype(BF16)
            wub[...] = wu_ref[0].astype(BF16)
            wdb[...] = wd_ref[0].astype(BF16)

        wait_rows(xbuf, slot, gsem)

        @pl.when(i >= 2)
        def _reuse_obuf():
            wait_rows(obuf, slot, ssem)

        gather_rows(srcn_ref, 1 - slot)
        xb = _from_token_tiles(xbuf.at[slot], M).astype(BF16)
        a = jnp.dot(xb, wgb[...], preferred_element_type=F32)
        u = jnp.dot(xb, wub[...], preferred_element_type=F32)
        hmid = (a * jax.nn.sigmoid(a)) * u
        _to_token_tiles(obuf.at[slot], jnp.dot(hmid.astype(BF16), wdb[...], preferred_element_type=F32))
        scatter_rows(dst_ref, slot)

        @pl.when(i == n_used - 1)
        def _drain():
            wait_rows(xbuf, 1 - slot, gsem)
            wait_rows(obuf, slot, ssem)

            @pl.when(i >= 1)
            def _drain_prev():
                wait_rows(obuf, 1 - slot, ssem)


def _moe_call(block_expert, n_used, src_tok, dst_row, hf, w_g, w_u, w_d):
    T = hf.shape[0] // ROW_TILE
    M = DISPATCH_BLOCK
    nblk = block_expert.shape[0]
    D, DE = w_g.shape[1], w_g.shape[2]
    assert D == ROW_TILE * LANES and hf.shape[1] == LANES
    idx_spec = lambda f: pl.BlockSpec((1, 1, M), f, memory_space=pltpu.SMEM)
    cur = lambda i, be, nu: (i, 0, 0)
    nxt = lambda i, be, nu: (jnp.minimum(i + 1, nblk - 1), 0, 0)
    wspec = lambda a, b: pl.BlockSpec((1, a, b), lambda i, be, nu: (be[i], 0, 0))
    grid_spec = pltpu.PrefetchScalarGridSpec(
        num_scalar_prefetch=2,
        grid=(nblk,),
        in_specs=[idx_spec(cur), idx_spec(nxt), idx_spec(cur),
                  pl.BlockSpec(memory_space=pl.ANY),
                  wspec(D, DE), wspec(D, DE), wspec(DE, D)],
        out_specs=pl.BlockSpec(memory_space=pl.ANY),
        scratch_shapes=[pltpu.VMEM((2, M * ROW_TILE, LANES), F32), pltpu.VMEM((2, M * ROW_TILE, LANES), F32),
                        pltpu.VMEM((D, DE), BF16), pltpu.VMEM((D, DE), BF16), pltpu.VMEM((DE, D), BF16),
                        pltpu.SemaphoreType.DMA((2,)), pltpu.SemaphoreType.DMA((2,))],
    )
    src3 = src_tok.reshape(nblk, 1, M)
    return pl.pallas_call(
        _moe_kernel,
        grid_spec=grid_spec,
        out_shape=jax.ShapeDtypeStruct(((EXPERT_TOPK * T + M) * ROW_TILE, LANES), F32),
        compiler_params=pltpu.CompilerParams(
            dimension_semantics=("arbitrary",), vmem_limit_bytes=VMEM_LIMIT),
        name="moe",
    )(block_expert, n_used, src3, src3, dst_row.reshape(nblk, 1, M), hf, w_g, w_u, w_d)


def _ple_kernel(x1_ref, m0_ref, m1_ref, wt_ref, p_ref, pg_ref, wgate_ref, wproj_ref, pog_ref, out_ref):
    proj = _rmsnorm(jnp.dot(p_ref[...].astype(BF16), wproj_ref[...], preferred_element_type=F32),
                    pog_ref[...])
    tm = x1_ref.shape[0]
    moe = (_from_token_tiles(m0_ref, tm) * wt_ref[:, 0:1] + _from_token_tiles(m1_ref, tm) * wt_ref[:, 1:2])
    x2 = x1_ref[...] + moe
    hn = _rmsnorm(x2, pg_ref[...]).astype(BF16)
    gate = jax.nn.sigmoid(jnp.dot(hn, wgate_ref[...], preferred_element_type=F32))
    out_ref[...] = x2 + gate * proj


def _ple_call(moe_rows, x1, wt_cols, p2d, pg, wgate, wproj, pog):
    T, D = x1.shape
    tm = PLE_TILE
    tok = lambda w: pl.BlockSpec((tm, w), lambda i: (i, 0))
    full = lambda a: pl.BlockSpec(a.shape, lambda i: (0, 0))
    first = pl.BlockSpec((tm * ROW_TILE, LANES), lambda i: (i, 0))
    second = pl.BlockSpec((tm * ROW_TILE, LANES), lambda i: (T // tm + i, 0))
    return pl.pallas_call(
        _ple_kernel,
        grid=(T // tm,),
        in_specs=[tok(D), first, second, tok(EXPERT_TOPK), tok(p2d.shape[1]),
                  full(pg), full(wgate), full(wproj), full(pog)],
        out_specs=tok(D),
        out_shape=jax.ShapeDtypeStruct((T, D), F32),
        compiler_params=pltpu.CompilerParams(
            dimension_semantics=("arbitrary",), vmem_limit_bytes=VMEM_LIMIT),
        name="ple",
    )(x1, moe_rows, moe_rows, wt_cols, p2d, pg, wgate, wproj, pog)


def _dispatch_plan(eid, T):
    A = EXPERT_TOPK * T
    M = DISPATCH_BLOCK
    nblk = A // M + N_EXPERTS
    e_flat = eid.reshape(A)
    experts = jnp.arange(N_EXPERTS, dtype=I32)
    counts = jnp.sum((e_flat[:, None] == experts[None, :]).astype(I32), axis=0)
    padded = (counts + M - 1) // M * M
    n_used = jnp.sum(padded) // M
    row = jnp.arange(M, dtype=I32)[None, :]
    pad_keys = jnp.where(row < (padded - counts)[:, None], 2 * experts[:, None] + 1, 2 * N_EXPERTS)
    keys = jnp.concatenate([2 * e_flat, pad_keys.reshape(-1)])
    vals = jnp.concatenate([jnp.arange(A, dtype=I32), jnp.full((N_EXPERTS * M,), -1, I32)])
    keys, vals = lax.sort((keys, vals), num_keys=1, is_stable=True)
    keys = keys.reshape(nblk, M)
    a_s = vals.reshape(nblk, M)
    blk = jnp.arange(nblk, dtype=I32)
    be = jnp.minimum(keys[:, 0] // 2, N_EXPERTS - 1)
    be = jnp.where(blk < n_used, be, be[jnp.maximum(n_used - 1, 0)])
    valid = a_s >= 0
    src_tok = jnp.where(valid, jnp.where(a_s >= T, a_s - T, a_s), 0)
    dst_row = jnp.where(valid, a_s, A + row)
    return be, n_used.reshape(1).astype(I32), src_tok, dst_row


def _layer(x2d, p2d, rel_bias, attn_norm, w_in, swa_q_norm, swa_k_norm, swa_sinks, moba_q_norm,
           moba_k_norm, swa_out_norm, moba_out_norm, w_out, ffn_norm, w_rg, b_rg, w_re, b_re,
           w_g, w_u, w_d, ple_norm, w_ple_gate, w_ple_proj, ple_out_norm, B, S):
    T, D = x2d.shape
    row = lambda v: v.reshape(1, -1).astype(F32)
    head_gain = jnp.concatenate([
        jnp.tile(swa_q_norm, SWA_Q_HEADS), jnp.tile(swa_k_norm, SWA_KV_HEADS),
        jnp.tile(moba_q_norm, MOBA_Q_HEADS), jnp.tile(moba_k_norm, MOBA_KV_HEADS)])
    w_in_b = w_in.astype(BF16)
    qa_w, kv_w = SWA_Q_HEADS * HEAD_DIM, SWA_KV_HEADS * HEAD_DIM
    qb_w = MOBA_Q_HEADS * HEAD_DIM
    c_va = qa_w + kv_w
    c_qb = c_va + kv_w
    c_vb = c_qb + qb_w + kv_w
    w_qk = jnp.concatenate([w_in_b[:, :c_va], w_in_b[:, c_qb:c_vb]], axis=1)
    w_vt = jnp.concatenate([w_in_b[:, c_va:c_qb], w_in_b[:, c_vb:]], axis=1).T
    qa, ka, qb, kb, vat, vbt = _qkv_call(x2d, row(attn_norm), w_qk, w_vt, row(head_gain))

    tbl_flat = rel_bias.astype(F32).reshape(-1)
    oa = _swa_call(tbl_flat, swa_sinks.astype(F32), qa, ka, vat, B, S)
    ob = _moba_call(tbl_flat, qb, kb, vbt, B, S)

    pad_rows = lambda n: jnp.zeros((n, D), F32)
    wr = jnp.concatenate([w_rg.T, pad_rows(EXPERT_ROW0 - N_GROUPS), w_re.T,
                          pad_rows(ROUTER_ROWS - EXPERT_ROW0 - N_EXPERTS)], axis=0)
    wr_hi = wr.astype(BF16)
    wr_lo = (wr - wr_hi.astype(F32)).astype(BF16)
    rb = jnp.concatenate([b_rg, jnp.zeros((EXPERT_ROW0 - N_GROUPS,), F32), b_re,
                          jnp.zeros((ROUTER_ROWS - EXPERT_ROW0 - N_EXPERTS,), F32)]).reshape(ROUTER_ROWS, 1)
    wo = w_out.astype(BF16)
    na_w = SWA_Q_HEADS * HEAD_DIM
    x1, hf, eid, wts = _outproj_call(oa, ob, x2d, row(swa_out_norm), row(moba_out_norm),
                                     wo[:na_w], wo[na_w:], row(ffn_norm), wr_hi, wr_lo, rb)

    be, n_used, src_tok, dst_row = _dispatch_plan(eid, T)
    moe_rows = _moe_call(be, n_used, src_tok, dst_row, hf, w_g, w_u, w_d)
    return _ple_call(moe_rows, x1, wts.T, p2d, row(ple_norm), w_ple_gate.astype(BF16),
                     w_ple_proj.astype(BF16), row(ple_out_norm))


def kernel(x, p, rel_bias, attn_norm, w_in, swa_q_norm, swa_k_norm, swa_sinks, moba_q_norm, moba_k_norm,
           swa_out_norm, moba_out_norm, w_out, ffn_norm, w_router_group, b_router_group, w_router_expert,
           b_router_expert, w_exp_gate, w_exp_up, w_exp_down, ple_norm, w_ple_gate, w_ple_proj, ple_out_norm):
    B, S, D = x.shape
    x2d = x.reshape(B * S, D)
    for i in range(p.shape[0]):
        x2d = _layer(x2d, p[i].reshape(B * S, -1), rel_bias, attn_norm[i], w_in[i], swa_q_norm[i],
                     swa_k_norm[i], swa_sinks[i], moba_q_norm[i], moba_k_norm[i], swa_out_norm[i],
                     moba_out_norm[i], w_out[i], ffn_norm[i], w_router_group[i], b_router_group[i],
                     w_router_expert[i], b_router_expert[i], w_exp_gate[i], w_exp_up[i], w_exp_down[i],
                     ple_norm[i], w_ple_gate[i], w_ple_proj[i], ple_out_norm[i], B, S)
    return x2d.reshape(B, S, D)
```

```python
import math

import numpy as np
import jax
import jax.numpy as jnp
from jax import lax
from jax.experimental import pallas as pl
from jax.experimental.pallas import tpu as pltpu

F32 = jnp.float32
BF16 = jnp.bfloat16
I32 = jnp.int32

HEAD_DIM = 64
SWA_Q_HEADS = 8
SWA_KV_HEADS = 2
SWA_WINDOW = 128
MOBA_Q_HEADS = 8
MOBA_KV_HEADS = 2
MOBA_BLOCK = 256
MOBA_TOPK = 3
N_HEADS = SWA_Q_HEADS + MOBA_Q_HEADS
REL_BUCKETS = 32
REL_MAX_DIST = 128
N_GROUPS = 4
EXPERTS_PER_GROUP = 8
N_EXPERTS = N_GROUPS * EXPERTS_PER_GROUP
EXPERT_TOPK = 2
DISPATCH_BLOCK = 256
EPS = 1e-6
NEG = -1e30

DMA_PRIORITIES = 2
LANES = 128
ROW_TILE = 8
BF16_SUBLANES = 16
GROUP_HEADS = 4
GROUP_W = GROUP_HEADS * HEAD_DIM
ROUTER_ROWS = 128
EXPERT_ROW0 = 8
TOKEN_TILE = 512
PLE_TILE = 512
VMEM_LIMIT = 52 * 1024 * 1024


def _dot_nt(a, b):
    return lax.dot_general(a, b, (((1,), (1,)), ((), ())), preferred_element_type=F32)


def _rmsnorm(x, g):
    ms = jnp.mean(x * x, axis=-1, keepdims=True)
    return x * lax.rsqrt(ms + EPS) * g


def _to_token_tiles(ref, x):
    m = x.shape[0]
    for j in range(x.shape[1] // LANES):
        ref[pl.ds(j, m, stride=ROW_TILE), :] = x[:, LANES * j:LANES * (j + 1)]


def _from_token_tiles(ref, m):
    return jnp.concatenate([ref[pl.ds(j, m, stride=ROW_TILE), :] for j in range(ref.shape[0] // m)], axis=1)


def _rel_bucket_np(dist):
    n = np.maximum(dist, 0)
    exact = REL_BUCKETS // 2
    nf = np.maximum(n, 1).astype(np.float32)
    large = exact + (np.log(nf / exact) / math.log(REL_MAX_DIST / exact)
                     * (REL_BUCKETS - exact)).astype(np.int32)
    return np.where(n < exact, n, np.minimum(large, REL_BUCKETS - 1)).astype(np.int32)


def _band_buckets(block):
    qi = np.arange(block)[:, None]
    kj = np.arange(2 * block)[None, :]
    return _rel_bucket_np(qi + block - kj)


def _bias_from_buckets(bkt, tbl_ref, head):
    acc = jnp.zeros(bkt.shape, F32)
    for j in range(REL_BUCKETS):
        acc = jnp.where(bkt == j, tbl_ref[j * N_HEADS + head], acc)
    return acc


def _qkv_kernel(x_ref, g_ref, w_ref, wvt_ref, hg_ref, qa_ref, ka_ref, qb_ref, kb_ref, vat_ref, vbt_ref):
    h = _rmsnorm(x_ref[...], g_ref[...]).astype(BF16)
    acc = jnp.dot(h, w_ref[...], preferred_element_type=F32)
    lo = lax.broadcasted_iota(I32, (1, LANES), 1) < HEAD_DIM

    def head_normed(c, scale):
        blk = acc[:, LANES * c:LANES * (c + 1)]
        sq = blk * blk
        s_lo = jnp.sum(jnp.where(lo, sq, 0.0), axis=-1, keepdims=True)
        s_hi = jnp.sum(jnp.where(lo, 0.0, sq), axis=-1, keepdims=True)
        inv = jnp.where(lo, lax.rsqrt(s_lo / HEAD_DIM + EPS), lax.rsqrt(s_hi / HEAD_DIM + EPS))
        return blk * inv * hg_ref[:, LANES * c:LANES * (c + 1)] * scale

    scale = HEAD_DIM ** -0.5
    for c in range(4):
        qa_ref[:, LANES * c:LANES * (c + 1)] = head_normed(c, scale).astype(BF16)
        qb_ref[:, LANES * c:LANES * (c + 1)] = head_normed(5 + c, scale).astype(BF16)
    ka_ref[...] = head_normed(4, 1.0).astype(BF16)
    kb_ref[...] = head_normed(9, 1.0).astype(BF16)
    vt = _dot_nt(wvt_ref[...], h).astype(BF16)
    for j in range(vat_ref.shape[0]):
        vat_ref[j] = vt[:LANES, SWA_WINDOW * j:SWA_WINDOW * (j + 1)]
    for j in range(vbt_ref.shape[0]):
        vbt_ref[j] = vt[LANES:, MOBA_BLOCK * j:MOBA_BLOCK * (j + 1)]


def _qkv_call(x2d, attn_g, w_qk, w_vt, head_gain):
    T, D = x2d.shape
    tm = TOKEN_TILE
    tok = lambda w: pl.BlockSpec((tm, w), lambda i: (i, 0))
    full = lambda a: pl.BlockSpec(a.shape, lambda i: (0, 0))
    out_w = (SWA_Q_HEADS * HEAD_DIM, LANES, MOBA_Q_HEADS * HEAD_DIM, LANES)
    slabs = lambda blk: pl.BlockSpec((tm // blk, LANES, blk), lambda i: (i, 0, 0))
    slab_shape = lambda blk: jax.ShapeDtypeStruct((T // blk, LANES, blk), BF16)
    return pl.pallas_call(
        _qkv_kernel,
        grid=(T // tm,),
        in_specs=[tok(D), full(attn_g), full(w_qk), full(w_vt), full(head_gain)],
        out_specs=[tok(w) for w in out_w] + [slabs(SWA_WINDOW), slabs(MOBA_BLOCK)],
        out_shape=[jax.ShapeDtypeStruct((T, w), BF16) for w in out_w]
        + [slab_shape(SWA_WINDOW), slab_shape(MOBA_BLOCK)],
        compiler_params=pltpu.CompilerParams(
            dimension_semantics=("arbitrary",), vmem_limit_bytes=VMEM_LIMIT),
        name="qkv",
    )(x2d, attn_g, w_qk, w_vt, head_gain)


def _padded_heads(qblk, heads):
    zeros = jnp.zeros((qblk.shape[0], HEAD_DIM), qblk.dtype)
    pieces = []
    for h in range(heads):
        piece = qblk[:, HEAD_DIM * h:HEAD_DIM * (h + 1)]
        pieces.append(jnp.concatenate([piece, zeros] if h < GROUP_HEADS else [zeros, piece], axis=1))
    return jnp.concatenate(pieces, axis=0)


def _swa_kernel(tbl_ref, sink_ref, bkt_ref, q_ref, k_ref, vt_ref, o_ref, bias_scr):
    W = SWA_WINDOW
    H = SWA_Q_HEADS
    S = q_ref.shape[0]

    @pl.when(pl.program_id(0) == 0)
    def _init_bias():
        bkt = bkt_ref[...]
        for h in range(H):
            bias_scr[:, W * h:W * (h + 1)] = _bias_from_buckets(bkt, tbl_ref, h)

    key_i = lax.broadcasted_iota(I32, (2 * W, H * W), 0)
    qry_i = lax.broadcasted_iota(I32, (2 * W, H * W), 1) & (W - 1)
    dist = qry_i + W - key_i
    band = (dist >= 0) & (dist < W)
    own_part = key_i >= W
    head_of_lane = lax.broadcasted_iota(I32, (1, H * W), 1) // W
    sinks = jnp.zeros((1, H * W), F32)
    for h in range(H):
        sinks = jnp.where(head_of_lane == h, sink_ref[h], sinks)

    def body(i, carry):
        r0 = pl.multiple_of(i * W, W)
        prev = jnp.maximum(i - 1, 0)
        p0 = pl.multiple_of(prev * W, W)
        q8 = _padded_heads(q_ref[pl.ds(r0, W), :], H)
        kband = jnp.concatenate([k_ref[pl.ds(p0, W), :], k_ref[pl.ds(r0, W), :]], axis=0)
        mask = band & (own_part | (i > 0))
        s = jnp.where(mask, _dot_nt(kband, q8) + bias_scr[...], NEG)
        m = jnp.maximum(jnp.max(s, axis=0, keepdims=True), sinks)
        e = jnp.exp(s - m)
        den = jnp.sum(e, axis=0, keepdims=True) + jnp.exp(sinks - m)
        vtband = jnp.concatenate([vt_ref[prev], vt_ref[i]], axis=1)
        ot = jnp.dot(vtband, e.astype(BF16), preferred_element_type=F32) / den
        for pr in range(H // 2):
            f0 = HEAD_DIM * (2 * pr // GROUP_HEADS)
            pair = ot[f0:f0 + HEAD_DIM, 2 * W * pr:2 * W * (pr + 1)]
            o_ref[pl.ds(r0, W), LANES * pr:LANES * (pr + 1)] = jnp.concatenate(
                [pair[:, :W], pair[:, W:]], axis=0).T
        return carry

    lax.fori_loop(0, S // W, body, 0)


def _swa_call(tbl_flat, sinks, qa, ka, vat, B, S):
    W = SWA_WINDOW
    bkt = jnp.asarray(np.ascontiguousarray(_band_buckets(W).T))
    smem = pl.BlockSpec(memory_space=pltpu.SMEM)
    return pl.pallas_call(
        _swa_kernel,
        grid=(B,),
        in_specs=[smem, smem,
                  pl.BlockSpec(bkt.shape, lambda b: (0, 0)),
                  pl.BlockSpec((S, qa.shape[1]), lambda b: (b, 0)),
                  pl.BlockSpec((S, LANES), lambda b: (b, 0)),
                  pl.BlockSpec((S // W, LANES, W), lambda b: (b, 0, 0))],
        out_specs=pl.BlockSpec((S, qa.shape[1]), lambda b: (b, 0)),
        out_shape=jax.ShapeDtypeStruct(qa.shape, F32),
        scratch_shapes=[pltpu.VMEM((2 * W, SWA_Q_HEADS * W), F32)],
        compiler_params=pltpu.CompilerParams(
            dimension_semantics=("arbitrary",), vmem_limit_bytes=VMEM_LIMIT),
        name="swa",
    )(tbl_flat, sinks, bkt, qa, ka, vat)


def _moba_kernel(tbl_ref, bkt_ref, q_ref, k_ref, vt_ref, o_ref,
                 bias_scr, km_scr, q8_scr, sel_scr, m_scr, l_scr, acc_scr):
    BS = MOBA_BLOCK
    H = MOBA_Q_HEADS
    S = q_ref.shape[0]
    NBK = S // BS
    NQ = H * BS
    GQ = GROUP_HEADS * BS

    @pl.when(pl.program_id(0) == 0)
    def _init_bias():
        bkt = bkt_ref[...]
        for h in range(H):
            band = _bias_from_buckets(bkt, tbl_ref, SWA_Q_HEADS + h)
            far = tbl_ref[(REL_BUCKETS - 1) * N_HEADS + SWA_Q_HEADS + h]
            bias_scr[0, :, BS * h:BS * (h + 1)] = jnp.full((BS, BS), far, F32)
            bias_scr[1, :, BS * h:BS * (h + 1)] = band[:BS]
            bias_scr[2, :, BS * h:BS * (h + 1)] = band[BS:]

    blk_row = lax.broadcasted_iota(I32, (BF16_SUBLANES, 1), 0)
    kmean = jnp.sum(k_ref[...].astype(F32).reshape(NBK, BS, LANES), axis=1) / BS
    kmean = jnp.concatenate([kmean, jnp.zeros((BF16_SUBLANES - NBK, LANES), F32)], axis=0)
    km_hi = kmean.astype(BF16)
    km_scr[0] = km_hi
    km_scr[1] = (kmean - km_hi.astype(F32)).astype(BF16)

    def scores(n, case):
        k0 = pl.multiple_of(n * BS, BS)
        return _dot_nt(k_ref[pl.ds(k0, BS), :], q8_scr[...]) + bias_scr[case]

    def weighted_values(n, e):
        pv = jnp.dot(vt_ref[n], e.astype(BF16), preferred_element_type=F32)
        return jnp.concatenate([pv[:HEAD_DIM, :GQ], pv[HEAD_DIM:, GQ:]], axis=1)

    def qblock(mi, carry):
        r0 = pl.multiple_of(mi * BS, BS)
        q8 = _padded_heads(q_ref[pl.ds(r0, BS), :], H)
        q8_scr[...] = q8
        gate = _dot_nt(km_scr[0], q8) + _dot_nt(km_scr[1], q8)
        past = blk_row < mi
        gm = jnp.where(past, gate, NEG)
        cnt = jnp.zeros(gm.shape, F32)
        for n in range(NBK - 1):
            col = gm[n:n + 1, :]
            beats = (gm > col) | ((gm == col) & (blk_row < n))
            c = jnp.sum(jnp.where(beats, 1.0, 0.0), axis=0, keepdims=True)
            cnt = jnp.where(blk_row == n, c, cnt)
        sel_scr[...] = jnp.where(past & (cnt < MOBA_TOPK), 1.0, 0.0)

        key_i = lax.broadcasted_iota(I32, (BS, NQ), 0)
        qry_i = lax.broadcasted_iota(I32, (BS, NQ), 1) & (BS - 1)
        s = jnp.where(key_i <= qry_i, scores(mi, 2), NEG)
        m0 = jnp.max(s, axis=0, keepdims=True)
        e = jnp.exp(s - m0)
        m_scr[...] = m0
        l_scr[...] = jnp.sum(e, axis=0, keepdims=True)
        acc_scr[...] = weighted_values(mi, e)

        def kvblock(j, c2):
            n = mi - j
            s = jnp.where(sel_scr[pl.ds(n, 1), :] > 0.5, scores(n, jnp.where(j == 1, 1, 0)), NEG)
            m_prev = m_scr[...]
            m_new = jnp.maximum(m_prev, jnp.max(s, axis=0, keepdims=True))
            alpha = jnp.exp(m_prev - m_new)
            e = jnp.exp(s - m_new)
            l_scr[...] = alpha * l_scr[...] + jnp.sum(e, axis=0, keepdims=True)
            acc_scr[...] = alpha * acc_scr[...] + weighted_values(n, e)
            m_scr[...] = m_new
            return c2

        lax.fori_loop(1, mi + 1, kvblock, 0)
        ot = acc_scr[...] / l_scr[...]
        for pr in range(H // 2):
            pair = ot[:, 2 * BS * pr:2 * BS * (pr + 1)]
            o_ref[pl.ds(r0, BS), LANES * pr:LANES * (pr + 1)] = jnp.concatenate(
                [pair[:, :BS], pair[:, BS:]], axis=0).T
        return carry

    lax.fori_loop(0, NBK, qblock, 0)


def _moba_call(tbl_flat, qb, kb, vbt, B, S):
    BS = MOBA_BLOCK
    NQ = MOBA_Q_HEADS * BS
    NBK = S // BS
    bkt = jnp.asarray(np.ascontiguousarray(_band_buckets(BS).T))
    smem = pl.BlockSpec(memory_space=pltpu.SMEM)
    return pl.pallas_call(
        _moba_kernel,
        grid=(B,),
        in_specs=[smem,
                  pl.BlockSpec(bkt.shape, lambda b: (0, 0)),
                  pl.BlockSpec((S, qb.shape[1]), lambda b: (b, 0)),
                  pl.BlockSpec((S, LANES), lambda b: (b, 0)),
                  pl.BlockSpec((NBK, LANES, BS), lambda b: (b, 0, 0))],
        out_specs=pl.BlockSpec((S, qb.shape[1]), lambda b: (b, 0)),
        out_shape=jax.ShapeDtypeStruct(qb.shape, F32),
        scratch_shapes=[pltpu.VMEM((3, BS, NQ), F32),
                        pltpu.VMEM((2, BF16_SUBLANES, LANES), BF16),
                        pltpu.VMEM((NQ, LANES), BF16),
                        pltpu.VMEM((BF16_SUBLANES, NQ), F32),
                        pltpu.VMEM((1, NQ), F32),
                        pltpu.VMEM((1, NQ), F32),
                        pltpu.VMEM((HEAD_DIM, NQ), F32)],
        compiler_params=pltpu.CompilerParams(
            dimension_semantics=("arbitrary",), vmem_limit_bytes=VMEM_LIMIT),
        name="moba",
    )(tbl_flat, bkt, qb, kb, vbt)


def _outproj_kernel(oa_ref, ob_ref, x_ref, ga_ref, gb_ref, woa_ref, wob_ref, fg_ref,
                    wrh_ref, wrl_ref, rb_ref, x1_ref, hf_ref, eid_ref, wt_ref):
    na = _rmsnorm(oa_ref[...], ga_ref[...]).astype(BF16)
    nb = _rmsnorm(ob_ref[...], gb_ref[...]).astype(BF16)
    y = (jnp.dot(na, woa_ref[...], preferred_element_type=F32)
         + jnp.dot(nb, wob_ref[...], preferred_element_type=F32))
    x1 = x_ref[...] + y
    x1_ref[...] = x1
    hf = _rmsnorm(x1, fg_ref[...])
    _to_token_tiles(hf_ref, hf)

    hf_hi = hf.astype(BF16)
    hf_lo = (hf - hf_hi.astype(F32)).astype(BF16)
    lt = (_dot_nt(wrh_ref[...], hf_hi) + _dot_nt(wrh_ref[...], hf_lo)
          + _dot_nt(wrl_ref[...], hf_hi) + rb_ref[...])
    gl = [lt[j:j + 1, :] for j in range(N_GROUPS)]
    best = gl[0]
    gsel = jnp.zeros(best.shape, I32)
    for j in range(1, N_GROUPS):
        better = gl[j] > best
        gsel = jnp.where(better, j, gsel)
        best = jnp.where(better, gl[j], best)
    gsum = jnp.zeros(best.shape, F32)
    for j in range(N_GROUPS):
        gsum = gsum + jnp.exp(gl[j] - best)
    gw = 1.0 / gsum
    E = EXPERTS_PER_GROUP
    es = lt[EXPERT_ROW0:EXPERT_ROW0 + E, :]
    for j in range(1, N_GROUPS):
        es = jnp.where(gsel == j, lt[EXPERT_ROW0 + E * j:EXPERT_ROW0 + E * (j + 1), :], es)
    row = lax.broadcasted_iota(I32, es.shape, 0)
    v1 = jnp.max(es, axis=0, keepdims=True)
    i1 = jnp.min(jnp.where(es == v1, row, E), axis=0, keepdims=True)
    es2 = jnp.where(row == i1, -jnp.inf, es)
    v2 = jnp.max(es2, axis=0, keepdims=True)
    i2 = jnp.min(jnp.where(es2 == v2, row, E), axis=0, keepdims=True)
    e2 = jnp.exp(v2 - v1)
    den = 1.0 + e2
    eid_ref[...] = jnp.concatenate([gsel * E + i1, gsel * E + i2], axis=0)
    wt_ref[...] = jnp.concatenate([gw * (1.0 / den), gw * (e2 / den)], axis=0)


def _outproj_call(oa, ob, x2d, ga, gb, woa, wob, fg, wrh, wrl, rb):
    T, D = x2d.shape
    tm = TOKEN_TILE
    tok = lambda w: pl.BlockSpec((tm, w), lambda i: (i, 0))
    full = lambda a: pl.BlockSpec(a.shape, lambda i: (0, 0))
    col = pl.BlockSpec((EXPERT_TOPK, tm), lambda i: (0, i))
    return pl.pallas_call(
        _outproj_kernel,
        grid=(T // tm,),
        in_specs=[tok(oa.shape[1]), tok(ob.shape[1]), tok(D), full(ga), full(gb), full(woa), full(wob),
                  full(fg), full(wrh), full(wrl), full(rb)],
        out_specs=[tok(D), pl.BlockSpec((tm * ROW_TILE, LANES), lambda i: (i, 0)), col, col],
        out_shape=[jax.ShapeDtypeStruct((T, D), F32), jax.ShapeDtypeStruct((T * ROW_TILE, LANES), F32),
                   jax.ShapeDtypeStruct((EXPERT_TOPK, T), I32), jax.ShapeDtypeStruct((EXPERT_TOPK, T), F32)],
        compiler_params=pltpu.CompilerParams(
            dimension_semantics=("arbitrary",), vmem_limit_bytes=VMEM_LIMIT),
        name="outproj",
    )(oa, ob, x2d, ga, gb, woa, wob, fg, wrh, wrl, rb)


def _moe_kernel(be_ref, nu_ref, src_ref, srcn_ref, dst_ref, hf_hbm, wg_ref, wu_ref, wd_ref, out_hbm,
                xbuf, obuf, wgb, wub, wdb, gsem, ssem):
    i = pl.program_id(0)
    n_used = nu_ref[0]
    M = xbuf.shape[1] // ROW_TILE
    slot = i % 2

    def gather_rows(idx_ref, buf_slot):
        for r in range(M):
            tok_row = pl.multiple_of(idx_ref[0, 0, r] * ROW_TILE, ROW_TILE)
            pltpu.make_async_copy(hf_hbm.at[pl.ds(tok_row, ROW_TILE)],
                                  xbuf.at[buf_slot, pl.ds(r * ROW_TILE, ROW_TILE)],
                                  gsem.at[buf_slot]).start(priority=r % DMA_PRIORITIES)

    def scatter_rows(idx_ref, buf_slot):
        for r in range(M):
            dst_row = pl.multiple_of(idx_ref[0, 0, r] * ROW_TILE, ROW_TILE)
            pltpu.make_async_copy(obuf.at[buf_slot, pl.ds(r * ROW_TILE, ROW_TILE)],
                                  out_hbm.at[pl.ds(dst_row, ROW_TILE)],
                                  ssem.at[buf_slot]).start(priority=r % DMA_PRIORITIES)

    def wait_rows(buf, buf_slot, sem):
        pltpu.make_async_copy(hf_hbm.at[pl.ds(0, M * ROW_TILE)], buf.at[buf_slot], sem.at[buf_slot]).wait()

    @pl.when(i < n_used)
    def _compute():
        @pl.when(i == 0)
        def _first_gather():
            gather_rows(src_ref, 0)
            obuf[1] = jnp.zeros(obuf.shape[1:], F32)
            spare = pltpu.make_async_copy(
                obuf.at[1], out_hbm.at[pl.ds(out_hbm.shape[0] - M * ROW_TILE, M * ROW_TILE)], ssem.at[1])
            spare.start()
            spare.wait()

        @pl.when((i == 0) | (be_ref[i] != be_ref[jnp.maximum(i - 1, 0)]))
        def _cast_weights():
            wgb[...] = wg_ref[0].astype(BF16)
            wub[...] = wu_ref[0].astype(BF16)
            wdb[...] = wd_ref[0].astype(BF16)

        wait_rows(xbuf, slot, gsem)

        @pl.when(i >= 2)
        def _reuse_obuf():
            wait_rows(obuf, slot, ssem)

        gather_rows(srcn_ref, 1 - slot)
        xb = _from_token_tiles(xbuf.at[slot], M).astype(BF16)
        a = jnp.dot(xb, wgb[...], preferred_element_type=F32)
        u = jnp.dot(xb, wub[...], preferred_element_type=F32)
        hmid = (a * jax.nn.sigmoid(a)) * u
        _to_token_tiles(obuf.at[slot], jnp.dot(hmid.astype(BF16), wdb[...], preferred_element_type=F32))
        scatter_rows(dst_ref, slot)

        @pl.when(i == n_used - 1)
        def _drain():
            wait_rows(xbuf, 1 - slot, gsem)
            wait_rows(obuf, slot, ssem)

            @pl.when(i >= 1)
            def _drain_prev():
                wait_rows(obuf, 1 - slot, ssem)


def _moe_call(block_expert, n_used, src_tok, dst_row, hf, w_g, w_u, w_d):
    T = hf.shape[0] // ROW_TILE
    M = DISPATCH_BLOCK
    nblk = block_expert.shape[0]
    D, DE = w_g.shape[1], w_g.shape[2]
    assert D == ROW_TILE * LANES and hf.shape[1] == LANES
    idx_spec = lambda f: pl.BlockSpec((1, 1, M), f, memory_space=pltpu.SMEM)
    cur = lambda i, be, nu: (i, 0, 0)
    nxt = lambda i, be, nu: (jnp.minimum(i + 1, nblk - 1), 0, 0)
    wspec = lambda a, b: pl.BlockSpec((1, a, b), lambda i, be, nu: (be[i], 0, 0))
    grid_spec = pltpu.PrefetchScalarGridSpec(
        num_scalar_prefetch=2,
        grid=(nblk,),
        in_specs=[idx_spec(cur), idx_spec(nxt), idx_spec(cur),
                  pl.BlockSpec(memory_space=pl.ANY),
                  wspec(D, DE), wspec(D, DE), wspec(DE, D)],
        out_specs=pl.BlockSpec(memory_space=pl.ANY),
        scratch_shapes=[pltpu.VMEM((2, M * ROW_TILE, LANES), F32), pltpu.VMEM((2, M * ROW_TILE, LANES), F32),
                        pltpu.VMEM((D, DE), BF16), pltpu.VMEM((D, DE), BF16), pltpu.VMEM((DE, D), BF16),
                        pltpu.SemaphoreType.DMA((2,)), pltpu.SemaphoreType.DMA((2,))],
    )
    src3 = src_tok.reshape(nblk, 1, M)
    return pl.pallas_call(
        _moe_kernel,
        grid_spec=grid_spec,
        out_shape=jax.ShapeDtypeStruct(((EXPERT_TOPK * T + M) * ROW_TILE, LANES), F32),
        compiler_params=pltpu.CompilerParams(
            dimension_semantics=("arbitrary",), vmem_limit_bytes=VMEM_LIMIT),
        name="moe",
    )(block_expert, n_used, src3, src3, dst_row.reshape(nblk, 1, M), hf, w_g, w_u, w_d)


def _ple_kernel(x1_ref, m0_ref, m1_ref, wt_ref, p_ref, pg_ref, wgate_ref, wproj_ref, pog_ref, out_ref):
    proj = _rmsnorm(jnp.dot(p_ref[...].astype(BF16), wproj_ref[...], preferred_element_type=F32),
                    pog_ref[...])
    tm = x1_ref.shape[0]
    moe = (_from_token_tiles(m0_ref, tm) * wt_ref[:, 0:1] + _from_token_tiles(m1_ref, tm) * wt_ref[:, 1:2])
    x2 = x1_ref[...] + moe
    hn = _rmsnorm(x2, pg_ref[...]).astype(BF16)
    gate = jax.nn.sigmoid(jnp.dot(hn, wgate_ref[...], preferred_element_type=F32))
    out_ref[...] = x2 + gate * proj


def _ple_call(moe_rows, x1, wt_cols, p2d, pg, wgate, wproj, pog):
    T, D = x1.shape
    tm = PLE_TILE
    tok = lambda w: pl.BlockSpec((tm, w), lambda i: (i, 0))
    full = lambda a: pl.BlockSpec(a.shape, lambda i: (0, 0))
    first = pl.BlockSpec((tm * ROW_TILE, LANES), lambda i: (i, 0))
    second = pl.BlockSpec((tm * ROW_TILE, LANES), lambda i: (T // tm + i, 0))
    return pl.pallas_call(
        _ple_kernel,
        grid=(T // tm,),
        in_specs=[tok(D), first, second, tok(EXPERT_TOPK), tok(p2d.shape[1]),
                  full(pg), full(wgate), full(wproj), full(pog)],
        out_specs=tok(D),
        out_shape=jax.ShapeDtypeStruct((T, D), F32),
        compiler_params=pltpu.CompilerParams(
            dimension_semantics=("arbitrary",), vmem_limit_bytes=VMEM_LIMIT),
        name="ple",
    )(x1, moe_rows, moe_rows, wt_cols, p2d, pg, wgate, wproj, pog)


def _dispatch_plan(eid, T):
    A = EXPERT_TOPK * T
    M = DISPATCH_BLOCK
    nblk = A // M + N_EXPERTS
    e_flat = eid.reshape(A)
    experts = jnp.arange(N_EXPERTS, dtype=I32)
    counts = jnp.sum((e_flat[:, None] == experts[None, :]).astype(I32), axis=0)
    padded = (counts + M - 1) // M * M
    n_used = jnp.sum(padded) // M
    row = jnp.arange(M, dtype=I32)[None, :]
    pad_keys = jnp.where(row < (padded - counts)[:, None], 2 * experts[:, None] + 1, 2 * N_EXPERTS)
    keys = jnp.concatenate([2 * e_flat, pad_keys.reshape(-1)])
    vals = jnp.concatenate([jnp.arange(A, dtype=I32), jnp.full((N_EXPERTS * M,), -1, I32)])
    keys, vals = lax.sort((keys, vals), num_keys=1, is_stable=True)
    keys = keys.reshape(nblk, M)
    a_s = vals.reshape(nblk, M)
    blk = jnp.arange(nblk, dtype=I32)
    be = jnp.minimum(keys[:, 0] // 2, N_EXPERTS - 1)
    be = jnp.where(blk < n_used, be, be[jnp.maximum(n_used - 1, 0)])
    valid = a_s >= 0
    src_tok = jnp.where(valid, jnp.where(a_s >= T, a_s - T, a_s), 0)
    dst_row = jnp.where(valid, a_s, A + row)
    return be, n_used.reshape(1).astype(I32), src_tok, dst_row


def _layer(x2d, p2d, rel_bias, attn_norm, w_in, swa_q_norm, swa_k_norm, swa_sinks, moba_q_norm,
           moba_k_norm, swa_out_norm, moba_out_norm, w_out, ffn_norm, w_rg, b_rg, w_re, b_re,
           w_g, w_u, w_d, ple_norm, w_ple_gate, w_ple_proj, ple_out_norm, B, S):
    T, D = x2d.shape
    row = lambda v: v.reshape(1, -1).astype(F32)
    head_gain = jnp.concatenate([
        jnp.tile(swa_q_norm, SWA_Q_HEADS), jnp.tile(swa_k_norm, SWA_KV_HEADS),
        jnp.tile(moba_q_norm, MOBA_Q_HEADS), jnp.tile(moba_k_norm, MOBA_KV_HEADS)])
    w_in_b = w_in.astype(BF16)
    qa_w, kv_w = SWA_Q_HEADS * HEAD_DIM, SWA_KV_HEADS * HEAD_DIM
    qb_w = MOBA_Q_HEADS * HEAD_DIM
    c_va = qa_w + kv_w
    c_qb = c_va + kv_w
    c_vb = c_qb + qb_w + kv_w
    w_qk = jnp.concatenate([w_in_b[:, :c_va], w_in_b[:, c_qb:c_vb]], axis=1)
    w_vt = jnp.concatenate([w_in_b[:, c_va:c_qb], w_in_b[:, c_vb:]], axis=1).T
    qa, ka, qb, kb, vat, vbt = _qkv_call(x2d, row(attn_norm), w_qk, w_vt, row(head_gain))

    tbl_flat = rel_bias.astype(F32).reshape(-1)
    oa = _swa_call(tbl_flat, swa_sinks.astype(F32), qa, ka, vat, B, S)
    ob = _moba_call(tbl_flat, qb, kb, vbt, B, S)

    pad_rows = lambda n: jnp.zeros((n, D), F32)
    wr = jnp.concatenate([w_rg.T, pad_rows(EXPERT_ROW0 - N_GROUPS), w_re.T,
                          pad_rows(ROUTER_ROWS - EXPERT_ROW0 - N_EXPERTS)], axis=0)
    wr_hi = wr.astype(BF16)
    wr_lo = (wr - wr_hi.astype(F32)).astype(BF16)
    rb = jnp.concatenate([b_rg, jnp.zeros((EXPERT_ROW0 - N_GROUPS,), F32), b_re,
                          jnp.zeros((ROUTER_ROWS - EXPERT_ROW0 - N_EXPERTS,), F32)]).reshape(ROUTER_ROWS, 1)
    wo = w_out.astype(BF16)
    na_w = SWA_Q_HEADS * HEAD_DIM
    x1, hf, eid, wts = _outproj_call(oa, ob, x2d, row(swa_out_norm), row(moba_out_norm),
                                     wo[:na_w], wo[na_w:], row(ffn_norm), wr_hi, wr_lo, rb)

    be, n_used, src_tok, dst_row = _dispatch_plan(eid, T)
    moe_rows = _moe_call(be, n_used, src_tok, dst_row, hf, w_g, w_u, w_d)
    return _ple_call(moe_rows, x1, wts.T, p2d, row(ple_norm), w_ple_gate.astype(BF16),
                     w_ple_proj.astype(BF16), row(ple_out_norm))


def kernel(x, p, rel_bias, attn_norm, w_in, swa_q_norm, swa_k_norm, swa_sinks, moba_q_norm, moba_k_norm,
           swa_out_norm, moba_out_norm, w_out, ffn_norm, w_router_group, b_router_group, w_router_expert,
           b_router_expert, w_exp_gate, w_exp_up, w_exp_down, ple_norm, w_ple_gate, w_ple_proj, ple_out_norm):
    B, S, D = x.shape
    x2d = x.reshape(B * S, D)
    for i in range(p.shape[0]):
        x2d = _layer(x2d, p[i].reshape(B * S, -1), rel_bias, attn_norm[i], w_in[i], swa_q_norm[i],
                     swa_k_norm[i], swa_sinks[i], moba_q_norm[i], moba_k_norm[i], swa_out_norm[i],
                     moba_out_norm[i], w_out[i], ffn_norm[i], w_router_group[i], b_router_group[i],
                     w_router_expert[i], b_router_expert[i], w_exp_gate[i], w_exp_up[i], w_exp_down[i],
                     ple_norm[i], w_ple_gate[i], w_ple_proj[i], ple_out_norm[i], B, S)
    return x2d.reshape(B, S, D)
```

```python
import math

import numpy as np
import jax
import jax.numpy as jnp
from jax import lax
from jax.experimental import pallas as pl
from jax.experimental.pallas import tpu as pltpu

F32 = jnp.float32
BF16 = jnp.bfloat16
I32 = jnp.int32

HEAD_DIM = 64
SWA_Q_HEADS = 8
SWA_KV_HEADS = 2
SWA_WINDOW = 128
MOBA_Q_HEADS = 8
MOBA_KV_HEADS = 2
MOBA_BLOCK = 256
MOBA_TOPK = 3
N_HEADS = SWA_Q_HEADS + MOBA_Q_HEADS
REL_BUCKETS = 32
REL_MAX_DIST = 128
N_GROUPS = 4
EXPERTS_PER_GROUP = 8
N_EXPERTS = N_GROUPS * EXPERTS_PER_GROUP
EXPERT_TOPK = 2
DISPATCH_BLOCK = 256
PAD_MARK = 0xFFFF
EPS = 1e-6
NEG = -1e30

DMA_PRIORITIES = 2
LANES = 128
ROW_TILE = 8
BF16_SUBLANES = 16
GROUP_HEADS = 4
GROUP_W = GROUP_HEADS * HEAD_DIM
ROUTER_ROWS = 128
EXPERT_ROW0 = 8
TOKEN_TILE = 512
PLE_TILE = 512
VMEM_LIMIT = 52 * 1024 * 1024


def _dot_nt(a, b):
    return lax.dot_general(a, b, (((1,), (1,)), ((), ())), preferred_element_type=F32)


def _rmsnorm(x, g):
    ms = jnp.mean(x * x, axis=-1, keepdims=True)
    return x * lax.rsqrt(ms + EPS) * g


def _to_token_tiles(ref, x):
    m = x.shape[0]
    for j in range(x.shape[1] // LANES):
        ref[pl.ds(j, m, stride=ROW_TILE), :] = x[:, LANES * j:LANES * (j + 1)]


def _from_token_tiles(ref, m):
    return jnp.concatenate([ref[pl.ds(j, m, stride=ROW_TILE), :] for j in range(ref.shape[0] // m)], axis=1)


def _rel_bucket_np(dist):
    n = np.maximum(dist, 0)
    exact = REL_BUCKETS // 2
    nf = np.maximum(n, 1).astype(np.float32)
    large = exact + (np.log(nf / exact) / math.log(REL_MAX_DIST / exact)
                     * (REL_BUCKETS - exact)).astype(np.int32)
    return np.where(n < exact, n, np.minimum(large, REL_BUCKETS - 1)).astype(np.int32)


def _band_buckets(block):
    qi = np.arange(block)[:, None]
    kj = np.arange(2 * block)[None, :]
    return _rel_bucket_np(qi + block - kj)


def _bias_from_buckets(bkt, tbl_ref, head):
    acc = jnp.zeros(bkt.shape, F32)
    for j in range(REL_BUCKETS):
        acc = jnp.where(bkt == j, tbl_ref[j * N_HEADS + head], acc)
    return acc


def _qkv_kernel(x_ref, g_ref, w_ref, wvt_ref, hg_ref, qa_ref, ka_ref, qb_ref, kb_ref, vat_ref, vbt_ref):
    h = _rmsnorm(x_ref[...], g_ref[...]).astype(BF16)
    acc = jnp.dot(h, w_ref[...], preferred_element_type=F32)
    lo = lax.broadcasted_iota(I32, (1, LANES), 1) < HEAD_DIM

    def head_normed(c, scale):
        blk = acc[:, LANES * c:LANES * (c + 1)]
        sq = blk * blk
        s_lo = jnp.sum(jnp.where(lo, sq, 0.0), axis=-1, keepdims=True)
        s_hi = jnp.sum(jnp.where(lo, 0.0, sq), axis=-1, keepdims=True)
        inv = jnp.where(lo, lax.rsqrt(s_lo / HEAD_DIM + EPS), lax.rsqrt(s_hi / HEAD_DIM + EPS))
        return blk * inv * hg_ref[:, LANES * c:LANES * (c + 1)] * scale

    scale = HEAD_DIM ** -0.5
    for c in range(4):
        qa_ref[:, LANES * c:LANES * (c + 1)] = head_normed(c, scale).astype(BF16)
        qb_ref[:, LANES * c:LANES * (c + 1)] = head_normed(5 + c, scale).astype(BF16)
    ka_ref[...] = head_normed(4, 1.0).astype(BF16)
    kb_ref[...] = head_normed(9, 1.0).astype(BF16)
    vt = _dot_nt(wvt_ref[...], h).astype(BF16)
    for j in range(vat_ref.shape[0]):
        vat_ref[j] = vt[:LANES, SWA_WINDOW * j:SWA_WINDOW * (j + 1)]
    for j in range(vbt_ref.shape[0]):
        vbt_ref[j] = vt[LANES:, MOBA_BLOCK * j:MOBA_BLOCK * (j + 1)]


def _qkv_call(x2d, attn_g, w_qk, w_vt, head_gain):
    T, D = x2d.shape
    tm = TOKEN_TILE
    tok = lambda w: pl.BlockSpec((tm, w), lambda i: (i, 0))
    full = lambda a: pl.BlockSpec(a.shape, lambda i: (0, 0))
    out_w = (SWA_Q_HEADS * HEAD_DIM, LANES, MOBA_Q_HEADS * HEAD_DIM, LANES)
    slabs = lambda blk: pl.BlockSpec((tm // blk, LANES, blk), lambda i: (i, 0, 0))
    slab_shape = lambda blk: jax.ShapeDtypeStruct((T // blk, LANES, blk), BF16)
    return pl.pallas_call(
        _qkv_kernel,
        grid=(T // tm,),
        in_specs=[tok(D), full(attn_g), full(w_qk), full(w_vt), full(head_gain)],
        out_specs=[tok(w) for w in out_w] + [slabs(SWA_WINDOW), slabs(MOBA_BLOCK)],
        out_shape=[jax.ShapeDtypeStruct((T, w), BF16) for w in out_w]
        + [slab_shape(SWA_WINDOW), slab_shape(MOBA_BLOCK)],
        compiler_params=pltpu.CompilerParams(
            dimension_semantics=("arbitrary",), vmem_limit_bytes=VMEM_LIMIT),
        name="qkv",
    )(x2d, attn_g, w_qk, w_vt, head_gain)


def _padded_heads(qblk, heads):
    zeros = jnp.zeros((qblk.shape[0], HEAD_DIM), qblk.dtype)
    pieces = []
    for h in range(heads):
        piece = qblk[:, HEAD_DIM * h:HEAD_DIM * (h + 1)]
        pieces.append(jnp.concatenate([piece, zeros] if h < GROUP_HEADS else [zeros, piece], axis=1))
    return jnp.concatenate(pieces, axis=0)


def _swa_kernel(tbl_ref, sink_ref, bkt_ref, q_ref, k_ref, vt_ref, o_ref, bias_scr):
    W = SWA_WINDOW
    H = SWA_Q_HEADS
    S = q_ref.shape[0]

    @pl.when(pl.program_id(0) == 0)
    def _init_bias():
        bkt = bkt_ref[...]
        for h in range(H):
            bias_scr[:, W * h:W * (h + 1)] = _bias_from_buckets(bkt, tbl_ref, h)

    key_i = lax.broadcasted_iota(I32, (2 * W, H * W), 0)
    qry_i = lax.broadcasted_iota(I32, (2 * W, H * W), 1) & (W - 1)
    dist = qry_i + W - key_i
    band = (dist >= 0) & (dist < W)
    own_part = key_i >= W
    head_of_lane = lax.broadcasted_iota(I32, (1, H * W), 1) // W
    sinks = jnp.zeros((1, H * W), F32)
    for h in range(H):
        sinks = jnp.where(head_of_lane == h, sink_ref[h], sinks)

    def body(i, carry):
        r0 = pl.multiple_of(i * W, W)
        prev = jnp.maximum(i - 1, 0)
        p0 = pl.multiple_of(prev * W, W)
        q8 = _padded_heads(q_ref[pl.ds(r0, W), :], H)
        kband = jnp.concatenate([k_ref[pl.ds(p0, W), :], k_ref[pl.ds(r0, W), :]], axis=0)
        mask = band & (own_part | (i > 0))
        s = jnp.where(mask, _dot_nt(kband, q8) + bias_scr[...], NEG)
        m = jnp.maximum(jnp.max(s, axis=0, keepdims=True), sinks)
        e = jnp.exp(s - m)
        den = jnp.sum(e, axis=0, keepdims=True) + jnp.exp(sinks - m)
        vtband = jnp.concatenate([vt_ref[prev], vt_ref[i]], axis=1)
        ot = jnp.dot(vtband, e.astype(BF16), preferred_element_type=F32) / den
        for pr in range(H // 2):
            f0 = HEAD_DIM * (2 * pr // GROUP_HEADS)
            pair = ot[f0:f0 + HEAD_DIM, 2 * W * pr:2 * W * (pr + 1)]
            o_ref[pl.ds(r0, W), LANES * pr:LANES * (pr + 1)] = jnp.concatenate(
                [pair[:, :W], pair[:, W:]], axis=0).T.astype(o_ref.dtype)
        return carry

    lax.fori_loop(0, S // W, body, 0)


def _swa_call(tbl_flat, sinks, qa, ka, vat, B, S):
    W = SWA_WINDOW
    bkt = jnp.asarray(np.ascontiguousarray(_band_buckets(W).T))
    smem = pl.BlockSpec(memory_space=pltpu.SMEM)
    return pl.pallas_call(
        _swa_kernel,
        grid=(B,),
        in_specs=[smem, smem,
                  pl.BlockSpec(bkt.shape, lambda b: (0, 0)),
                  pl.BlockSpec((S, qa.shape[1]), lambda b: (b, 0)),
                  pl.BlockSpec((S, LANES), lambda b: (b, 0)),
                  pl.BlockSpec((S // W, LANES, W), lambda b: (b, 0, 0))],
        out_specs=pl.BlockSpec((S, qa.shape[1]), lambda b: (b, 0)),
        out_shape=jax.ShapeDtypeStruct(qa.shape, BF16),
        scratch_shapes=[pltpu.VMEM((2 * W, SWA_Q_HEADS * W), F32)],
        compiler_params=pltpu.CompilerParams(
            dimension_semantics=("arbitrary",), vmem_limit_bytes=VMEM_LIMIT),
        name="swa",
    )(tbl_flat, sinks, bkt, qa, ka, vat)


def _moba_kernel(tbl_ref, bkt_ref, q_ref, k_ref, vt_ref, o_ref,
                 bias_scr, km_scr, q8_scr, sel_scr, m_scr, l_scr, acc_scr):
    BS = MOBA_BLOCK
    H = MOBA_Q_HEADS
    S = q_ref.shape[0]
    NBK = S // BS
    NQ = H * BS
    GQ = GROUP_HEADS * BS

    @pl.when(pl.program_id(0) == 0)
    def _init_bias():
        bkt = bkt_ref[...]
        for h in range(H):
            band = _bias_from_buckets(bkt, tbl_ref, SWA_Q_HEADS + h)
            far = tbl_ref[(REL_BUCKETS - 1) * N_HEADS + SWA_Q_HEADS + h]
            bias_scr[0, :, BS * h:BS * (h + 1)] = jnp.full((BS, BS), far, F32)
            bias_scr[1, :, BS * h:BS * (h + 1)] = band[:BS]
            bias_scr[2, :, BS * h:BS * (h + 1)] = band[BS:]

    blk_row = lax.broadcasted_iota(I32, (BF16_SUBLANES, 1), 0)
    kmean = jnp.sum(k_ref[...].astype(F32).reshape(NBK, BS, LANES), axis=1) / BS
    kmean = jnp.concatenate([kmean, jnp.zeros((BF16_SUBLANES - NBK, LANES), F32)], axis=0)
    km_hi = kmean.astype(BF16)
    km_scr[0] = km_hi
    km_scr[1] = (kmean - km_hi.astype(F32)).astype(BF16)

    def scores(n, case):
        k0 = pl.multiple_of(n * BS, BS)
        return _dot_nt(k_ref[pl.ds(k0, BS), :], q8_scr[...]) + bias_scr[case]

    def weighted_values(n, e):
        pv = jnp.dot(vt_ref[n], e.astype(BF16), preferred_element_type=F32)
        return jnp.concatenate([pv[:HEAD_DIM, :GQ], pv[HEAD_DIM:, GQ:]], axis=1)

    def qblock(mi, carry):
        r0 = pl.multiple_of(mi * BS, BS)
        q8 = _padded_heads(q_ref[pl.ds(r0, BS), :], H)
        q8_scr[...] = q8
        gate = _dot_nt(km_scr[0], q8) + _dot_nt(km_scr[1], q8)
        past = blk_row < mi
        gm = jnp.where(past, gate, NEG)
        cnt = jnp.zeros(gm.shape, F32)
        for n in range(NBK - 1):
            col = gm[n:n + 1, :]
            beats = (gm > col) | ((gm == col) & (blk_row < n))
            c = jnp.sum(jnp.where(beats, 1.0, 0.0), axis=0, keepdims=True)
            cnt = jnp.where(blk_row == n, c, cnt)
        sel_scr[...] = jnp.where(past & (cnt < MOBA_TOPK), 1.0, 0.0)

        key_i = lax.broadcasted_iota(I32, (BS, NQ), 0)
        qry_i = lax.broadcasted_iota(I32, (BS, NQ), 1) & (BS - 1)
        s = jnp.where(key_i <= qry_i, scores(mi, 2), NEG)
        m0 = jnp.max(s, axis=0, keepdims=True)
        e = jnp.exp(s - m0)
        m_scr[...] = m0
        l_scr[...] = jnp.sum(e, axis=0, keepdims=True)
        acc_scr[...] = weighted_values(mi, e)

        def kvblock(j, c2):
            n = mi - j
            s = jnp.where(sel_scr[pl.ds(n, 1), :] > 0.5, scores(n, jnp.where(j == 1, 1, 0)), NEG)
            m_prev = m_scr[...]
            m_new = jnp.maximum(m_prev, jnp.max(s, axis=0, keepdims=True))
            alpha = jnp.exp(m_prev - m_new)
            e = jnp.exp(s - m_new)
            l_scr[...] = alpha * l_scr[...] + jnp.sum(e, axis=0, keepdims=True)
            acc_scr[...] = alpha * acc_scr[...] + weighted_values(n, e)
            m_scr[...] = m_new
            return c2

        lax.fori_loop(1, mi + 1, kvblock, 0)
        ot = acc_scr[...] / l_scr[...]
        for pr in range(H // 2):
            pair = ot[:, 2 * BS * pr:2 * BS * (pr + 1)]
            o_ref[pl.ds(r0, BS), LANES * pr:LANES * (pr + 1)] = jnp.concatenate(
                [pair[:, :BS], pair[:, BS:]], axis=0).T.astype(o_ref.dtype)
        return carry

    lax.fori_loop(0, NBK, qblock, 0)


def _moba_call(tbl_flat, qb, kb, vbt, B, S):
    BS = MOBA_BLOCK
    NQ = MOBA_Q_HEADS * BS
    NBK = S // BS
    bkt = jnp.asarray(np.ascontiguousarray(_band_buckets(BS).T))
    smem = pl.BlockSpec(memory_space=pltpu.SMEM)
    return pl.pallas_call(
        _moba_kernel,
        grid=(B,),
        in_specs=[smem,
                  pl.BlockSpec(bkt.shape, lambda b: (0, 0)),
                  pl.BlockSpec((S, qb.shape[1]), lambda b: (b, 0)),
                  pl.BlockSpec((S, LANES), lambda b: (b, 0)),
                  pl.BlockSpec((NBK, LANES, BS), lambda b: (b, 0, 0))],
        out_specs=pl.BlockSpec((S, qb.shape[1]), lambda b: (b, 0)),
        out_shape=jax.ShapeDtypeStruct(qb.shape, BF16),
        scratch_shapes=[pltpu.VMEM((3, BS, NQ), F32),
                        pltpu.VMEM((2, BF16_SUBLANES, LANES), BF16),
                        pltpu.VMEM((NQ, LANES), BF16),
                        pltpu.VMEM((BF16_SUBLANES, NQ), F32),
                        pltpu.VMEM((1, NQ), F32),
                        pltpu.VMEM((1, NQ), F32),
                        pltpu.VMEM((HEAD_DIM, NQ), F32)],
        compiler_params=pltpu.CompilerParams(
            dimension_semantics=("arbitrary",), vmem_limit_bytes=VMEM_LIMIT),
        name="moba",
    )(tbl_flat, bkt, qb, kb, vbt)


def _outproj_kernel(oa_ref, ob_ref, x_ref, ga_ref, gb_ref, woa_ref, wob_ref, fg_ref,
                    wrh_ref, wrl_ref, rb_ref, x1_ref, hf_ref, eid_ref, wt_ref):
    na = _rmsnorm(oa_ref[...].astype(F32), ga_ref[...]).astype(BF16)
    nb = _rmsnorm(ob_ref[...].astype(F32), gb_ref[...]).astype(BF16)
    y = (jnp.dot(na, woa_ref[...], preferred_element_type=F32)
         + jnp.dot(nb, wob_ref[...], preferred_element_type=F32))
    x1 = x_ref[...] + y
    x1_ref[...] = x1
    hf = _rmsnorm(x1, fg_ref[...])
    _to_token_tiles(hf_ref, hf)

    hf_hi = hf.astype(BF16)
    hf_lo = (hf - hf_hi.astype(F32)).astype(BF16)
    lt = (_dot_nt(wrh_ref[...], hf_hi) + _dot_nt(wrh_ref[...], hf_lo)
          + _dot_nt(wrl_ref[...], hf_hi) + rb_ref[...])
    gl = [lt[j:j + 1, :] for j in range(N_GROUPS)]
    best = gl[0]
    gsel = jnp.zeros(best.shape, I32)
    for j in range(1, N_GROUPS):
        better = gl[j] > best
        gsel = jnp.where(better, j, gsel)
        best = jnp.where(better, gl[j], best)
    gsum = jnp.zeros(best.shape, F32)
    for j in range(N_GROUPS):
        gsum = gsum + jnp.exp(gl[j] - best)
    gw = 1.0 / gsum
    E = EXPERTS_PER_GROUP
    es = lt[EXPERT_ROW0:EXPERT_ROW0 + E, :]
    for j in range(1, N_GROUPS):
        es = jnp.where(gsel == j, lt[EXPERT_ROW0 + E * j:EXPERT_ROW0 + E * (j + 1), :], es)
    row = lax.broadcasted_iota(I32, es.shape, 0)
    v1 = jnp.max(es, axis=0, keepdims=True)
    i1 = jnp.min(jnp.where(es == v1, row, E), axis=0, keepdims=True)
    es2 = jnp.where(row == i1, -jnp.inf, es)
    v2 = jnp.max(es2, axis=0, keepdims=True)
    i2 = jnp.min(jnp.where(es2 == v2, row, E), axis=0, keepdims=True)
    e2 = jnp.exp(v2 - v1)
    den = 1.0 + e2
    eid_ref[...] = jnp.concatenate([gsel * E + i1, gsel * E + i2], axis=0)
    wt_ref[...] = jnp.concatenate([gw * (1.0 / den), gw * (e2 / den)], axis=0)


def _outproj_call(oa, ob, x2d, ga, gb, woa, wob, fg, wrh, wrl, rb):
    T, D = x2d.shape
    tm = TOKEN_TILE
    tok = lambda w: pl.BlockSpec((tm, w), lambda i: (i, 0))
    full = lambda a: pl.BlockSpec(a.shape, lambda i: (0, 0))
    col = pl.BlockSpec((EXPERT_TOPK, tm), lambda i: (0, i))
    return pl.pallas_call(
        _outproj_kernel,
        grid=(T // tm,),
        in_specs=[tok(oa.shape[1]), tok(ob.shape[1]), tok(D), full(ga), full(gb), full(woa), full(wob),
                  full(fg), full(wrh), full(wrl), full(rb)],
        out_specs=[tok(D), pl.BlockSpec((tm * ROW_TILE, LANES), lambda i: (i, 0)), col, col],
        out_shape=[jax.ShapeDtypeStruct((T, D), F32), jax.ShapeDtypeStruct((T * ROW_TILE, LANES), F32),
                   jax.ShapeDtypeStruct((EXPERT_TOPK, T), I32), jax.ShapeDtypeStruct((EXPERT_TOPK, T), F32)],
        compiler_params=pltpu.CompilerParams(
            dimension_semantics=("arbitrary",), vmem_limit_bytes=VMEM_LIMIT),
        name="outproj",
    )(oa, ob, x2d, ga, gb, woa, wob, fg, wrh, wrl, rb)


def _moe_kernel(be_ref, nu_ref, src_ref, srcn_ref, dst_ref, hf_hbm, wg_ref, wu_ref, wd_ref, out_hbm,
                xbuf, obuf, wgb, wub, wdb, gsem, ssem):
    i = pl.program_id(0)
    n_used = nu_ref[0]
    M = xbuf.shape[1] // ROW_TILE
    slot = i % 2

    def gather_rows(idx_ref, buf_slot):
        for r in range(M):
            tok_row = pl.multiple_of(idx_ref[0, 0, r] * ROW_TILE, ROW_TILE)
            pltpu.make_async_copy(hf_hbm.at[pl.ds(tok_row, ROW_TILE)],
                                  xbuf.at[buf_slot, pl.ds(r * ROW_TILE, ROW_TILE)],
                                  gsem.at[buf_slot]).start(priority=r % DMA_PRIORITIES)

    def scatter_rows(idx_ref, buf_slot):
        for r in range(M):
            dst_row = pl.multiple_of(idx_ref[0, 0, r] * ROW_TILE, ROW_TILE)
            pltpu.make_async_copy(obuf.at[buf_slot, pl.ds(r * ROW_TILE, ROW_TILE)],
                                  out_hbm.at[pl.ds(dst_row, ROW_TILE)],
                                  ssem.at[buf_slot]).start(priority=r % DMA_PRIORITIES)

    def wait_rows(buf, buf_slot, sem):
        pltpu.make_async_copy(hf_hbm.at[pl.ds(0, M * ROW_TILE)], buf.at[buf_slot], sem.at[buf_slot]).wait()

    @pl.when(i < n_used)
    def _compute():
        @pl.when(i == 0)
        def _first_gather():
            gather_rows(src_ref, 0)
            obuf[1] = jnp.zeros(obuf.shape[1:], F32)
            spare = pltpu.make_async_copy(
                obuf.at[1], out_hbm.at[pl.ds(out_hbm.shape[0] - M * ROW_TILE, M * ROW_TILE)], ssem.at[1])
            spare.start()
            spare.wait()

        @pl.when((i == 0) | (be_ref[i] != be_ref[jnp.maximum(i - 1, 0)]))
        def _cast_weights():
            wgb[...] = wg_ref[0].astype(BF16)
            wub[...] = wu_ref[0].astype(BF16)
            wdb[...] = wd_ref[0].astype(BF16)

        wait_rows(xbuf, slot, gsem)

        @pl.when(i >= 2)
        def _reuse_obuf():
            wait_rows(obuf, slot, ssem)

        gather_rows(srcn_ref, 1 - slot)
        xb = _from_token_tiles(xbuf.at[slot], M).astype(BF16)
        a = jnp.dot(xb, wgb[...], preferred_element_type=F32)
        u = jnp.dot(xb, wub[...], preferred_element_type=F32)
        hmid = (a * jax.nn.sigmoid(a)) * u
        _to_token_tiles(obuf.at[slot], jnp.dot(hmid.astype(BF16), wdb[...], preferred_element_type=F32))
        scatter_rows(dst_ref, slot)

        @pl.when(i == n_used - 1)
        def _drain():
            wait_rows(xbuf, 1 - slot, gsem)
            wait_rows(obuf, slot, ssem)

            @pl.when(i >= 1)
            def _drain_prev():
                wait_rows(obuf, 1 - slot, ssem)


def _moe_call(block_expert, n_used, src_tok, dst_row, hf, w_g, w_u, w_d):
    T = hf.shape[0] // ROW_TILE
    M = DISPATCH_BLOCK
    nblk = block_expert.shape[0]
    D, DE = w_g.shape[1], w_g.shape[2]
    assert D == ROW_TILE * LANES and hf.shape[1] == LANES
    idx_spec = lambda f: pl.BlockSpec((1, 1, M), f, memory_space=pltpu.SMEM)
    cur = lambda i, be, nu: (i, 0, 0)
    nxt = lambda i, be, nu: (jnp.minimum(i + 1, nblk - 1), 0, 0)
    wspec = lambda a, b: pl.BlockSpec((1, a, b), lambda i, be, nu: (be[i], 0, 0))
    grid_spec = pltpu.PrefetchScalarGridSpec(
        num_scalar_prefetch=2,
        grid=(nblk,),
        in_specs=[idx_spec(cur), idx_spec(nxt), idx_spec(cur),
                  pl.BlockSpec(memory_space=pl.ANY),
                  wspec(D, DE), wspec(D, DE), wspec(DE, D)],
        out_specs=pl.BlockSpec(memory_space=pl.ANY),
        scratch_shapes=[pltpu.VMEM((2, M * ROW_TILE, LANES), F32), pltpu.VMEM((2, M * ROW_TILE, LANES), F32),
                        pltpu.VMEM((D, DE), BF16), pltpu.VMEM((D, DE), BF16), pltpu.VMEM((DE, D), BF16),
                        pltpu.SemaphoreType.DMA((2,)), pltpu.SemaphoreType.DMA((2,))],
    )
    src3 = src_tok.reshape(nblk, 1, M)
    return pl.pallas_call(
        _moe_kernel,
        grid_spec=grid_spec,
        out_shape=jax.ShapeDtypeStruct(((EXPERT_TOPK * T + M) * ROW_TILE, LANES), F32),
        compiler_params=pltpu.CompilerParams(
            dimension_semantics=("arbitrary",), vmem_limit_bytes=VMEM_LIMIT),
        name="moe",
    )(block_expert, n_used, src3, src3, dst_row.reshape(nblk, 1, M), hf, w_g, w_u, w_d)


def _ple_kernel(x1_ref, m0_ref, m1_ref, wt_ref, p_ref, pg_ref, wgate_ref, wproj_ref, pog_ref, out_ref):
    proj = _rmsnorm(jnp.dot(p_ref[...].astype(BF16), wproj_ref[...], preferred_element_type=F32),
                    pog_ref[...])
    tm = x1_ref.shape[0]
    moe = (_from_token_tiles(m0_ref, tm) * wt_ref[:, 0:1] + _from_token_tiles(m1_ref, tm) * wt_ref[:, 1:2])
    x2 = x1_ref[...] + moe
    hn = _rmsnorm(x2, pg_ref[...]).astype(BF16)
    gate = jax.nn.sigmoid(jnp.dot(hn, wgate_ref[...], preferred_element_type=F32))
    out_ref[...] = x2 + gate * proj


def _ple_call(moe_rows, x1, wt_cols, p2d, pg, wgate, wproj, pog):
    T, D = x1.shape
    tm = PLE_TILE
    tok = lambda w: pl.BlockSpec((tm, w), lambda i: (i, 0))
    full = lambda a: pl.BlockSpec(a.shape, lambda i: (0, 0))
    first = pl.BlockSpec((tm * ROW_TILE, LANES), lambda i: (i, 0))
    second = pl.BlockSpec((tm * ROW_TILE, LANES), lambda i: (T // tm + i, 0))
    return pl.pallas_call(
        _ple_kernel,
        grid=(T // tm,),
        in_specs=[tok(D), first, second, tok(EXPERT_TOPK), tok(p2d.shape[1]),
                  full(pg), full(wgate), full(wproj), full(pog)],
        out_specs=tok(D),
        out_shape=jax.ShapeDtypeStruct((T, D), F32),
        compiler_params=pltpu.CompilerParams(
            dimension_semantics=("arbitrary",), vmem_limit_bytes=VMEM_LIMIT),
        name="ple",
    )(x1, moe_rows, moe_rows, wt_cols, p2d, pg, wgate, wproj, pog)


def _dispatch_plan(eid, T):
    A = EXPERT_TOPK * T
    M = DISPATCH_BLOCK
    nblk = A // M + N_EXPERTS
    assert A <= PAD_MARK
    e_flat = eid.reshape(A)
    experts = jnp.arange(N_EXPERTS, dtype=I32)
    counts = jnp.sum((e_flat[:, None] == experts[None, :]).astype(I32), axis=0)
    padded = (counts + M - 1) // M * M
    n_used = jnp.sum(padded) // M
    row = jnp.arange(M, dtype=I32)[None, :]
    pad_keys = jnp.where(row < (padded - counts)[:, None], 2 * experts[:, None] + 1, 2 * N_EXPERTS)
    keys = jnp.concatenate([2 * e_flat, pad_keys.reshape(-1)])
    vals = jnp.concatenate([jnp.arange(A, dtype=I32), jnp.full((N_EXPERTS * M,), PAD_MARK, I32)])
    packed = jnp.sort(keys * (PAD_MARK + 1) + vals).reshape(nblk, M)
    a_s = packed & PAD_MARK
    blk = jnp.arange(nblk, dtype=I32)
    be = jnp.minimum(packed[:, 0] // (2 * (PAD_MARK + 1)), N_EXPERTS - 1)
    be = jnp.where(blk < n_used, be, be[jnp.maximum(n_used - 1, 0)])
    valid = a_s != PAD_MARK
    src_tok = jnp.where(valid, jnp.where(a_s >= T, a_s - T, a_s), 0)
    dst_row = jnp.where(valid, a_s, A + row)
    return be, n_used.reshape(1).astype(I32), src_tok, dst_row


def _layer(x2d, p2d, rel_bias, attn_norm, w_in, swa_q_norm, swa_k_norm, swa_sinks, moba_q_norm,
           moba_k_norm, swa_out_norm, moba_out_norm, w_out, ffn_norm, w_rg, b_rg, w_re, b_re,
           w_g, w_u, w_d, ple_norm, w_ple_gate, w_ple_proj, ple_out_norm, B, S):
    T, D = x2d.shape
    row = lambda v: v.reshape(1, -1).astype(F32)
    head_gain = jnp.concatenate([
        jnp.tile(swa_q_norm, SWA_Q_HEADS), jnp.tile(swa_k_norm, SWA_KV_HEADS),
        jnp.tile(moba_q_norm, MOBA_Q_HEADS), jnp.tile(moba_k_norm, MOBA_KV_HEADS)])
    w_in_b = w_in.astype(BF16)
    qa_w, kv_w = SWA_Q_HEADS * HEAD_DIM, SWA_KV_HEADS * HEAD_DIM
    qb_w = MOBA_Q_HEADS * HEAD_DIM
    c_va = qa_w + kv_w
    c_qb = c_va + kv_w
    c_vb = c_qb + qb_w + kv_w
    w_qk = jnp.concatenate([w_in_b[:, :c_va], w_in_b[:, c_qb:c_vb]], axis=1)
    w_vt = jnp.concatenate([w_in_b[:, c_va:c_qb], w_in_b[:, c_vb:]], axis=1).T
    qa, ka, qb, kb, vat, vbt = _qkv_call(x2d, row(attn_norm), w_qk, w_vt, row(head_gain))

    tbl_flat = rel_bias.astype(F32).reshape(-1)
    oa = _swa_call(tbl_flat, swa_sinks.astype(F32), qa, ka, vat, B, S)
    ob = _moba_call(tbl_flat, qb, kb, vbt, B, S)

    pad_rows = lambda n: jnp.zeros((n, D), F32)
    wr = jnp.concatenate([w_rg.T, pad_rows(EXPERT_ROW0 - N_GROUPS), w_re.T,
                          pad_rows(ROUTER_ROWS - EXPERT_ROW0 - N_EXPERTS)], axis=0)
    wr_hi = wr.astype(BF16)
    wr_lo = (wr - wr_hi.astype(F32)).astype(BF16)
    rb = jnp.concatenate([b_rg, jnp.zeros((EXPERT_ROW0 - N_GROUPS,), F32), b_re,
                          jnp.zeros((ROUTER_ROWS - EXPERT_ROW0 - N_EXPERTS,), F32)]).reshape(ROUTER_ROWS, 1)
    wo = w_out.astype(BF16)
    na_w = SWA_Q_HEADS * HEAD_DIM
    x1, hf, eid, wts = _outproj_call(oa, ob, x2d, row(swa_out_norm), row(moba_out_norm),
                                     wo[:na_w], wo[na_w:], row(ffn_norm), wr_hi, wr_lo, rb)

    be, n_used, src_tok, dst_row = _dispatch_plan(eid, T)
    moe_rows = _moe_call(be, n_used, src_tok, dst_row, hf, w_g, w_u, w_d)
    return _ple_call(moe_rows, x1, wts.T, p2d, row(ple_norm), w_ple_gate.astype(BF16),
                     w_ple_proj.astype(BF16), row(ple_out_norm))


def kernel(x, p, rel_bias, attn_norm, w_in, swa_q_norm, swa_k_norm, swa_sinks, moba_q_norm, moba_k_norm,
           swa_out_norm, moba_out_norm, w_out, ffn_norm, w_router_group, b_router_group, w_router_expert,
           b_router_expert, w_exp_gate, w_exp_up, w_exp_down, ple_norm, w_ple_gate, w_ple_proj, ple_out_norm):
    B, S, D = x.shape
    x2d = x.reshape(B * S, D)
    for i in range(p.shape[0]):
        x2d = _layer(x2d, p[i].reshape(B * S, -1), rel_bias, attn_norm[i], w_in[i], swa_q_norm[i],
                     swa_k_norm[i], swa_sinks[i], moba_q_norm[i], moba_k_norm[i], swa_out_norm[i],
                     moba_out_norm[i], w_out[i], ffn_norm[i], w_router_group[i], b_router_group[i],
                     w_router_expert[i], b_router_expert[i], w_exp_gate[i], w_exp_up[i], w_exp_down[i],
                     ple_norm[i], w_ple_gate[i], w_ple_proj[i], ple_out_norm[i], B, S)
    return x2d.reshape(B, S, D)
```

```python
import math

import numpy as np
import jax
import jax.numpy as jnp
from jax import lax
from jax.experimental import pallas as pl
from jax.experimental.pallas import tpu as pltpu

F32 = jnp.float32
BF16 = jnp.bfloat16
I32 = jnp.int32

HEAD_DIM = 64
SWA_Q_HEADS = 8
SWA_KV_HEADS = 2
SWA_WINDOW = 128
MOBA_Q_HEADS = 8
MOBA_KV_HEADS = 2
MOBA_BLOCK = 256
MOBA_TOPK = 3
N_HEADS = SWA_Q_HEADS + MOBA_Q_HEADS
REL_BUCKETS = 32
REL_MAX_DIST = 128
N_GROUPS = 4
EXPERTS_PER_GROUP = 8
N_EXPERTS = N_GROUPS * EXPERTS_PER_GROUP
EXPERT_TOPK = 2
DISPATCH_BLOCK = 256
PAD_MARK = 0xFFFF
EPS = 1e-6
NEG = -1e30
LOG2E = math.log2(math.e)

DMA_PRIORITIES = 2
LANES = 128
ROW_TILE = 8
BF16_SUBLANES = 16
GROUP_HEADS = 4
GROUP_W = GROUP_HEADS * HEAD_DIM
ROUTER_ROWS = 128
EXPERT_ROW0 = 8
TOKEN_TILE = 512
PLE_TILE = 512
VMEM_LIMIT = 52 * 1024 * 1024


def _dot_nt(a, b):
    return lax.dot_general(a, b, (((1,), (1,)), ((), ())), preferred_element_type=F32)


def _rmsnorm(x, g):
    ms = jnp.mean(x * x, axis=-1, keepdims=True)
    return x * lax.rsqrt(ms + EPS) * g


def _to_token_tiles(ref, x):
    m = x.shape[0]
    for j in range(x.shape[1] // LANES):
        ref[pl.ds(j, m, stride=ROW_TILE), :] = x[:, LANES * j:LANES * (j + 1)]


def _from_token_tiles(ref, m):
    return jnp.concatenate([ref[pl.ds(j, m, stride=ROW_TILE), :] for j in range(ref.shape[0] // m)], axis=1)


def _rel_bucket_np(dist):
    n = np.maximum(dist, 0)
    exact = REL_BUCKETS // 2
    nf = np.maximum(n, 1).astype(np.float32)
    large = exact + (np.log(nf / exact) / math.log(REL_MAX_DIST / exact)
                     * (REL_BUCKETS - exact)).astype(np.int32)
    return np.where(n < exact, n, np.minimum(large, REL_BUCKETS - 1)).astype(np.int32)


def _band_buckets(block):
    qi = np.arange(block)[:, None]
    kj = np.arange(2 * block)[None, :]
    return _rel_bucket_np(qi + block - kj)


def _bias_from_buckets(bkt, tbl_ref, head):
    acc = jnp.zeros(bkt.shape, F32)
    for j in range(REL_BUCKETS):
        acc = jnp.where(bkt == j, tbl_ref[j * N_HEADS + head] * LOG2E, acc)
    return acc


def _qkv_kernel(x_ref, g_ref, w_ref, wvt_ref, hg_ref, qa_ref, ka_ref, qb_ref, kb_ref, vat_ref, vbt_ref):
    h = _rmsnorm(x_ref[...], g_ref[...]).astype(BF16)
    acc = jnp.dot(h, w_ref[...], preferred_element_type=F32)
    lo = lax.broadcasted_iota(I32, (1, LANES), 1) < HEAD_DIM

    def head_normed(c, scale):
        blk = acc[:, LANES * c:LANES * (c + 1)]
        sq = blk * blk
        s_lo = jnp.sum(jnp.where(lo, sq, 0.0), axis=-1, keepdims=True)
        s_hi = jnp.sum(jnp.where(lo, 0.0, sq), axis=-1, keepdims=True)
        inv = jnp.where(lo, lax.rsqrt(s_lo / HEAD_DIM + EPS), lax.rsqrt(s_hi / HEAD_DIM + EPS))
        return blk * inv * hg_ref[:, LANES * c:LANES * (c + 1)] * scale

    scale = HEAD_DIM ** -0.5 * LOG2E
    for c in range(4):
        qa_ref[:, LANES * c:LANES * (c + 1)] = head_normed(c, scale).astype(BF16)
        qb_ref[:, LANES * c:LANES * (c + 1)] = head_normed(5 + c, scale).astype(BF16)
    ka_ref[...] = head_normed(4, 1.0).astype(BF16)
    kb_ref[...] = head_normed(9, 1.0).astype(BF16)
    vt = _dot_nt(wvt_ref[...], h).astype(BF16)
    for j in range(vat_ref.shape[0]):
        vat_ref[j] = vt[:LANES, SWA_WINDOW * j:SWA_WINDOW * (j + 1)]
    for j in range(vbt_ref.shape[0]):
        vbt_ref[j] = vt[LANES:, MOBA_BLOCK * j:MOBA_BLOCK * (j + 1)]


def _qkv_call(x2d, attn_g, w_qk, w_vt, head_gain):
    T, D = x2d.shape
    tm = TOKEN_TILE
    tok = lambda w: pl.BlockSpec((tm, w), lambda i: (i, 0))
    full = lambda a: pl.BlockSpec(a.shape, lambda i: (0, 0))
    out_w = (SWA_Q_HEADS * HEAD_DIM, LANES, MOBA_Q_HEADS * HEAD_DIM, LANES)
    slabs = lambda blk: pl.BlockSpec((tm // blk, LANES, blk), lambda i: (i, 0, 0))
    slab_shape = lambda blk: jax.ShapeDtypeStruct((T // blk, LANES, blk), BF16)
    return pl.pallas_call(
        _qkv_kernel,
        grid=(T // tm,),
        in_specs=[tok(D), full(attn_g), full(w_qk), full(w_vt), full(head_gain)],
        out_specs=[tok(w) for w in out_w] + [slabs(SWA_WINDOW), slabs(MOBA_BLOCK)],
        out_shape=[jax.ShapeDtypeStruct((T, w), BF16) for w in out_w]
        + [slab_shape(SWA_WINDOW), slab_shape(MOBA_BLOCK)],
        compiler_params=pltpu.CompilerParams(
            dimension_semantics=("arbitrary",), vmem_limit_bytes=VMEM_LIMIT),
        name="qkv",
    )(x2d, attn_g, w_qk, w_vt, head_gain)


def _padded_heads(qblk, heads):
    zeros = jnp.zeros((qblk.shape[0], HEAD_DIM), qblk.dtype)
    pieces = []
    for h in range(heads):
        piece = qblk[:, HEAD_DIM * h:HEAD_DIM * (h + 1)]
        pieces.append(jnp.concatenate([piece, zeros] if h < GROUP_HEADS else [zeros, piece], axis=1))
    return jnp.concatenate(pieces, axis=0)


def _swa_kernel(tbl_ref, sink_ref, bkt_ref, q_ref, k_ref, vt_ref, o_ref, bias_scr):
    W = SWA_WINDOW
    H = SWA_Q_HEADS
    S = q_ref.shape[0]

    @pl.when(pl.program_id(0) == 0)
    def _init_bias():
        bkt = bkt_ref[...]
        for h in range(H):
            bias_scr[:, W * h:W * (h + 1)] = _bias_from_buckets(bkt, tbl_ref, h)

    key_i = lax.broadcasted_iota(I32, (2 * W, H * W), 0)
    qry_i = lax.broadcasted_iota(I32, (2 * W, H * W), 1) & (W - 1)
    dist = qry_i + W - key_i
    band = (dist >= 0) & (dist < W)
    own_part = key_i >= W
    head_of_lane = lax.broadcasted_iota(I32, (1, H * W), 1) // W
    sinks = jnp.zeros((1, H * W), F32)
    for h in range(H):
        sinks = jnp.where(head_of_lane == h, sink_ref[h] * LOG2E, sinks)

    def body(i, carry):
        r0 = pl.multiple_of(i * W, W)
        prev = jnp.maximum(i - 1, 0)
        p0 = pl.multiple_of(prev * W, W)
        q8 = _padded_heads(q_ref[pl.ds(r0, W), :], H)
        kband = jnp.concatenate([k_ref[pl.ds(p0, W), :], k_ref[pl.ds(r0, W), :]], axis=0)
        mask = band & (own_part | (i > 0))
        s = jnp.where(mask, _dot_nt(kband, q8) + bias_scr[...], NEG)
        m = jnp.maximum(jnp.max(s, axis=0, keepdims=True), sinks)
        e = jnp.exp2(s - m)
        den = jnp.sum(e, axis=0, keepdims=True) + jnp.exp2(sinks - m)
        vtband = jnp.concatenate([vt_ref[prev], vt_ref[i]], axis=1)
        ot = jnp.dot(vtband, e.astype(BF16), preferred_element_type=F32) / den
        for pr in range(H // 2):
            f0 = HEAD_DIM * (2 * pr // GROUP_HEADS)
            pair = ot[f0:f0 + HEAD_DIM, 2 * W * pr:2 * W * (pr + 1)]
            o_ref[pl.ds(r0, W), LANES * pr:LANES * (pr + 1)] = jnp.concatenate(
                [pair[:, :W], pair[:, W:]], axis=0).T
        return carry

    lax.fori_loop(0, S // W, body, 0)


def _swa_call(tbl_flat, sinks, qa, ka, vat, B, S):
    W = SWA_WINDOW
    bkt = jnp.asarray(np.ascontiguousarray(_band_buckets(W).T))
    smem = pl.BlockSpec(memory_space=pltpu.SMEM)
    return pl.pallas_call(
        _swa_kernel,
        grid=(B,),
        in_specs=[smem, smem,
                  pl.BlockSpec(bkt.shape, lambda b: (0, 0)),
                  pl.BlockSpec((S, qa.shape[1]), lambda b: (b, 0)),
                  pl.BlockSpec((S, LANES), lambda b: (b, 0)),
                  pl.BlockSpec((S // W, LANES, W), lambda b: (b, 0, 0))],
        out_specs=pl.BlockSpec((S, qa.shape[1]), lambda b: (b, 0)),
        out_shape=jax.ShapeDtypeStruct(qa.shape, F32),
        scratch_shapes=[pltpu.VMEM((2 * W, SWA_Q_HEADS * W), F32)],
        compiler_params=pltpu.CompilerParams(
            dimension_semantics=("arbitrary",), vmem_limit_bytes=VMEM_LIMIT),
        name="swa",
    )(tbl_flat, sinks, bkt, qa, ka, vat)


def _moba_kernel(tbl_ref, bkt_ref, q_ref, k_ref, vt_ref, o_ref,
                 bias_scr, km_scr, q8_scr, sel_scr, m_scr, l_scr, acc_scr):
    BS = MOBA_BLOCK
    H = MOBA_Q_HEADS
    S = q_ref.shape[0]
    NBK = S // BS
    NQ = H * BS
    GQ = GROUP_HEADS * BS

    @pl.when(pl.program_id(0) == 0)
    def _init_bias():
        bkt = bkt_ref[...]
        for h in range(H):
            band = _bias_from_buckets(bkt, tbl_ref, SWA_Q_HEADS + h)
            far = tbl_ref[(REL_BUCKETS - 1) * N_HEADS + SWA_Q_HEADS + h] * LOG2E
            bias_scr[0, :, BS * h:BS * (h + 1)] = jnp.full((BS, BS), far, F32)
            bias_scr[1, :, BS * h:BS * (h + 1)] = band[:BS]
            bias_scr[2, :, BS * h:BS * (h + 1)] = band[BS:]

    blk_row = lax.broadcasted_iota(I32, (BF16_SUBLANES, 1), 0)
    kmean = jnp.sum(k_ref[...].astype(F32).reshape(NBK, BS, LANES), axis=1) / BS
    kmean = jnp.concatenate([kmean, jnp.zeros((BF16_SUBLANES - NBK, LANES), F32)], axis=0)
    km_hi = kmean.astype(BF16)
    km_scr[0] = km_hi
    km_scr[1] = (kmean - km_hi.astype(F32)).astype(BF16)

    def scores(n, case):
        k0 = pl.multiple_of(n * BS, BS)
        return _dot_nt(k_ref[pl.ds(k0, BS), :], q8_scr[...]) + bias_scr[case]

    def weighted_values(n, e):
        pv = jnp.dot(vt_ref[n], e.astype(BF16), preferred_element_type=F32)
        return jnp.concatenate([pv[:HEAD_DIM, :GQ], pv[HEAD_DIM:, GQ:]], axis=1)

    def qblock(mi, carry):
        r0 = pl.multiple_of(mi * BS, BS)
        q8 = _padded_heads(q_ref[pl.ds(r0, BS), :], H)
        q8_scr[...] = q8
        gate = _dot_nt(km_scr[0], q8) + _dot_nt(km_scr[1], q8)
        past = blk_row < mi
        gm = jnp.where(past, gate, NEG)
        cnt = jnp.zeros(gm.shape, F32)
        for n in range(NBK - 1):
            col = gm[n:n + 1, :]
            beats = (gm > col) | ((gm == col) & (blk_row < n))
            c = jnp.sum(jnp.where(beats, 1.0, 0.0), axis=0, keepdims=True)
            cnt = jnp.where(blk_row == n, c, cnt)
        sel_scr[...] = jnp.where(past & (cnt < MOBA_TOPK), 1.0, 0.0)

        key_i = lax.broadcasted_iota(I32, (BS, NQ), 0)
        qry_i = lax.broadcasted_iota(I32, (BS, NQ), 1) & (BS - 1)
        s = jnp.where(key_i <= qry_i, scores(mi, 2), NEG)
        m0 = jnp.max(s, axis=0, keepdims=True)
        e = jnp.exp2(s - m0)
        m_scr[...] = m0
        l_scr[...] = jnp.sum(e, axis=0, keepdims=True)
        acc_scr[...] = weighted_values(mi, e)

        def kvblock(j, c2):
            n = mi - j
            s = jnp.where(sel_scr[pl.ds(n, 1), :] > 0.5, scores(n, jnp.where(j == 1, 1, 0)), NEG)
            m_prev = m_scr[...]
            m_new = jnp.maximum(m_prev, jnp.max(s, axis=0, keepdims=True))
            alpha = jnp.exp2(m_prev - m_new)
            e = jnp.exp2(s - m_new)
            l_scr[...] = alpha * l_scr[...] + jnp.sum(e, axis=0, keepdims=True)
            acc_scr[...] = alpha * acc_scr[...] + weighted_values(n, e)
            m_scr[...] = m_new
            return c2

        lax.fori_loop(1, mi + 1, kvblock, 0)
        ot = acc_scr[...] / l_scr[...]
        for pr in range(H // 2):
            pair = ot[:, 2 * BS * pr:2 * BS * (pr + 1)]
            o_ref[pl.ds(r0, BS), LANES * pr:LANES * (pr + 1)] = jnp.concatenate(
                [pair[:, :BS], pair[:, BS:]], axis=0).T
        return carry

    lax.fori_loop(0, NBK, qblock, 0)


def _moba_call(tbl_flat, qb, kb, vbt, B, S):
    BS = MOBA_BLOCK
    NQ = MOBA_Q_HEADS * BS
    NBK = S // BS
    bkt = jnp.asarray(np.ascontiguousarray(_band_buckets(BS).T))
    smem = pl.BlockSpec(memory_space=pltpu.SMEM)
    return pl.pallas_call(
        _moba_kernel,
        grid=(B,),
        in_specs=[smem,
                  pl.BlockSpec(bkt.shape, lambda b: (0, 0)),
                  pl.BlockSpec((S, qb.shape[1]), lambda b: (b, 0)),
                  pl.BlockSpec((S, LANES), lambda b: (b, 0)),
                  pl.BlockSpec((NBK, LANES, BS), lambda b: (b, 0, 0))],
        out_specs=pl.BlockSpec((S, qb.shape[1]), lambda b: (b, 0)),
        out_shape=jax.ShapeDtypeStruct(qb.shape, F32),
        scratch_shapes=[pltpu.VMEM((3, BS, NQ), F32),
                        pltpu.VMEM((2, BF16_SUBLANES, LANES), BF16),
                        pltpu.VMEM((NQ, LANES), BF16),
                        pltpu.VMEM((BF16_SUBLANES, NQ), F32),
                        pltpu.VMEM((1, NQ), F32),
                        pltpu.VMEM((1, NQ), F32),
                        pltpu.VMEM((HEAD_DIM, NQ), F32)],
        compiler_params=pltpu.CompilerParams(
            dimension_semantics=("arbitrary",), vmem_limit_bytes=VMEM_LIMIT),
        name="moba",
    )(tbl_flat, bkt, qb, kb, vbt)


def _outproj_kernel(oa_ref, ob_ref, x_ref, ga_ref, gb_ref, woa_ref, wob_ref, fg_ref,
                    wrh_ref, wrl_ref, rb_ref, x1_ref, hf_ref, eid_ref, wt_ref):
    na = _rmsnorm(oa_ref[...], ga_ref[...]).astype(BF16)
    nb = _rmsnorm(ob_ref[...], gb_ref[...]).astype(BF16)
    y = (jnp.dot(na, woa_ref[...], preferred_element_type=F32)
         + jnp.dot(nb, wob_ref[...], preferred_element_type=F32))
    x1 = x_ref[...] + y
    x1_ref[...] = x1
    hf = _rmsnorm(x1, fg_ref[...])
    _to_token_tiles(hf_ref, hf)

    hf_hi = hf.astype(BF16)
    hf_lo = (hf - hf_hi.astype(F32)).astype(BF16)
    lt = (_dot_nt(wrh_ref[...], hf_hi) + _dot_nt(wrh_ref[...], hf_lo)
          + _dot_nt(wrl_ref[...], hf_hi) + rb_ref[...])
    gl = [lt[j:j + 1, :] for j in range(N_GROUPS)]
    best = gl[0]
    gsel = jnp.zeros(best.shape, I32)
    for j in range(1, N_GROUPS):
        better = gl[j] > best
        gsel = jnp.where(better, j, gsel)
        best = jnp.where(better, gl[j], best)
    gsum = jnp.zeros(best.shape, F32)
    for j in range(N_GROUPS):
        gsum = gsum + jnp.exp(gl[j] - best)
    gw = 1.0 / gsum
    E = EXPERTS_PER_GROUP
    es = lt[EXPERT_ROW0:EXPERT_ROW0 + E, :]
    for j in range(1, N_GROUPS):
        es = jnp.where(gsel == j, lt[EXPERT_ROW0 + E * j:EXPERT_ROW0 + E * (j + 1), :], es)
    row = lax.broadcasted_iota(I32, es.shape, 0)
    v1 = jnp.max(es, axis=0, keepdims=True)
    i1 = jnp.min(jnp.where(es == v1, row, E), axis=0, keepdims=True)
    es2 = jnp.where(row == i1, -jnp.inf, es)
    v2 = jnp.max(es2, axis=0, keepdims=True)
    i2 = jnp.min(jnp.where(es2 == v2, row, E), axis=0, keepdims=True)
    e2 = jnp.exp(v2 - v1)
    den = 1.0 + e2
    eid_ref[...] = jnp.concatenate([gsel * E + i1, gsel * E + i2], axis=0)
    wt_ref[...] = jnp.concatenate([gw * (1.0 / den), gw * (e2 / den)], axis=0)


def _outproj_call(oa, ob, x2d, ga, gb, woa, wob, fg, wrh, wrl, rb):
    T, D = x2d.shape
    tm = TOKEN_TILE
    tok = lambda w: pl.BlockSpec((tm, w), lambda i: (i, 0))
    full = lambda a: pl.BlockSpec(a.shape, lambda i: (0, 0))
    col = pl.BlockSpec((EXPERT_TOPK, tm), lambda i: (0, i))
    return pl.pallas_call(
        _outproj_kernel,
        grid=(T // tm,),
        in_specs=[tok(oa.shape[1]), tok(ob.shape[1]), tok(D), full(ga), full(gb), full(woa), full(wob),
                  full(fg), full(wrh), full(wrl), full(rb)],
        out_specs=[tok(D), pl.BlockSpec((tm * ROW_TILE, LANES), lambda i: (i, 0)), col, col],
        out_shape=[jax.ShapeDtypeStruct((T, D), F32), jax.ShapeDtypeStruct((T * ROW_TILE, LANES), F32),
                   jax.ShapeDtypeStruct((EXPERT_TOPK, T), I32), jax.ShapeDtypeStruct((EXPERT_TOPK, T), F32)],
        compiler_params=pltpu.CompilerParams(
            dimension_semantics=("arbitrary",), vmem_limit_bytes=VMEM_LIMIT),
        name="outproj",
    )(oa, ob, x2d, ga, gb, woa, wob, fg, wrh, wrl, rb)


COPY_GROUP = 32


def _moe_kernel(be_ref, nv_ref, src_ref, srcn_ref, dst_ref, hf_hbm, wg_ref, wu_ref, wd_ref, out_hbm,
                xbuf, obuf, wgb, wub, wdb, gsem, ssem):
    i = pl.program_id(0)
    nblk = pl.num_programs(0)
    M = xbuf.shape[1] // ROW_TILE
    slot = i % 2
    nv_cur = nv_ref[i]
    nv_next = jnp.where(i + 1 < nblk, nv_ref[jnp.minimum(i + 1, nblk - 1)], 0)

    def for_real_groups(n_real, fn):
        for k in range(M // COPY_GROUP):
            @pl.when(k * COPY_GROUP < n_real)
            def _group():
                fn(range(k * COPY_GROUP, (k + 1) * COPY_GROUP))

    def gather_rows(idx_ref, buf_slot, n_real):
        def start(rows):
            for r in rows:
                tok_row = pl.multiple_of(idx_ref[0, 0, r] * ROW_TILE, ROW_TILE)
                pltpu.make_async_copy(hf_hbm.at[pl.ds(tok_row, ROW_TILE)],
                                      xbuf.at[buf_slot, pl.ds(r * ROW_TILE, ROW_TILE)],
                                      gsem.at[buf_slot]).start(priority=r % DMA_PRIORITIES)
        for_real_groups(n_real, start)

    def scatter_rows(idx_ref, buf_slot, n_real):
        def start(rows):
            for r in rows:
                dst_row = pl.multiple_of(idx_ref[0, 0, r] * ROW_TILE, ROW_TILE)
                pltpu.make_async_copy(obuf.at[buf_slot, pl.ds(r * ROW_TILE, ROW_TILE)],
                                      out_hbm.at[pl.ds(dst_row, ROW_TILE)],
                                      ssem.at[buf_slot]).start(priority=r % DMA_PRIORITIES)
        for_real_groups(n_real, start)

    def wait_rows(buf, buf_slot, sem, n_real):
        span = COPY_GROUP * ROW_TILE
        for_real_groups(n_real, lambda rows: pltpu.make_async_copy(
            hf_hbm.at[pl.ds(0, span)], buf.at[buf_slot, pl.ds(0, span)], sem.at[buf_slot]).wait())

    @pl.when(nv_cur > 0)
    def _compute():
        @pl.when(i == 0)
        def _first_gather():
            gather_rows(src_ref, 0, nv_cur)
            obuf[1] = jnp.zeros(obuf.shape[1:], F32)
            spare = pltpu.make_async_copy(
                obuf.at[1], out_hbm.at[pl.ds(out_hbm.shape[0] - M * ROW_TILE, M * ROW_TILE)], ssem.at[1])
            spare.start()
            spare.wait()

        @pl.when((i == 0) | (be_ref[i] != be_ref[jnp.maximum(i - 1, 0)]))
        def _cast_weights():
            wgb[...] = wg_ref[0].astype(BF16)
            wub[...] = wu_ref[0].astype(BF16)
            wdb[...] = wd_ref[0].astype(BF16)

        wait_rows(xbuf, slot, gsem, nv_cur)

        @pl.when(i >= 2)
        def _reuse_obuf():
            wait_rows(obuf, slot, ssem, nv_ref[jnp.maximum(i - 2, 0)])

        gather_rows(srcn_ref, 1 - slot, nv_next)
        xb = _from_token_tiles(xbuf.at[slot], M).astype(BF16)
        a = jnp.dot(xb, wgb[...], preferred_element_type=F32)
        u = jnp.dot(xb, wub[...], preferred_element_type=F32)
        hmid = (a * jax.nn.sigmoid(a)) * u
        _to_token_tiles(obuf.at[slot], jnp.dot(hmid.astype(BF16), wdb[...], preferred_element_type=F32))
        scatter_rows(dst_ref, slot, nv_cur)

        @pl.when(nv_next == 0)
        def _drain():
            wait_rows(obuf, slot, ssem, nv_cur)

            @pl.when(i >= 1)
            def _drain_prev():
                wait_rows(obuf, 1 - slot, ssem, nv_ref[jnp.maximum(i - 1, 0)])


def _moe_call(block_expert, n_real, src_tok, dst_row, hf, w_g, w_u, w_d):
    T = hf.shape[0] // ROW_TILE
    M = DISPATCH_BLOCK
    nblk = block_expert.shape[0]
    D, DE = w_g.shape[1], w_g.shape[2]
    assert D == ROW_TILE * LANES and hf.shape[1] == LANES and M % COPY_GROUP == 0
    idx_spec = lambda f: pl.BlockSpec((1, 1, M), f, memory_space=pltpu.SMEM)
    cur = lambda i, be, nu: (i, 0, 0)
    nxt = lambda i, be, nu: (jnp.minimum(i + 1, nblk - 1), 0, 0)
    wspec = lambda a, b: pl.BlockSpec((1, a, b), lambda i, be, nu: (be[i], 0, 0))
    grid_spec = pltpu.PrefetchScalarGridSpec(
        num_scalar_prefetch=2,
        grid=(nblk,),
        in_specs=[idx_spec(cur), idx_spec(nxt), idx_spec(cur),
                  pl.BlockSpec(memory_space=pl.ANY),
                  wspec(D, DE), wspec(D, DE), wspec(DE, D)],
        out_specs=pl.BlockSpec(memory_space=pl.ANY),
        scratch_shapes=[pltpu.VMEM((2, M * ROW_TILE, LANES), F32), pltpu.VMEM((2, M * ROW_TILE, LANES), F32),
                        pltpu.VMEM((D, DE), BF16), pltpu.VMEM((D, DE), BF16), pltpu.VMEM((DE, D), BF16),
                        pltpu.SemaphoreType.DMA((2,)), pltpu.SemaphoreType.DMA((2,))],
    )
    src3 = src_tok.reshape(nblk, 1, M)
    return pl.pallas_call(
        _moe_kernel,
        grid_spec=grid_spec,
        out_shape=jax.ShapeDtypeStruct(((EXPERT_TOPK * T + M) * ROW_TILE, LANES), F32),
        compiler_params=pltpu.CompilerParams(
            dimension_semantics=("arbitrary",), vmem_limit_bytes=VMEM_LIMIT),
        name="moe",
    )(block_expert, n_real, src3, src3, dst_row.reshape(nblk, 1, M), hf, w_g, w_u, w_d)


def _ple_kernel(x1_ref, m0_ref, m1_ref, wt_ref, p_ref, pg_ref, wgate_ref, wproj_ref, pog_ref, out_ref):
    proj = _rmsnorm(jnp.dot(p_ref[...].astype(BF16), wproj_ref[...], preferred_element_type=F32),
                    pog_ref[...])
    tm = x1_ref.shape[0]
    moe = (_from_token_tiles(m0_ref, tm) * wt_ref[:, 0:1] + _from_token_tiles(m1_ref, tm) * wt_ref[:, 1:2])
    x2 = x1_ref[...] + moe
    hn = _rmsnorm(x2, pg_ref[...]).astype(BF16)
    gate = jax.nn.sigmoid(jnp.dot(hn, wgate_ref[...], preferred_element_type=F32))
    out_ref[...] = x2 + gate * proj


def _ple_call(moe_rows, x1, wt_cols, p2d, pg, wgate, wproj, pog):
    T, D = x1.shape
    tm = PLE_TILE
    tok = lambda w: pl.BlockSpec((tm, w), lambda i: (i, 0))
    full = lambda a: pl.BlockSpec(a.shape, lambda i: (0, 0))
    first = pl.BlockSpec((tm * ROW_TILE, LANES), lambda i: (i, 0))
    second = pl.BlockSpec((tm * ROW_TILE, LANES), lambda i: (T // tm + i, 0))
    return pl.pallas_call(
        _ple_kernel,
        grid=(T // tm,),
        in_specs=[tok(D), first, second, tok(EXPERT_TOPK), tok(p2d.shape[1]),
                  full(pg), full(wgate), full(wproj), full(pog)],
        out_specs=tok(D),
        out_shape=jax.ShapeDtypeStruct((T, D), F32),
        compiler_params=pltpu.CompilerParams(
            dimension_semantics=("arbitrary",), vmem_limit_bytes=VMEM_LIMIT),
        name="ple",
    )(x1, moe_rows, moe_rows, wt_cols, p2d, pg, wgate, wproj, pog)


def _dispatch_plan(eid, T):
    A = EXPERT_TOPK * T
    M = DISPATCH_BLOCK
    nblk = A // M + N_EXPERTS
    assert A <= PAD_MARK
    e_flat = eid.reshape(A)
    experts = jnp.arange(N_EXPERTS, dtype=I32)
    counts = jnp.sum((e_flat[:, None] == experts[None, :]).astype(I32), axis=0)
    padded = (counts + M - 1) // M * M
    n_used = jnp.sum(padded) // M
    row = jnp.arange(M, dtype=I32)[None, :]
    pad_keys = jnp.where(row < (padded - counts)[:, None], 2 * experts[:, None] + 1, 2 * N_EXPERTS)
    keys = jnp.concatenate([2 * e_flat, pad_keys.reshape(-1)])
    vals = jnp.concatenate([jnp.arange(A, dtype=I32), jnp.full((N_EXPERTS * M,), PAD_MARK, I32)])
    packed = jnp.sort(keys * (PAD_MARK + 1) + vals).reshape(nblk, M)
    a_s = packed & PAD_MARK
    blk = jnp.arange(nblk, dtype=I32)
    be = jnp.minimum(packed[:, 0] // (2 * (PAD_MARK + 1)), N_EXPERTS - 1)
    be = jnp.where(blk < n_used, be, be[jnp.maximum(n_used - 1, 0)])
    valid = a_s != PAD_MARK
    src_tok = jnp.where(valid, jnp.where(a_s >= T, a_s - T, a_s), 0)
    dst_row = jnp.where(valid, a_s, A + row)
    return be, jnp.sum(valid.astype(I32), axis=1), src_tok, dst_row


def _layer(x2d, p2d, rel_bias, attn_norm, w_in, swa_q_norm, swa_k_norm, swa_sinks, moba_q_norm,
           moba_k_norm, swa_out_norm, moba_out_norm, w_out, ffn_norm, w_rg, b_rg, w_re, b_re,
           w_g, w_u, w_d, ple_norm, w_ple_gate, w_ple_proj, ple_out_norm, B, S):
    T, D = x2d.shape
    row = lambda v: v.reshape(1, -1).astype(F32)
    head_gain = jnp.concatenate([
        jnp.tile(swa_q_norm, SWA_Q_HEADS), jnp.tile(swa_k_norm, SWA_KV_HEADS),
        jnp.tile(moba_q_norm, MOBA_Q_HEADS), jnp.tile(moba_k_norm, MOBA_KV_HEADS)])
    w_in_b = w_in.astype(BF16)
    qa_w, kv_w = SWA_Q_HEADS * HEAD_DIM, SWA_KV_HEADS * HEAD_DIM
    qb_w = MOBA_Q_HEADS * HEAD_DIM
    c_va = qa_w + kv_w
    c_qb = c_va + kv_w
    c_vb = c_qb + qb_w + kv_w
    w_qk = jnp.concatenate([w_in_b[:, :c_va], w_in_b[:, c_qb:c_vb]], axis=1)
    w_vt = jnp.concatenate([w_in_b[:, c_va:c_qb], w_in_b[:, c_vb:]], axis=1).T
    qa, ka, qb, kb, vat, vbt = _qkv_call(x2d, row(attn_norm), w_qk, w_vt, row(head_gain))

    tbl_flat = rel_bias.astype(F32).reshape(-1)
    oa = _swa_call(tbl_flat, swa_sinks.astype(F32), qa, ka, vat, B, S)
    ob = _moba_call(tbl_flat, qb, kb, vbt, B, S)

    pad_rows = lambda n: jnp.zeros((n, D), F32)
    wr = jnp.concatenate([w_rg.T, pad_rows(EXPERT_ROW0 - N_GROUPS), w_re.T,
                          pad_rows(ROUTER_ROWS - EXPERT_ROW0 - N_EXPERTS)], axis=0)
    wr_hi = wr.astype(BF16)
    wr_lo = (wr - wr_hi.astype(F32)).astype(BF16)
    rb = jnp.concatenate([b_rg, jnp.zeros((EXPERT_ROW0 - N_GROUPS,), F32), b_re,
                          jnp.zeros((ROUTER_ROWS - EXPERT_ROW0 - N_EXPERTS,), F32)]).reshape(ROUTER_ROWS, 1)
    wo = w_out.astype(BF16)
    na_w = SWA_Q_HEADS * HEAD_DIM
    x1, hf, eid, wts = _outproj_call(oa, ob, x2d, row(swa_out_norm), row(moba_out_norm),
                                     wo[:na_w], wo[na_w:], row(ffn_norm), wr_hi, wr_lo, rb)

    be, n_real, src_tok, dst_row = _dispatch_plan(eid, T)
    moe_rows = _moe_call(be, n_real, src_tok, dst_row, hf, w_g, w_u, w_d)
    return _ple_call(moe_rows, x1, wts.T, p2d, row(ple_norm), w_ple_gate.astype(BF16),
                     w_ple_proj.astype(BF16), row(ple_out_norm))


def kernel(x, p, rel_bias, attn_norm, w_in, swa_q_norm, swa_k_norm, swa_sinks, moba_q_norm, moba_k_norm,
           swa_out_norm, moba_out_norm, w_out, ffn_norm, w_router_group, b_router_group, w_router_expert,
           b_router_expert, w_exp_gate, w_exp_up, w_exp_down, ple_norm, w_ple_gate, w_ple_proj, ple_out_norm):
    B, S, D = x.shape
    x2d = x.reshape(B * S, D)
    for i in range(p.shape[0]):
        x2d = _layer(x2d, p[i].reshape(B * S, -1), rel_bias, attn_norm[i], w_in[i], swa_q_norm[i],
                     swa_k_norm[i], swa_sinks[i], moba_q_norm[i], moba_k_norm[i], swa_out_norm[i],
                     moba_out_norm[i], w_out[i], ffn_norm[i], w_router_group[i], b_router_group[i],
                     w_router_expert[i], b_router_expert[i], w_exp_gate[i], w_exp_up[i], w_exp_down[i],
                     ple_norm[i], w_ple_gate[i], w_ple_proj[i], ple_out_norm[i], B, S)
    return x2d.reshape(B, S, D)
```

```python
import math

import numpy as np
import jax
import jax.numpy as jnp
from jax import lax
from jax.experimental import pallas as pl
from jax.experimental.pallas import tpu as pltpu

F32 = jnp.float32
BF16 = jnp.bfloat16
I32 = jnp.int32

HEAD_DIM = 64
SWA_Q_HEADS = 8
SWA_KV_HEADS = 2
SWA_WINDOW = 128
MOBA_Q_HEADS = 8
MOBA_KV_HEADS = 2
MOBA_BLOCK = 256
MOBA_TOPK = 3
N_HEADS = SWA_Q_HEADS + MOBA_Q_HEADS
REL_BUCKETS = 32
REL_MAX_DIST = 128
N_GROUPS = 4
EXPERTS_PER_GROUP = 8
N_EXPERTS = N_GROUPS * EXPERTS_PER_GROUP
EXPERT_TOPK = 2
DISPATCH_BLOCK = 256
PAD_MARK = 0xFFFF
EPS = 1e-6
NEG = -1e30
LOG2E = math.log2(math.e)

DMA_PRIORITIES = 2
LANES = 128
ROW_TILE = 8
BF16_SUBLANES = 16
GROUP_HEADS = 4
GROUP_W = GROUP_HEADS * HEAD_DIM
ROUTER_ROWS = 128
EXPERT_ROW0 = 8
TOKEN_TILE = 512
PLE_TILE = 512
VMEM_LIMIT = 52 * 1024 * 1024


def _dot_nt(a, b):
    return lax.dot_general(a, b, (((1,), (1,)), ((), ())), preferred_element_type=F32)


def _rmsnorm(x, g):
    ms = jnp.mean(x * x, axis=-1, keepdims=True)
    return x * lax.rsqrt(ms + EPS) * g


def _to_token_tiles(ref, x):
    m = x.shape[0]
    for j in range(x.shape[1] // LANES):
        ref[pl.ds(j, m, stride=ROW_TILE), :] = x[:, LANES * j:LANES * (j + 1)]


def _from_token_tiles(ref, m):
    return jnp.concatenate([ref[pl.ds(j, m, stride=ROW_TILE), :] for j in range(ref.shape[0] // m)], axis=1)


def _rel_bucket_np(dist):
    n = np.maximum(dist, 0)
    exact = REL_BUCKETS // 2
    nf = np.maximum(n, 1).astype(np.float32)
    large = exact + (np.log(nf / exact) / math.log(REL_MAX_DIST / exact)
                     * (REL_BUCKETS - exact)).astype(np.int32)
    return np.where(n < exact, n, np.minimum(large, REL_BUCKETS - 1)).astype(np.int32)


def _band_buckets(block):
    qi = np.arange(block)[:, None]
    kj = np.arange(2 * block)[None, :]
    return _rel_bucket_np(qi + block - kj)


def _bias_from_buckets(bkt, tbl_ref, head):
    acc = jnp.zeros(bkt.shape, F32)
    for j in range(REL_BUCKETS):
        acc = jnp.where(bkt == j, tbl_ref[j * N_HEADS + head] * LOG2E, acc)
    return acc


def _qkv_kernel(x_ref, g_ref, w_ref, wvt_ref, hg_ref, qa_ref, ka_ref, qb_ref, kb_ref, vat_ref, vbt_ref):
    h = _rmsnorm(x_ref[...], g_ref[...]).astype(BF16)
    acc = jnp.dot(h, w_ref[...], preferred_element_type=F32)
    lo = lax.broadcasted_iota(I32, (1, LANES), 1) < HEAD_DIM

    def head_normed(c, scale):
        blk = acc[:, LANES * c:LANES * (c + 1)]
        sq = blk * blk
        s_lo = jnp.sum(jnp.where(lo, sq, 0.0), axis=-1, keepdims=True)
        s_hi = jnp.sum(jnp.where(lo, 0.0, sq), axis=-1, keepdims=True)
        inv = jnp.where(lo, lax.rsqrt(s_lo / HEAD_DIM + EPS), lax.rsqrt(s_hi / HEAD_DIM + EPS))
        return blk * inv * hg_ref[:, LANES * c:LANES * (c + 1)] * scale

    scale = HEAD_DIM ** -0.5 * LOG2E
    for c in range(4):
        qa_ref[:, LANES * c:LANES * (c + 1)] = head_normed(c, scale).astype(BF16)
        qb_ref[:, LANES * c:LANES * (c + 1)] = head_normed(5 + c, scale).astype(BF16)
    ka_ref[...] = head_normed(4, 1.0).astype(BF16)
    kb_ref[...] = head_normed(9, 1.0).astype(BF16)
    vt = _dot_nt(wvt_ref[...], h).astype(BF16)
    for j in range(vat_ref.shape[0]):
        vat_ref[j] = vt[:LANES, SWA_WINDOW * j:SWA_WINDOW * (j + 1)]
    for j in range(vbt_ref.shape[0]):
        vbt_ref[j] = vt[LANES:, MOBA_BLOCK * j:MOBA_BLOCK * (j + 1)]


def _qkv_call(x2d, attn_g, w_qk, w_vt, head_gain):
    T, D = x2d.shape
    tm = TOKEN_TILE
    tok = lambda w: pl.BlockSpec((tm, w), lambda i: (i, 0))
    full = lambda a: pl.BlockSpec(a.shape, lambda i: (0, 0))
    out_w = (SWA_Q_HEADS * HEAD_DIM, LANES, MOBA_Q_HEADS * HEAD_DIM, LANES)
    slabs = lambda blk: pl.BlockSpec((tm // blk, LANES, blk), lambda i: (i, 0, 0))
    slab_shape = lambda blk: jax.ShapeDtypeStruct((T // blk, LANES, blk), BF16)
    return pl.pallas_call(
        _qkv_kernel,
        grid=(T // tm,),
        in_specs=[tok(D), full(attn_g), full(w_qk), full(w_vt), full(head_gain)],
        out_specs=[tok(w) for w in out_w] + [slabs(SWA_WINDOW), slabs(MOBA_BLOCK)],
        out_shape=[jax.ShapeDtypeStruct((T, w), BF16) for w in out_w]
        + [slab_shape(SWA_WINDOW), slab_shape(MOBA_BLOCK)],
        compiler_params=pltpu.CompilerParams(
            dimension_semantics=("arbitrary",), vmem_limit_bytes=VMEM_LIMIT),
        name="qkv",
    )(x2d, attn_g, w_qk, w_vt, head_gain)


def _padded_heads(qblk, heads):
    zeros = jnp.zeros((qblk.shape[0], HEAD_DIM), qblk.dtype)
    pieces = []
    for h in range(heads):
        piece = qblk[:, HEAD_DIM * h:HEAD_DIM * (h + 1)]
        pieces.append(jnp.concatenate([piece, zeros] if h < GROUP_HEADS else [zeros, piece], axis=1))
    return jnp.concatenate(pieces, axis=0)


def _swa_kernel(tbl_ref, sink_ref, bkt_ref, q_ref, k_ref, vt_ref, o_ref, bias_scr):
    W = SWA_WINDOW
    H = SWA_Q_HEADS
    S = q_ref.shape[0]

    @pl.when(pl.program_id(0) == 0)
    def _init_bias():
        bkt = bkt_ref[...]
        for h in range(H):
            bias_scr[:, W * h:W * (h + 1)] = _bias_from_buckets(bkt, tbl_ref, h)

    key_i = lax.broadcasted_iota(I32, (2 * W, H * W), 0)
    qry_i = lax.broadcasted_iota(I32, (2 * W, H * W), 1) & (W - 1)
    dist = qry_i + W - key_i
    band = (dist >= 0) & (dist < W)
    own_part = key_i >= W
    head_of_lane = lax.broadcasted_iota(I32, (1, H * W), 1) // W
    sinks = jnp.zeros((1, H * W), F32)
    for h in range(H):
        sinks = jnp.where(head_of_lane == h, sink_ref[h] * LOG2E, sinks)

    def body(i, carry):
        r0 = pl.multiple_of(i * W, W)
        prev = jnp.maximum(i - 1, 0)
        p0 = pl.multiple_of(prev * W, W)
        q8 = _padded_heads(q_ref[pl.ds(r0, W), :], H)
        kband = jnp.concatenate([k_ref[pl.ds(p0, W), :], k_ref[pl.ds(r0, W), :]], axis=0)
        mask = band & (own_part | (i > 0))
        s = jnp.where(mask, _dot_nt(kband, q8) + bias_scr[...], NEG)
        m = jnp.maximum(jnp.max(s, axis=0, keepdims=True), sinks)
        e = jnp.exp2(s - m)
        den = jnp.sum(e, axis=0, keepdims=True) + jnp.exp2(sinks - m)
        vtband = jnp.concatenate([vt_ref[prev], vt_ref[i]], axis=1)
        ot = jnp.dot(vtband, e.astype(BF16), preferred_element_type=F32) / den
        for pr in range(H // 2):
            f0 = HEAD_DIM * (2 * pr // GROUP_HEADS)
            pair = ot[f0:f0 + HEAD_DIM, 2 * W * pr:2 * W * (pr + 1)]
            o_ref[pl.ds(r0, W), LANES * pr:LANES * (pr + 1)] = jnp.concatenate(
                [pair[:, :W], pair[:, W:]], axis=0).T
        return carry

    lax.fori_loop(0, S // W, body, 0)


def _swa_call(tbl_flat, sinks, qa, ka, vat, B, S):
    W = SWA_WINDOW
    bkt = jnp.asarray(np.ascontiguousarray(_band_buckets(W).T))
    smem = pl.BlockSpec(memory_space=pltpu.SMEM)
    return pl.pallas_call(
        _swa_kernel,
        grid=(B,),
        in_specs=[smem, smem,
                  pl.BlockSpec(bkt.shape, lambda b: (0, 0)),
                  pl.BlockSpec((S, qa.shape[1]), lambda b: (b, 0)),
                  pl.BlockSpec((S, LANES), lambda b: (b, 0)),
                  pl.BlockSpec((S // W, LANES, W), lambda b: (b, 0, 0))],
        out_specs=pl.BlockSpec((S, qa.shape[1]), lambda b: (b, 0)),
        out_shape=jax.ShapeDtypeStruct(qa.shape, F32),
        scratch_shapes=[pltpu.VMEM((2 * W, SWA_Q_HEADS * W), F32)],
        compiler_params=pltpu.CompilerParams(
            dimension_semantics=("arbitrary",), vmem_limit_bytes=VMEM_LIMIT),
        name="swa",
    )(tbl_flat, sinks, bkt, qa, ka, vat)


def _moba_kernel(tbl_ref, bkt_ref, q_ref, k_ref, vt_ref, o_ref,
                 bias_scr, km_scr, q8_scr, sel_scr, m_scr, l_scr, acc_scr):
    BS = MOBA_BLOCK
    H = MOBA_Q_HEADS
    S = q_ref.shape[0]
    NBK = S // BS
    NQ = H * BS
    GQ = GROUP_HEADS * BS

    @pl.when(pl.program_id(0) == 0)
    def _init_bias():
        bkt = bkt_ref[...]
        for h in range(H):
            band = _bias_from_buckets(bkt, tbl_ref, SWA_Q_HEADS + h)
            far = tbl_ref[(REL_BUCKETS - 1) * N_HEADS + SWA_Q_HEADS + h] * LOG2E
            bias_scr[0, :, BS * h:BS * (h + 1)] = jnp.full((BS, BS), far, F32)
            bias_scr[1, :, BS * h:BS * (h + 1)] = band[:BS]
            bias_scr[2, :, BS * h:BS * (h + 1)] = band[BS:]

    blk_row = lax.broadcasted_iota(I32, (BF16_SUBLANES, 1), 0)
    kmean = jnp.sum(k_ref[...].astype(F32).reshape(NBK, BS, LANES), axis=1) / BS
    kmean = jnp.concatenate([kmean, jnp.zeros((BF16_SUBLANES - NBK, LANES), F32)], axis=0)
    km_hi = kmean.astype(BF16)
    km_scr[0] = km_hi
    km_scr[1] = (kmean - km_hi.astype(F32)).astype(BF16)

    def scores(n, case):
        k0 = pl.multiple_of(n * BS, BS)
        return _dot_nt(k_ref[pl.ds(k0, BS), :], q8_scr[...]) + bias_scr[case]

    def weighted_values(n, e):
        pv = jnp.dot(vt_ref[n], e.astype(BF16), preferred_element_type=F32)
        return jnp.concatenate([pv[:HEAD_DIM, :GQ], pv[HEAD_DIM:, GQ:]], axis=1)

    def qblock(mi, carry):
        r0 = pl.multiple_of(mi * BS, BS)
        q8 = _padded_heads(q_ref[pl.ds(r0, BS), :], H)
        q8_scr[...] = q8
        gate = _dot_nt(km_scr[0], q8) + _dot_nt(km_scr[1], q8)
        past = blk_row < mi
        gm = jnp.where(past, gate, NEG)
        cnt = jnp.zeros(gm.shape, F32)
        for n in range(NBK - 1):
            col = gm[n:n + 1, :]
            beats = (gm > col) | ((gm == col) & (blk_row < n))
            c = jnp.sum(jnp.where(beats, 1.0, 0.0), axis=0, keepdims=True)
            cnt = jnp.where(blk_row == n, c, cnt)
        sel_scr[...] = jnp.where(past & (cnt < MOBA_TOPK), 1.0, 0.0)

        key_i = lax.broadcasted_iota(I32, (BS, NQ), 0)
        qry_i = lax.broadcasted_iota(I32, (BS, NQ), 1) & (BS - 1)
        s = jnp.where(key_i <= qry_i, scores(mi, 2), NEG)
        m0 = jnp.max(s, axis=0, keepdims=True)
        e = jnp.exp2(s - m0)
        m_scr[...] = m0
        l_scr[...] = jnp.sum(e, axis=0, keepdims=True)
        acc_scr[...] = weighted_values(mi, e)

        def kvblock(j, c2):
            n = mi - j
            s = jnp.where(sel_scr[pl.ds(n, 1), :] > 0.5, scores(n, jnp.where(j == 1, 1, 0)), NEG)
            m_prev = m_scr[...]
            m_new = jnp.maximum(m_prev, jnp.max(s, axis=0, keepdims=True))
            alpha = jnp.exp2(m_prev - m_new)
            e = jnp.exp2(s - m_new)
            l_scr[...] = alpha * l_scr[...] + jnp.sum(e, axis=0, keepdims=True)
            acc_scr[...] = alpha * acc_scr[...] + weighted_values(n, e)
            m_scr[...] = m_new
            return c2

        lax.fori_loop(1, mi + 1, kvblock, 0)
        ot = acc_scr[...] / l_scr[...]
        for pr in range(H // 2):
            pair = ot[:, 2 * BS * pr:2 * BS * (pr + 1)]
            o_ref[pl.ds(r0, BS), LANES * pr:LANES * (pr + 1)] = jnp.concatenate(
                [pair[:, :BS], pair[:, BS:]], axis=0).T
        return carry

    lax.fori_loop(0, NBK, qblock, 0)


def _moba_call(tbl_flat, qb, kb, vbt, B, S):
    BS = MOBA_BLOCK
    NQ = MOBA_Q_HEADS * BS
    NBK = S // BS
    bkt = jnp.asarray(np.ascontiguousarray(_band_buckets(BS).T))
    smem = pl.BlockSpec(memory_space=pltpu.SMEM)
    return pl.pallas_call(
        _moba_kernel,
        grid=(B,),
        in_specs=[smem,
                  pl.BlockSpec(bkt.shape, lambda b: (0, 0)),
                  pl.BlockSpec((S, qb.shape[1]), lambda b: (b, 0)),
                  pl.BlockSpec((S, LANES), lambda b: (b, 0)),
                  pl.BlockSpec((NBK, LANES, BS), lambda b: (b, 0, 0))],
        out_specs=pl.BlockSpec((S, qb.shape[1]), lambda b: (b, 0)),
        out_shape=jax.ShapeDtypeStruct(qb.shape, F32),
        scratch_shapes=[pltpu.VMEM((3, BS, NQ), F32),
                        pltpu.VMEM((2, BF16_SUBLANES, LANES), BF16),
                        pltpu.VMEM((NQ, LANES), BF16),
                        pltpu.VMEM((BF16_SUBLANES, NQ), F32),
                        pltpu.VMEM((1, NQ), F32),
                        pltpu.VMEM((1, NQ), F32),
                        pltpu.VMEM((HEAD_DIM, NQ), F32)],
        compiler_params=pltpu.CompilerParams(
            dimension_semantics=("arbitrary",), vmem_limit_bytes=VMEM_LIMIT),
        name="moba",
    )(tbl_flat, bkt, qb, kb, vbt)


def _outproj_kernel(oa_ref, ob_ref, x_ref, ga_ref, gb_ref, woa_ref, wob_ref, fg_ref,
                    wrh_ref, wrl_ref, rb_ref, x1_ref, hf_ref, eid_ref, wt_ref):
    na = _rmsnorm(oa_ref[...], ga_ref[...]).astype(BF16)
    nb = _rmsnorm(ob_ref[...], gb_ref[...]).astype(BF16)
    y = (jnp.dot(na, woa_ref[...], preferred_element_type=F32)
         + jnp.dot(nb, wob_ref[...], preferred_element_type=F32))
    x1 = x_ref[...] + y
    x1_ref[...] = x1
    hf = _rmsnorm(x1, fg_ref[...])
    _to_token_tiles(hf_ref, hf)

    hf_hi = hf.astype(BF16)
    hf_lo = (hf - hf_hi.astype(F32)).astype(BF16)
    lt = (_dot_nt(wrh_ref[...], hf_hi) + _dot_nt(wrh_ref[...], hf_lo)
          + _dot_nt(wrl_ref[...], hf_hi) + rb_ref[...])
    gl = [lt[j:j + 1, :] for j in range(N_GROUPS)]
    best = gl[0]
    gsel = jnp.zeros(best.shape, I32)
    for j in range(1, N_GROUPS):
        better = gl[j] > best
        gsel = jnp.where(better, j, gsel)
        best = jnp.where(better, gl[j], best)
    gsum = jnp.zeros(best.shape, F32)
    for j in range(N_GROUPS):
        gsum = gsum + jnp.exp(gl[j] - best)
    gw = 1.0 / gsum
    E = EXPERTS_PER_GROUP
    es = lt[EXPERT_ROW0:EXPERT_ROW0 + E, :]
    for j in range(1, N_GROUPS):
        es = jnp.where(gsel == j, lt[EXPERT_ROW0 + E * j:EXPERT_ROW0 + E * (j + 1), :], es)
    row = lax.broadcasted_iota(I32, es.shape, 0)
    v1 = jnp.max(es, axis=0, keepdims=True)
    i1 = jnp.min(jnp.where(es == v1, row, E), axis=0, keepdims=True)
    es2 = jnp.where(row == i1, -jnp.inf, es)
    v2 = jnp.max(es2, axis=0, keepdims=True)
    i2 = jnp.min(jnp.where(es2 == v2, row, E), axis=0, keepdims=True)
    e2 = jnp.exp(v2 - v1)
    den = 1.0 + e2
    eid_ref[...] = jnp.concatenate([gsel * E + i1, gsel * E + i2], axis=0)
    wt_ref[...] = jnp.concatenate([gw * (1.0 / den), gw * (e2 / den)], axis=0)


def _outproj_call(oa, ob, x2d, ga, gb, woa, wob, fg, wrh, wrl, rb):
    T, D = x2d.shape
    tm = TOKEN_TILE
    tok = lambda w: pl.BlockSpec((tm, w), lambda i: (i, 0))
    full = lambda a: pl.BlockSpec(a.shape, lambda i: (0, 0))
    col = pl.BlockSpec((EXPERT_TOPK, tm), lambda i: (0, i))
    return pl.pallas_call(
        _outproj_kernel,
        grid=(T // tm,),
        in_specs=[tok(oa.shape[1]), tok(ob.shape[1]), tok(D), full(ga), full(gb), full(woa), full(wob),
                  full(fg), full(wrh), full(wrl), full(rb)],
        out_specs=[tok(D), pl.BlockSpec((tm * ROW_TILE, LANES), lambda i: (i, 0)), col, col],
        out_shape=[jax.ShapeDtypeStruct((T, D), F32), jax.ShapeDtypeStruct((T * ROW_TILE, LANES), F32),
                   jax.ShapeDtypeStruct((EXPERT_TOPK, T), I32), jax.ShapeDtypeStruct((EXPERT_TOPK, T), F32)],
        compiler_params=pltpu.CompilerParams(
            dimension_semantics=("arbitrary",), vmem_limit_bytes=VMEM_LIMIT),
        name="outproj",
    )(oa, ob, x2d, ga, gb, woa, wob, fg, wrh, wrl, rb)


COPY_GROUP = 16


def _moe_kernel(be_ref, nv_ref, src_ref, srcn_ref, dst_ref, hf_hbm, wg_ref, wu_ref, wd_ref, out_hbm,
                xbuf, obuf, wgb, wub, wdb, gsem, ssem):
    i = pl.program_id(0)
    nblk = pl.num_programs(0)
    M = xbuf.shape[1] // ROW_TILE
    slot = i % 2
    nv_cur = nv_ref[i]
    nv_next = jnp.where(i + 1 < nblk, nv_ref[jnp.minimum(i + 1, nblk - 1)], 0)

    def for_real_groups(n_real, fn):
        for k in range(M // COPY_GROUP):
            @pl.when(k * COPY_GROUP < n_real)
            def _group():
                fn(range(k * COPY_GROUP, (k + 1) * COPY_GROUP))

    def gather_rows(idx_ref, buf_slot, n_real):
        def start(rows):
            for r in rows:
                tok_row = pl.multiple_of(idx_ref[0, 0, r] * ROW_TILE, ROW_TILE)
                pltpu.make_async_copy(hf_hbm.at[pl.ds(tok_row, ROW_TILE)],
                                      xbuf.at[buf_slot, pl.ds(r * ROW_TILE, ROW_TILE)],
                                      gsem.at[buf_slot]).start(priority=r % DMA_PRIORITIES)
        for_real_groups(n_real, start)

    def scatter_rows(idx_ref, buf_slot, n_real):
        def start(rows):
            for r in rows:
                dst_row = pl.multiple_of(idx_ref[0, 0, r] * ROW_TILE, ROW_TILE)
                pltpu.make_async_copy(obuf.at[buf_slot, pl.ds(r * ROW_TILE, ROW_TILE)],
                                      out_hbm.at[pl.ds(dst_row, ROW_TILE)],
                                      ssem.at[buf_slot]).start(priority=r % DMA_PRIORITIES)
        for_real_groups(n_real, start)

    def wait_rows(buf, buf_slot, sem, n_real):
        span = COPY_GROUP * ROW_TILE
        for_real_groups(n_real, lambda rows: pltpu.make_async_copy(
            hf_hbm.at[pl.ds(0, span)], buf.at[buf_slot, pl.ds(0, span)], sem.at[buf_slot]).wait())

    @pl.when(nv_cur > 0)
    def _compute():
        @pl.when(i == 0)
        def _first_gather():
            gather_rows(src_ref, 0, nv_cur)
            obuf[1] = jnp.zeros(obuf.shape[1:], F32)
            spare = pltpu.make_async_copy(
                obuf.at[1], out_hbm.at[pl.ds(out_hbm.shape[0] - M * ROW_TILE, M * ROW_TILE)], ssem.at[1])
            spare.start()
            spare.wait()

        @pl.when((i == 0) | (be_ref[i] != be_ref[jnp.maximum(i - 1, 0)]))
        def _cast_weights():
            wgb[...] = wg_ref[0].astype(BF16)
            wub[...] = wu_ref[0].astype(BF16)
            wdb[...] = wd_ref[0].astype(BF16)

        wait_rows(xbuf, slot, gsem, nv_cur)

        @pl.when(i >= 2)
        def _reuse_obuf():
            wait_rows(obuf, slot, ssem, nv_ref[jnp.maximum(i - 2, 0)])

        gather_rows(srcn_ref, 1 - slot, nv_next)
        xb = _from_token_tiles(xbuf.at[slot], M).astype(BF16)
        a = jnp.dot(xb, wgb[...], preferred_element_type=F32)
        u = jnp.dot(xb, wub[...], preferred_element_type=F32)
        hmid = (a * jax.nn.sigmoid(a)) * u
        _to_token_tiles(obuf.at[slot], jnp.dot(hmid.astype(BF16), wdb[...], preferred_element_type=F32))
        scatter_rows(dst_ref, slot, nv_cur)

        @pl.when(nv_next == 0)
        def _drain():
            wait_rows(obuf, slot, ssem, nv_cur)

            @pl.when(i >= 1)
            def _drain_prev():
                wait_rows(obuf, 1 - slot, ssem, nv_ref[jnp.maximum(i - 1, 0)])


def _moe_call(block_expert, n_real, src_tok, dst_row, hf, w_g, w_u, w_d):
    T = hf.shape[0] // ROW_TILE
    M = DISPATCH_BLOCK
    nblk = block_expert.shape[0]
    D, DE = w_g.shape[1], w_g.shape[2]
    assert D == ROW_TILE * LANES and hf.shape[1] == LANES and M % COPY_GROUP == 0
    idx_spec = lambda f: pl.BlockSpec((1, 1, M), f, memory_space=pltpu.SMEM)
    cur = lambda i, be, nu: (i, 0, 0)
    nxt = lambda i, be, nu: (jnp.minimum(i + 1, nblk - 1), 0, 0)
    wspec = lambda a, b: pl.BlockSpec((1, a, b), lambda i, be, nu: (be[i], 0, 0))
    grid_spec = pltpu.PrefetchScalarGridSpec(
        num_scalar_prefetch=2,
        grid=(nblk,),
        in_specs=[idx_spec(cur), idx_spec(nxt), idx_spec(cur),
                  pl.BlockSpec(memory_space=pl.ANY),
                  wspec(D, DE), wspec(D, DE), wspec(DE, D)],
        out_specs=pl.BlockSpec(memory_space=pl.ANY),
        scratch_shapes=[pltpu.VMEM((2, M * ROW_TILE, LANES), F32), pltpu.VMEM((2, M * ROW_TILE, LANES), F32),
                        pltpu.VMEM((D, DE), BF16), pltpu.VMEM((D, DE), BF16), pltpu.VMEM((DE, D), BF16),
                        pltpu.SemaphoreType.DMA((2,)), pltpu.SemaphoreType.DMA((2,))],
    )
    src3 = src_tok.reshape(nblk, 1, M)
    return pl.pallas_call(
        _moe_kernel,
        grid_spec=grid_spec,
        out_shape=jax.ShapeDtypeStruct(((EXPERT_TOPK * T + M) * ROW_TILE, LANES), F32),
        compiler_params=pltpu.CompilerParams(
            dimension_semantics=("arbitrary",), vmem_limit_bytes=VMEM_LIMIT),
        name="moe",
    )(block_expert, n_real, src3, src3, dst_row.reshape(nblk, 1, M), hf, w_g, w_u, w_d)


def _ple_kernel(x1_ref, m0_ref, m1_ref, wt_ref, p_ref, pg_ref, wgate_ref, wproj_ref, pog_ref, out_ref):
    proj = _rmsnorm(jnp.dot(p_ref[...].astype(BF16), wproj_ref[...], preferred_element_type=F32),
                    pog_ref[...])
    tm = x1_ref.shape[0]
    moe = (_from_token_tiles(m0_ref, tm) * wt_ref[:, 0:1] + _from_token_tiles(m1_ref, tm) * wt_ref[:, 1:2])
    x2 = x1_ref[...] + moe
    hn = _rmsnorm(x2, pg_ref[...]).astype(BF16)
    gate = jax.nn.sigmoid(jnp.dot(hn, wgate_ref[...], preferred_element_type=F32))
    out_ref[...] = x2 + gate * proj


def _ple_call(moe_rows, x1, wt_cols, p2d, pg, wgate, wproj, pog):
    T, D = x1.shape
    tm = PLE_TILE
    tok = lambda w: pl.BlockSpec((tm, w), lambda i: (i, 0))
    full = lambda a: pl.BlockSpec(a.shape, lambda i: (0, 0))
    first = pl.BlockSpec((tm * ROW_TILE, LANES), lambda i: (i, 0))
    second = pl.BlockSpec((tm * ROW_TILE, LANES), lambda i: (T // tm + i, 0))
    return pl.pallas_call(
        _ple_kernel,
        grid=(T // tm,),
        in_specs=[tok(D), first, second, tok(EXPERT_TOPK), tok(p2d.shape[1]),
                  full(pg), full(wgate), full(wproj), full(pog)],
        out_specs=tok(D),
        out_shape=jax.ShapeDtypeStruct((T, D), F32),
        compiler_params=pltpu.CompilerParams(
            dimension_semantics=("arbitrary",), vmem_limit_bytes=VMEM_LIMIT),
        name="ple",
    )(x1, moe_rows, moe_rows, wt_cols, p2d, pg, wgate, wproj, pog)


def _dispatch_plan(eid, T):
    A = EXPERT_TOPK * T
    M = DISPATCH_BLOCK
    nblk = A // M + N_EXPERTS
    assert A <= PAD_MARK
    e_flat = eid.reshape(A)
    experts = jnp.arange(N_EXPERTS, dtype=I32)
    counts = jnp.sum((e_flat[:, None] == experts[None, :]).astype(I32), axis=0)
    padded = (counts + M - 1) // M * M
    n_used = jnp.sum(padded) // M
    row = jnp.arange(M, dtype=I32)[None, :]
    pad_keys = jnp.where(row < (padded - counts)[:, None], 2 * experts[:, None] + 1, 2 * N_EXPERTS)
    keys = jnp.concatenate([2 * e_flat, pad_keys.reshape(-1)])
    vals = jnp.concatenate([jnp.arange(A, dtype=I32), jnp.full((N_EXPERTS * M,), PAD_MARK, I32)])
    packed = jnp.sort(keys * (PAD_MARK + 1) + vals).reshape(nblk, M)
    a_s = packed & PAD_MARK
    blk = jnp.arange(nblk, dtype=I32)
    be = jnp.minimum(packed[:, 0] // (2 * (PAD_MARK + 1)), N_EXPERTS - 1)
    be = jnp.where(blk < n_used, be, be[jnp.maximum(n_used - 1, 0)])
    valid = a_s != PAD_MARK
    src_tok = jnp.where(valid, jnp.where(a_s >= T, a_s - T, a_s), row)
    dst_row = jnp.where(valid, a_s, A + row)
    return be, jnp.sum(valid.astype(I32), axis=1), src_tok, dst_row


def _layer(x2d, p2d, rel_bias, attn_norm, w_in, swa_q_norm, swa_k_norm, swa_sinks, moba_q_norm,
           moba_k_norm, swa_out_norm, moba_out_norm, w_out, ffn_norm, w_rg, b_rg, w_re, b_re,
           w_g, w_u, w_d, ple_norm, w_ple_gate, w_ple_proj, ple_out_norm, B, S):
    T, D = x2d.shape
    row = lambda v: v.reshape(1, -1).astype(F32)
    head_gain = jnp.concatenate([
        jnp.tile(swa_q_norm, SWA_Q_HEADS), jnp.tile(swa_k_norm, SWA_KV_HEADS),
        jnp.tile(moba_q_norm, MOBA_Q_HEADS), jnp.tile(moba_k_norm, MOBA_KV_HEADS)])
    w_in_b = w_in.astype(BF16)
    qa_w, kv_w = SWA_Q_HEADS * HEAD_DIM, SWA_KV_HEADS * HEAD_DIM
    qb_w = MOBA_Q_HEADS * HEAD_DIM
    c_va = qa_w + kv_w
    c_qb = c_va + kv_w
    c_vb = c_qb + qb_w + kv_w
    w_qk = jnp.concatenate([w_in_b[:, :c_va], w_in_b[:, c_qb:c_vb]], axis=1)
    w_vt = jnp.concatenate([w_in_b[:, c_va:c_qb], w_in_b[:, c_vb:]], axis=1).T
    qa, ka, qb, kb, vat, vbt = _qkv_call(x2d, row(attn_norm), w_qk, w_vt, row(head_gain))

    tbl_flat = rel_bias.astype(F32).reshape(-1)
    oa = _swa_call(tbl_flat, swa_sinks.astype(F32), qa, ka, vat, B, S)
    ob = _moba_call(tbl_flat, qb, kb, vbt, B, S)

    pad_rows = lambda n: jnp.zeros((n, D), F32)
    wr = jnp.concatenate([w_rg.T, pad_rows(EXPERT_ROW0 - N_GROUPS), w_re.T,
                          pad_rows(ROUTER_ROWS - EXPERT_ROW0 - N_EXPERTS)], axis=0)
    wr_hi = wr.astype(BF16)
    wr_lo = (wr - wr_hi.astype(F32)).astype(BF16)
    rb = jnp.concatenate([b_rg, jnp.zeros((EXPERT_ROW0 - N_GROUPS,), F32), b_re,
                          jnp.zeros((ROUTER_ROWS - EXPERT_ROW0 - N_EXPERTS,), F32)]).reshape(ROUTER_ROWS, 1)
    wo = w_out.astype(BF16)
    na_w = SWA_Q_HEADS * HEAD_DIM
    x1, hf, eid, wts = _outproj_call(oa, ob, x2d, row(swa_out_norm), row(moba_out_norm),
                                     wo[:na_w], wo[na_w:], row(ffn_norm), wr_hi, wr_lo, rb)

    be, n_real, src_tok, dst_row = _dispatch_plan(eid, T)
    moe_rows = _moe_call(be, n_real, src_tok, dst_row, hf, w_g, w_u, w_d)
    return _ple_call(moe_rows, x1, wts.T, p2d, row(ple_norm), w_ple_gate.astype(BF16),
                     w_ple_proj.astype(BF16), row(ple_out_norm))


def kernel(x, p, rel_bias, attn_norm, w_in, swa_q_norm, swa_k_norm, swa_sinks, moba_q_norm, moba_k_norm,
           swa_out_norm, moba_out_norm, w_out, ffn_norm, w_router_group, b_router_group, w_router_expert,
           b_router_expert, w_exp_gate, w_exp_up, w_exp_down, ple_norm, w_ple_gate, w_ple_proj, ple_out_norm):
    B, S, D = x.shape
    x2d = x.reshape(B * S, D)
    for i in range(p.shape[0]):
        x2d = _layer(x2d, p[i].reshape(B * S, -1), rel_bias, attn_norm[i], w_in[i], swa_q_norm[i],
                     swa_k_norm[i], swa_sinks[i], moba_q_norm[i], moba_k_norm[i], swa_out_norm[i],
                     moba_out_norm[i], w_out[i], ffn_norm[i], w_router_group[i], b_router_group[i],
                     w_router_expert[i], b_router_expert[i], w_exp_gate[i], w_exp_up[i], w_exp_down[i],
                     ple_norm[i], w_ple_gate[i], w_ple_proj[i], ple_out_norm[i], B, S)
    return x2d.reshape(B, S, D)
```

```python
import math

import numpy as np
import jax
import jax.numpy as jnp
from jax import lax
from jax.experimental import pallas as pl
from jax.experimental.pallas import tpu as pltpu

F32 = jnp.float32
BF16 = jnp.bfloat16
I32 = jnp.int32

HEAD_DIM = 64
SWA_Q_HEADS = 8
SWA_KV_HEADS = 2
SWA_WINDOW = 128
MOBA_Q_HEADS = 8
MOBA_KV_HEADS = 2
MOBA_BLOCK = 256
MOBA_TOPK = 3
N_HEADS = SWA_Q_HEADS + MOBA_Q_HEADS
REL_BUCKETS = 32
REL_MAX_DIST = 128
N_GROUPS = 4
EXPERTS_PER_GROUP = 8
N_EXPERTS = N_GROUPS * EXPERTS_PER_GROUP
EXPERT_TOPK = 2
DISPATCH_BLOCK = 256
PAD_MARK = 0xFFFF
EPS = 1e-6
NEG = -1e30
LOG2E = math.log2(math.e)

DMA_PRIORITIES = 2
LANES = 128
ROW_TILE = 8
BF16_SUBLANES = 16
GROUP_HEADS = 4
GROUP_W = GROUP_HEADS * HEAD_DIM
ROUTER_ROWS = 128
EXPERT_ROW0 = 8
TOKEN_TILE = 512
PLE_TILE = 512
VMEM_LIMIT = 52 * 1024 * 1024


def _dot_nt(a, b):
    return lax.dot_general(a, b, (((1,), (1,)), ((), ())), preferred_element_type=F32)


def _rmsnorm(x, g):
    ms = jnp.mean(x * x, axis=-1, keepdims=True)
    return x * lax.rsqrt(ms + EPS) * g


def _to_token_tiles(ref, x):
    m = x.shape[0]
    for j in range(x.shape[1] // LANES):
        ref[pl.ds(j, m, stride=ROW_TILE), :] = x[:, LANES * j:LANES * (j + 1)]


def _from_token_tiles(ref, m):
    return jnp.concatenate([ref[pl.ds(j, m, stride=ROW_TILE), :] for j in range(ref.shape[0] // m)], axis=1)


def _rel_bucket_np(dist):
    n = np.maximum(dist, 0)
    exact = REL_BUCKETS // 2
    nf = np.maximum(n, 1).astype(np.float32)
    large = exact + (np.log(nf / exact) / math.log(REL_MAX_DIST / exact)
                     * (REL_BUCKETS - exact)).astype(np.int32)
    return np.where(n < exact, n, np.minimum(large, REL_BUCKETS - 1)).astype(np.int32)


def _band_buckets(block):
    qi = np.arange(block)[:, None]
    kj = np.arange(2 * block)[None, :]
    return _rel_bucket_np(qi + block - kj)


def _bias_from_buckets(bkt, tbl_ref, head):
    acc = jnp.zeros(bkt.shape, F32)
    for j in range(REL_BUCKETS):
        acc = jnp.where(bkt == j, tbl_ref[j * N_HEADS + head] * LOG2E, acc)
    return acc


def _qkv_kernel(x_ref, g_ref, w_ref, wvt_ref, hg_ref, qa_ref, ka_ref, qb_ref, kb_ref, vat_ref, vbt_ref):
    h = _rmsnorm(x_ref[...], g_ref[...]).astype(BF16)
    acc = jnp.dot(h, w_ref[...], preferred_element_type=F32)
    lo = lax.broadcasted_iota(I32, (1, LANES), 1) < HEAD_DIM

    def head_normed(c, scale):
        blk = acc[:, LANES * c:LANES * (c + 1)]
        sq = blk * blk
        s_lo = jnp.sum(jnp.where(lo, sq, 0.0), axis=-1, keepdims=True)
        s_hi = jnp.sum(jnp.where(lo, 0.0, sq), axis=-1, keepdims=True)
        inv = jnp.where(lo, lax.rsqrt(s_lo / HEAD_DIM + EPS), lax.rsqrt(s_hi / HEAD_DIM + EPS))
        return blk * inv * hg_ref[:, LANES * c:LANES * (c + 1)] * scale

    scale = HEAD_DIM ** -0.5 * LOG2E
    for c in range(4):
        qa_ref[:, LANES * c:LANES * (c + 1)] = head_normed(c, scale).astype(BF16)
        qb_ref[:, LANES * c:LANES * (c + 1)] = head_normed(5 + c, scale).astype(BF16)
    ka_ref[...] = head_normed(4, 1.0).astype(BF16)
    kb_ref[...] = head_normed(9, 1.0).astype(BF16)
    vt = _dot_nt(wvt_ref[...], h).astype(BF16)
    for j in range(vat_ref.shape[0]):
        vat_ref[j] = vt[:LANES, SWA_WINDOW * j:SWA_WINDOW * (j + 1)]
    for j in range(vbt_ref.shape[0]):
        vbt_ref[j] = vt[LANES:, MOBA_BLOCK * j:MOBA_BLOCK * (j + 1)]


def _qkv_call(x2d, attn_g, w_qk, w_vt, head_gain):
    T, D = x2d.shape
    tm = TOKEN_TILE
    tok = lambda w: pl.BlockSpec((tm, w), lambda i: (i, 0))
    full = lambda a: pl.BlockSpec(a.shape, lambda i: (0, 0))
    out_w = (SWA_Q_HEADS * HEAD_DIM, LANES, MOBA_Q_HEADS * HEAD_DIM, LANES)
    slabs = lambda blk: pl.BlockSpec((tm // blk, LANES, blk), lambda i: (i, 0, 0))
    slab_shape = lambda blk: jax.ShapeDtypeStruct((T // blk, LANES, blk), BF16)
    return pl.pallas_call(
        _qkv_kernel,
        grid=(T // tm,),
        in_specs=[tok(D), full(attn_g), full(w_qk), full(w_vt), full(head_gain)],
        out_specs=[tok(w) for w in out_w] + [slabs(SWA_WINDOW), slabs(MOBA_BLOCK)],
        out_shape=[jax.ShapeDtypeStruct((T, w), BF16) for w in out_w]
        + [slab_shape(SWA_WINDOW), slab_shape(MOBA_BLOCK)],
        compiler_params=pltpu.CompilerParams(
            dimension_semantics=("arbitrary",), vmem_limit_bytes=VMEM_LIMIT),
        name="qkv",
    )(x2d, attn_g, w_qk, w_vt, head_gain)


def _padded_heads(qblk, heads):
    zeros = jnp.zeros((qblk.shape[0], HEAD_DIM), qblk.dtype)
    pieces = []
    for h in range(heads):
        piece = qblk[:, HEAD_DIM * h:HEAD_DIM * (h + 1)]
        pieces.append(jnp.concatenate([piece, zeros] if h < GROUP_HEADS else [zeros, piece], axis=1))
    return jnp.concatenate(pieces, axis=0)


def _swa_kernel(tbl_ref, sink_ref, bkt_ref, q_ref, k_ref, vt_ref, o_ref, bias_scr):
    W = SWA_WINDOW
    H = SWA_Q_HEADS
    S = q_ref.shape[0]

    @pl.when(pl.program_id(0) == 0)
    def _init_bias():
        bkt = bkt_ref[...]
        for h in range(H):
            bias_scr[:, W * h:W * (h + 1)] = _bias_from_buckets(bkt, tbl_ref, h)

    key_i = lax.broadcasted_iota(I32, (2 * W, H * W), 0)
    qry_i = lax.broadcasted_iota(I32, (2 * W, H * W), 1) & (W - 1)
    dist = qry_i + W - key_i
    band = (dist >= 0) & (dist < W)
    own_part = key_i >= W
    head_of_lane = lax.broadcasted_iota(I32, (1, H * W), 1) // W
    sinks = jnp.zeros((1, H * W), F32)
    for h in range(H):
        sinks = jnp.where(head_of_lane == h, sink_ref[h] * LOG2E, sinks)

    def body(i, carry):
        r0 = pl.multiple_of(i * W, W)
        prev = jnp.maximum(i - 1, 0)
        p0 = pl.multiple_of(prev * W, W)
        q8 = _padded_heads(q_ref[pl.ds(r0, W), :], H)
        kband = jnp.concatenate([k_ref[pl.ds(p0, W), :], k_ref[pl.ds(r0, W), :]], axis=0)
        mask = band & (own_part | (i > 0))
        s = jnp.where(mask, _dot_nt(kband, q8) + bias_scr[...], NEG)
        m = jnp.maximum(jnp.max(s, axis=0, keepdims=True), sinks)
        e = jnp.exp2(s - m)
        den = jnp.sum(e, axis=0, keepdims=True) + jnp.exp2(sinks - m)
        vtband = jnp.concatenate([vt_ref[prev], vt_ref[i]], axis=1)
        ot = jnp.dot(vtband, e.astype(BF16), preferred_element_type=F32) / den
        for pr in range(H // 2):
            f0 = HEAD_DIM * (2 * pr // GROUP_HEADS)
            pair = ot[f0:f0 + HEAD_DIM, 2 * W * pr:2 * W * (pr + 1)]
            o_ref[pl.ds(r0, W), LANES * pr:LANES * (pr + 1)] = jnp.concatenate(
                [pair[:, :W], pair[:, W:]], axis=0).T
        return carry

    lax.fori_loop(0, S // W, body, 0)


def _swa_call(tbl_flat, sinks, qa, ka, vat, B, S):
    W = SWA_WINDOW
    bkt = jnp.asarray(np.ascontiguousarray(_band_buckets(W).T))
    smem = pl.BlockSpec(memory_space=pltpu.SMEM)
    return pl.pallas_call(
        _swa_kernel,
        grid=(B,),
        in_specs=[smem, smem,
                  pl.BlockSpec(bkt.shape, lambda b: (0, 0)),
                  pl.BlockSpec((S, qa.shape[1]), lambda b: (b, 0)),
                  pl.BlockSpec((S, LANES), lambda b: (b, 0)),
                  pl.BlockSpec((S // W, LANES, W), lambda b: (b, 0, 0))],
        out_specs=pl.BlockSpec((S, qa.shape[1]), lambda b: (b, 0)),
        out_shape=jax.ShapeDtypeStruct(qa.shape, F32),
        scratch_shapes=[pltpu.VMEM((2 * W, SWA_Q_HEADS * W), F32)],
        compiler_params=pltpu.CompilerParams(
            dimension_semantics=("arbitrary",), vmem_limit_bytes=VMEM_LIMIT),
        name="swa",
    )(tbl_flat, sinks, bkt, qa, ka, vat)


def _moba_kernel(tbl_ref, bkt_ref, q_ref, k_ref, vt_ref, o_ref,
                 bias_scr, km_scr, q8_scr, sel_scr, m_scr, l_scr, acc_scr):
    BS = MOBA_BLOCK
    H = MOBA_Q_HEADS
    S = q_ref.shape[0]
    NBK = S // BS
    NQ = H * BS
    GQ = GROUP_HEADS * BS

    @pl.when(pl.program_id(0) == 0)
    def _init_bias():
        bkt = bkt_ref[...]
        for h in range(H):
            band = _bias_from_buckets(bkt, tbl_ref, SWA_Q_HEADS + h)
            far = tbl_ref[(REL_BUCKETS - 1) * N_HEADS + SWA_Q_HEADS + h] * LOG2E
            bias_scr[0, :, BS * h:BS * (h + 1)] = jnp.full((BS, BS), far, F32)
            bias_scr[1, :, BS * h:BS * (h + 1)] = band[:BS]
            bias_scr[2, :, BS * h:BS * (h + 1)] = band[BS:]

    blk_row = lax.broadcasted_iota(I32, (BF16_SUBLANES, 1), 0)
    kmean = jnp.sum(k_ref[...].astype(F32).reshape(NBK, BS, LANES), axis=1) / BS
    kmean = jnp.concatenate([kmean, jnp.zeros((BF16_SUBLANES - NBK, LANES), F32)], axis=0)
    km_hi = kmean.astype(BF16)
    km_scr[0] = km_hi
    km_scr[1] = (kmean - km_hi.astype(F32)).astype(BF16)

    def scores(n, case):
        k0 = pl.multiple_of(n * BS, BS)
        return _dot_nt(k_ref[pl.ds(k0, BS), :], q8_scr[...]) + bias_scr[case]

    def weighted_values(n, e):
        pv = jnp.dot(vt_ref[n], e.astype(BF16), preferred_element_type=F32)
        return jnp.concatenate([pv[:HEAD_DIM, :GQ], pv[HEAD_DIM:, GQ:]], axis=1)

    def qblock(mi, carry):
        r0 = pl.multiple_of(mi * BS, BS)
        q8 = _padded_heads(q_ref[pl.ds(r0, BS), :], H)
        q8_scr[...] = q8
        gate = _dot_nt(km_scr[0], q8) + _dot_nt(km_scr[1], q8)
        past = blk_row < mi
        gm = jnp.where(past, gate, NEG)
        cnt = jnp.zeros(gm.shape, F32)
        for n in range(NBK - 1):
            col = gm[n:n + 1, :]
            beats = (gm > col) | ((gm == col) & (blk_row < n))
            c = jnp.sum(jnp.where(beats, 1.0, 0.0), axis=0, keepdims=True)
            cnt = jnp.where(blk_row == n, c, cnt)
        sel_scr[...] = jnp.where(past & (cnt < MOBA_TOPK), 1.0, 0.0)

        key_i = lax.broadcasted_iota(I32, (BS, NQ), 0)
        qry_i = lax.broadcasted_iota(I32, (BS, NQ), 1) & (BS - 1)
        s = jnp.where(key_i <= qry_i, scores(mi, 2), NEG)
        m0 = jnp.max(s, axis=0, keepdims=True)
        e = jnp.exp2(s - m0)
        m_scr[...] = m0
        l_scr[...] = jnp.sum(e, axis=0, keepdims=True)
        acc_scr[...] = weighted_values(mi, e)

        def kvblock(j, c2):
            n = mi - j
            s = jnp.where(sel_scr[pl.ds(n, 1), :] > 0.5, scores(n, jnp.where(j == 1, 1, 0)), NEG)
            m_prev = m_scr[...]
            m_new = jnp.maximum(m_prev, jnp.max(s, axis=0, keepdims=True))
            alpha = jnp.exp2(m_prev - m_new)
            e = jnp.exp2(s - m_new)
            l_scr[...] = alpha * l_scr[...] + jnp.sum(e, axis=0, keepdims=True)
            acc_scr[...] = alpha * acc_scr[...] + weighted_values(n, e)
            m_scr[...] = m_new
            return c2

        lax.fori_loop(1, mi + 1, kvblock, 0)
        ot = acc_scr[...] / l_scr[...]
        for pr in range(H // 2):
            pair = ot[:, 2 * BS * pr:2 * BS * (pr + 1)]
            o_ref[pl.ds(r0, BS), LANES * pr:LANES * (pr + 1)] = jnp.concatenate(
                [pair[:, :BS], pair[:, BS:]], axis=0).T
        return carry

    lax.fori_loop(0, NBK, qblock, 0)


def _moba_call(tbl_flat, qb, kb, vbt, B, S):
    BS = MOBA_BLOCK
    NQ = MOBA_Q_HEADS * BS
    NBK = S // BS
    bkt = jnp.asarray(np.ascontiguousarray(_band_buckets(BS).T))
    smem = pl.BlockSpec(memory_space=pltpu.SMEM)
    return pl.pallas_call(
        _moba_kernel,
        grid=(B,),
        in_specs=[smem,
                  pl.BlockSpec(bkt.shape, lambda b: (0, 0)),
                  pl.BlockSpec((S, qb.shape[1]), lambda b: (b, 0)),
                  pl.BlockSpec((S, LANES), lambda b: (b, 0)),
                  pl.BlockSpec((NBK, LANES, BS), lambda b: (b, 0, 0))],
        out_specs=pl.BlockSpec((S, qb.shape[1]), lambda b: (b, 0)),
        out_shape=jax.ShapeDtypeStruct(qb.shape, F32),
        scratch_shapes=[pltpu.VMEM((3, BS, NQ), F32),
                        pltpu.VMEM((2, BF16_SUBLANES, LANES), BF16),
                        pltpu.VMEM((NQ, LANES), BF16),
                        pltpu.VMEM((BF16_SUBLANES, NQ), F32),
                        pltpu.VMEM((1, NQ), F32),
                        pltpu.VMEM((1, NQ), F32),
                        pltpu.VMEM((HEAD_DIM, NQ), F32)],
        compiler_params=pltpu.CompilerParams(
            dimension_semantics=("arbitrary",), vmem_limit_bytes=VMEM_LIMIT),
        name="moba",
    )(tbl_flat, bkt, qb, kb, vbt)


def _outproj_kernel(oa_ref, ob_ref, x_ref, ga_ref, gb_ref, woa_ref, wob_ref, fg_ref,
                    wrh_ref, wrl_ref, rb_ref, x1_ref, hf_ref, eid_ref, wt_ref):
    na = _rmsnorm(oa_ref[...], ga_ref[...]).astype(BF16)
    nb = _rmsnorm(ob_ref[...], gb_ref[...]).astype(BF16)
    y = (jnp.dot(na, woa_ref[...], preferred_element_type=F32)
         + jnp.dot(nb, wob_ref[...], preferred_element_type=F32))
    x1 = x_ref[...] + y
    x1_ref[...] = x1
    hf = _rmsnorm(x1, fg_ref[...])
    _to_token_tiles(hf_ref, hf)

    hf_hi = hf.astype(BF16)
    hf_lo = (hf - hf_hi.astype(F32)).astype(BF16)
    lt = (_dot_nt(wrh_ref[...], hf_hi) + _dot_nt(wrh_ref[...], hf_lo)
          + _dot_nt(wrl_ref[...], hf_hi) + rb_ref[...])
    gl = [lt[j:j + 1, :] for j in range(N_GROUPS)]
    best = gl[0]
    gsel = jnp.zeros(best.shape, I32)
    for j in range(1, N_GROUPS):
        better = gl[j] > best
        gsel = jnp.where(better, j, gsel)
        best = jnp.where(better, gl[j], best)
    gsum = jnp.zeros(best.shape, F32)
    for j in range(N_GROUPS):
        gsum = gsum + jnp.exp(gl[j] - best)
    gw = 1.0 / gsum
    E = EXPERTS_PER_GROUP
    es = lt[EXPERT_ROW0:EXPERT_ROW0 + E, :]
    for j in range(1, N_GROUPS):
        es = jnp.where(gsel == j, lt[EXPERT_ROW0 + E * j:EXPERT_ROW0 + E * (j + 1), :], es)
    row = lax.broadcasted_iota(I32, es.shape, 0)
    v1 = jnp.max(es, axis=0, keepdims=True)
    i1 = jnp.min(jnp.where(es == v1, row, E), axis=0, keepdims=True)
    es2 = jnp.where(row == i1, -jnp.inf, es)
    v2 = jnp.max(es2, axis=0, keepdims=True)
    i2 = jnp.min(jnp.where(es2 == v2, row, E), axis=0, keepdims=True)
    e2 = jnp.exp(v2 - v1)
    den = 1.0 + e2
    eid_ref[...] = jnp.concatenate([gsel * E + i1, gsel * E + i2], axis=0)
    wt_ref[...] = jnp.concatenate([gw * (1.0 / den), gw * (e2 / den)], axis=0)


def _outproj_call(oa, ob, x2d, ga, gb, woa, wob, fg, wrh, wrl, rb):
    T, D = x2d.shape
    tm = TOKEN_TILE
    tok = lambda w: pl.BlockSpec((tm, w), lambda i: (i, 0))
    full = lambda a: pl.BlockSpec(a.shape, lambda i: (0, 0))
    col = pl.BlockSpec((EXPERT_TOPK, tm), lambda i: (0, i))
    return pl.pallas_call(
        _outproj_kernel,
        grid=(T // tm,),
        in_specs=[tok(oa.shape[1]), tok(ob.shape[1]), tok(D), full(ga), full(gb), full(woa), full(wob),
                  full(fg), full(wrh), full(wrl), full(rb)],
        out_specs=[tok(D), pl.BlockSpec((tm * ROW_TILE, LANES), lambda i: (i, 0)), col, col],
        out_shape=[jax.ShapeDtypeStruct((T, D), F32), jax.ShapeDtypeStruct((T * ROW_TILE, LANES), F32),
                   jax.ShapeDtypeStruct((EXPERT_TOPK, T), I32), jax.ShapeDtypeStruct((EXPERT_TOPK, T), F32)],
        compiler_params=pltpu.CompilerParams(
            dimension_semantics=("arbitrary",), vmem_limit_bytes=VMEM_LIMIT),
        name="outproj",
    )(oa, ob, x2d, ga, gb, woa, wob, fg, wrh, wrl, rb)


COPY_GROUP = 64


def _moe_kernel(be_ref, nv_ref, src_ref, srcn_ref, dst_ref, hf_hbm, wg_ref, wu_ref, wd_ref, out_hbm,
                xbuf, obuf, wgb, wub, wdb, gsem, ssem):
    i = pl.program_id(0)
    nblk = pl.num_programs(0)
    M = xbuf.shape[1] // ROW_TILE
    slot = i % 2
    nv_cur = nv_ref[i]
    nv_next = jnp.where(i + 1 < nblk, nv_ref[jnp.minimum(i + 1, nblk - 1)], 0)

    def for_real_groups(n_real, fn):
        for k in range(M // COPY_GROUP):
            @pl.when(k * COPY_GROUP < n_real)
            def _group():
                fn(range(k * COPY_GROUP, (k + 1) * COPY_GROUP))

    def gather_rows(idx_ref, buf_slot, n_real):
        def start(rows):
            for r in rows:
                tok_row = pl.multiple_of(idx_ref[0, 0, r] * ROW_TILE, ROW_TILE)
                pltpu.make_async_copy(hf_hbm.at[pl.ds(tok_row, ROW_TILE)],
                                      xbuf.at[buf_slot, pl.ds(r * ROW_TILE, ROW_TILE)],
                                      gsem.at[buf_slot]).start(priority=r % DMA_PRIORITIES)
        for_real_groups(n_real, start)

    def scatter_rows(idx_ref, buf_slot, n_real):
        def start(rows):
            for r in rows:
                dst_row = pl.multiple_of(idx_ref[0, 0, r] * ROW_TILE, ROW_TILE)
                pltpu.make_async_copy(obuf.at[buf_slot, pl.ds(r * ROW_TILE, ROW_TILE)],
                                      out_hbm.at[pl.ds(dst_row, ROW_TILE)],
                                      ssem.at[buf_slot]).start(priority=r % DMA_PRIORITIES)
        for_real_groups(n_real, start)

    def wait_rows(buf, buf_slot, sem, n_real):
        span = COPY_GROUP * ROW_TILE
        for_real_groups(n_real, lambda rows: pltpu.make_async_copy(
            hf_hbm.at[pl.ds(0, span)], buf.at[buf_slot, pl.ds(0, span)], sem.at[buf_slot]).wait())

    @pl.when(nv_cur > 0)
    def _compute():
        @pl.when(i == 0)
        def _first_gather():
            gather_rows(src_ref, 0, nv_cur)
            obuf[1] = jnp.zeros(obuf.shape[1:], F32)
            spare = pltpu.make_async_copy(
                obuf.at[1], out_hbm.at[pl.ds(out_hbm.shape[0] - M * ROW_TILE, M * ROW_TILE)], ssem.at[1])
            spare.start()
            spare.wait()

        @pl.when((i == 0) | (be_ref[i] != be_ref[jnp.maximum(i - 1, 0)]))
        def _cast_weights():
            wgb[...] = wg_ref[0].astype(BF16)
            wub[...] = wu_ref[0].astype(BF16)
            wdb[...] = wd_ref[0].astype(BF16)

        wait_rows(xbuf, slot, gsem, nv_cur)

        @pl.when(i >= 2)
        def _reuse_obuf():
            wait_rows(obuf, slot, ssem, nv_ref[jnp.maximum(i - 2, 0)])

        gather_rows(srcn_ref, 1 - slot, nv_next)
        xb = _from_token_tiles(xbuf.at[slot], M).astype(BF16)
        a = jnp.dot(xb, wgb[...], preferred_element_type=F32)
        u = jnp.dot(xb, wub[...], preferred_element_type=F32)
        hmid = (a * jax.nn.sigmoid(a)) * u
        _to_token_tiles(obuf.at[slot], jnp.dot(hmid.astype(BF16), wdb[...], preferred_element_type=F32))
        scatter_rows(dst_ref, slot, nv_cur)

        @pl.when(nv_next == 0)
        def _drain():
            wait_rows(obuf, slot, ssem, nv_cur)

            @pl.when(i >= 1)
            def _drain_prev():
                wait_rows(obuf, 1 - slot, ssem, nv_ref[jnp.maximum(i - 1, 0)])


def _moe_call(block_expert, n_real, src_tok, dst_row, hf, w_g, w_u, w_d):
    T = hf.shape[0] // ROW_TILE
    M = DISPATCH_BLOCK
    nblk = block_expert.shape[0]
    D, DE = w_g.shape[1], w_g.shape[2]
    assert D == ROW_TILE * LANES and hf.shape[1] == LANES and M % COPY_GROUP == 0
    idx_spec = lambda f: pl.BlockSpec((1, 1, M), f, memory_space=pltpu.SMEM)
    cur = lambda i, be, nu: (i, 0, 0)
    nxt = lambda i, be, nu: (jnp.minimum(i + 1, nblk - 1), 0, 0)
    wspec = lambda a, b: pl.BlockSpec((1, a, b), lambda i, be, nu: (be[i], 0, 0))
    grid_spec = pltpu.PrefetchScalarGridSpec(
        num_scalar_prefetch=2,
        grid=(nblk,),
        in_specs=[idx_spec(cur), idx_spec(nxt), idx_spec(cur),
                  pl.BlockSpec(memory_space=pl.ANY),
                  wspec(D, DE), wspec(D, DE), wspec(DE, D)],
        out_specs=pl.BlockSpec(memory_space=pl.ANY),
        scratch_shapes=[pltpu.VMEM((2, M * ROW_TILE, LANES), F32), pltpu.VMEM((2, M * ROW_TILE, LANES), F32),
                        pltpu.VMEM((D, DE), BF16), pltpu.VMEM((D, DE), BF16), pltpu.VMEM((DE, D), BF16),
                        pltpu.SemaphoreType.DMA((2,)), pltpu.SemaphoreType.DMA((2,))],
    )
    src3 = src_tok.reshape(nblk, 1, M)
    return pl.pallas_call(
        _moe_kernel,
        grid_spec=grid_spec,
        out_shape=jax.ShapeDtypeStruct(((EXPERT_TOPK * T + M) * ROW_TILE, LANES), F32),
        compiler_params=pltpu.CompilerParams(
            dimension_semantics=("arbitrary",), vmem_limit_bytes=VMEM_LIMIT),
        name="moe",
    )(block_expert, n_real, src3, src3, dst_row.reshape(nblk, 1, M), hf, w_g, w_u, w_d)


def _ple_kernel(x1_ref, m0_ref, m1_ref, wt_ref, p_ref, pg_ref, wgate_ref, wproj_ref, pog_ref, out_ref):
    proj = _rmsnorm(jnp.dot(p_ref[...].astype(BF16), wproj_ref[...], preferred_element_type=F32),
                    pog_ref[...])
    tm = x1_ref.shape[0]
    moe = (_from_token_tiles(m0_ref, tm) * wt_ref[:, 0:1] + _from_token_tiles(m1_ref, tm) * wt_ref[:, 1:2])
    x2 = x1_ref[...] + moe
    hn = _rmsnorm(x2, pg_ref[...]).astype(BF16)
    gate = jax.nn.sigmoid(jnp.dot(hn, wgate_ref[...], preferred_element_type=F32))
    out_ref[...] = x2 + gate * proj


def _ple_call(moe_rows, x1, wt_cols, p2d, pg, wgate, wproj, pog):
    T, D = x1.shape
    tm = PLE_TILE
    tok = lambda w: pl.BlockSpec((tm, w), lambda i: (i, 0))
    full = lambda a: pl.BlockSpec(a.shape, lambda i: (0, 0))
    first = pl.BlockSpec((tm * ROW_TILE, LANES), lambda i: (i, 0))
    second = pl.BlockSpec((tm * ROW_TILE, LANES), lambda i: (T // tm + i, 0))
    return pl.pallas_call(
        _ple_kernel,
        grid=(T // tm,),
        in_specs=[tok(D), first, second, tok(EXPERT_TOPK), tok(p2d.shape[1]),
                  full(pg), full(wgate), full(wproj), full(pog)],
        out_specs=tok(D),
        out_shape=jax.ShapeDtypeStruct((T, D), F32),
        compiler_params=pltpu.CompilerParams(
            dimension_semantics=("arbitrary",), vmem_limit_bytes=VMEM_LIMIT),
        name="ple",
    )(x1, moe_rows, moe_rows, wt_cols, p2d, pg, wgate, wproj, pog)


def _dispatch_plan(eid, T):
    A = EXPERT_TOPK * T
    M = DISPATCH_BLOCK
    nblk = A // M + N_EXPERTS
    assert A <= PAD_MARK
    e_flat = eid.reshape(A)
    experts = jnp.arange(N_EXPERTS, dtype=I32)
    counts = jnp.sum((e_flat[:, None] == experts[None, :]).astype(I32), axis=0)
    padded = (counts + M - 1) // M * M
    n_used = jnp.sum(padded) // M
    row = jnp.arange(M, dtype=I32)[None, :]
    pad_keys = jnp.where(row < (padded - counts)[:, None], 2 * experts[:, None] + 1, 2 * N_EXPERTS)
    keys = jnp.concatenate([2 * e_flat, pad_keys.reshape(-1)])
    vals = jnp.concatenate([jnp.arange(A, dtype=I32), jnp.full((N_EXPERTS * M,), PAD_MARK, I32)])
    packed = jnp.sort(keys * (PAD_MARK + 1) + vals).reshape(nblk, M)
    a_s = packed & PAD_MARK
    blk = jnp.arange(nblk, dtype=I32)
    be = jnp.minimum(packed[:, 0] // (2 * (PAD_MARK + 1)), N_EXPERTS - 1)
    be = jnp.where(blk < n_used, be, be[jnp.maximum(n_used - 1, 0)])
    valid = a_s != PAD_MARK
    src_tok = jnp.where(valid, jnp.where(a_s >= T, a_s - T, a_s), 0)
    dst_row = jnp.where(valid, a_s, A + row)
    return be, jnp.sum(valid.astype(I32), axis=1), src_tok, dst_row


def _layer(x2d, p2d, rel_bias, attn_norm, w_in, swa_q_norm, swa_k_norm, swa_sinks, moba_q_norm,
           moba_k_norm, swa_out_norm, moba_out_norm, w_out, ffn_norm, w_rg, b_rg, w_re, b_re,
           w_g, w_u, w_d, ple_norm, w_ple_gate, w_ple_proj, ple_out_norm, B, S):
    T, D = x2d.shape
    row = lambda v: v.reshape(1, -1).astype(F32)
    head_gain = jnp.concatenate([
        jnp.tile(swa_q_norm, SWA_Q_HEADS), jnp.tile(swa_k_norm, SWA_KV_HEADS),
        jnp.tile(moba_q_norm, MOBA_Q_HEADS), jnp.tile(moba_k_norm, MOBA_KV_HEADS)])
    w_in_b = w_in.astype(BF16)
    qa_w, kv_w = SWA_Q_HEADS * HEAD_DIM, SWA_KV_HEADS * HEAD_DIM
    qb_w = MOBA_Q_HEADS * HEAD_DIM
    c_va = qa_w + kv_w
    c_qb = c_va + kv_w
    c_vb = c_qb + qb_w + kv_w
    w_qk = jnp.concatenate([w_in_b[:, :c_va], w_in_b[:, c_qb:c_vb]], axis=1)
    w_vt = jnp.concatenate([w_in_b[:, c_va:c_qb], w_in_b[:, c_vb:]], axis=1).T
    qa, ka, qb, kb, vat, vbt = _qkv_call(x2d, row(attn_norm), w_qk, w_vt, row(head_gain))

    tbl_flat = rel_bias.astype(F32).reshape(-1)
    oa = _swa_call(tbl_flat, swa_sinks.astype(F32), qa, ka, vat, B, S)
    ob = _moba_call(tbl_flat, qb, kb, vbt, B, S)

    pad_rows = lambda n: jnp.zeros((n, D), F32)
    wr = jnp.concatenate([w_rg.T, pad_rows(EXPERT_ROW0 - N_GROUPS), w_re.T,
                          pad_rows(ROUTER_ROWS - EXPERT_ROW0 - N_EXPERTS)], axis=0)
    wr_hi = wr.astype(BF16)
    wr_lo = (wr - wr_hi.astype(F32)).astype(BF16)
    rb = jnp.concatenate([b_rg, jnp.zeros((EXPERT_ROW0 - N_GROUPS,), F32), b_re,
                          jnp.zeros((ROUTER_ROWS - EXPERT_ROW0 - N_EXPERTS,), F32)]).reshape(ROUTER_ROWS, 1)
    wo = w_out.astype(BF16)
    na_w = SWA_Q_HEADS * HEAD_DIM
    x1, hf, eid, wts = _outproj_call(oa, ob, x2d, row(swa_out_norm), row(moba_out_norm),
                                     wo[:na_w], wo[na_w:], row(ffn_norm), wr_hi, wr_lo, rb)

    be, n_real, src_tok, dst_row = _dispatch_plan(eid, T)
    moe_rows = _moe_call(be, n_real, src_tok, dst_row, hf, w_g, w_u, w_d)
    return _ple_call(moe_rows, x1, wts.T, p2d, row(ple_norm), w_ple_gate.astype(BF16),
                     w_ple_proj.astype(BF16), row(ple_out_norm))


def kernel(x, p, rel_bias, attn_norm, w_in, swa_q_norm, swa_k_norm, swa_sinks, moba_q_norm, moba_k_norm,
           swa_out_norm, moba_out_norm, w_out, ffn_norm, w_router_group, b_router_group, w_router_expert,
           b_router_expert, w_exp_gate, w_exp_up, w_exp_down, ple_norm, w_ple_gate, w_ple_proj, ple_out_norm):
    B, S, D = x.shape
    x2d = x.reshape(B * S, D)
    for i in range(p.shape[0]):
        x2d = _layer(x2d, p[i].reshape(B * S, -1), rel_bias, attn_norm[i], w_in[i], swa_q_norm[i],
                     swa_k_norm[i], swa_sinks[i], moba_q_norm[i], moba_k_norm[i], swa_out_norm[i],
                     moba_out_norm[i], w_out[i], ffn_norm[i], w_router_group[i], b_router_group[i],
                     w_router_expert[i], b_router_expert[i], w_exp_gate[i], w_exp_up[i], w_exp_down[i],
                     ple_norm[i], w_ple_gate[i], w_ple_proj[i], ple_out_norm[i], B, S)
    return x2d.reshape(B, S, D)
```

```python
import math

import numpy as np
import jax
import jax.numpy as jnp
from jax import lax
from jax.experimental import pallas as pl
from jax.experimental.pallas import tpu as pltpu

F32 = jnp.float32
BF16 = jnp.bfloat16
I32 = jnp.int32

HEAD_DIM = 64
SWA_Q_HEADS = 8
SWA_KV_HEADS = 2
SWA_WINDOW = 128
MOBA_Q_HEADS = 8
MOBA_KV_HEADS = 2
MOBA_BLOCK = 256
MOBA_TOPK = 3
N_HEADS = SWA_Q_HEADS + MOBA_Q_HEADS
REL_BUCKETS = 32
REL_MAX_DIST = 128
N_GROUPS = 4
EXPERTS_PER_GROUP = 8
N_EXPERTS = N_GROUPS * EXPERTS_PER_GROUP
EXPERT_TOPK = 2
DISPATCH_BLOCK = 256
PAD_MARK = 0xFFFF
EPS = 1e-6
NEG = -1e30
LOG2E = math.log2(math.e)

DMA_PRIORITIES = 2
LANES = 128
ROW_TILE = 8
BF16_SUBLANES = 16
GROUP_HEADS = 4
GROUP_W = GROUP_HEADS * HEAD_DIM
ROUTER_ROWS = 128
EXPERT_ROW0 = 8
TOKEN_TILE = 512
PLE_TILE = 512
VMEM_LIMIT = 52 * 1024 * 1024


def _dot_nt(a, b):
    return lax.dot_general(a, b, (((1,), (1,)), ((), ())), preferred_element_type=F32)


def _rmsnorm(x, g):
    ms = jnp.mean(x * x, axis=-1, keepdims=True)
    return x * lax.rsqrt(ms + EPS) * g


def _to_token_tiles(ref, x):
    m = x.shape[0]
    for j in range(x.shape[1] // LANES):
        ref[pl.ds(j, m, stride=ROW_TILE), :] = x[:, LANES * j:LANES * (j + 1)]


def _from_token_tiles(ref, m):
    return jnp.concatenate([ref[pl.ds(j, m, stride=ROW_TILE), :] for j in range(ref.shape[0] // m)], axis=1)


def _rel_bucket_np(dist):
    n = np.maximum(dist, 0)
    exact = REL_BUCKETS // 2
    nf = np.maximum(n, 1).astype(np.float32)
    large = exact + (np.log(nf / exact) / math.log(REL_MAX_DIST / exact)
                     * (REL_BUCKETS - exact)).astype(np.int32)
    return np.where(n < exact, n, np.minimum(large, REL_BUCKETS - 1)).astype(np.int32)


def _band_buckets(block):
    qi = np.arange(block)[:, None]
    kj = np.arange(2 * block)[None, :]
    return _rel_bucket_np(qi + block - kj)


def _bias_from_buckets(bkt, tbl_ref, head):
    acc = jnp.zeros(bkt.shape, F32)
    for j in range(REL_BUCKETS):
        acc = jnp.where(bkt == j, tbl_ref[j * N_HEADS + head] * LOG2E, acc)
    return acc


def _qkv_kernel(x_ref, g_ref, w_ref, wvt_ref, hg_ref, qa_ref, ka_ref, qb_ref, kb_ref, vat_ref, vbt_ref):
    h = _rmsnorm(x_ref[...], g_ref[...]).astype(BF16)
    acc = jnp.dot(h, w_ref[...], preferred_element_type=F32)
    lo = lax.broadcasted_iota(I32, (1, LANES), 1) < HEAD_DIM

    def head_normed(c, scale):
        blk = acc[:, LANES * c:LANES * (c + 1)]
        sq = blk * blk
        s_lo = jnp.sum(jnp.where(lo, sq, 0.0), axis=-1, keepdims=True)
        s_hi = jnp.sum(jnp.where(lo, 0.0, sq), axis=-1, keepdims=True)
        inv = jnp.where(lo, lax.rsqrt(s_lo / HEAD_DIM + EPS), lax.rsqrt(s_hi / HEAD_DIM + EPS))
        return blk * inv * hg_ref[:, LANES * c:LANES * (c + 1)] * scale

    scale = HEAD_DIM ** -0.5 * LOG2E
    for c in range(4):
        qa_ref[:, LANES * c:LANES * (c + 1)] = head_normed(c, scale).astype(BF16)
        qb_ref[:, LANES * c:LANES * (c + 1)] = head_normed(5 + c, scale).astype(BF16)
    ka_ref[...] = head_normed(4, 1.0).astype(BF16)
    kb_ref[...] = head_normed(9, 1.0).astype(BF16)
    vt = _dot_nt(wvt_ref[...], h).astype(BF16)
    for j in range(vat_ref.shape[0]):
        vat_ref[j] = vt[:LANES, SWA_WINDOW * j:SWA_WINDOW * (j + 1)]
    for j in range(vbt_ref.shape[0]):
        vbt_ref[j] = vt[LANES:, MOBA_BLOCK * j:MOBA_BLOCK * (j + 1)]


def _qkv_call(x2d, attn_g, w_qk, w_vt, head_gain):
    T, D = x2d.shape
    tm = TOKEN_TILE
    tok = lambda w: pl.BlockSpec((tm, w), lambda i: (i, 0))
    full = lambda a: pl.BlockSpec(a.shape, lambda i: (0, 0))
    out_w = (SWA_Q_HEADS * HEAD_DIM, LANES, MOBA_Q_HEADS * HEAD_DIM, LANES)
    slabs = lambda blk: pl.BlockSpec((tm // blk, LANES, blk), lambda i: (i, 0, 0))
    slab_shape = lambda blk: jax.ShapeDtypeStruct((T // blk, LANES, blk), BF16)
    return pl.pallas_call(
        _qkv_kernel,
        grid=(T // tm,),
        in_specs=[tok(D), full(attn_g), full(w_qk), full(w_vt), full(head_gain)],
        out_specs=[tok(w) for w in out_w] + [slabs(SWA_WINDOW), slabs(MOBA_BLOCK)],
        out_shape=[jax.ShapeDtypeStruct((T, w), BF16) for w in out_w]
        + [slab_shape(SWA_WINDOW), slab_shape(MOBA_BLOCK)],
        compiler_params=pltpu.CompilerParams(
            dimension_semantics=("arbitrary",), vmem_limit_bytes=VMEM_LIMIT),
        name="qkv",
    )(x2d, attn_g, w_qk, w_vt, head_gain)


def _padded_heads(qblk, heads):
    zeros = jnp.zeros((qblk.shape[0], HEAD_DIM), qblk.dtype)
    pieces = []
    for h in range(heads):
        piece = qblk[:, HEAD_DIM * h:HEAD_DIM * (h + 1)]
        pieces.append(jnp.concatenate([piece, zeros] if h < GROUP_HEADS else [zeros, piece], axis=1))
    return jnp.concatenate(pieces, axis=0)


def _swa_kernel(tbl_ref, sink_ref, bkt_ref, q_ref, k_ref, vt_ref, o_ref, bias_scr):
    W = SWA_WINDOW
    H = SWA_Q_HEADS
    S = q_ref.shape[0]

    @pl.when(pl.program_id(0) == 0)
    def _init_bias():
        bkt = bkt_ref[...]
        for h in range(H):
            bias_scr[:, W * h:W * (h + 1)] = _bias_from_buckets(bkt, tbl_ref, h)

    key_i = lax.broadcasted_iota(I32, (2 * W, H * W), 0)
    qry_i = lax.broadcasted_iota(I32, (2 * W, H * W), 1) & (W - 1)
    dist = qry_i + W - key_i
    band = (dist >= 0) & (dist < W)
    own_part = key_i >= W
    head_of_lane = lax.broadcasted_iota(I32, (1, H * W), 1) // W
    sinks = jnp.zeros((1, H * W), F32)
    for h in range(H):
        sinks = jnp.where(head_of_lane == h, sink_ref[h] * LOG2E, sinks)

    def body(i, carry):
        r0 = pl.multiple_of(i * W, W)
        prev = jnp.maximum(i - 1, 0)
        p0 = pl.multiple_of(prev * W, W)
        q8 = _padded_heads(q_ref[pl.ds(r0, W), :], H)
        kband = jnp.concatenate([k_ref[pl.ds(p0, W), :], k_ref[pl.ds(r0, W), :]], axis=0)
        mask = band & (own_part | (i > 0))
        s = jnp.where(mask, _dot_nt(kband, q8) + bias_scr[...], NEG)
        m = jnp.maximum(jnp.max(s, axis=0, keepdims=True), sinks)
        e = jnp.exp2(s - m)
        den = jnp.sum(e, axis=0, keepdims=True) + jnp.exp2(sinks - m)
        vtband = jnp.concatenate([vt_ref[prev], vt_ref[i]], axis=1)
        ot = jnp.dot(vtband, e.astype(BF16), preferred_element_type=F32) / den
        for pr in range(H // 2):
            f0 = HEAD_DIM * (2 * pr // GROUP_HEADS)
            pair = ot[f0:f0 + HEAD_DIM, 2 * W * pr:2 * W * (pr + 1)]
            o_ref[pl.ds(r0, W), LANES * pr:LANES * (pr + 1)] = jnp.concatenate(
                [pair[:, :W], pair[:, W:]], axis=0).T
        return carry

    lax.fori_loop(0, S // W, body, 0)


def _swa_call(tbl_flat, sinks, qa, ka, vat, B, S):
    W = SWA_WINDOW
    bkt = jnp.asarray(np.ascontiguousarray(_band_buckets(W).T))
    smem = pl.BlockSpec(memory_space=pltpu.SMEM)
    return pl.pallas_call(
        _swa_kernel,
        grid=(B,),
        in_specs=[smem, smem,
                  pl.BlockSpec(bkt.shape, lambda b: (0, 0)),
                  pl.BlockSpec((S, qa.shape[1]), lambda b: (b, 0)),
                  pl.BlockSpec((S, LANES), lambda b: (b, 0)),
                  pl.BlockSpec((S // W, LANES, W), lambda b: (b, 0, 0))],
        out_specs=pl.BlockSpec((S, qa.shape[1]), lambda b: (b, 0)),
        out_shape=jax.ShapeDtypeStruct(qa.shape, F32),
        scratch_shapes=[pltpu.VMEM((2 * W, SWA_Q_HEADS * W), F32)],
        compiler_params=pltpu.CompilerParams(
            dimension_semantics=("arbitrary",), vmem_limit_bytes=VMEM_LIMIT),
        name="swa",
    )(tbl_flat, sinks, bkt, qa, ka, vat)


def _moba_kernel(tbl_ref, bkt_ref, q_ref, k_ref, vt_ref, o_ref,
                 bias_scr, km_scr, q8_scr, sel_scr, m_scr, l_scr, acc_scr):
    BS = MOBA_BLOCK
    H = MOBA_Q_HEADS
    S = q_ref.shape[0]
    NBK = S // BS
    NQ = H * BS
    GQ = GROUP_HEADS * BS

    @pl.when(pl.program_id(0) == 0)
    def _init_bias():
        bkt = bkt_ref[...]
        for h in range(H):
            band = _bias_from_buckets(bkt, tbl_ref, SWA_Q_HEADS + h)
            far = tbl_ref[(REL_BUCKETS - 1) * N_HEADS + SWA_Q_HEADS + h] * LOG2E
            bias_scr[0, :, BS * h:BS * (h + 1)] = jnp.full((BS, BS), far, F32)
            bias_scr[1, :, BS * h:BS * (h + 1)] = band[:BS]
            bias_scr[2, :, BS * h:BS * (h + 1)] = band[BS:]

    blk_row = lax.broadcasted_iota(I32, (BF16_SUBLANES, 1), 0)
    kmean = jnp.sum(k_ref[...].astype(F32).reshape(NBK, BS, LANES), axis=1) / BS
    kmean = jnp.concatenate([kmean, jnp.zeros((BF16_SUBLANES - NBK, LANES), F32)], axis=0)
    km_hi = kmean.astype(BF16)
    km_scr[0] = km_hi
    km_scr[1] = (kmean - km_hi.astype(F32)).astype(BF16)

    def scores(n, case):
        k0 = pl.multiple_of(n * BS, BS)
        return _dot_nt(k_ref[pl.ds(k0, BS), :], q8_scr[...]) + bias_scr[case]

    def weighted_values(n, e):
        pv = jnp.dot(vt_ref[n], e.astype(BF16), preferred_element_type=F32)
        return jnp.concatenate([pv[:HEAD_DIM, :GQ], pv[HEAD_DIM:, GQ:]], axis=1)

    def qblock(mi, carry):
        r0 = pl.multiple_of(mi * BS, BS)
        q8 = _padded_heads(q_ref[pl.ds(r0, BS), :], H)
        q8_scr[...] = q8
        gate = _dot_nt(km_scr[0], q8) + _dot_nt(km_scr[1], q8)
        past = blk_row < mi
        gm = jnp.where(past, gate, NEG)
        cnt = jnp.zeros(gm.shape, F32)
        for n in range(NBK - 1):
            col = gm[n:n + 1, :]
            beats = (gm > col) | ((gm == col) & (blk_row < n))
            c = jnp.sum(jnp.where(beats, 1.0, 0.0), axis=0, keepdims=True)
            cnt = jnp.where(blk_row == n, c, cnt)
        sel_scr[...] = jnp.where(past & (cnt < MOBA_TOPK), 1.0, 0.0)

        key_i = lax.broadcasted_iota(I32, (BS, NQ), 0)
        qry_i = lax.broadcasted_iota(I32, (BS, NQ), 1) & (BS - 1)
        s = jnp.where(key_i <= qry_i, scores(mi, 2), NEG)
        m0 = jnp.max(s, axis=0, keepdims=True)
        e = jnp.exp2(s - m0)
        m_scr[...] = m0
        l_scr[...] = jnp.sum(e, axis=0, keepdims=True)
        acc_scr[...] = weighted_values(mi, e)

        def kvblock(j, c2):
            n = mi - j
            s = jnp.where(sel_scr[pl.ds(n, 1), :] > 0.5, scores(n, jnp.where(j == 1, 1, 0)), NEG)
            m_prev = m_scr[...]
            m_new = jnp.maximum(m_prev, jnp.max(s, axis=0, keepdims=True))
            alpha = jnp.exp2(m_prev - m_new)
            e = jnp.exp2(s - m_new)
            l_scr[...] = alpha * l_scr[...] + jnp.sum(e, axis=0, keepdims=True)
            acc_scr[...] = alpha * acc_scr[...] + weighted_values(n, e)
            m_scr[...] = m_new
            return c2

        lax.fori_loop(1, mi + 1, kvblock, 0)
        ot = acc_scr[...] / l_scr[...]
        for pr in range(H // 2):
            pair = ot[:, 2 * BS * pr:2 * BS * (pr + 1)]
            o_ref[pl.ds(r0, BS), LANES * pr:LANES * (pr + 1)] = jnp.concatenate(
                [pair[:, :BS], pair[:, BS:]], axis=0).T
        return carry

    lax.fori_loop(0, NBK, qblock, 0)


def _moba_call(tbl_flat, qb, kb, vbt, B, S):
    BS = MOBA_BLOCK
    NQ = MOBA_Q_HEADS * BS
    NBK = S // BS
    bkt = jnp.asarray(np.ascontiguousarray(_band_buckets(BS).T))
    smem = pl.BlockSpec(memory_space=pltpu.SMEM)
    return pl.pallas_call(
        _moba_kernel,
        grid=(B,),
        in_specs=[smem,
                  pl.BlockSpec(bkt.shape, lambda b: (0, 0)),
                  pl.BlockSpec((S, qb.shape[1]), lambda b: (b, 0)),
                  pl.BlockSpec((S, LANES), lambda b: (b, 0)),
                  pl.BlockSpec((NBK, LANES, BS), lambda b: (b, 0, 0))],
        out_specs=pl.BlockSpec((S, qb.shape[1]), lambda b: (b, 0)),
        out_shape=jax.ShapeDtypeStruct(qb.shape, F32),
        scratch_shapes=[pltpu.VMEM((3, BS, NQ), F32),
                        pltpu.VMEM((2, BF16_SUBLANES, LANES), BF16),
                        pltpu.VMEM((NQ, LANES), BF16),
                        pltpu.VMEM((BF16_SUBLANES, NQ), F32),
                        pltpu.VMEM((1, NQ), F32),
                        pltpu.VMEM((1, NQ), F32),
                        pltpu.VMEM((HEAD_DIM, NQ), F32)],
        compiler_params=pltpu.CompilerParams(
            dimension_semantics=("arbitrary",), vmem_limit_bytes=VMEM_LIMIT),
        name="moba",
    )(tbl_flat, bkt, qb, kb, vbt)


def _outproj_kernel(oa_ref, ob_ref, x_ref, ga_ref, gb_ref, woa_ref, wob_ref, fg_ref,
                    wrh_ref, wrl_ref, rb_ref, x1_ref, hf_ref, eid_ref, wt_ref):
    na = _rmsnorm(oa_ref[...], ga_ref[...]).astype(BF16)
    nb = _rmsnorm(ob_ref[...], gb_ref[...]).astype(BF16)
    y = (jnp.dot(na, woa_ref[...], preferred_element_type=F32)
         + jnp.dot(nb, wob_ref[...], preferred_element_type=F32))
    x1 = x_ref[...] + y
    x1_ref[...] = x1
    hf = _rmsnorm(x1, fg_ref[...])
    _to_token_tiles(hf_ref, hf)

    hf_hi = hf.astype(BF16)
    hf_lo = (hf - hf_hi.astype(F32)).astype(BF16)
    lt = (_dot_nt(wrh_ref[...], hf_hi) + _dot_nt(wrh_ref[...], hf_lo)
          + _dot_nt(wrl_ref[...], hf_hi) + rb_ref[...])
    gl = [lt[j:j + 1, :] for j in range(N_GROUPS)]
    best = gl[0]
    gsel = jnp.zeros(best.shape, I32)
    for j in range(1, N_GROUPS):
        better = gl[j] > best
        gsel = jnp.where(better, j, gsel)
        best = jnp.where(better, gl[j], best)
    gsum = jnp.zeros(best.shape, F32)
    for j in range(N_GROUPS):
        gsum = gsum + jnp.exp(gl[j] - best)
    gw = 1.0 / gsum
    E = EXPERTS_PER_GROUP
    es = lt[EXPERT_ROW0:EXPERT_ROW0 + E, :]
    for j in range(1, N_GROUPS):
        es = jnp.where(gsel == j, lt[EXPERT_ROW0 + E * j:EXPERT_ROW0 + E * (j + 1), :], es)
    row = lax.broadcasted_iota(I32, es.shape, 0)
    v1 = jnp.max(es, axis=0, keepdims=True)
    i1 = jnp.min(jnp.where(es == v1, row, E), axis=0, keepdims=True)
    es2 = jnp.where(row == i1, -jnp.inf, es)
    v2 = jnp.max(es2, axis=0, keepdims=True)
    i2 = jnp.min(jnp.where(es2 == v2, row, E), axis=0, keepdims=True)
    e2 = jnp.exp(v2 - v1)
    den = 1.0 + e2
    eid_ref[...] = jnp.concatenate([gsel * E + i1, gsel * E + i2], axis=0)
    wt_ref[...] = jnp.concatenate([gw * (1.0 / den), gw * (e2 / den)], axis=0)


def _outproj_call(oa, ob, x2d, ga, gb, woa, wob, fg, wrh, wrl, rb):
    T, D = x2d.shape
    tm = TOKEN_TILE
    tok = lambda w: pl.BlockSpec((tm, w), lambda i: (i, 0))
    full = lambda a: pl.BlockSpec(a.shape, lambda i: (0, 0))
    col = pl.BlockSpec((EXPERT_TOPK, tm), lambda i: (0, i))
    return pl.pallas_call(
        _outproj_kernel,
        grid=(T // tm,),
        in_specs=[tok(oa.shape[1]), tok(ob.shape[1]), tok(D), full(ga), full(gb), full(woa), full(wob),
                  full(fg), full(wrh), full(wrl), full(rb)],
        out_specs=[tok(D), pl.BlockSpec((tm * ROW_TILE, LANES), lambda i: (i, 0)), col, col],
        out_shape=[jax.ShapeDtypeStruct((T, D), F32), jax.ShapeDtypeStruct((T * ROW_TILE, LANES), F32),
                   jax.ShapeDtypeStruct((EXPERT_TOPK, T), I32), jax.ShapeDtypeStruct((EXPERT_TOPK, T), F32)],
        compiler_params=pltpu.CompilerParams(
            dimension_semantics=("arbitrary",), vmem_limit_bytes=VMEM_LIMIT),
        name="outproj",
    )(oa, ob, x2d, ga, gb, woa, wob, fg, wrh, wrl, rb)


COPY_GROUP = 32


def _moe_kernel(be_ref, nv_ref, src_ref, srcn_ref, dst_ref, hf_hbm, wg_ref, wu_ref, wd_ref, out_hbm,
                xbuf, obuf, wgb, wub, wdb, gsem, ssem):
    i = pl.program_id(0)
    nblk = pl.num_programs(0)
    M = xbuf.shape[1] // ROW_TILE
    slot = i % 2
    nv_cur = nv_ref[i]
    nv_next = jnp.where(i + 1 < nblk, nv_ref[jnp.minimum(i + 1, nblk - 1)], 0)

    all_groups = range(M // COPY_GROUP)

    def for_real_groups(n_real, fn, groups=all_groups):
        for k in groups:
            @pl.when(k * COPY_GROUP < n_real)
            def _group():
                fn(range(k * COPY_GROUP, (k + 1) * COPY_GROUP))

    def gather_rows(idx_ref, buf_slot, n_real, groups=all_groups):
        def start(rows):
            for r in rows:
                tok_row = pl.multiple_of(idx_ref[0, 0, r] * ROW_TILE, ROW_TILE)
                pltpu.make_async_copy(hf_hbm.at[pl.ds(tok_row, ROW_TILE)],
                                      xbuf.at[buf_slot, pl.ds(r * ROW_TILE, ROW_TILE)],
                                      gsem.at[buf_slot]).start(priority=r % DMA_PRIORITIES)
        for_real_groups(n_real, start, groups)

    def scatter_rows(idx_ref, buf_slot, n_real):
        def start(rows):
            for r in rows:
                dst_row = pl.multiple_of(idx_ref[0, 0, r] * ROW_TILE, ROW_TILE)
                pltpu.make_async_copy(obuf.at[buf_slot, pl.ds(r * ROW_TILE, ROW_TILE)],
                                      out_hbm.at[pl.ds(dst_row, ROW_TILE)],
                                      ssem.at[buf_slot]).start(priority=r % DMA_PRIORITIES)
        for_real_groups(n_real, start)

    def wait_rows(buf, buf_slot, sem, n_real):
        span = COPY_GROUP * ROW_TILE
        for_real_groups(n_real, lambda rows: pltpu.make_async_copy(
            hf_hbm.at[pl.ds(0, span)], buf.at[buf_slot, pl.ds(0, span)], sem.at[buf_slot]).wait())

    @pl.when(nv_cur > 0)
    def _compute():
        @pl.when(i == 0)
        def _first_gather():
            gather_rows(src_ref, 0, nv_cur)
            obuf[1] = jnp.zeros(obuf.shape[1:], F32)
            spare = pltpu.make_async_copy(
                obuf.at[1], out_hbm.at[pl.ds(out_hbm.shape[0] - M * ROW_TILE, M * ROW_TILE)], ssem.at[1])
            spare.start()
            spare.wait()

        @pl.when((i == 0) | (be_ref[i] != be_ref[jnp.maximum(i - 1, 0)]))
        def _cast_weights():
            wgb[...] = wg_ref[0].astype(BF16)
            wub[...] = wu_ref[0].astype(BF16)
            wdb[...] = wd_ref[0].astype(BF16)

        wait_rows(xbuf, slot, gsem, nv_cur)

        @pl.when(i >= 2)
        def _reuse_obuf():
            wait_rows(obuf, slot, ssem, nv_ref[jnp.maximum(i - 2, 0)])

        half = len(all_groups) // 2
        xb = _from_token_tiles(xbuf.at[slot], M).astype(BF16)
        a = jnp.dot(xb, wgb[...], preferred_element_type=F32)
        gather_rows(srcn_ref, 1 - slot, nv_next, all_groups[:half])
        u = jnp.dot(xb, wub[...], preferred_element_type=F32)
        gather_rows(srcn_ref, 1 - slot, nv_next, all_groups[half:])
        hmid = (a * jax.nn.sigmoid(a)) * u
        _to_token_tiles(obuf.at[slot], jnp.dot(hmid.astype(BF16), wdb[...], preferred_element_type=F32))
        scatter_rows(dst_ref, slot, nv_cur)

        @pl.when(nv_next == 0)
        def _drain():
            wait_rows(obuf, slot, ssem, nv_cur)

            @pl.when(i >= 1)
            def _drain_prev():
                wait_rows(obuf, 1 - slot, ssem, nv_ref[jnp.maximum(i - 1, 0)])


def _moe_call(block_expert, n_real, src_tok, dst_row, hf, w_g, w_u, w_d):
    T = hf.shape[0] // ROW_TILE
    M = DISPATCH_BLOCK
    nblk = block_expert.shape[0]
    D, DE = w_g.shape[1], w_g.shape[2]
    assert D == ROW_TILE * LANES and hf.shape[1] == LANES and M % COPY_GROUP == 0
    idx_spec = lambda f: pl.BlockSpec((1, 1, M), f, memory_space=pltpu.SMEM)
    cur = lambda i, be, nu: (i, 0, 0)
    nxt = lambda i, be, nu: (jnp.minimum(i + 1, nblk - 1), 0, 0)
    wspec = lambda a, b: pl.BlockSpec((1, a, b), lambda i, be, nu: (be[i], 0, 0))
    grid_spec = pltpu.PrefetchScalarGridSpec(
        num_scalar_prefetch=2,
        grid=(nblk,),
        in_specs=[idx_spec(cur), idx_spec(nxt), idx_spec(cur),
                  pl.BlockSpec(memory_space=pl.ANY),
                  wspec(D, DE), wspec(D, DE), wspec(DE, D)],
        out_specs=pl.BlockSpec(memory_space=pl.ANY),
        scratch_shapes=[pltpu.VMEM((2, M * ROW_TILE, LANES), F32), pltpu.VMEM((2, M * ROW_TILE, LANES), F32),
                        pltpu.VMEM((D, DE), BF16), pltpu.VMEM((D, DE), BF16), pltpu.VMEM((DE, D), BF16),
                        pltpu.SemaphoreType.DMA((2,)), pltpu.SemaphoreType.DMA((2,))],
    )
    src3 = src_tok.reshape(nblk, 1, M)
    return pl.pallas_call(
        _moe_kernel,
        grid_spec=grid_spec,
        out_shape=jax.ShapeDtypeStruct(((EXPERT_TOPK * T + M) * ROW_TILE, LANES), F32),
        compiler_params=pltpu.CompilerParams(
            dimension_semantics=("arbitrary",), vmem_limit_bytes=VMEM_LIMIT),
        name="moe",
    )(block_expert, n_real, src3, src3, dst_row.reshape(nblk, 1, M), hf, w_g, w_u, w_d)


def _ple_kernel(x1_ref, m0_ref, m1_ref, wt_ref, p_ref, pg_ref, wgate_ref, wproj_ref, pog_ref, out_ref):
    proj = _rmsnorm(jnp.dot(p_ref[...].astype(BF16), wproj_ref[...], preferred_element_type=F32),
                    pog_ref[...])
    tm = x1_ref.shape[0]
    moe = (_from_token_tiles(m0_ref, tm) * wt_ref[:, 0:1] + _from_token_tiles(m1_ref, tm) * wt_ref[:, 1:2])
    x2 = x1_ref[...] + moe
    hn = _rmsnorm(x2, pg_ref[...]).astype(BF16)
    gate = jax.nn.sigmoid(jnp.dot(hn, wgate_ref[...], preferred_element_type=F32))
    out_ref[...] = x2 + gate * proj


def _ple_call(moe_rows, x1, wt_cols, p2d, pg, wgate, wproj, pog):
    T, D = x1.shape
    tm = PLE_TILE
    tok = lambda w: pl.BlockSpec((tm, w), lambda i: (i, 0))
    full = lambda a: pl.BlockSpec(a.shape, lambda i: (0, 0))
    first = pl.BlockSpec((tm * ROW_TILE, LANES), lambda i: (i, 0))
    second = pl.BlockSpec((tm * ROW_TILE, LANES), lambda i: (T // tm + i, 0))
    return pl.pallas_call(
        _ple_kernel,
        grid=(T // tm,),
        in_specs=[tok(D), first, second, tok(EXPERT_TOPK), tok(p2d.shape[1]),
                  full(pg), full(wgate), full(wproj), full(pog)],
        out_specs=tok(D),
        out_shape=jax.ShapeDtypeStruct((T, D), F32),
        compiler_params=pltpu.CompilerParams(
            dimension_semantics=("arbitrary",), vmem_limit_bytes=VMEM_LIMIT),
        name="ple",
    )(x1, moe_rows, moe_rows, wt_cols, p2d, pg, wgate, wproj, pog)


def _dispatch_plan(eid, T):
    A = EXPERT_TOPK * T
    M = DISPATCH_BLOCK
    nblk = A // M + N_EXPERTS
    assert A <= PAD_MARK
    e_flat = eid.reshape(A)
    experts = jnp.arange(N_EXPERTS, dtype=I32)
    counts = jnp.sum((e_flat[:, None] == experts[None, :]).astype(I32), axis=0)
    padded = (counts + M - 1) // M * M
    n_used = jnp.sum(padded) // M
    row = jnp.arange(M, dtype=I32)[None, :]
    pad_keys = jnp.where(row < (padded - counts)[:, None], 2 * experts[:, None] + 1, 2 * N_EXPERTS)
    keys = jnp.concatenate([2 * e_flat, pad_keys.reshape(-1)])
    vals = jnp.concatenate([jnp.arange(A, dtype=I32), jnp.full((N_EXPERTS * M,), PAD_MARK, I32)])
    packed = jnp.sort(keys * (PAD_MARK + 1) + vals).reshape(nblk, M)
    a_s = packed & PAD_MARK
    blk = jnp.arange(nblk, dtype=I32)
    be = jnp.minimum(packed[:, 0] // (2 * (PAD_MARK + 1)), N_EXPERTS - 1)
    be = jnp.where(blk < n_used, be, be[jnp.maximum(n_used - 1, 0)])
    valid = a_s != PAD_MARK
    src_tok = jnp.where(valid, jnp.where(a_s >= T, a_s - T, a_s), 0)
    dst_row = jnp.where(valid, a_s, A + row)
    return be, jnp.sum(valid.astype(I32), axis=1), src_tok, dst_row


def _layer(x2d, p2d, rel_bias, attn_norm, w_in, swa_q_norm, swa_k_norm, swa_sinks, moba_q_norm,
           moba_k_norm, swa_out_norm, moba_out_norm, w_out, ffn_norm, w_rg, b_rg, w_re, b_re,
           w_g, w_u, w_d, ple_norm, w_ple_gate, w_ple_proj, ple_out_norm, B, S):
    T, D = x2d.shape
    row = lambda v: v.reshape(1, -1).astype(F32)
    head_gain = jnp.concatenate([
        jnp.tile(swa_q_norm, SWA_Q_HEADS), jnp.tile(swa_k_norm, SWA_KV_HEADS),
        jnp.tile(moba_q_norm, MOBA_Q_HEADS), jnp.tile(moba_k_norm, MOBA_KV_HEADS)])
    w_in_b = w_in.astype(BF16)
    qa_w, kv_w = SWA_Q_HEADS * HEAD_DIM, SWA_KV_HEADS * HEAD_DIM
    qb_w = MOBA_Q_HEADS * HEAD_DIM
    c_va = qa_w + kv_w
    c_qb = c_va + kv_w
    c_vb = c_qb + qb_w + kv_w
    w_qk = jnp.concatenate([w_in_b[:, :c_va], w_in_b[:, c_qb:c_vb]], axis=1)
    w_vt = jnp.concatenate([w_in_b[:, c_va:c_qb], w_in_b[:, c_vb:]], axis=1).T
    qa, ka, qb, kb, vat, vbt = _qkv_call(x2d, row(attn_norm), w_qk, w_vt, row(head_gain))

    tbl_flat = rel_bias.astype(F32).reshape(-1)
    oa = _swa_call(tbl_flat, swa_sinks.astype(F32), qa, ka, vat, B, S)
    ob = _moba_call(tbl_flat, qb, kb, vbt, B, S)

    pad_rows = lambda n: jnp.zeros((n, D), F32)
    wr = jnp.concatenate([w_rg.T, pad_rows(EXPERT_ROW0 - N_GROUPS), w_re.T,
                          pad_rows(ROUTER_ROWS - EXPERT_ROW0 - N_EXPERTS)], axis=0)
    wr_hi = wr.astype(BF16)
    wr_lo = (wr - wr_hi.astype(F32)).astype(BF16)
    rb = jnp.concatenate([b_rg, jnp.zeros((EXPERT_ROW0 - N_GROUPS,), F32), b_re,
                          jnp.zeros((ROUTER_ROWS - EXPERT_ROW0 - N_EXPERTS,), F32)]).reshape(ROUTER_ROWS, 1)
    wo = w_out.astype(BF16)
    na_w = SWA_Q_HEADS * HEAD_DIM
    x1, hf, eid, wts = _outproj_call(oa, ob, x2d, row(swa_out_norm), row(moba_out_norm),
                                     wo[:na_w], wo[na_w:], row(ffn_norm), wr_hi, wr_lo, rb)

    be, n_real, src_tok, dst_row = _dispatch_plan(eid, T)
    moe_rows = _moe_call(be, n_real, src_tok, dst_row, hf, w_g, w_u, w_d)
    return _ple_call(moe_rows, x1, wts.T, p2d, row(ple_norm), w_ple_gate.astype(BF16),
                     w_ple_proj.astype(BF16), row(ple_out_norm))


def kernel(x, p, rel_bias, attn_norm, w_in, swa_q_norm, swa_k_norm, swa_sinks, moba_q_norm, moba_k_norm,
           swa_out_norm, moba_out_norm, w_out, ffn_norm, w_router_group, b_router_group, w_router_expert,
           b_router_expert, w_exp_gate, w_exp_up, w_exp_down, ple_norm, w_ple_gate, w_ple_proj, ple_out_norm):
    B, S, D = x.shape
    x2d = x.reshape(B * S, D)
    for i in range(p.shape[0]):
        x2d = _layer(x2d, p[i].reshape(B * S, -1), rel_bias, attn_norm[i], w_in[i], swa_q_norm[i],
                     swa_k_norm[i], swa_sinks[i], moba_q_norm[i], moba_k_norm[i], swa_out_norm[i],
                     moba_out_norm[i], w_out[i], ffn_norm[i], w_router_group[i], b_router_group[i],
                     w_router_expert[i], b_router_expert[i], w_exp_gate[i], w_exp_up[i], w_exp_down[i],
                     ple_norm[i], w_ple_gate[i], w_ple_proj[i], ple_out_norm[i], B, S)
    return x2d.reshape(B, S, D)
```

```python
import math

import numpy as np
import jax
import jax.numpy as jnp
from jax import lax
from jax.experimental import pallas as pl
from jax.experimental.pallas import tpu as pltpu

F32 = jnp.float32
BF16 = jnp.bfloat16
I32 = jnp.int32

HEAD_DIM = 64
SWA_Q_HEADS = 8
SWA_KV_HEADS = 2
SWA_WINDOW = 128
MOBA_Q_HEADS = 8
MOBA_KV_HEADS = 2
MOBA_BLOCK = 256
MOBA_TOPK = 3
N_HEADS = SWA_Q_HEADS + MOBA_Q_HEADS
REL_BUCKETS = 32
REL_MAX_DIST = 128
N_GROUPS = 4
EXPERTS_PER_GROUP = 8
N_EXPERTS = N_GROUPS * EXPERTS_PER_GROUP
EXPERT_TOPK = 2
DISPATCH_BLOCK = 512
PAD_MARK = 0xFFFF
EPS = 1e-6
NEG = -1e30
LOG2E = math.log2(math.e)

DMA_PRIORITIES = 2
LANES = 128
ROW_TILE = 8
BF16_SUBLANES = 16
GROUP_HEADS = 4
GROUP_W = GROUP_HEADS * HEAD_DIM
ROUTER_ROWS = 128
EXPERT_ROW0 = 8
TOKEN_TILE = 512
PLE_TILE = 512
VMEM_LIMIT = 52 * 1024 * 1024


def _dot_nt(a, b):
    return lax.dot_general(a, b, (((1,), (1,)), ((), ())), preferred_element_type=F32)


def _rmsnorm(x, g):
    ms = jnp.mean(x * x, axis=-1, keepdims=True)
    return x * lax.rsqrt(ms + EPS) * g


def _to_token_tiles(ref, x):
    m = x.shape[0]
    for j in range(x.shape[1] // LANES):
        ref[pl.ds(j, m, stride=ROW_TILE), :] = x[:, LANES * j:LANES * (j + 1)]


def _from_token_tiles(ref, m):
    return jnp.concatenate([ref[pl.ds(j, m, stride=ROW_TILE), :] for j in range(ref.shape[0] // m)], axis=1)


def _rel_bucket_np(dist):
    n = np.maximum(dist, 0)
    exact = REL_BUCKETS // 2
    nf = np.maximum(n, 1).astype(np.float32)
    large = exact + (np.log(nf / exact) / math.log(REL_MAX_DIST / exact)
                     * (REL_BUCKETS - exact)).astype(np.int32)
    return np.where(n < exact, n, np.minimum(large, REL_BUCKETS - 1)).astype(np.int32)


def _band_buckets(block):
    qi = np.arange(block)[:, None]
    kj = np.arange(2 * block)[None, :]
    return _rel_bucket_np(qi + block - kj)


def _bias_from_buckets(bkt, tbl_ref, head):
    acc = jnp.zeros(bkt.shape, F32)
    for j in range(REL_BUCKETS):
        acc = jnp.where(bkt == j, tbl_ref[j * N_HEADS + head] * LOG2E, acc)
    return acc


def _qkv_kernel(x_ref, g_ref, w_ref, wvt_ref, hg_ref, qa_ref, ka_ref, qb_ref, kb_ref, vat_ref, vbt_ref):
    h = _rmsnorm(x_ref[...], g_ref[...]).astype(BF16)
    acc = jnp.dot(h, w_ref[...], preferred_element_type=F32)
    lo = lax.broadcasted_iota(I32, (1, LANES), 1) < HEAD_DIM

    def head_normed(c, scale):
        blk = acc[:, LANES * c:LANES * (c + 1)]
        sq = blk * blk
        s_lo = jnp.sum(jnp.where(lo, sq, 0.0), axis=-1, keepdims=True)
        s_hi = jnp.sum(jnp.where(lo, 0.0, sq), axis=-1, keepdims=True)
        inv = jnp.where(lo, lax.rsqrt(s_lo / HEAD_DIM + EPS), lax.rsqrt(s_hi / HEAD_DIM + EPS))
        return blk * inv * hg_ref[:, LANES * c:LANES * (c + 1)] * scale

    scale = HEAD_DIM ** -0.5 * LOG2E
    for c in range(4):
        qa_ref[:, LANES * c:LANES * (c + 1)] = head_normed(c, scale).astype(BF16)
        qb_ref[:, LANES * c:LANES * (c + 1)] = head_normed(5 + c, scale).astype(BF16)
    ka_ref[...] = head_normed(4, 1.0).astype(BF16)
    kb_ref[...] = head_normed(9, 1.0).astype(BF16)
    vt = _dot_nt(wvt_ref[...], h).astype(BF16)
    for j in range(vat_ref.shape[0]):
        vat_ref[j] = vt[:LANES, SWA_WINDOW * j:SWA_WINDOW * (j + 1)]
    for j in range(vbt_ref.shape[0]):
        vbt_ref[j] = vt[LANES:, MOBA_BLOCK * j:MOBA_BLOCK * (j + 1)]


def _qkv_call(x2d, attn_g, w_qk, w_vt, head_gain):
    T, D = x2d.shape
    tm = TOKEN_TILE
    tok = lambda w: pl.BlockSpec((tm, w), lambda i: (i, 0))
    full = lambda a: pl.BlockSpec(a.shape, lambda i: (0, 0))
    out_w = (SWA_Q_HEADS * HEAD_DIM, LANES, MOBA_Q_HEADS * HEAD_DIM, LANES)
    slabs = lambda blk: pl.BlockSpec((tm // blk, LANES, blk), lambda i: (i, 0, 0))
    slab_shape = lambda blk: jax.ShapeDtypeStruct((T // blk, LANES, blk), BF16)
    return pl.pallas_call(
        _qkv_kernel,
        grid=(T // tm,),
        in_specs=[tok(D), full(attn_g), full(w_qk), full(w_vt), full(head_gain)],
        out_specs=[tok(w) for w in out_w] + [slabs(SWA_WINDOW), slabs(MOBA_BLOCK)],
        out_shape=[jax.ShapeDtypeStruct((T, w), BF16) for w in out_w]
        + [slab_shape(SWA_WINDOW), slab_shape(MOBA_BLOCK)],
        compiler_params=pltpu.CompilerParams(
            dimension_semantics=("arbitrary",), vmem_limit_bytes=VMEM_LIMIT),
        name="qkv",
    )(x2d, attn_g, w_qk, w_vt, head_gain)


def _padded_heads(qblk, heads):
    zeros = jnp.zeros((qblk.shape[0], HEAD_DIM), qblk.dtype)
    pieces = []
    for h in range(heads):
        piece = qblk[:, HEAD_DIM * h:HEAD_DIM * (h + 1)]
        pieces.append(jnp.concatenate([piece, zeros] if h < GROUP_HEADS else [zeros, piece], axis=1))
    return jnp.concatenate(pieces, axis=0)


def _swa_kernel(tbl_ref, sink_ref, bkt_ref, q_ref, k_ref, vt_ref, o_ref, bias_scr):
    W = SWA_WINDOW
    H = SWA_Q_HEADS
    S = q_ref.shape[0]

    @pl.when(pl.program_id(0) == 0)
    def _init_bias():
        bkt = bkt_ref[...]
        for h in range(H):
            bias_scr[:, W * h:W * (h + 1)] = _bias_from_buckets(bkt, tbl_ref, h)

    key_i = lax.broadcasted_iota(I32, (2 * W, H * W), 0)
    qry_i = lax.broadcasted_iota(I32, (2 * W, H * W), 1) & (W - 1)
    dist = qry_i + W - key_i
    band = (dist >= 0) & (dist < W)
    own_part = key_i >= W
    head_of_lane = lax.broadcasted_iota(I32, (1, H * W), 1) // W
    sinks = jnp.zeros((1, H * W), F32)
    for h in range(H):
        sinks = jnp.where(head_of_lane == h, sink_ref[h] * LOG2E, sinks)

    def body(i, carry):
        r0 = pl.multiple_of(i * W, W)
        prev = jnp.maximum(i - 1, 0)
        p0 = pl.multiple_of(prev * W, W)
        q8 = _padded_heads(q_ref[pl.ds(r0, W), :], H)
        kband = jnp.concatenate([k_ref[pl.ds(p0, W), :], k_ref[pl.ds(r0, W), :]], axis=0)
        mask = band & (own_part | (i > 0))
        s = jnp.where(mask, _dot_nt(kband, q8) + bias_scr[...], NEG)
        m = jnp.maximum(jnp.max(s, axis=0, keepdims=True), sinks)
        e = jnp.exp2(s - m)
        den = jnp.sum(e, axis=0, keepdims=True) + jnp.exp2(sinks - m)
        vtband = jnp.concatenate([vt_ref[prev], vt_ref[i]], axis=1)
        ot = jnp.dot(vtband, e.astype(BF16), preferred_element_type=F32) / den
        for pr in range(H // 2):
            f0 = HEAD_DIM * (2 * pr // GROUP_HEADS)
            pair = ot[f0:f0 + HEAD_DIM, 2 * W * pr:2 * W * (pr + 1)]
            o_ref[pl.ds(r0, W), LANES * pr:LANES * (pr + 1)] = jnp.concatenate(
                [pair[:, :W], pair[:, W:]], axis=0).T
        return carry

    lax.fori_loop(0, S // W, body, 0)


def _swa_call(tbl_flat, sinks, qa, ka, vat, B, S):
    W = SWA_WINDOW
    bkt = jnp.asarray(np.ascontiguousarray(_band_buckets(W).T))
    smem = pl.BlockSpec(memory_space=pltpu.SMEM)
    return pl.pallas_call(
        _swa_kernel,
        grid=(B,),
        in_specs=[smem, smem,
                  pl.BlockSpec(bkt.shape, lambda b: (0, 0)),
                  pl.BlockSpec((S, qa.shape[1]), lambda b: (b, 0)),
                  pl.BlockSpec((S, LANES), lambda b: (b, 0)),
                  pl.BlockSpec((S // W, LANES, W), lambda b: (b, 0, 0))],
        out_specs=pl.BlockSpec((S, qa.shape[1]), lambda b: (b, 0)),
        out_shape=jax.ShapeDtypeStruct(qa.shape, F32),
        scratch_shapes=[pltpu.VMEM((2 * W, SWA_Q_HEADS * W), F32)],
        compiler_params=pltpu.CompilerParams(
            dimension_semantics=("arbitrary",), vmem_limit_bytes=VMEM_LIMIT),
        name="swa",
    )(tbl_flat, sinks, bkt, qa, ka, vat)


def _moba_kernel(tbl_ref, bkt_ref, q_ref, k_ref, vt_ref, o_ref,
                 bias_scr, km_scr, q8_scr, sel_scr, m_scr, l_scr, acc_scr):
    BS = MOBA_BLOCK
    H = MOBA_Q_HEADS
    S = q_ref.shape[0]
    NBK = S // BS
    NQ = H * BS
    GQ = GROUP_HEADS * BS

    @pl.when(pl.program_id(0) == 0)
    def _init_bias():
        bkt = bkt_ref[...]
        for h in range(H):
            band = _bias_from_buckets(bkt, tbl_ref, SWA_Q_HEADS + h)
            far = tbl_ref[(REL_BUCKETS - 1) * N_HEADS + SWA_Q_HEADS + h] * LOG2E
            bias_scr[0, :, BS * h:BS * (h + 1)] = jnp.full((BS, BS), far, F32)
            bias_scr[1, :, BS * h:BS * (h + 1)] = band[:BS]
            bias_scr[2, :, BS * h:BS * (h + 1)] = band[BS:]

    blk_row = lax.broadcasted_iota(I32, (BF16_SUBLANES, 1), 0)
    kmean = jnp.sum(k_ref[...].astype(F32).reshape(NBK, BS, LANES), axis=1) / BS
    kmean = jnp.concatenate([kmean, jnp.zeros((BF16_SUBLANES - NBK, LANES), F32)], axis=0)
    km_hi = kmean.astype(BF16)
    km_scr[0] = km_hi
    km_scr[1] = (kmean - km_hi.astype(F32)).astype(BF16)

    def scores(n, case):
        k0 = pl.multiple_of(n * BS, BS)
        return _dot_nt(k_ref[pl.ds(k0, BS), :], q8_scr[...]) + bias_scr[case]

    def weighted_values(n, e):
        pv = jnp.dot(vt_ref[n], e.astype(BF16), preferred_element_type=F32)
        return jnp.concatenate([pv[:HEAD_DIM, :GQ], pv[HEAD_DIM:, GQ:]], axis=1)

    def qblock(mi, carry):
        r0 = pl.multiple_of(mi * BS, BS)
        q8 = _padded_heads(q_ref[pl.ds(r0, BS), :], H)
        q8_scr[...] = q8
        gate = _dot_nt(km_scr[0], q8) + _dot_nt(km_scr[1], q8)
        past = blk_row < mi
        gm = jnp.where(past, gate, NEG)
        cnt = jnp.zeros(gm.shape, F32)
        for n in range(NBK - 1):
            col = gm[n:n + 1, :]
            beats = (gm > col) | ((gm == col) & (blk_row < n))
            c = jnp.sum(jnp.where(beats, 1.0, 0.0), axis=0, keepdims=True)
            cnt = jnp.where(blk_row == n, c, cnt)
        sel_scr[...] = jnp.where(past & (cnt < MOBA_TOPK), 1.0, 0.0)

        key_i = lax.broadcasted_iota(I32, (BS, NQ), 0)
        qry_i = lax.broadcasted_iota(I32, (BS, NQ), 1) & (BS - 1)
        s = jnp.where(key_i <= qry_i, scores(mi, 2), NEG)
        m0 = jnp.max(s, axis=0, keepdims=True)
        e = jnp.exp2(s - m0)
        m_scr[...] = m0
        l_scr[...] = jnp.sum(e, axis=0, keepdims=True)
        acc_scr[...] = weighted_values(mi, e)

        def kvblock(j, c2):
            n = mi - j
            s = jnp.where(sel_scr[pl.ds(n, 1), :] > 0.5, scores(n, jnp.where(j == 1, 1, 0)), NEG)
            m_prev = m_scr[...]
            m_new = jnp.maximum(m_prev, jnp.max(s, axis=0, keepdims=True))
            alpha = jnp.exp2(m_prev - m_new)
            e = jnp.exp2(s - m_new)
            l_scr[...] = alpha * l_scr[...] + jnp.sum(e, axis=0, keepdims=True)
            acc_scr[...] = alpha * acc_scr[...] + weighted_values(n, e)
            m_scr[...] = m_new
            return c2

        lax.fori_loop(1, mi + 1, kvblock, 0)
        ot = acc_scr[...] / l_scr[...]
        for pr in range(H // 2):
            pair = ot[:, 2 * BS * pr:2 * BS * (pr + 1)]
            o_ref[pl.ds(r0, BS), LANES * pr:LANES * (pr + 1)] = jnp.concatenate(
                [pair[:, :BS], pair[:, BS:]], axis=0).T
        return carry

    lax.fori_loop(0, NBK, qblock, 0)


def _moba_call(tbl_flat, qb, kb, vbt, B, S):
    BS = MOBA_BLOCK
    NQ = MOBA_Q_HEADS * BS
    NBK = S // BS
    bkt = jnp.asarray(np.ascontiguousarray(_band_buckets(BS).T))
    smem = pl.BlockSpec(memory_space=pltpu.SMEM)
    return pl.pallas_call(
        _moba_kernel,
        grid=(B,),
        in_specs=[smem,
                  pl.BlockSpec(bkt.shape, lambda b: (0, 0)),
                  pl.BlockSpec((S, qb.shape[1]), lambda b: (b, 0)),
                  pl.BlockSpec((S, LANES), lambda b: (b, 0)),
                  pl.BlockSpec((NBK, LANES, BS), lambda b: (b, 0, 0))],
        out_specs=pl.BlockSpec((S, qb.shape[1]), lambda b: (b, 0)),
        out_shape=jax.ShapeDtypeStruct(qb.shape, F32),
        scratch_shapes=[pltpu.VMEM((3, BS, NQ), F32),
                        pltpu.VMEM((2, BF16_SUBLANES, LANES), BF16),
                        pltpu.VMEM((NQ, LANES), BF16),
                        pltpu.VMEM((BF16_SUBLANES, NQ), F32),
                        pltpu.VMEM((1, NQ), F32),
                        pltpu.VMEM((1, NQ), F32),
                        pltpu.VMEM((HEAD_DIM, NQ), F32)],
        compiler_params=pltpu.CompilerParams(
            dimension_semantics=("arbitrary",), vmem_limit_bytes=VMEM_LIMIT),
        name="moba",
    )(tbl_flat, bkt, qb, kb, vbt)


def _outproj_kernel(oa_ref, ob_ref, x_ref, ga_ref, gb_ref, woa_ref, wob_ref, fg_ref,
                    wrh_ref, wrl_ref, rb_ref, x1_ref, hf_ref, eid_ref, wt_ref):
    na = _rmsnorm(oa_ref[...], ga_ref[...]).astype(BF16)
    nb = _rmsnorm(ob_ref[...], gb_ref[...]).astype(BF16)
    y = (jnp.dot(na, woa_ref[...], preferred_element_type=F32)
         + jnp.dot(nb, wob_ref[...], preferred_element_type=F32))
    x1 = x_ref[...] + y
    x1_ref[...] = x1
    hf = _rmsnorm(x1, fg_ref[...])
    _to_token_tiles(hf_ref, hf)

    hf_hi = hf.astype(BF16)
    hf_lo = (hf - hf_hi.astype(F32)).astype(BF16)
    lt = (_dot_nt(wrh_ref[...], hf_hi) + _dot_nt(wrh_ref[...], hf_lo)
          + _dot_nt(wrl_ref[...], hf_hi) + rb_ref[...])
    gl = [lt[j:j + 1, :] for j in range(N_GROUPS)]
    best = gl[0]
    gsel = jnp.zeros(best.shape, I32)
    for j in range(1, N_GROUPS):
        better = gl[j] > best
        gsel = jnp.where(better, j, gsel)
        best = jnp.where(better, gl[j], best)
    gsum = jnp.zeros(best.shape, F32)
    for j in range(N_GROUPS):
        gsum = gsum + jnp.exp(gl[j] - best)
    gw = 1.0 / gsum
    E = EXPERTS_PER_GROUP
    es = lt[EXPERT_ROW0:EXPERT_ROW0 + E, :]
    for j in range(1, N_GROUPS):
        es = jnp.where(gsel == j, lt[EXPERT_ROW0 + E * j:EXPERT_ROW0 + E * (j + 1), :], es)
    row = lax.broadcasted_iota(I32, es.shape, 0)
    v1 = jnp.max(es, axis=0, keepdims=True)
    i1 = jnp.min(jnp.where(es == v1, row, E), axis=0, keepdims=True)
    es2 = jnp.where(row == i1, -jnp.inf, es)
    v2 = jnp.max(es2, axis=0, keepdims=True)
    i2 = jnp.min(jnp.where(es2 == v2, row, E), axis=0, keepdims=True)
    e2 = jnp.exp(v2 - v1)
    den = 1.0 + e2
    eid_ref[...] = jnp.concatenate([gsel * E + i1, gsel * E + i2], axis=0)
    wt_ref[...] = jnp.concatenate([gw * (1.0 / den), gw * (e2 / den)], axis=0)


def _outproj_call(oa, ob, x2d, ga, gb, woa, wob, fg, wrh, wrl, rb):
    T, D = x2d.shape
    tm = TOKEN_TILE
    tok = lambda w: pl.BlockSpec((tm, w), lambda i: (i, 0))
    full = lambda a: pl.BlockSpec(a.shape, lambda i: (0, 0))
    col = pl.BlockSpec((EXPERT_TOPK, tm), lambda i: (0, i))
    return pl.pallas_call(
        _outproj_kernel,
        grid=(T // tm,),
        in_specs=[tok(oa.shape[1]), tok(ob.shape[1]), tok(D), full(ga), full(gb), full(woa), full(wob),
                  full(fg), full(wrh), full(wrl), full(rb)],
        out_specs=[tok(D), pl.BlockSpec((tm * ROW_TILE, LANES), lambda i: (i, 0)), col, col],
        out_shape=[jax.ShapeDtypeStruct((T, D), F32), jax.ShapeDtypeStruct((T * ROW_TILE, LANES), F32),
                   jax.ShapeDtypeStruct((EXPERT_TOPK, T), I32), jax.ShapeDtypeStruct((EXPERT_TOPK, T), F32)],
        compiler_params=pltpu.CompilerParams(
            dimension_semantics=("arbitrary",), vmem_limit_bytes=VMEM_LIMIT),
        name="outproj",
    )(oa, ob, x2d, ga, gb, woa, wob, fg, wrh, wrl, rb)


COPY_GROUP = 32


def _moe_kernel(be_ref, nv_ref, src_ref, srcn_ref, dst_ref, hf_hbm, wg_ref, wu_ref, wd_ref, out_hbm,
                xbuf, obuf, wgb, wub, wdb, gsem, ssem):
    i = pl.program_id(0)
    nblk = pl.num_programs(0)
    M = xbuf.shape[1] // ROW_TILE
    slot = i % 2
    nv_cur = nv_ref[i]
    nv_next = jnp.where(i + 1 < nblk, nv_ref[jnp.minimum(i + 1, nblk - 1)], 0)

    def for_real_groups(n_real, fn):
        for k in range(M // COPY_GROUP):
            @pl.when(k * COPY_GROUP < n_real)
            def _group():
                fn(range(k * COPY_GROUP, (k + 1) * COPY_GROUP))

    def gather_rows(idx_ref, buf_slot, n_real):
        def start(rows):
            for r in rows:
                tok_row = pl.multiple_of(idx_ref[0, 0, r] * ROW_TILE, ROW_TILE)
                pltpu.make_async_copy(hf_hbm.at[pl.ds(tok_row, ROW_TILE)],
                                      xbuf.at[buf_slot, pl.ds(r * ROW_TILE, ROW_TILE)],
                                      gsem.at[buf_slot]).start(priority=r % DMA_PRIORITIES)
        for_real_groups(n_real, start)

    def scatter_rows(idx_ref, buf_slot, n_real):
        def start(rows):
            for r in rows:
                dst_row = pl.multiple_of(idx_ref[0, 0, r] * ROW_TILE, ROW_TILE)
                pltpu.make_async_copy(obuf.at[buf_slot, pl.ds(r * ROW_TILE, ROW_TILE)],
                                      out_hbm.at[pl.ds(dst_row, ROW_TILE)],
                                      ssem.at[buf_slot]).start(priority=r % DMA_PRIORITIES)
        for_real_groups(n_real, start)

    def wait_rows(buf, buf_slot, sem, n_real):
        span = COPY_GROUP * ROW_TILE
        for_real_groups(n_real, lambda rows: pltpu.make_async_copy(
            hf_hbm.at[pl.ds(0, span)], buf.at[buf_slot, pl.ds(0, span)], sem.at[buf_slot]).wait())

    @pl.when(nv_cur > 0)
    def _compute():
        @pl.when(i == 0)
        def _first_gather():
            gather_rows(src_ref, 0, nv_cur)
            obuf[1] = jnp.zeros(obuf.shape[1:], F32)
            spare = pltpu.make_async_copy(
                obuf.at[1], out_hbm.at[pl.ds(out_hbm.shape[0] - M * ROW_TILE, M * ROW_TILE)], ssem.at[1])
            spare.start()
            spare.wait()

        @pl.when((i == 0) | (be_ref[i] != be_ref[jnp.maximum(i - 1, 0)]))
        def _cast_weights():
            wgb[...] = wg_ref[0].astype(BF16)
            wub[...] = wu_ref[0].astype(BF16)
            wdb[...] = wd_ref[0].astype(BF16)

        wait_rows(xbuf, slot, gsem, nv_cur)

        @pl.when(i >= 2)
        def _reuse_obuf():
            wait_rows(obuf, slot, ssem, nv_ref[jnp.maximum(i - 2, 0)])

        gather_rows(srcn_ref, 1 - slot, nv_next)
        xb = _from_token_tiles(xbuf.at[slot], M).astype(BF16)
        a = jnp.dot(xb, wgb[...], preferred_element_type=F32)
        u = jnp.dot(xb, wub[...], preferred_element_type=F32)
        hmid = (a * jax.nn.sigmoid(a)) * u
        _to_token_tiles(obuf.at[slot], jnp.dot(hmid.astype(BF16), wdb[...], preferred_element_type=F32))
        scatter_rows(dst_ref, slot, nv_cur)

        @pl.when(nv_next == 0)
        def _drain():
            wait_rows(obuf, slot, ssem, nv_cur)

            @pl.when(i >= 1)
            def _drain_prev():
                wait_rows(obuf, 1 - slot, ssem, nv_ref[jnp.maximum(i - 1, 0)])


def _moe_call(block_expert, n_real, src_tok, dst_row, hf, w_g, w_u, w_d):
    T = hf.shape[0] // ROW_TILE
    M = DISPATCH_BLOCK
    nblk = block_expert.shape[0]
    D, DE = w_g.shape[1], w_g.shape[2]
    assert D == ROW_TILE * LANES and hf.shape[1] == LANES and M % COPY_GROUP == 0
    idx_spec = lambda f: pl.BlockSpec((1, 1, M), f, memory_space=pltpu.SMEM)
    cur = lambda i, be, nu: (i, 0, 0)
    nxt = lambda i, be, nu: (jnp.minimum(i + 1, nblk - 1), 0, 0)
    wspec = lambda a, b: pl.BlockSpec((1, a, b), lambda i, be, nu: (be[i], 0, 0))
    grid_spec = pltpu.PrefetchScalarGridSpec(
        num_scalar_prefetch=2,
        grid=(nblk,),
        in_specs=[idx_spec(cur), idx_spec(nxt), idx_spec(cur),
                  pl.BlockSpec(memory_space=pl.ANY),
                  wspec(D, DE), wspec(D, DE), wspec(DE, D)],
        out_specs=pl.BlockSpec(memory_space=pl.ANY),
        scratch_shapes=[pltpu.VMEM((2, M * ROW_TILE, LANES), F32), pltpu.VMEM((2, M * ROW_TILE, LANES), F32),
                        pltpu.VMEM((D, DE), BF16), pltpu.VMEM((D, DE), BF16), pltpu.VMEM((DE, D), BF16),
                        pltpu.SemaphoreType.DMA((2,)), pltpu.SemaphoreType.DMA((2,))],
    )
    src3 = src_tok.reshape(nblk, 1, M)
    return pl.pallas_call(
        _moe_kernel,
        grid_spec=grid_spec,
        out_shape=jax.ShapeDtypeStruct(((EXPERT_TOPK * T + M) * ROW_TILE, LANES), F32),
        compiler_params=pltpu.CompilerParams(
            dimension_semantics=("arbitrary",), vmem_limit_bytes=VMEM_LIMIT),
        name="moe",
    )(block_expert, n_real, src3, src3, dst_row.reshape(nblk, 1, M), hf, w_g, w_u, w_d)


def _ple_kernel(x1_ref, m0_ref, m1_ref, wt_ref, p_ref, pg_ref, wgate_ref, wproj_ref, pog_ref, out_ref):
    proj = _rmsnorm(jnp.dot(p_ref[...].astype(BF16), wproj_ref[...], preferred_element_type=F32),
                    pog_ref[...])
    tm = x1_ref.shape[0]
    moe = (_from_token_tiles(m0_ref, tm) * wt_ref[:, 0:1] + _from_token_tiles(m1_ref, tm) * wt_ref[:, 1:2])
    x2 = x1_ref[...] + moe
    hn = _rmsnorm(x2, pg_ref[...]).astype(BF16)
    gate = jax.nn.sigmoid(jnp.dot(hn, wgate_ref[...], preferred_element_type=F32))
    out_ref[...] = x2 + gate * proj


def _ple_call(moe_rows, x1, wt_cols, p2d, pg, wgate, wproj, pog):
    T, D = x1.shape
    tm = PLE_TILE
    tok = lambda w: pl.BlockSpec((tm, w), lambda i: (i, 0))
    full = lambda a: pl.BlockSpec(a.shape, lambda i: (0, 0))
    first = pl.BlockSpec((tm * ROW_TILE, LANES), lambda i: (i, 0))
    second = pl.BlockSpec((tm * ROW_TILE, LANES), lambda i: (T // tm + i, 0))
    return pl.pallas_call(
        _ple_kernel,
        grid=(T // tm,),
        in_specs=[tok(D), first, second, tok(EXPERT_TOPK), tok(p2d.shape[1]),
                  full(pg), full(wgate), full(wproj), full(pog)],
        out_specs=tok(D),
        out_shape=jax.ShapeDtypeStruct((T, D), F32),
        compiler_params=pltpu.CompilerParams(
            dimension_semantics=("arbitrary",), vmem_limit_bytes=VMEM_LIMIT),
        name="ple",
    )(x1, moe_rows, moe_rows, wt_cols, p2d, pg, wgate, wproj, pog)


def _dispatch_plan(eid, T):
    A = EXPERT_TOPK * T
    M = DISPATCH_BLOCK
    nblk = A // M + N_EXPERTS
    assert A <= PAD_MARK
    e_flat = eid.reshape(A)
    experts = jnp.arange(N_EXPERTS, dtype=I32)
    counts = jnp.sum((e_flat[:, None] == experts[None, :]).astype(I32), axis=0)
    padded = (counts + M - 1) // M * M
    n_used = jnp.sum(padded) // M
    row = jnp.arange(M, dtype=I32)[None, :]
    pad_keys = jnp.where(row < (padded - counts)[:, None], 2 * experts[:, None] + 1, 2 * N_EXPERTS)
    keys = jnp.concatenate([2 * e_flat, pad_keys.reshape(-1)])
    vals = jnp.concatenate([jnp.arange(A, dtype=I32), jnp.full((N_EXPERTS * M,), PAD_MARK, I32)])
    packed = jnp.sort(keys * (PAD_MARK + 1) + vals).reshape(nblk, M)
    a_s = packed & PAD_MARK
    blk = jnp.arange(nblk, dtype=I32)
    be = jnp.minimum(packed[:, 0] // (2 * (PAD_MARK + 1)), N_EXPERTS - 1)
    be = jnp.where(blk < n_used, be, be[jnp.maximum(n_used - 1, 0)])
    valid = a_s != PAD_MARK
    src_tok = jnp.where(valid, jnp.where(a_s >= T, a_s - T, a_s), 0)
    dst_row = jnp.where(valid, a_s, A + row)
    return be, jnp.sum(valid.astype(I32), axis=1), src_tok, dst_row


def _layer(x2d, p2d, rel_bias, attn_norm, w_in, swa_q_norm, swa_k_norm, swa_sinks, moba_q_norm,
           moba_k_norm, swa_out_norm, moba_out_norm, w_out, ffn_norm, w_rg, b_rg, w_re, b_re,
           w_g, w_u, w_d, ple_norm, w_ple_gate, w_ple_proj, ple_out_norm, B, S):
    T, D = x2d.shape
    row = lambda v: v.reshape(1, -1).astype(F32)
    head_gain = jnp.concatenate([
        jnp.tile(swa_q_norm, SWA_Q_HEADS), jnp.tile(swa_k_norm, SWA_KV_HEADS),
        jnp.tile(moba_q_norm, MOBA_Q_HEADS), jnp.tile(moba_k_norm, MOBA_KV_HEADS)])
    w_in_b = w_in.astype(BF16)
    qa_w, kv_w = SWA_Q_HEADS * HEAD_DIM, SWA_KV_HEADS * HEAD_DIM
    qb_w = MOBA_Q_HEADS * HEAD_DIM
    c_va = qa_w + kv_w
    c_qb = c_va + kv_w
    c_vb = c_qb + qb_w + kv_w
    w_qk = jnp.concatenate([w_in_b[:, :c_va], w_in_b[:, c_qb:c_vb]], axis=1)
    w_vt = jnp.concatenate([w_in_b[:, c_va:c_qb], w_in_b[:, c_vb:]], axis=1).T
    qa, ka, qb, kb, vat, vbt = _qkv_call(x2d, row(attn_norm), w_qk, w_vt, row(head_gain))

    tbl_flat = rel_bias.astype(F32).reshape(-1)
    oa = _swa_call(tbl_flat, swa_sinks.astype(F32), qa, ka, vat, B, S)
    ob = _moba_call(tbl_flat, qb, kb, vbt, B, S)

    pad_rows = lambda n: jnp.zeros((n, D), F32)
    wr = jnp.concatenate([w_rg.T, pad_rows(EXPERT_ROW0 - N_GROUPS), w_re.T,
                          pad_rows(ROUTER_ROWS - EXPERT_ROW0 - N_EXPERTS)], axis=0)
    wr_hi = wr.astype(BF16)
    wr_lo = (wr - wr_hi.astype(F32)).astype(BF16)
    rb = jnp.concatenate([b_rg, jnp.zeros((EXPERT_ROW0 - N_GROUPS,), F32), b_re,
                          jnp.zeros((ROUTER_ROWS - EXPERT_ROW0 - N_EXPERTS,), F32)]).reshape(ROUTER_ROWS, 1)
    wo = w_out.astype(BF16)
    na_w = SWA_Q_HEADS * HEAD_DIM
    x1, hf, eid, wts = _outproj_call(oa, ob, x2d, row(swa_out_norm), row(moba_out_norm),
                                     wo[:na_w], wo[na_w:], row(ffn_norm), wr_hi, wr_lo, rb)

    be, n_real, src_tok, dst_row = _dispatch_plan(eid, T)
    moe_rows = _moe_call(be, n_real, src_tok, dst_row, hf, w_g, w_u, w_d)
    return _ple_call(moe_rows, x1, wts.T, p2d, row(ple_norm), w_ple_gate.astype(BF16),
                     w_ple_proj.astype(BF16), row(ple_out_norm))


def kernel(x, p, rel_bias, attn_norm, w_in, swa_q_norm, swa_k_norm, swa_sinks, moba_q_norm, moba_k_norm,
           swa_out_norm, moba_out_norm, w_out, ffn_norm, w_router_group, b_router_group, w_router_expert,
           b_router_expert, w_exp_gate, w_exp_up, w_exp_down, ple_norm, w_ple_gate, w_ple_proj, ple_out_norm):
    B, S, D = x.shape
    x2d = x.reshape(B * S, D)
    for i in range(p.shape[0]):
        x2d = _layer(x2d, p[i].reshape(B * S, -1), rel_bias, attn_norm[i], w_in[i], swa_q_norm[i],
                     swa_k_norm[i], swa_sinks[i], moba_q_norm[i], moba_k_norm[i], swa_out_norm[i],
                     moba_out_norm[i], w_out[i], ffn_norm[i], w_router_group[i], b_router_group[i],
                     w_router_expert[i], b_router_expert[i], w_exp_gate[i], w_exp_up[i], w_exp_down[i],
                     ple_norm[i], w_ple_gate[i], w_ple_proj[i], ple_out_norm[i], B, S)
    return x2d.reshape(B, S, D)
```

```python
import math

import numpy as np
import jax
import jax.numpy as jnp
from jax import lax
from jax.experimental import pallas as pl
from jax.experimental.pallas import tpu as pltpu

F32 = jnp.float32
BF16 = jnp.bfloat16
I32 = jnp.int32

HEAD_DIM = 64
SWA_Q_HEADS = 8
SWA_KV_HEADS = 2
SWA_WINDOW = 128
MOBA_Q_HEADS = 8
MOBA_KV_HEADS = 2
MOBA_BLOCK = 256
MOBA_TOPK = 3
N_HEADS = SWA_Q_HEADS + MOBA_Q_HEADS
REL_BUCKETS = 32
REL_MAX_DIST = 128
N_GROUPS = 4
EXPERTS_PER_GROUP = 8
N_EXPERTS = N_GROUPS * EXPERTS_PER_GROUP
EXPERT_TOPK = 2
DISPATCH_BLOCK = 512
PAD_MARK = 0xFFFF
EPS = 1e-6
NEG = -1e30
LOG2E = math.log2(math.e)

DMA_PRIORITIES = 2
LANES = 128
ROW_TILE = 8
BF16_SUBLANES = 16
GROUP_HEADS = 4
GROUP_W = GROUP_HEADS * HEAD_DIM
ROUTER_ROWS = 128
EXPERT_ROW0 = 8
TOKEN_TILE = 1024
PLE_TILE = 1024
VMEM_LIMIT = 52 * 1024 * 1024


def _dot_nt(a, b):
    return lax.dot_general(a, b, (((1,), (1,)), ((), ())), preferred_element_type=F32)


def _rmsnorm(x, g):
    ms = jnp.mean(x * x, axis=-1, keepdims=True)
    return x * lax.rsqrt(ms + EPS) * g


def _to_token_tiles(ref, x):
    m = x.shape[0]
    for j in range(x.shape[1] // LANES):
        ref[pl.ds(j, m, stride=ROW_TILE), :] = x[:, LANES * j:LANES * (j + 1)]


def _from_token_tiles(ref, m):
    return jnp.concatenate([ref[pl.ds(j, m, stride=ROW_TILE), :] for j in range(ref.shape[0] // m)], axis=1)


def _rel_bucket_np(dist):
    n = np.maximum(dist, 0)
    exact = REL_BUCKETS // 2
    nf = np.maximum(n, 1).astype(np.float32)
    large = exact + (np.log(nf / exact) / math.log(REL_MAX_DIST / exact)
                     * (REL_BUCKETS - exact)).astype(np.int32)
    return np.where(n < exact, n, np.minimum(large, REL_BUCKETS - 1)).astype(np.int32)


def _band_buckets(block):
    qi = np.arange(block)[:, None]
    kj = np.arange(2 * block)[None, :]
    return _rel_bucket_np(qi + block - kj)


def _bias_from_buckets(bkt, tbl_ref, head):
    acc = jnp.zeros(bkt.shape, F32)
    for j in range(REL_BUCKETS):
        acc = jnp.where(bkt == j, tbl_ref[j * N_HEADS + head] * LOG2E, acc)
    return acc


def _qkv_kernel(x_ref, g_ref, w_ref, wvt_ref, hg_ref, qa_ref, ka_ref, qb_ref, kb_ref, vat_ref, vbt_ref):
    h = _rmsnorm(x_ref[...], g_ref[...]).astype(BF16)
    acc = jnp.dot(h, w_ref[...], preferred_element_type=F32)
    lo = lax.broadcasted_iota(I32, (1, LANES), 1) < HEAD_DIM

    def head_normed(c, scale):
        blk = acc[:, LANES * c:LANES * (c + 1)]
        sq = blk * blk
        s_lo = jnp.sum(jnp.where(lo, sq, 0.0), axis=-1, keepdims=True)
        s_hi = jnp.sum(jnp.where(lo, 0.0, sq), axis=-1, keepdims=True)
        inv = jnp.where(lo, lax.rsqrt(s_lo / HEAD_DIM + EPS), lax.rsqrt(s_hi / HEAD_DIM + EPS))
        return blk * inv * hg_ref[:, LANES * c:LANES * (c + 1)] * scale

    scale = HEAD_DIM ** -0.5 * LOG2E
    for c in range(4):
        qa_ref[:, LANES * c:LANES * (c + 1)] = head_normed(c, scale).astype(BF16)
        qb_ref[:, LANES * c:LANES * (c + 1)] = head_normed(5 + c, scale).astype(BF16)
    ka_ref[...] = head_normed(4, 1.0).astype(BF16)
    kb_ref[...] = head_normed(9, 1.0).astype(BF16)
    vt = _dot_nt(wvt_ref[...], h).astype(BF16)
    for j in range(vat_ref.shape[0]):
        vat_ref[j] = vt[:LANES, SWA_WINDOW * j:SWA_WINDOW * (j + 1)]
    for j in range(vbt_ref.shape[0]):
        vbt_ref[j] = vt[LANES:, MOBA_BLOCK * j:MOBA_BLOCK * (j + 1)]


def _qkv_call(x2d, attn_g, w_qk, w_vt, head_gain):
    T, D = x2d.shape
    tm = TOKEN_TILE
    tok = lambda w: pl.BlockSpec((tm, w), lambda i: (i, 0))
    full = lambda a: pl.BlockSpec(a.shape, lambda i: (0, 0))
    out_w = (SWA_Q_HEADS * HEAD_DIM, LANES, MOBA_Q_HEADS * HEAD_DIM, LANES)
    slabs = lambda blk: pl.BlockSpec((tm // blk, LANES, blk), lambda i: (i, 0, 0))
    slab_shape = lambda blk: jax.ShapeDtypeStruct((T // blk, LANES, blk), BF16)
    return pl.pallas_call(
        _qkv_kernel,
        grid=(T // tm,),
        in_specs=[tok(D), full(attn_g), full(w_qk), full(w_vt), full(head_gain)],
        out_specs=[tok(w) for w in out_w] + [slabs(SWA_WINDOW), slabs(MOBA_BLOCK)],
        out_shape=[jax.ShapeDtypeStruct((T, w), BF16) for w in out_w]
        + [slab_shape(SWA_WINDOW), slab_shape(MOBA_BLOCK)],
        compiler_params=pltpu.CompilerParams(
            dimension_semantics=("arbitrary",), vmem_limit_bytes=VMEM_LIMIT),
        name="qkv",
    )(x2d, attn_g, w_qk, w_vt, head_gain)


def _padded_heads(qblk, heads):
    zeros = jnp.zeros((qblk.shape[0], HEAD_DIM), qblk.dtype)
    pieces = []
    for h in range(heads):
        piece = qblk[:, HEAD_DIM * h:HEAD_DIM * (h + 1)]
        pieces.append(jnp.concatenate([piece, zeros] if h < GROUP_HEADS else [zeros, piece], axis=1))
    return jnp.concatenate(pieces, axis=0)


def _swa_kernel(tbl_ref, sink_ref, bkt_ref, q_ref, k_ref, vt_ref, o_ref, bias_scr):
    W = SWA_WINDOW
    H = SWA_Q_HEADS
    S = q_ref.shape[0]

    @pl.when(pl.program_id(0) == 0)
    def _init_bias():
        bkt = bkt_ref[...]
        for h in range(H):
            bias_scr[:, W * h:W * (h + 1)] = _bias_from_buckets(bkt, tbl_ref, h)

    key_i = lax.broadcasted_iota(I32, (2 * W, H * W), 0)
    qry_i = lax.broadcasted_iota(I32, (2 * W, H * W), 1) & (W - 1)
    dist = qry_i + W - key_i
    band = (dist >= 0) & (dist < W)
    own_part = key_i >= W
    head_of_lane = lax.broadcasted_iota(I32, (1, H * W), 1) // W
    sinks = jnp.zeros((1, H * W), F32)
    for h in range(H):
        sinks = jnp.where(head_of_lane == h, sink_ref[h] * LOG2E, sinks)

    def body(i, carry):
        r0 = pl.multiple_of(i * W, W)
        prev = jnp.maximum(i - 1, 0)
        p0 = pl.multiple_of(prev * W, W)
        q8 = _padded_heads(q_ref[pl.ds(r0, W), :], H)
        kband = jnp.concatenate([k_ref[pl.ds(p0, W), :], k_ref[pl.ds(r0, W), :]], axis=0)
        mask = band & (own_part | (i > 0))
        s = jnp.where(mask, _dot_nt(kband, q8) + bias_scr[...], NEG)
        m = jnp.maximum(jnp.max(s, axis=0, keepdims=True), sinks)
        e = jnp.exp2(s - m)
        den = jnp.sum(e, axis=0, keepdims=True) + jnp.exp2(sinks - m)
        vtband = jnp.concatenate([vt_ref[prev], vt_ref[i]], axis=1)
        ot = jnp.dot(vtband, e.astype(BF16), preferred_element_type=F32) / den
        for pr in range(H // 2):
            f0 = HEAD_DIM * (2 * pr // GROUP_HEADS)
            pair = ot[f0:f0 + HEAD_DIM, 2 * W * pr:2 * W * (pr + 1)]
            o_ref[pl.ds(r0, W), LANES * pr:LANES * (pr + 1)] = jnp.concatenate(
                [pair[:, :W], pair[:, W:]], axis=0).T
        return carry

    lax.fori_loop(0, S // W, body, 0)


def _swa_call(tbl_flat, sinks, qa, ka, vat, B, S):
    W = SWA_WINDOW
    bkt = jnp.asarray(np.ascontiguousarray(_band_buckets(W).T))
    smem = pl.BlockSpec(memory_space=pltpu.SMEM)
    return pl.pallas_call(
        _swa_kernel,
        grid=(B,),
        in_specs=[smem, smem,
                  pl.BlockSpec(bkt.shape, lambda b: (0, 0)),
                  pl.BlockSpec((S, qa.shape[1]), lambda b: (b, 0)),
                  pl.BlockSpec((S, LANES), lambda b: (b, 0)),
                  pl.BlockSpec((S // W, LANES, W), lambda b: (b, 0, 0))],
        out_specs=pl.BlockSpec((S, qa.shape[1]), lambda b: (b, 0)),
        out_shape=jax.ShapeDtypeStruct(qa.shape, F32),
        scratch_shapes=[pltpu.VMEM((2 * W, SWA_Q_HEADS * W), F32)],
        compiler_params=pltpu.CompilerParams(
            dimension_semantics=("arbitrary",), vmem_limit_bytes=VMEM_LIMIT),
        name="swa",
    )(tbl_flat, sinks, bkt, qa, ka, vat)


def _moba_kernel(tbl_ref, bkt_ref, q_ref, k_ref, vt_ref, o_ref,
                 bias_scr, km_scr, q8_scr, sel_scr, m_scr, l_scr, acc_scr):
    BS = MOBA_BLOCK
    H = MOBA_Q_HEADS
    S = q_ref.shape[0]
    NBK = S // BS
    NQ = H * BS
    GQ = GROUP_HEADS * BS

    @pl.when(pl.program_id(0) == 0)
    def _init_bias():
        bkt = bkt_ref[...]
        for h in range(H):
            band = _bias_from_buckets(bkt, tbl_ref, SWA_Q_HEADS + h)
            far = tbl_ref[(REL_BUCKETS - 1) * N_HEADS + SWA_Q_HEADS + h] * LOG2E
            bias_scr[0, :, BS * h:BS * (h + 1)] = jnp.full((BS, BS), far, F32)
            bias_scr[1, :, BS * h:BS * (h + 1)] = band[:BS]
            bias_scr[2, :, BS * h:BS * (h + 1)] = band[BS:]

    blk_row = lax.broadcasted_iota(I32, (BF16_SUBLANES, 1), 0)
    kmean = jnp.sum(k_ref[...].astype(F32).reshape(NBK, BS, LANES), axis=1) / BS
    kmean = jnp.concatenate([kmean, jnp.zeros((BF16_SUBLANES - NBK, LANES), F32)], axis=0)
    km_hi = kmean.astype(BF16)
    km_scr[0] = km_hi
    km_scr[1] = (kmean - km_hi.astype(F32)).astype(BF16)

    def scores(n, case):
        k0 = pl.multiple_of(n * BS, BS)
        return _dot_nt(k_ref[pl.ds(k0, BS), :], q8_scr[...]) + bias_scr[case]

    def weighted_values(n, e):
        pv = jnp.dot(vt_ref[n], e.astype(BF16), preferred_element_type=F32)
        return jnp.concatenate([pv[:HEAD_DIM, :GQ], pv[HEAD_DIM:, GQ:]], axis=1)

    def qblock(mi, carry):
        r0 = pl.multiple_of(mi * BS, BS)
        q8 = _padded_heads(q_ref[pl.ds(r0, BS), :], H)
        q8_scr[...] = q8
        gate = _dot_nt(km_scr[0], q8) + _dot_nt(km_scr[1], q8)
        past = blk_row < mi
        gm = jnp.where(past, gate, NEG)
        cnt = jnp.zeros(gm.shape, F32)
        for n in range(NBK - 1):
            col = gm[n:n + 1, :]
            beats = (gm > col) | ((gm == col) & (blk_row < n))
            c = jnp.sum(jnp.where(beats, 1.0, 0.0), axis=0, keepdims=True)
            cnt = jnp.where(blk_row == n, c, cnt)
        sel_scr[...] = jnp.where(past & (cnt < MOBA_TOPK), 1.0, 0.0)

        key_i = lax.broadcasted_iota(I32, (BS, NQ), 0)
        qry_i = lax.broadcasted_iota(I32, (BS, NQ), 1) & (BS - 1)
        s = jnp.where(key_i <= qry_i, scores(mi, 2), NEG)
        m0 = jnp.max(s, axis=0, keepdims=True)
        e = jnp.exp2(s - m0)
        m_scr[...] = m0
        l_scr[...] = jnp.sum(e, axis=0, keepdims=True)
        acc_scr[...] = weighted_values(mi, e)

        def kvblock(j, c2):
            n = mi - j
            s = jnp.where(sel_scr[pl.ds(n, 1), :] > 0.5, scores(n, jnp.where(j == 1, 1, 0)), NEG)
            m_prev = m_scr[...]
            m_new = jnp.maximum(m_prev, jnp.max(s, axis=0, keepdims=True))
            alpha = jnp.exp2(m_prev - m_new)
            e = jnp.exp2(s - m_new)
            l_scr[...] = alpha * l_scr[...] + jnp.sum(e, axis=0, keepdims=True)
            acc_scr[...] = alpha * acc_scr[...] + weighted_values(n, e)
            m_scr[...] = m_new
            return c2

        lax.fori_loop(1, mi + 1, kvblock, 0)
        ot = acc_scr[...] / l_scr[...]
        for pr in range(H // 2):
            pair = ot[:, 2 * BS * pr:2 * BS * (pr + 1)]
            o_ref[pl.ds(r0, BS), LANES * pr:LANES * (pr + 1)] = jnp.concatenate(
                [pair[:, :BS], pair[:, BS:]], axis=0).T
        return carry

    lax.fori_loop(0, NBK, qblock, 0)


def _moba_call(tbl_flat, qb, kb, vbt, B, S):
    BS = MOBA_BLOCK
    NQ = MOBA_Q_HEADS * BS
    NBK = S // BS
    bkt = jnp.asarray(np.ascontiguousarray(_band_buckets(BS).T))
    smem = pl.BlockSpec(memory_space=pltpu.SMEM)
    return pl.pallas_call(
        _moba_kernel,
        grid=(B,),
        in_specs=[smem,
                  pl.BlockSpec(bkt.shape, lambda b: (0, 0)),
                  pl.BlockSpec((S, qb.shape[1]), lambda b: (b, 0)),
                  pl.BlockSpec((S, LANES), lambda b: (b, 0)),
                  pl.BlockSpec((NBK, LANES, BS), lambda b: (b, 0, 0))],
        out_specs=pl.BlockSpec((S, qb.shape[1]), lambda b: (b, 0)),
        out_shape=jax.ShapeDtypeStruct(qb.shape, F32),
        scratch_shapes=[pltpu.VMEM((3, BS, NQ), F32),
                        pltpu.VMEM((2, BF16_SUBLANES, LANES), BF16),
                        pltpu.VMEM((NQ, LANES), BF16),
                        pltpu.VMEM((BF16_SUBLANES, NQ), F32),
                        pltpu.VMEM((1, NQ), F32),
                        pltpu.VMEM((1, NQ), F32),
                        pltpu.VMEM((HEAD_DIM, NQ), F32)],
        compiler_params=pltpu.CompilerParams(
            dimension_semantics=("arbitrary",), vmem_limit_bytes=VMEM_LIMIT),
        name="moba",
    )(tbl_flat, bkt, qb, kb, vbt)


def _outproj_kernel(oa_ref, ob_ref, x_ref, ga_ref, gb_ref, woa_ref, wob_ref, fg_ref,
                    wrh_ref, wrl_ref, rb_ref, x1_ref, hf_ref, eid_ref, wt_ref):
    na = _rmsnorm(oa_ref[...], ga_ref[...]).astype(BF16)
    nb = _rmsnorm(ob_ref[...], gb_ref[...]).astype(BF16)
    y = (jnp.dot(na, woa_ref[...], preferred_element_type=F32)
         + jnp.dot(nb, wob_ref[...], preferred_element_type=F32))
    x1 = x_ref[...] + y
    x1_ref[...] = x1
    hf = _rmsnorm(x1, fg_ref[...])
    _to_token_tiles(hf_ref, hf)

    hf_hi = hf.astype(BF16)
    hf_lo = (hf - hf_hi.astype(F32)).astype(BF16)
    lt = (_dot_nt(wrh_ref[...], hf_hi) + _dot_nt(wrh_ref[...], hf_lo)
          + _dot_nt(wrl_ref[...], hf_hi) + rb_ref[...])
    gl = [lt[j:j + 1, :] for j in range(N_GROUPS)]
    best = gl[0]
    gsel = jnp.zeros(best.shape, I32)
    for j in range(1, N_GROUPS):
        better = gl[j] > best
        gsel = jnp.where(better, j, gsel)
        best = jnp.where(better, gl[j], best)
    gsum = jnp.zeros(best.shape, F32)
    for j in range(N_GROUPS):
        gsum = gsum + jnp.exp(gl[j] - best)
    gw = 1.0 / gsum
    E = EXPERTS_PER_GROUP
    es = lt[EXPERT_ROW0:EXPERT_ROW0 + E, :]
    for j in range(1, N_GROUPS):
        es = jnp.where(gsel == j, lt[EXPERT_ROW0 + E * j:EXPERT_ROW0 + E * (j + 1), :], es)
    row = lax.broadcasted_iota(I32, es.shape, 0)
    v1 = jnp.max(es, axis=0, keepdims=True)
    i1 = jnp.min(jnp.where(es == v1, row, E), axis=0, keepdims=True)
    es2 = jnp.where(row == i1, -jnp.inf, es)
    v2 = jnp.max(es2, axis=0, keepdims=True)
    i2 = jnp.min(jnp.where(es2 == v2, row, E), axis=0, keepdims=True)
    e2 = jnp.exp(v2 - v1)
    den = 1.0 + e2
    eid_ref[...] = jnp.concatenate([gsel * E + i1, gsel * E + i2], axis=0)
    wt_ref[...] = jnp.concatenate([gw * (1.0 / den), gw * (e2 / den)], axis=0)


def _outproj_call(oa, ob, x2d, ga, gb, woa, wob, fg, wrh, wrl, rb):
    T, D = x2d.shape
    tm = TOKEN_TILE
    tok = lambda w: pl.BlockSpec((tm, w), lambda i: (i, 0))
    full = lambda a: pl.BlockSpec(a.shape, lambda i: (0, 0))
    col = pl.BlockSpec((EXPERT_TOPK, tm), lambda i: (0, i))
    return pl.pallas_call(
        _outproj_kernel,
        grid=(T // tm,),
        in_specs=[tok(oa.shape[1]), tok(ob.shape[1]), tok(D), full(ga), full(gb), full(woa), full(wob),
                  full(fg), full(wrh), full(wrl), full(rb)],
        out_specs=[tok(D), pl.BlockSpec((tm * ROW_TILE, LANES), lambda i: (i, 0)), col, col],
        out_shape=[jax.ShapeDtypeStruct((T, D), F32), jax.ShapeDtypeStruct((T * ROW_TILE, LANES), F32),
                   jax.ShapeDtypeStruct((EXPERT_TOPK, T), I32), jax.ShapeDtypeStruct((EXPERT_TOPK, T), F32)],
        compiler_params=pltpu.CompilerParams(
            dimension_semantics=("arbitrary",), vmem_limit_bytes=VMEM_LIMIT),
        name="outproj",
    )(oa, ob, x2d, ga, gb, woa, wob, fg, wrh, wrl, rb)


COPY_GROUP = 32


def _moe_kernel(be_ref, nv_ref, src_ref, srcn_ref, dst_ref, hf_hbm, wg_ref, wu_ref, wd_ref, out_hbm,
                xbuf, obuf, wgb, wub, wdb, gsem, ssem):
    i = pl.program_id(0)
    nblk = pl.num_programs(0)
    M = xbuf.shape[1] // ROW_TILE
    slot = i % 2
    nv_cur = nv_ref[i]
    nv_next = jnp.where(i + 1 < nblk, nv_ref[jnp.minimum(i + 1, nblk - 1)], 0)

    def for_real_groups(n_real, fn):
        for k in range(M // COPY_GROUP):
            @pl.when(k * COPY_GROUP < n_real)
            def _group():
                fn(range(k * COPY_GROUP, (k + 1) * COPY_GROUP))

    def gather_rows(idx_ref, buf_slot, n_real):
        def start(rows):
            for r in rows:
                tok_row = pl.multiple_of(idx_ref[0, 0, r] * ROW_TILE, ROW_TILE)
                pltpu.make_async_copy(hf_hbm.at[pl.ds(tok_row, ROW_TILE)],
                                      xbuf.at[buf_slot, pl.ds(r * ROW_TILE, ROW_TILE)],
                                      gsem.at[buf_slot]).start(priority=r % DMA_PRIORITIES)
        for_real_groups(n_real, start)

    def scatter_rows(idx_ref, buf_slot, n_real):
        def start(rows):
            for r in rows:
                dst_row = pl.multiple_of(idx_ref[0, 0, r] * ROW_TILE, ROW_TILE)
                pltpu.make_async_copy(obuf.at[buf_slot, pl.ds(r * ROW_TILE, ROW_TILE)],
                                      out_hbm.at[pl.ds(dst_row, ROW_TILE)],
                                      ssem.at[buf_slot]).start(priority=r % DMA_PRIORITIES)
        for_real_groups(n_real, start)

    def wait_rows(buf, buf_slot, sem, n_real):
        span = COPY_GROUP * ROW_TILE
        for_real_groups(n_real, lambda rows: pltpu.make_async_copy(
            hf_hbm.at[pl.ds(0, span)], buf.at[buf_slot, pl.ds(0, span)], sem.at[buf_slot]).wait())

    @pl.when(nv_cur > 0)
    def _compute():
        @pl.when(i == 0)
        def _first_gather():
            gather_rows(src_ref, 0, nv_cur)
            obuf[1] = jnp.zeros(obuf.shape[1:], F32)
            spare = pltpu.make_async_copy(
                obuf.at[1], out_hbm.at[pl.ds(out_hbm.shape[0] - M * ROW_TILE, M * ROW_TILE)], ssem.at[1])
            spare.start()
            spare.wait()

        @pl.when((i == 0) | (be_ref[i] != be_ref[jnp.maximum(i - 1, 0)]))
        def _cast_weights():
            wgb[...] = wg_ref[0].astype(BF16)
            wub[...] = wu_ref[0].astype(BF16)
            wdb[...] = wd_ref[0].astype(BF16)

        wait_rows(xbuf, slot, gsem, nv_cur)

        @pl.when(i >= 2)
        def _reuse_obuf():
            wait_rows(obuf, slot, ssem, nv_ref[jnp.maximum(i - 2, 0)])

        gather_rows(srcn_ref, 1 - slot, nv_next)
        xb = _from_token_tiles(xbuf.at[slot], M).astype(BF16)
        a = jnp.dot(xb, wgb[...], preferred_element_type=F32)
        u = jnp.dot(xb, wub[...], preferred_element_type=F32)
        hmid = (a * jax.nn.sigmoid(a)) * u
        _to_token_tiles(obuf.at[slot], jnp.dot(hmid.astype(BF16), wdb[...], preferred_element_type=F32))
        scatter_rows(dst_ref, slot, nv_cur)

        @pl.when(nv_next == 0)
        def _drain():
            wait_rows(obuf, slot, ssem, nv_cur)

            @pl.when(i >= 1)
            def _drain_prev():
                wait_rows(obuf, 1 - slot, ssem, nv_ref[jnp.maximum(i - 1, 0)])


def _moe_call(block_expert, n_real, src_tok, dst_row, hf, w_g, w_u, w_d):
    T = hf.shape[0] // ROW_TILE
    M = DISPATCH_BLOCK
    nblk = block_expert.shape[0]
    D, DE = w_g.shape[1], w_g.shape[2]
    assert D == ROW_TILE * LANES and hf.shape[1] == LANES and M % COPY_GROUP == 0
    idx_spec = lambda f: pl.BlockSpec((1, 1, M), f, memory_space=pltpu.SMEM)
    cur = lambda i, be, nu: (i, 0, 0)
    nxt = lambda i, be, nu: (jnp.minimum(i + 1, nblk - 1), 0, 0)
    wspec = lambda a, b: pl.BlockSpec((1, a, b), lambda i, be, nu: (be[i], 0, 0))
    grid_spec = pltpu.PrefetchScalarGridSpec(
        num_scalar_prefetch=2,
        grid=(nblk,),
        in_specs=[idx_spec(cur), idx_spec(nxt), idx_spec(cur),
                  pl.BlockSpec(memory_space=pl.ANY),
                  wspec(D, DE), wspec(D, DE), wspec(DE, D)],
        out_specs=pl.BlockSpec(memory_space=pl.ANY),
        scratch_shapes=[pltpu.VMEM((2, M * ROW_TILE, LANES), F32), pltpu.VMEM((2, M * ROW_TILE, LANES), F32),
                        pltpu.VMEM((D, DE), BF16), pltpu.VMEM((D, DE), BF16), pltpu.VMEM((DE, D), BF16),
                        pltpu.SemaphoreType.DMA((2,)), pltpu.SemaphoreType.DMA((2,))],
    )
    src3 = src_tok.reshape(nblk, 1, M)
    return pl.pallas_call(
        _moe_kernel,
        grid_spec=grid_spec,
        out_shape=jax.ShapeDtypeStruct(((EXPERT_TOPK * T + M) * ROW_TILE, LANES), F32),
        compiler_params=pltpu.CompilerParams(
            dimension_semantics=("arbitrary",), vmem_limit_bytes=VMEM_LIMIT),
        name="moe",
    )(block_expert, n_real, src3, src3, dst_row.reshape(nblk, 1, M), hf, w_g, w_u, w_d)


def _ple_kernel(x1_ref, m0_ref, m1_ref, wt_ref, p_ref, pg_ref, wgate_ref, wproj_ref, pog_ref, out_ref):
    proj = _rmsnorm(jnp.dot(p_ref[...].astype(BF16), wproj_ref[...], preferred_element_type=F32),
                    pog_ref[...])
    tm = x1_ref.shape[0]
    moe = (_from_token_tiles(m0_ref, tm) * wt_ref[:, 0:1] + _from_token_tiles(m1_ref, tm) * wt_ref[:, 1:2])
    x2 = x1_ref[...] + moe
    hn = _rmsnorm(x2, pg_ref[...]).astype(BF16)
    gate = jax.nn.sigmoid(jnp.dot(hn, wgate_ref[...], preferred_element_type=F32))
    out_ref[...] = x2 + gate * proj


def _ple_call(moe_rows, x1, wt_cols, p2d, pg, wgate, wproj, pog):
    T, D = x1.shape
    tm = PLE_TILE
    tok = lambda w: pl.BlockSpec((tm, w), lambda i: (i, 0))
    full = lambda a: pl.BlockSpec(a.shape, lambda i: (0, 0))
    first = pl.BlockSpec((tm * ROW_TILE, LANES), lambda i: (i, 0))
    second = pl.BlockSpec((tm * ROW_TILE, LANES), lambda i: (T // tm + i, 0))
    return pl.pallas_call(
        _ple_kernel,
        grid=(T // tm,),
        in_specs=[tok(D), first, second, tok(EXPERT_TOPK), tok(p2d.shape[1]),
                  full(pg), full(wgate), full(wproj), full(pog)],
        out_specs=tok(D),
        out_shape=jax.ShapeDtypeStruct((T, D), F32),
        compiler_params=pltpu.CompilerParams(
            dimension_semantics=("arbitrary",), vmem_limit_bytes=VMEM_LIMIT),
        name="ple",
    )(x1, moe_rows, moe_rows, wt_cols, p2d, pg, wgate, wproj, pog)


def _dispatch_plan(eid, T):
    A = EXPERT_TOPK * T
    M = DISPATCH_BLOCK
    nblk = A // M + N_EXPERTS
    assert A <= PAD_MARK
    e_flat = eid.reshape(A)
    experts = jnp.arange(N_EXPERTS, dtype=I32)
    counts = jnp.sum((e_flat[:, None] == experts[None, :]).astype(I32), axis=0)
    padded = (counts + M - 1) // M * M
    n_used = jnp.sum(padded) // M
    row = jnp.arange(M, dtype=I32)[None, :]
    pad_keys = jnp.where(row < (padded - counts)[:, None], 2 * experts[:, None] + 1, 2 * N_EXPERTS)
    keys = jnp.concatenate([2 * e_flat, pad_keys.reshape(-1)])
    vals = jnp.concatenate([jnp.arange(A, dtype=I32), jnp.full((N_EXPERTS * M,), PAD_MARK, I32)])
    packed = jnp.sort(keys * (PAD_MARK + 1) + vals).reshape(nblk, M)
    a_s = packed & PAD_MARK
    blk = jnp.arange(nblk, dtype=I32)
    be = jnp.minimum(packed[:, 0] // (2 * (PAD_MARK + 1)), N_EXPERTS - 1)
    be = jnp.where(blk < n_used, be, be[jnp.maximum(n_used - 1, 0)])
    valid = a_s != PAD_MARK
    src_tok = jnp.where(valid, jnp.where(a_s >= T, a_s - T, a_s), 0)
    dst_row = jnp.where(valid, a_s, A + row)
    return be, jnp.sum(valid.astype(I32), axis=1), src_tok, dst_row


def _layer(x2d, p2d, rel_bias, attn_norm, w_in, swa_q_norm, swa_k_norm, swa_sinks, moba_q_norm,
           moba_k_norm, swa_out_norm, moba_out_norm, w_out, ffn_norm, w_rg, b_rg, w_re, b_re,
           w_g, w_u, w_d, ple_norm, w_ple_gate, w_ple_proj, ple_out_norm, B, S):
    T, D = x2d.shape
    row = lambda v: v.reshape(1, -1).astype(F32)
    head_gain = jnp.concatenate([
        jnp.tile(swa_q_norm, SWA_Q_HEADS), jnp.tile(swa_k_norm, SWA_KV_HEADS),
        jnp.tile(moba_q_norm, MOBA_Q_HEADS), jnp.tile(moba_k_norm, MOBA_KV_HEADS)])
    w_in_b = w_in.astype(BF16)
    qa_w, kv_w = SWA_Q_HEADS * HEAD_DIM, SWA_KV_HEADS * HEAD_DIM
    qb_w = MOBA_Q_HEADS * HEAD_DIM
    c_va = qa_w + kv_w
    c_qb = c_va + kv_w
    c_vb = c_qb + qb_w + kv_w
    w_qk = jnp.concatenate([w_in_b[:, :c_va], w_in_b[:, c_qb:c_vb]], axis=1)
    w_vt = jnp.concatenate([w_in_b[:, c_va:c_qb], w_in_b[:, c_vb:]], axis=1).T
    qa, ka, qb, kb, vat, vbt = _qkv_call(x2d, row(attn_norm), w_qk, w_vt, row(head_gain))

    tbl_flat = rel_bias.astype(F32).reshape(-1)
    oa = _swa_call(tbl_flat, swa_sinks.astype(F32), qa, ka, vat, B, S)
    ob = _moba_call(tbl_flat, qb, kb, vbt, B, S)

    pad_rows = lambda n: jnp.zeros((n, D), F32)
    wr = jnp.concatenate([w_rg.T, pad_rows(EXPERT_ROW0 - N_GROUPS), w_re.T,
                          pad_rows(ROUTER_ROWS - EXPERT_ROW0 - N_EXPERTS)], axis=0)
    wr_hi = wr.astype(BF16)
    wr_lo = (wr - wr_hi.astype(F32)).astype(BF16)
    rb = jnp.concatenate([b_rg, jnp.zeros((EXPERT_ROW0 - N_GROUPS,), F32), b_re,
                          jnp.zeros((ROUTER_ROWS - EXPERT_ROW0 - N_EXPERTS,), F32)]).reshape(ROUTER_ROWS, 1)
    wo = w_out.astype(BF16)
    na_w = SWA_Q_HEADS * HEAD_DIM
    x1, hf, eid, wts = _outproj_call(oa, ob, x2d, row(swa_out_norm), row(moba_out_norm),
                                     wo[:na_w], wo[na_w:], row(ffn_norm), wr_hi, wr_lo, rb)

    be, n_real, src_tok, dst_row = _dispatch_plan(eid, T)
    moe_rows = _moe_call(be, n_real, src_tok, dst_row, hf, w_g, w_u, w_d)
    return _ple_call(moe_rows, x1, wts.T, p2d, row(ple_norm), w_ple_gate.astype(BF16),
                     w_ple_proj.astype(BF16), row(ple_out_norm))


def kernel(x, p, rel_bias, attn_norm, w_in, swa_q_norm, swa_k_norm, swa_sinks, moba_q_norm, moba_k_norm,
           swa_out_norm, moba_out_norm, w_out, ffn_norm, w_router_group, b_router_group, w_router_expert,
           b_router_expert, w_exp_gate, w_exp_up, w_exp_down, ple_norm, w_ple_gate, w_ple_proj, ple_out_norm):
    B, S, D = x.shape
    x2d = x.reshape(B * S, D)
    for i in range(p.shape[0]):
        x2d = _layer(x2d, p[i].reshape(B * S, -1), rel_bias, attn_norm[i], w_in[i], swa_q_norm[i],
                     swa_k_norm[i], swa_sinks[i], moba_q_norm[i], moba_k_norm[i], swa_out_norm[i],
                     moba_out_norm[i], w_out[i], ffn_norm[i], w_router_group[i], b_router_group[i],
                     w_router_expert[i], b_router_expert[i], w_exp_gate[i], w_exp_up[i], w_exp_down[i],
                     ple_norm[i], w_ple_gate[i], w_ple_proj[i], ple_out_norm[i], B, S)
    return x2d.reshape(B, S, D)
```

```python
import math

import numpy as np
import jax
import jax.numpy as jnp
from jax import lax
from jax.experimental import pallas as pl
from jax.experimental.pallas import tpu as pltpu

F32 = jnp.float32
BF16 = jnp.bfloat16
I32 = jnp.int32

HEAD_DIM = 64
SWA_Q_HEADS = 8
SWA_KV_HEADS = 2
SWA_WINDOW = 128
MOBA_Q_HEADS = 8
MOBA_KV_HEADS = 2
MOBA_BLOCK = 256
MOBA_TOPK = 3
N_HEADS = SWA_Q_HEADS + MOBA_Q_HEADS
REL_BUCKETS = 32
REL_MAX_DIST = 128
N_GROUPS = 4
EXPERTS_PER_GROUP = 8
N_EXPERTS = N_GROUPS * EXPERTS_PER_GROUP
EXPERT_TOPK = 2
DISPATCH_BLOCK = 512
PAD_MARK = 0xFFFF
EPS = 1e-6
NEG = -1e30
LOG2E = math.log2(math.e)

DMA_PRIORITIES = 2
LANES = 128
ROW_TILE = 8
BF16_SUBLANES = 16
GROUP_HEADS = 4
GROUP_W = GROUP_HEADS * HEAD_DIM
ROUTER_ROWS = 128
EXPERT_ROW0 = 8
TOKEN_TILE = 1024
PLE_TILE = 1024
VMEM_LIMIT = 52 * 1024 * 1024


def _dot_nt(a, b):
    return lax.dot_general(a, b, (((1,), (1,)), ((), ())), preferred_element_type=F32)


def _rmsnorm(x, g):
    ms = jnp.mean(x * x, axis=-1, keepdims=True)
    return x * lax.rsqrt(ms + EPS) * g


def _to_token_tiles(ref, x):
    m = x.shape[0]
    for j in range(x.shape[1] // LANES):
        ref[pl.ds(j, m, stride=ROW_TILE), :] = x[:, LANES * j:LANES * (j + 1)]


def _from_token_tiles(ref, m):
    return jnp.concatenate([ref[pl.ds(j, m, stride=ROW_TILE), :] for j in range(ref.shape[0] // m)], axis=1)


def _rel_bucket_np(dist):
    n = np.maximum(dist, 0)
    exact = REL_BUCKETS // 2
    nf = np.maximum(n, 1).astype(np.float32)
    large = exact + (np.log(nf / exact) / math.log(REL_MAX_DIST / exact)
                     * (REL_BUCKETS - exact)).astype(np.int32)
    return np.where(n < exact, n, np.minimum(large, REL_BUCKETS - 1)).astype(np.int32)


def _band_buckets(block):
    qi = np.arange(block)[:, None]
    kj = np.arange(2 * block)[None, :]
    return _rel_bucket_np(qi + block - kj)


def _bias_from_buckets(bkt, tbl_ref, head):
    acc = jnp.zeros(bkt.shape, F32)
    for j in range(REL_BUCKETS):
        acc = jnp.where(bkt == j, tbl_ref[j * N_HEADS + head] * LOG2E, acc)
    return acc


def _qkv_kernel(x_ref, g_ref, w_ref, wvt_ref, hg_ref, qa_ref, ka_ref, qb_ref, kb_ref, vat_ref, vbt_ref):
    h = _rmsnorm(x_ref[...], g_ref[...]).astype(BF16)
    acc = jnp.dot(h, w_ref[...], preferred_element_type=F32)
    lo = lax.broadcasted_iota(I32, (1, LANES), 1) < HEAD_DIM

    def head_normed(c, scale):
        blk = acc[:, LANES * c:LANES * (c + 1)]
        sq = blk * blk
        s_lo = jnp.sum(jnp.where(lo, sq, 0.0), axis=-1, keepdims=True)
        s_hi = jnp.sum(jnp.where(lo, 0.0, sq), axis=-1, keepdims=True)
        inv = jnp.where(lo, lax.rsqrt(s_lo / HEAD_DIM + EPS), lax.rsqrt(s_hi / HEAD_DIM + EPS))
        return blk * inv * hg_ref[:, LANES * c:LANES * (c + 1)] * scale

    scale = HEAD_DIM ** -0.5 * LOG2E
    for c in range(4):
        qa_ref[:, LANES * c:LANES * (c + 1)] = head_normed(c, scale).astype(BF16)
        qb_ref[:, LANES * c:LANES * (c + 1)] = head_normed(5 + c, scale).astype(BF16)
    ka_ref[...] = head_normed(4, 1.0).astype(BF16)
    kb_ref[...] = head_normed(9, 1.0).astype(BF16)
    vt = _dot_nt(wvt_ref[...], h).astype(BF16)
    for j in range(vat_ref.shape[0]):
        vat_ref[j] = vt[:LANES, SWA_WINDOW * j:SWA_WINDOW * (j + 1)]
    for j in range(vbt_ref.shape[0]):
        vbt_ref[j] = vt[LANES:, MOBA_BLOCK * j:MOBA_BLOCK * (j + 1)]


def _qkv_call(x2d, attn_g, w_qk, w_vt, head_gain):
    T, D = x2d.shape
    tm = TOKEN_TILE
    tok = lambda w: pl.BlockSpec((tm, w), lambda i: (i, 0))
    full = lambda a: pl.BlockSpec(a.shape, lambda i: (0, 0))
    out_w = (SWA_Q_HEADS * HEAD_DIM, LANES, MOBA_Q_HEADS * HEAD_DIM, LANES)
    slabs = lambda blk: pl.BlockSpec((tm // blk, LANES, blk), lambda i: (i, 0, 0))
    slab_shape = lambda blk: jax.ShapeDtypeStruct((T // blk, LANES, blk), BF16)
    return pl.pallas_call(
        _qkv_kernel,
        grid=(T // tm,),
        in_specs=[tok(D), full(attn_g), full(w_qk), full(w_vt), full(head_gain)],
        out_specs=[tok(w) for w in out_w] + [slabs(SWA_WINDOW), slabs(MOBA_BLOCK)],
        out_shape=[jax.ShapeDtypeStruct((T, w), BF16) for w in out_w]
        + [slab_shape(SWA_WINDOW), slab_shape(MOBA_BLOCK)],
        compiler_params=pltpu.CompilerParams(
            dimension_semantics=("arbitrary",), vmem_limit_bytes=VMEM_LIMIT),
        name="qkv",
    )(x2d, attn_g, w_qk, w_vt, head_gain)


def _padded_heads(qblk, heads):
    zeros = jnp.zeros((qblk.shape[0], HEAD_DIM), qblk.dtype)
    pieces = []
    for h in range(heads):
        piece = qblk[:, HEAD_DIM * h:HEAD_DIM * (h + 1)]
        pieces.append(jnp.concatenate([piece, zeros] if h < GROUP_HEADS else [zeros, piece], axis=1))
    return jnp.concatenate(pieces, axis=0)


def _swa_kernel(tbl_ref, sink_ref, bkt_ref, q_ref, k_ref, vt_ref, o_ref, bias_scr):
    W = SWA_WINDOW
    H = SWA_Q_HEADS
    S = q_ref.shape[0]

    @pl.when(pl.program_id(0) == 0)
    def _init_bias():
        bkt = bkt_ref[...]
        for h in range(H):
            bias_scr[:, W * h:W * (h + 1)] = _bias_from_buckets(bkt, tbl_ref, h)

    key_i = lax.broadcasted_iota(I32, (2 * W, H * W), 0)
    qry_i = lax.broadcasted_iota(I32, (2 * W, H * W), 1) & (W - 1)
    dist = qry_i + W - key_i
    band = (dist >= 0) & (dist < W)
    own_part = key_i >= W
    head_of_lane = lax.broadcasted_iota(I32, (1, H * W), 1) // W
    sinks = jnp.zeros((1, H * W), F32)
    for h in range(H):
        sinks = jnp.where(head_of_lane == h, sink_ref[h] * LOG2E, sinks)

    def body(i, carry):
        r0 = pl.multiple_of(i * W, W)
        prev = jnp.maximum(i - 1, 0)
        p0 = pl.multiple_of(prev * W, W)
        q8 = _padded_heads(q_ref[pl.ds(r0, W), :], H)
        kband = jnp.concatenate([k_ref[pl.ds(p0, W), :], k_ref[pl.ds(r0, W), :]], axis=0)
        mask = band & (own_part | (i > 0))
        s = jnp.where(mask, _dot_nt(kband, q8) + bias_scr[...], NEG)
        m = jnp.maximum(jnp.max(s, axis=0, keepdims=True), sinks)
        e = jnp.exp2(s - m)
        den = jnp.sum(e, axis=0, keepdims=True) + jnp.exp2(sinks - m)
        vtband = jnp.concatenate([vt_ref[prev], vt_ref[i]], axis=1)
        ot = jnp.dot(vtband, e.astype(BF16), preferred_element_type=F32) / den
        for pr in range(H // 2):
            f0 = HEAD_DIM * (2 * pr // GROUP_HEADS)
            pair = ot[f0:f0 + HEAD_DIM, 2 * W * pr:2 * W * (pr + 1)]
            o_ref[pl.ds(r0, W), LANES * pr:LANES * (pr + 1)] = jnp.concatenate(
                [pair[:, :W], pair[:, W:]], axis=0).T
        return carry

    lax.fori_loop(0, S // W, body, 0)


def _swa_call(tbl_flat, sinks, qa, ka, vat, B, S):
    W = SWA_WINDOW
    bkt = jnp.asarray(np.ascontiguousarray(_band_buckets(W).T))
    smem = pl.BlockSpec(memory_space=pltpu.SMEM)
    return pl.pallas_call(
        _swa_kernel,
        grid=(B,),
        in_specs=[smem, smem,
                  pl.BlockSpec(bkt.shape, lambda b: (0, 0)),
                  pl.BlockSpec((S, qa.shape[1]), lambda b: (b, 0)),
                  pl.BlockSpec((S, LANES), lambda b: (b, 0)),
                  pl.BlockSpec((S // W, LANES, W), lambda b: (b, 0, 0))],
        out_specs=pl.BlockSpec((S, qa.shape[1]), lambda b: (b, 0)),
        out_shape=jax.ShapeDtypeStruct(qa.shape, F32),
        scratch_shapes=[pltpu.VMEM((2 * W, SWA_Q_HEADS * W), F32)],
        compiler_params=pltpu.CompilerParams(
            dimension_semantics=("arbitrary",), vmem_limit_bytes=VMEM_LIMIT),
        name="swa",
    )(tbl_flat, sinks, bkt, qa, ka, vat)


def _moba_kernel(tbl_ref, bkt_ref, q_ref, k_ref, vt_ref, o_ref,
                 bias_scr, km_scr, q8_scr, sel_scr, m_scr, l_scr, acc_scr):
    BS = MOBA_BLOCK
    H = MOBA_Q_HEADS
    S = q_ref.shape[0]
    NBK = S // BS
    NQ = H * BS
    GQ = GROUP_HEADS * BS

    @pl.when(pl.program_id(0) == 0)
    def _init_bias():
        bkt = bkt_ref[...]
        for h in range(H):
            band = _bias_from_buckets(bkt, tbl_ref, SWA_Q_HEADS + h)
            far = tbl_ref[(REL_BUCKETS - 1) * N_HEADS + SWA_Q_HEADS + h] * LOG2E
            bias_scr[0, :, BS * h:BS * (h + 1)] = jnp.full((BS, BS), far, F32)
            bias_scr[1, :, BS * h:BS * (h + 1)] = band[:BS]
            bias_scr[2, :, BS * h:BS * (h + 1)] = band[BS:]

    blk_row = lax.broadcasted_iota(I32, (BF16_SUBLANES, 1), 0)
    kmean = jnp.sum(k_ref[...].astype(F32).reshape(NBK, BS, LANES), axis=1) / BS
    kmean = jnp.concatenate([kmean, jnp.zeros((BF16_SUBLANES - NBK, LANES), F32)], axis=0)
    km_hi = kmean.astype(BF16)
    km_scr[0] = km_hi
    km_scr[1] = (kmean - km_hi.astype(F32)).astype(BF16)

    def scores(n, case):
        k0 = pl.multiple_of(n * BS, BS)
        return _dot_nt(k_ref[pl.ds(k0, BS), :], q8_scr[...]) + bias_scr[case]

    def weighted_values(n, e):
        pv = jnp.dot(vt_ref[n], e.astype(BF16), preferred_element_type=F32)
        return jnp.concatenate([pv[:HEAD_DIM, :GQ], pv[HEAD_DIM:, GQ:]], axis=1)

    def qblock(mi, carry):
        r0 = pl.multiple_of(mi * BS, BS)
        q8 = _padded_heads(q_ref[pl.ds(r0, BS), :], H)
        q8_scr[...] = q8
        gate = _dot_nt(km_scr[0], q8) + _dot_nt(km_scr[1], q8)
        past = blk_row < mi
        gm = jnp.where(past, gate, NEG)
        cnt = jnp.zeros(gm.shape, F32)
        for n in range(NBK - 1):
            col = gm[n:n + 1, :]
            beats = (gm > col) | ((gm == col) & (blk_row < n))
            c = jnp.sum(jnp.where(beats, 1.0, 0.0), axis=0, keepdims=True)
            cnt = jnp.where(blk_row == n, c, cnt)
        sel_scr[...] = jnp.where(past & (cnt < MOBA_TOPK), 1.0, 0.0)

        key_i = lax.broadcasted_iota(I32, (BS, NQ), 0)
        qry_i = lax.broadcasted_iota(I32, (BS, NQ), 1) & (BS - 1)
        s = jnp.where(key_i <= qry_i, scores(mi, 2), NEG)
        m0 = jnp.max(s, axis=0, keepdims=True)
        e = jnp.exp2(s - m0)
        m_scr[...] = m0
        l_scr[...] = jnp.sum(e, axis=0, keepdims=True)
        acc_scr[...] = weighted_values(mi, e)

        def kvblock(j, c2):
            n = mi - j
            s = jnp.where(sel_scr[pl.ds(n, 1), :] > 0.5, scores(n, jnp.where(j == 1, 1, 0)), NEG)
            m_prev = m_scr[...]
            m_new = jnp.maximum(m_prev, jnp.max(s, axis=0, keepdims=True))
            alpha = jnp.exp2(m_prev - m_new)
            e = jnp.exp2(s - m_new)
            l_scr[...] = alpha * l_scr[...] + jnp.sum(e, axis=0, keepdims=True)
            acc_scr[...] = alpha * acc_scr[...] + weighted_values(n, e)
            m_scr[...] = m_new
            return c2

        lax.fori_loop(1, mi + 1, kvblock, 0)
        ot = acc_scr[...] / l_scr[...]
        for pr in range(H // 2):
            pair = ot[:, 2 * BS * pr:2 * BS * (pr + 1)]
            o_ref[pl.ds(r0, BS), LANES * pr:LANES * (pr + 1)] = jnp.concatenate(
                [pair[:, :BS], pair[:, BS:]], axis=0).T
        return carry

    lax.fori_loop(0, NBK, qblock, 0)


def _moba_call(tbl_flat, qb, kb, vbt, B, S):
    BS = MOBA_BLOCK
    NQ = MOBA_Q_HEADS * BS
    NBK = S // BS
    bkt = jnp.asarray(np.ascontiguousarray(_band_buckets(BS).T))
    smem = pl.BlockSpec(memory_space=pltpu.SMEM)
    return pl.pallas_call(
        _moba_kernel,
        grid=(B,),
        in_specs=[smem,
                  pl.BlockSpec(bkt.shape, lambda b: (0, 0)),
                  pl.BlockSpec((S, qb.shape[1]), lambda b: (b, 0)),
                  pl.BlockSpec((S, LANES), lambda b: (b, 0)),
                  pl.BlockSpec((NBK, LANES, BS), lambda b: (b, 0, 0))],
        out_specs=pl.BlockSpec((S, qb.shape[1]), lambda b: (b, 0)),
        out_shape=jax.ShapeDtypeStruct(qb.shape, F32),
        scratch_shapes=[pltpu.VMEM((3, BS, NQ), F32),
                        pltpu.VMEM((2, BF16_SUBLANES, LANES), BF16),
                        pltpu.VMEM((NQ, LANES), BF16),
                        pltpu.VMEM((BF16_SUBLANES, NQ), F32),
                        pltpu.VMEM((1, NQ), F32),
                        pltpu.VMEM((1, NQ), F32),
                        pltpu.VMEM((HEAD_DIM, NQ), F32)],
        compiler_params=pltpu.CompilerParams(
            dimension_semantics=("arbitrary",), vmem_limit_bytes=VMEM_LIMIT),
        name="moba",
    )(tbl_flat, bkt, qb, kb, vbt)


def _outproj_kernel(oa_ref, ob_ref, x_ref, ga_ref, gb_ref, woa_ref, wob_ref, fg_ref,
                    wrh_ref, wrl_ref, rb_ref, x1_ref, hf_ref, eid_ref, wt_ref):
    na = _rmsnorm(oa_ref[...], ga_ref[...]).astype(BF16)
    nb = _rmsnorm(ob_ref[...], gb_ref[...]).astype(BF16)
    y = (jnp.dot(na, woa_ref[...], preferred_element_type=F32)
         + jnp.dot(nb, wob_ref[...], preferred_element_type=F32))
    x1 = x_ref[...] + y
    x1_ref[...] = x1
    hf = _rmsnorm(x1, fg_ref[...])
    _to_token_tiles(hf_ref, hf)

    hf_hi = hf.astype(BF16)
    hf_lo = (hf - hf_hi.astype(F32)).astype(BF16)
    lt = (_dot_nt(wrh_ref[...], hf_hi) + _dot_nt(wrh_ref[...], hf_lo)
          + _dot_nt(wrl_ref[...], hf_hi) + rb_ref[...])
    gl = [lt[j:j + 1, :] for j in range(N_GROUPS)]
    best = gl[0]
    gsel = jnp.zeros(best.shape, I32)
    for j in range(1, N_GROUPS):
        better = gl[j] > best
        gsel = jnp.where(better, j, gsel)
        best = jnp.where(better, gl[j], best)
    gsum = jnp.zeros(best.shape, F32)
    for j in range(N_GROUPS):
        gsum = gsum + jnp.exp(gl[j] - best)
    gw = 1.0 / gsum
    E = EXPERTS_PER_GROUP
    es = lt[EXPERT_ROW0:EXPERT_ROW0 + E, :]
    for j in range(1, N_GROUPS):
        es = jnp.where(gsel == j, lt[EXPERT_ROW0 + E * j:EXPERT_ROW0 + E * (j + 1), :], es)
    row = lax.broadcasted_iota(I32, es.shape, 0)
    v1 = jnp.max(es, axis=0, keepdims=True)
    i1 = jnp.min(jnp.where(es == v1, row, E), axis=0, keepdims=True)
    es2 = jnp.where(row == i1, -jnp.inf, es)
    v2 = jnp.max(es2, axis=0, keepdims=True)
    i2 = jnp.min(jnp.where(es2 == v2, row, E), axis=0, keepdims=True)
    e2 = jnp.exp(v2 - v1)
    den = 1.0 + e2
    eid_ref[...] = jnp.concatenate([gsel * E + i1, gsel * E + i2], axis=0)
    wt_ref[...] = jnp.concatenate([gw * (1.0 / den), gw * (e2 / den)], axis=0)


def _outproj_call(oa, ob, x2d, ga, gb, woa, wob, fg, wrh, wrl, rb):
    T, D = x2d.shape
    tm = TOKEN_TILE
    tok = lambda w: pl.BlockSpec((tm, w), lambda i: (i, 0))
    full = lambda a: pl.BlockSpec(a.shape, lambda i: (0, 0))
    col = pl.BlockSpec((EXPERT_TOPK, tm), lambda i: (0, i))
    return pl.pallas_call(
        _outproj_kernel,
        grid=(T // tm,),
        in_specs=[tok(oa.shape[1]), tok(ob.shape[1]), tok(D), full(ga), full(gb), full(woa), full(wob),
                  full(fg), full(wrh), full(wrl), full(rb)],
        out_specs=[tok(D), pl.BlockSpec((tm * ROW_TILE, LANES), lambda i: (i, 0)), col, col],
        out_shape=[jax.ShapeDtypeStruct((T, D), F32), jax.ShapeDtypeStruct((T * ROW_TILE, LANES), F32),
                   jax.ShapeDtypeStruct((EXPERT_TOPK, T), I32), jax.ShapeDtypeStruct((EXPERT_TOPK, T), F32)],
        compiler_params=pltpu.CompilerParams(
            dimension_semantics=("arbitrary",), vmem_limit_bytes=VMEM_LIMIT),
        name="outproj",
    )(oa, ob, x2d, ga, gb, woa, wob, fg, wrh, wrl, rb)


COPY_GROUP = 32


def _moe_kernel(be_ref, nv_ref, src_ref, srcn_ref, dst_ref, hf_hbm, wg_ref, wu_ref, wd_ref, out_hbm,
                xbuf, obuf, gsem, ssem):
    i = pl.program_id(0)
    nblk = pl.num_programs(0)
    M = xbuf.shape[1] // ROW_TILE
    slot = i % 2
    nv_cur = nv_ref[i]
    nv_next = jnp.where(i + 1 < nblk, nv_ref[jnp.minimum(i + 1, nblk - 1)], 0)

    def for_real_groups(n_real, fn):
        for k in range(M // COPY_GROUP):
            @pl.when(k * COPY_GROUP < n_real)
            def _group():
                fn(range(k * COPY_GROUP, (k + 1) * COPY_GROUP))

    def gather_rows(idx_ref, buf_slot, n_real):
        def start(rows):
            for r in rows:
                tok_row = pl.multiple_of(idx_ref[0, 0, r] * ROW_TILE, ROW_TILE)
                pltpu.make_async_copy(hf_hbm.at[pl.ds(tok_row, ROW_TILE)],
                                      xbuf.at[buf_slot, pl.ds(r * ROW_TILE, ROW_TILE)],
                                      gsem.at[buf_slot]).start(priority=r % DMA_PRIORITIES)
        for_real_groups(n_real, start)

    def scatter_rows(idx_ref, buf_slot, n_real):
        def start(rows):
            for r in rows:
                dst_row = pl.multiple_of(idx_ref[0, 0, r] * ROW_TILE, ROW_TILE)
                pltpu.make_async_copy(obuf.at[buf_slot, pl.ds(r * ROW_TILE, ROW_TILE)],
                                      out_hbm.at[pl.ds(dst_row, ROW_TILE)],
                                      ssem.at[buf_slot]).start(priority=r % DMA_PRIORITIES)
        for_real_groups(n_real, start)

    def wait_rows(buf, buf_slot, sem, n_real):
        span = COPY_GROUP * ROW_TILE
        for_real_groups(n_real, lambda rows: pltpu.make_async_copy(
            hf_hbm.at[pl.ds(0, span)], buf.at[buf_slot, pl.ds(0, span)], sem.at[buf_slot]).wait())

    @pl.when(nv_cur > 0)
    def _compute():
        @pl.when(i == 0)
        def _first_gather():
            gather_rows(src_ref, 0, nv_cur)
            obuf[1] = jnp.zeros(obuf.shape[1:], F32)
            spare = pltpu.make_async_copy(
                obuf.at[1], out_hbm.at[pl.ds(out_hbm.shape[0] - M * ROW_TILE, M * ROW_TILE)], ssem.at[1])
            spare.start()
            spare.wait()

        wait_rows(xbuf, slot, gsem, nv_cur)

        @pl.when(i >= 2)
        def _reuse_obuf():
            wait_rows(obuf, slot, ssem, nv_ref[jnp.maximum(i - 2, 0)])

        gather_rows(srcn_ref, 1 - slot, nv_next)
        xb = _from_token_tiles(xbuf.at[slot], M).astype(BF16)
        a = jnp.dot(xb, wg_ref[0].astype(BF16), preferred_element_type=F32)
        u = jnp.dot(xb, wu_ref[0].astype(BF16), preferred_element_type=F32)
        hmid = (a * jax.nn.sigmoid(a)) * u
        _to_token_tiles(obuf.at[slot], jnp.dot(hmid.astype(BF16), wd_ref[0].astype(BF16),
                                               preferred_element_type=F32))
        scatter_rows(dst_ref, slot, nv_cur)

        @pl.when(nv_next == 0)
        def _drain():
            wait_rows(obuf, slot, ssem, nv_cur)

            @pl.when(i >= 1)
            def _drain_prev():
                wait_rows(obuf, 1 - slot, ssem, nv_ref[jnp.maximum(i - 1, 0)])


def _moe_call(block_expert, n_real, src_tok, dst_row, hf, w_g, w_u, w_d):
    T = hf.shape[0] // ROW_TILE
    M = DISPATCH_BLOCK
    nblk = block_expert.shape[0]
    D, DE = w_g.shape[1], w_g.shape[2]
    assert D == ROW_TILE * LANES and hf.shape[1] == LANES and M % COPY_GROUP == 0
    idx_spec = lambda f: pl.BlockSpec((1, 1, M), f, memory_space=pltpu.SMEM)
    cur = lambda i, be, nu: (i, 0, 0)
    nxt = lambda i, be, nu: (jnp.minimum(i + 1, nblk - 1), 0, 0)
    wspec = lambda a, b: pl.BlockSpec((1, a, b), lambda i, be, nu: (be[i], 0, 0))
    grid_spec = pltpu.PrefetchScalarGridSpec(
        num_scalar_prefetch=2,
        grid=(nblk,),
        in_specs=[idx_spec(cur), idx_spec(nxt), idx_spec(cur),
                  pl.BlockSpec(memory_space=pl.ANY),
                  wspec(D, DE), wspec(D, DE), wspec(DE, D)],
        out_specs=pl.BlockSpec(memory_space=pl.ANY),
        scratch_shapes=[pltpu.VMEM((2, M * ROW_TILE, LANES), F32), pltpu.VMEM((2, M * ROW_TILE, LANES), F32),
                        pltpu.SemaphoreType.DMA((2,)), pltpu.SemaphoreType.DMA((2,))],
    )
    src3 = src_tok.reshape(nblk, 1, M)
    return pl.pallas_call(
        _moe_kernel,
        grid_spec=grid_spec,
        out_shape=jax.ShapeDtypeStruct(((EXPERT_TOPK * T + M) * ROW_TILE, LANES), F32),
        compiler_params=pltpu.CompilerParams(
            dimension_semantics=("arbitrary",), vmem_limit_bytes=VMEM_LIMIT),
        name="moe",
    )(block_expert, n_real, src3, src3, dst_row.reshape(nblk, 1, M), hf, w_g, w_u, w_d)


def _ple_kernel(x1_ref, m0_ref, m1_ref, wt_ref, p_ref, pg_ref, wgate_ref, wproj_ref, pog_ref, out_ref):
    proj = _rmsnorm(jnp.dot(p_ref[...].astype(BF16), wproj_ref[...], preferred_element_type=F32),
                    pog_ref[...])
    tm = x1_ref.shape[0]
    moe = (_from_token_tiles(m0_ref, tm) * wt_ref[:, 0:1] + _from_token_tiles(m1_ref, tm) * wt_ref[:, 1:2])
    x2 = x1_ref[...] + moe
    hn = _rmsnorm(x2, pg_ref[...]).astype(BF16)
    gate = jax.nn.sigmoid(jnp.dot(hn, wgate_ref[...], preferred_element_type=F32))
    out_ref[...] = x2 + gate * proj


def _ple_call(moe_rows, x1, wt_cols, p2d, pg, wgate, wproj, pog):
    T, D = x1.shape
    tm = PLE_TILE
    tok = lambda w: pl.BlockSpec((tm, w), lambda i: (i, 0))
    full = lambda a: pl.BlockSpec(a.shape, lambda i: (0, 0))
    first = pl.BlockSpec((tm * ROW_TILE, LANES), lambda i: (i, 0))
    second = pl.BlockSpec((tm * ROW_TILE, LANES), lambda i: (T // tm + i, 0))
    return pl.pallas_call(
        _ple_kernel,
        grid=(T // tm,),
        in_specs=[tok(D), first, second, tok(EXPERT_TOPK), tok(p2d.shape[1]),
                  full(pg), full(wgate), full(wproj), full(pog)],
        out_specs=tok(D),
        out_shape=jax.ShapeDtypeStruct((T, D), F32),
        compiler_params=pltpu.CompilerParams(
            dimension_semantics=("arbitrary",), vmem_limit_bytes=VMEM_LIMIT),
        name="ple",
    )(x1, moe_rows, moe_rows, wt_cols, p2d, pg, wgate, wproj, pog)


def _dispatch_plan(eid, T):
    A = EXPERT_TOPK * T
    M = DISPATCH_BLOCK
    nblk = A // M + N_EXPERTS
    assert A <= PAD_MARK
    e_flat = eid.reshape(A)
    experts = jnp.arange(N_EXPERTS, dtype=I32)
    counts = jnp.sum((e_flat[:, None] == experts[None, :]).astype(I32), axis=0)
    padded = (counts + M - 1) // M * M
    n_used = jnp.sum(padded) // M
    row = jnp.arange(M, dtype=I32)[None, :]
    pad_keys = jnp.where(row < (padded - counts)[:, None], 2 * experts[:, None] + 1, 2 * N_EXPERTS)
    keys = jnp.concatenate([2 * e_flat, pad_keys.reshape(-1)])
    vals = jnp.concatenate([jnp.arange(A, dtype=I32), jnp.full((N_EXPERTS * M,), PAD_MARK, I32)])
    packed = jnp.sort(keys * (PAD_MARK + 1) + vals).reshape(nblk, M)
    a_s = packed & PAD_MARK
    blk = jnp.arange(nblk, dtype=I32)
    be = jnp.minimum(packed[:, 0] // (2 * (PAD_MARK + 1)), N_EXPERTS - 1)
    be = jnp.where(blk < n_used, be, be[jnp.maximum(n_used - 1, 0)])
    valid = a_s != PAD_MARK
    src_tok = jnp.where(valid, jnp.where(a_s >= T, a_s - T, a_s), 0)
    dst_row = jnp.where(valid, a_s, A + row)
    return be, jnp.sum(valid.astype(I32), axis=1), src_tok, dst_row


def _layer(x2d, p2d, rel_bias, attn_norm, w_in, swa_q_norm, swa_k_norm, swa_sinks, moba_q_norm,
           moba_k_norm, swa_out_norm, moba_out_norm, w_out, ffn_norm, w_rg, b_rg, w_re, b_re,
           w_g, w_u, w_d, ple_norm, w_ple_gate, w_ple_proj, ple_out_norm, B, S):
    T, D = x2d.shape
    row = lambda v: v.reshape(1, -1).astype(F32)
    head_gain = jnp.concatenate([
        jnp.tile(swa_q_norm, SWA_Q_HEADS), jnp.tile(swa_k_norm, SWA_KV_HEADS),
        jnp.tile(moba_q_norm, MOBA_Q_HEADS), jnp.tile(moba_k_norm, MOBA_KV_HEADS)])
    w_in_b = w_in.astype(BF16)
    qa_w, kv_w = SWA_Q_HEADS * HEAD_DIM, SWA_KV_HEADS * HEAD_DIM
    qb_w = MOBA_Q_HEADS * HEAD_DIM
    c_va = qa_w + kv_w
    c_qb = c_va + kv_w
    c_vb = c_qb + qb_w + kv_w
    w_qk = jnp.concatenate([w_in_b[:, :c_va], w_in_b[:, c_qb:c_vb]], axis=1)
    w_vt = jnp.concatenate([w_in_b[:, c_va:c_qb], w_in_b[:, c_vb:]], axis=1).T
    qa, ka, qb, kb, vat, vbt = _qkv_call(x2d, row(attn_norm), w_qk, w_vt, row(head_gain))

    tbl_flat = rel_bias.astype(F32).reshape(-1)
    oa = _swa_call(tbl_flat, swa_sinks.astype(F32), qa, ka, vat, B, S)
    ob = _moba_call(tbl_flat, qb, kb, vbt, B, S)

    pad_rows = lambda n: jnp.zeros((n, D), F32)
    wr = jnp.concatenate([w_rg.T, pad_rows(EXPERT_ROW0 - N_GROUPS), w_re.T,
                          pad_rows(ROUTER_ROWS - EXPERT_ROW0 - N_EXPERTS)], axis=0)
    wr_hi = wr.astype(BF16)
    wr_lo = (wr - wr_hi.astype(F32)).astype(BF16)
    rb = jnp.concatenate([b_rg, jnp.zeros((EXPERT_ROW0 - N_GROUPS,), F32), b_re,
                          jnp.zeros((ROUTER_ROWS - EXPERT_ROW0 - N_EXPERTS,), F32)]).reshape(ROUTER_ROWS, 1)
    wo = w_out.astype(BF16)
    na_w = SWA_Q_HEADS * HEAD_DIM
    x1, hf, eid, wts = _outproj_call(oa, ob, x2d, row(swa_out_norm), row(moba_out_norm),
                                     wo[:na_w], wo[na_w:], row(ffn_norm), wr_hi, wr_lo, rb)

    be, n_real, src_tok, dst_row = _dispatch_plan(eid, T)
    moe_rows = _moe_call(be, n_real, src_tok, dst_row, hf, w_g, w_u, w_d)
    return _ple_call(moe_rows, x1, wts.T, p2d, row(ple_norm), w_ple_gate.astype(BF16),
                     w_ple_proj.astype(BF16), row(ple_out_norm))


def kernel(x, p, rel_bias, attn_norm, w_in, swa_q_norm, swa_k_norm, swa_sinks, moba_q_norm, moba_k_norm,
           swa_out_norm, moba_out_norm, w_out, ffn_norm, w_router_group, b_router_group, w_router_expert,
           b_router_expert, w_exp_gate, w_exp_up, w_exp_down, ple_norm, w_ple_gate, w_ple_proj, ple_out_norm):
    B, S, D = x.shape
    x2d = x.reshape(B * S, D)
    for i in range(p.shape[0]):
        x2d = _layer(x2d, p[i].reshape(B * S, -1), rel_bias, attn_norm[i], w_in[i], swa_q_norm[i],
                     swa_k_norm[i], swa_sinks[i], moba_q_norm[i], moba_k_norm[i], swa_out_norm[i],
                     moba_out_norm[i], w_out[i], ffn_norm[i], w_router_group[i], b_router_group[i],
                     w_router_expert[i], b_router_expert[i], w_exp_gate[i], w_exp_up[i], w_exp_down[i],
                     ple_norm[i], w_ple_gate[i], w_ple_proj[i], ple_out_norm[i], B, S)
    return x2d.reshape(B, S, D)
```

```python
import math

import numpy as np
import jax
import jax.numpy as jnp
from jax import lax
from jax.experimental import pallas as pl
from jax.experimental.pallas import tpu as pltpu

F32 = jnp.float32
BF16 = jnp.bfloat16
I32 = jnp.int32

HEAD_DIM = 64
SWA_Q_HEADS = 8
SWA_KV_HEADS = 2
SWA_WINDOW = 128
MOBA_Q_HEADS = 8
MOBA_KV_HEADS = 2
MOBA_BLOCK = 256
MOBA_TOPK = 3
N_HEADS = SWA_Q_HEADS + MOBA_Q_HEADS
REL_BUCKETS = 32
REL_MAX_DIST = 128
N_GROUPS = 4
EXPERTS_PER_GROUP = 8
N_EXPERTS = N_GROUPS * EXPERTS_PER_GROUP
EXPERT_TOPK = 2
DISPATCH_BLOCK = 512
PAD_MARK = 0xFFFF
EPS = 1e-6
NEG = -1e30
LOG2E = math.log2(math.e)

DMA_PRIORITIES = 2
LANES = 128
ROW_TILE = 8
BF16_SUBLANES = 16
GROUP_HEADS = 4
GROUP_W = GROUP_HEADS * HEAD_DIM
ROUTER_ROWS = 128
EXPERT_ROW0 = 8
TOKEN_TILE = 1024
PLE_TILE = 1024
VMEM_LIMIT = 52 * 1024 * 1024


def _dot_nt(a, b):
    return lax.dot_general(a, b, (((1,), (1,)), ((), ())), preferred_element_type=F32)


def _rmsnorm(x, g):
    ms = jnp.mean(x * x, axis=-1, keepdims=True)
    return x * lax.rsqrt(ms + EPS) * g


def _to_token_tiles(ref, x):
    m = x.shape[0]
    for j in range(x.shape[1] // LANES):
        ref[pl.ds(j, m, stride=ROW_TILE), :] = x[:, LANES * j:LANES * (j + 1)]


def _from_token_tiles(ref, m):
    return jnp.concatenate([ref[pl.ds(j, m, stride=ROW_TILE), :] for j in range(ref.shape[0] // m)], axis=1)


def _rel_bucket_np(dist):
    n = np.maximum(dist, 0)
    exact = REL_BUCKETS // 2
    nf = np.maximum(n, 1).astype(np.float32)
    large = exact + (np.log(nf / exact) / math.log(REL_MAX_DIST / exact)
                     * (REL_BUCKETS - exact)).astype(np.int32)
    return np.where(n < exact, n, np.minimum(large, REL_BUCKETS - 1)).astype(np.int32)


def _band_buckets(block):
    qi = np.arange(block)[:, None]
    kj = np.arange(2 * block)[None, :]
    return _rel_bucket_np(qi + block - kj)


def _bias_from_buckets(bkt, tbl_ref, head):
    acc = jnp.zeros(bkt.shape, F32)
    for j in range(REL_BUCKETS):
        acc = jnp.where(bkt == j, tbl_ref[j * N_HEADS + head] * LOG2E, acc)
    return acc


def _qkv_kernel(x_ref, g_ref, w_ref, wvt_ref, hg_ref, qa_ref, ka_ref, qb_ref, kb_ref, vat_ref, vbt_ref):
    h = _rmsnorm(x_ref[...], g_ref[...]).astype(BF16)
    acc = jnp.dot(h, w_ref[...], preferred_element_type=F32)
    lo = lax.broadcasted_iota(I32, (1, LANES), 1) < HEAD_DIM

    def head_normed(c, scale):
        blk = acc[:, LANES * c:LANES * (c + 1)]
        sq = blk * blk
        s_lo = jnp.sum(jnp.where(lo, sq, 0.0), axis=-1, keepdims=True)
        s_hi = jnp.sum(jnp.where(lo, 0.0, sq), axis=-1, keepdims=True)
        inv = jnp.where(lo, lax.rsqrt(s_lo / HEAD_DIM + EPS), lax.rsqrt(s_hi / HEAD_DIM + EPS))
        return blk * inv * hg_ref[:, LANES * c:LANES * (c + 1)] * scale

    scale = HEAD_DIM ** -0.5 * LOG2E
    for c in range(4):
        qa_ref[:, LANES * c:LANES * (c + 1)] = head_normed(c, scale).astype(BF16)
        qb_ref[:, LANES * c:LANES * (c + 1)] = head_normed(5 + c, scale).astype(BF16)
    ka_ref[...] = head_normed(4, 1.0).astype(BF16)
    kb_ref[...] = head_normed(9, 1.0).astype(BF16)
    vt = _dot_nt(wvt_ref[...], h).astype(BF16)
    for j in range(vat_ref.shape[0]):
        vat_ref[j] = vt[:LANES, SWA_WINDOW * j:SWA_WINDOW * (j + 1)]
    for j in range(vbt_ref.shape[0]):
        vbt_ref[j] = vt[LANES:, MOBA_BLOCK * j:MOBA_BLOCK * (j + 1)]


def _qkv_call(x2d, attn_g, w_qk, w_vt, head_gain):
    T, D = x2d.shape
    tm = TOKEN_TILE
    tok = lambda w: pl.BlockSpec((tm, w), lambda i: (i, 0))
    full = lambda a: pl.BlockSpec(a.shape, lambda i: (0, 0))
    out_w = (SWA_Q_HEADS * HEAD_DIM, LANES, MOBA_Q_HEADS * HEAD_DIM, LANES)
    slabs = lambda blk: pl.BlockSpec((tm // blk, LANES, blk), lambda i: (i, 0, 0))
    slab_shape = lambda blk: jax.ShapeDtypeStruct((T // blk, LANES, blk), BF16)
    return pl.pallas_call(
        _qkv_kernel,
        grid=(T // tm,),
        in_specs=[tok(D), full(attn_g), full(w_qk), full(w_vt), full(head_gain)],
        out_specs=[tok(w) for w in out_w] + [slabs(SWA_WINDOW), slabs(MOBA_BLOCK)],
        out_shape=[jax.ShapeDtypeStruct((T, w), BF16) for w in out_w]
        + [slab_shape(SWA_WINDOW), slab_shape(MOBA_BLOCK)],
        compiler_params=pltpu.CompilerParams(
            dimension_semantics=("arbitrary",), vmem_limit_bytes=VMEM_LIMIT),
        name="qkv",
    )(x2d, attn_g, w_qk, w_vt, head_gain)


def _padded_heads(qblk, heads):
    zeros = jnp.zeros((qblk.shape[0], HEAD_DIM), qblk.dtype)
    pieces = []
    for h in range(heads):
        piece = qblk[:, HEAD_DIM * h:HEAD_DIM * (h + 1)]
        pieces.append(jnp.concatenate([piece, zeros] if h < GROUP_HEADS else [zeros, piece], axis=1))
    return jnp.concatenate(pieces, axis=0)


def _swa_kernel(tbl_ref, sink_ref, bkt_ref, q_ref, k_ref, vt_ref, o_ref, bias_scr):
    W = SWA_WINDOW
    H = SWA_Q_HEADS
    S = q_ref.shape[0]

    @pl.when(pl.program_id(0) == 0)
    def _init_bias():
        bkt = bkt_ref[...]
        for h in range(H):
            bias_scr[:, W * h:W * (h + 1)] = _bias_from_buckets(bkt, tbl_ref, h)

    key_i = lax.broadcasted_iota(I32, (2 * W, H * W), 0)
    qry_i = lax.broadcasted_iota(I32, (2 * W, H * W), 1) & (W - 1)
    dist = qry_i + W - key_i
    band = (dist >= 0) & (dist < W)
    own_part = key_i >= W
    head_of_lane = lax.broadcasted_iota(I32, (1, H * W), 1) // W
    sinks = jnp.zeros((1, H * W), F32)
    for h in range(H):
        sinks = jnp.where(head_of_lane == h, sink_ref[h] * LOG2E, sinks)
    ones_rows = jnp.ones((BF16_SUBLANES, 2 * W), BF16)

    def body(i, carry):
        r0 = pl.multiple_of(i * W, W)
        prev = jnp.maximum(i - 1, 0)
        p0 = pl.multiple_of(prev * W, W)
        q8 = _padded_heads(q_ref[pl.ds(r0, W), :], H)
        kband = jnp.concatenate([k_ref[pl.ds(p0, W), :], k_ref[pl.ds(r0, W), :]], axis=0)
        mask = band & (own_part | (i > 0))
        s = jnp.where(mask, _dot_nt(kband, q8) + bias_scr[...], NEG)
        m = jnp.maximum(jnp.max(s, axis=0, keepdims=True), sinks)
        e = jnp.exp2(s - m)
        vtband = jnp.concatenate([jnp.concatenate([vt_ref[prev], vt_ref[i]], axis=1), ones_rows], axis=0)
        pv = jnp.dot(vtband, e.astype(BF16), preferred_element_type=F32)
        den = pv[LANES:LANES + 1, :] + jnp.exp2(sinks - m)
        ot = pv[:LANES, :] / den
        for pr in range(H // 2):
            f0 = HEAD_DIM * (2 * pr // GROUP_HEADS)
            pair = ot[f0:f0 + HEAD_DIM, 2 * W * pr:2 * W * (pr + 1)]
            o_ref[pl.ds(r0, W), LANES * pr:LANES * (pr + 1)] = jnp.concatenate(
                [pair[:, :W], pair[:, W:]], axis=0).T
        return carry

    lax.fori_loop(0, S // W, body, 0)


def _swa_call(tbl_flat, sinks, qa, ka, vat, B, S):
    W = SWA_WINDOW
    bkt = jnp.asarray(np.ascontiguousarray(_band_buckets(W).T))
    smem = pl.BlockSpec(memory_space=pltpu.SMEM)
    return pl.pallas_call(
        _swa_kernel,
        grid=(B,),
        in_specs=[smem, smem,
                  pl.BlockSpec(bkt.shape, lambda b: (0, 0)),
                  pl.BlockSpec((S, qa.shape[1]), lambda b: (b, 0)),
                  pl.BlockSpec((S, LANES), lambda b: (b, 0)),
                  pl.BlockSpec((S // W, LANES, W), lambda b: (b, 0, 0))],
        out_specs=pl.BlockSpec((S, qa.shape[1]), lambda b: (b, 0)),
        out_shape=jax.ShapeDtypeStruct(qa.shape, F32),
        scratch_shapes=[pltpu.VMEM((2 * W, SWA_Q_HEADS * W), F32)],
        compiler_params=pltpu.CompilerParams(
            dimension_semantics=("arbitrary",), vmem_limit_bytes=VMEM_LIMIT),
        name="swa",
    )(tbl_flat, sinks, bkt, qa, ka, vat)


def _moba_kernel(tbl_ref, bkt_ref, q_ref, k_ref, vt_ref, o_ref,
                 bias_scr, km_scr, q8_scr, sel_scr, m_scr, l_scr, acc_scr):
    BS = MOBA_BLOCK
    H = MOBA_Q_HEADS
    S = q_ref.shape[0]
    NBK = S // BS
    NQ = H * BS
    GQ = GROUP_HEADS * BS

    @pl.when(pl.program_id(0) == 0)
    def _init_bias():
        bkt = bkt_ref[...]
        for h in range(H):
            band = _bias_from_buckets(bkt, tbl_ref, SWA_Q_HEADS + h)
            far = tbl_ref[(REL_BUCKETS - 1) * N_HEADS + SWA_Q_HEADS + h] * LOG2E
            bias_scr[0, :, BS * h:BS * (h + 1)] = jnp.full((BS, BS), far, F32)
            bias_scr[1, :, BS * h:BS * (h + 1)] = band[:BS]
            bias_scr[2, :, BS * h:BS * (h + 1)] = band[BS:]

    blk_row = lax.broadcasted_iota(I32, (BF16_SUBLANES, 1), 0)
    kmean = jnp.sum(k_ref[...].astype(F32).reshape(NBK, BS, LANES), axis=1) / BS
    kmean = jnp.concatenate([kmean, jnp.zeros((BF16_SUBLANES - NBK, LANES), F32)], axis=0)
    km_hi = kmean.astype(BF16)
    km_scr[0] = km_hi
    km_scr[1] = (kmean - km_hi.astype(F32)).astype(BF16)

    def scores(n, case):
        k0 = pl.multiple_of(n * BS, BS)
        return _dot_nt(k_ref[pl.ds(k0, BS), :], q8_scr[...]) + bias_scr[case]

    ones_rows = jnp.ones((BF16_SUBLANES, BS), BF16)

    def weighted_values(n, e):
        vt_ext = jnp.concatenate([vt_ref[n], ones_rows], axis=0)
        pv = jnp.dot(vt_ext, e.astype(BF16), preferred_element_type=F32)
        vals = jnp.concatenate([pv[:HEAD_DIM, :GQ], pv[HEAD_DIM:LANES, GQ:]], axis=1)
        return vals, pv[LANES:LANES + 1, :]

    def qblock(mi, carry):
        r0 = pl.multiple_of(mi * BS, BS)
        q8 = _padded_heads(q_ref[pl.ds(r0, BS), :], H)
        q8_scr[...] = q8
        gate = _dot_nt(km_scr[0], q8) + _dot_nt(km_scr[1], q8)
        past = blk_row < mi
        gm = jnp.where(past, gate, NEG)
        cnt = jnp.zeros(gm.shape, F32)
        for n in range(NBK - 1):
            col = gm[n:n + 1, :]
            beats = (gm > col) | ((gm == col) & (blk_row < n))
            c = jnp.sum(jnp.where(beats, 1.0, 0.0), axis=0, keepdims=True)
            cnt = jnp.where(blk_row == n, c, cnt)
        sel_scr[...] = jnp.where(past & (cnt < MOBA_TOPK), 1.0, 0.0)

        key_i = lax.broadcasted_iota(I32, (BS, NQ), 0)
        qry_i = lax.broadcasted_iota(I32, (BS, NQ), 1) & (BS - 1)
        s = jnp.where(key_i <= qry_i, scores(mi, 2), NEG)
        m0 = jnp.max(s, axis=0, keepdims=True)
        e = jnp.exp2(s - m0)
        m_scr[...] = m0
        acc_scr[...], l_scr[...] = weighted_values(mi, e)

        def kvblock(j, c2):
            n = mi - j
            s = jnp.where(sel_scr[pl.ds(n, 1), :] > 0.5, scores(n, jnp.where(j == 1, 1, 0)), NEG)
            m_prev = m_scr[...]
            m_new = jnp.maximum(m_prev, jnp.max(s, axis=0, keepdims=True))
            alpha = jnp.exp2(m_prev - m_new)
            e = jnp.exp2(s - m_new)
            vals, row_sum = weighted_values(n, e)
            l_scr[...] = alpha * l_scr[...] + row_sum
            acc_scr[...] = alpha * acc_scr[...] + vals
            m_scr[...] = m_new
            return c2

        lax.fori_loop(1, mi + 1, kvblock, 0)
        ot = acc_scr[...] / l_scr[...]
        for pr in range(H // 2):
            pair = ot[:, 2 * BS * pr:2 * BS * (pr + 1)]
            o_ref[pl.ds(r0, BS), LANES * pr:LANES * (pr + 1)] = jnp.concatenate(
                [pair[:, :BS], pair[:, BS:]], axis=0).T
        return carry

    lax.fori_loop(0, NBK, qblock, 0)


def _moba_call(tbl_flat, qb, kb, vbt, B, S):
    BS = MOBA_BLOCK
    NQ = MOBA_Q_HEADS * BS
    NBK = S // BS
    bkt = jnp.asarray(np.ascontiguousarray(_band_buckets(BS).T))
    smem = pl.BlockSpec(memory_space=pltpu.SMEM)
    return pl.pallas_call(
        _moba_kernel,
        grid=(B,),
        in_specs=[smem,
                  pl.BlockSpec(bkt.shape, lambda b: (0, 0)),
                  pl.BlockSpec((S, qb.shape[1]), lambda b: (b, 0)),
                  pl.BlockSpec((S, LANES), lambda b: (b, 0)),
                  pl.BlockSpec((NBK, LANES, BS), lambda b: (b, 0, 0))],
        out_specs=pl.BlockSpec((S, qb.shape[1]), lambda b: (b, 0)),
        out_shape=jax.ShapeDtypeStruct(qb.shape, F32),
        scratch_shapes=[pltpu.VMEM((3, BS, NQ), F32),
                        pltpu.VMEM((2, BF16_SUBLANES, LANES), BF16),
                        pltpu.VMEM((NQ, LANES), BF16),
                        pltpu.VMEM((BF16_SUBLANES, NQ), F32),
                        pltpu.VMEM((1, NQ), F32),
                        pltpu.VMEM((1, NQ), F32),
                        pltpu.VMEM((HEAD_DIM, NQ), F32)],
        compiler_params=pltpu.CompilerParams(
            dimension_semantics=("arbitrary",), vmem_limit_bytes=VMEM_LIMIT),
        name="moba",
    )(tbl_flat, bkt, qb, kb, vbt)


def _outproj_kernel(oa_ref, ob_ref, x_ref, ga_ref, gb_ref, woa_ref, wob_ref, fg_ref,
                    wrh_ref, wrl_ref, rb_ref, x1_ref, hf_ref, eid_ref, wt_ref):
    na = _rmsnorm(oa_ref[...], ga_ref[...]).astype(BF16)
    nb = _rmsnorm(ob_ref[...], gb_ref[...]).astype(BF16)
    y = (jnp.dot(na, woa_ref[...], preferred_element_type=F32)
         + jnp.dot(nb, wob_ref[...], preferred_element_type=F32))
    x1 = x_ref[...] + y
    x1_ref[...] = x1
    hf = _rmsnorm(x1, fg_ref[...])
    _to_token_tiles(hf_ref, hf)

    hf_hi = hf.astype(BF16)
    hf_lo = (hf - hf_hi.astype(F32)).astype(BF16)
    lt = (_dot_nt(wrh_ref[...], hf_hi) + _dot_nt(wrh_ref[...], hf_lo)
          + _dot_nt(wrl_ref[...], hf_hi) + rb_ref[...])
    gl = [lt[j:j + 1, :] for j in range(N_GROUPS)]
    best = gl[0]
    gsel = jnp.zeros(best.shape, I32)
    for j in range(1, N_GROUPS):
        better = gl[j] > best
        gsel = jnp.where(better, j, gsel)
        best = jnp.where(better, gl[j], best)
    gsum = jnp.zeros(best.shape, F32)
    for j in range(N_GROUPS):
        gsum = gsum + jnp.exp(gl[j] - best)
    gw = 1.0 / gsum
    E = EXPERTS_PER_GROUP
    es = lt[EXPERT_ROW0:EXPERT_ROW0 + E, :]
    for j in range(1, N_GROUPS):
        es = jnp.where(gsel == j, lt[EXPERT_ROW0 + E * j:EXPERT_ROW0 + E * (j + 1), :], es)
    row = lax.broadcasted_iota(I32, es.shape, 0)
    v1 = jnp.max(es, axis=0, keepdims=True)
    i1 = jnp.min(jnp.where(es == v1, row, E), axis=0, keepdims=True)
    es2 = jnp.where(row == i1, -jnp.inf, es)
    v2 = jnp.max(es2, axis=0, keepdims=True)
    i2 = jnp.min(jnp.where(es2 == v2, row, E), axis=0, keepdims=True)
    e2 = jnp.exp(v2 - v1)
    den = 1.0 + e2
    eid_ref[...] = jnp.concatenate([gsel * E + i1, gsel * E + i2], axis=0)
    wt_ref[...] = jnp.concatenate([gw * (1.0 / den), gw * (e2 / den)], axis=0)


def _outproj_call(oa, ob, x2d, ga, gb, woa, wob, fg, wrh, wrl, rb):
    T, D = x2d.shape
    tm = TOKEN_TILE
    tok = lambda w: pl.BlockSpec((tm, w), lambda i: (i, 0))
    full = lambda a: pl.BlockSpec(a.shape, lambda i: (0, 0))
    col = pl.BlockSpec((EXPERT_TOPK, tm), lambda i: (0, i))
    return pl.pallas_call(
        _outproj_kernel,
        grid=(T // tm,),
        in_specs=[tok(oa.shape[1]), tok(ob.shape[1]), tok(D), full(ga), full(gb), full(woa), full(wob),
                  full(fg), full(wrh), full(wrl), full(rb)],
        out_specs=[tok(D), pl.BlockSpec((tm * ROW_TILE, LANES), lambda i: (i, 0)), col, col],
        out_shape=[jax.ShapeDtypeStruct((T, D), F32), jax.ShapeDtypeStruct((T * ROW_TILE, LANES), F32),
                   jax.ShapeDtypeStruct((EXPERT_TOPK, T), I32), jax.ShapeDtypeStruct((EXPERT_TOPK, T), F32)],
        compiler_params=pltpu.CompilerParams(
            dimension_semantics=("arbitrary",), vmem_limit_bytes=VMEM_LIMIT),
        name="outproj",
    )(oa, ob, x2d, ga, gb, woa, wob, fg, wrh, wrl, rb)


COPY_GROUP = 32


def _moe_kernel(be_ref, nv_ref, src_ref, srcn_ref, dst_ref, hf_hbm, wg_ref, wu_ref, wd_ref, out_hbm,
                xbuf, obuf, gsem, ssem):
    i = pl.program_id(0)
    nblk = pl.num_programs(0)
    M = xbuf.shape[1] // ROW_TILE
    slot = i % 2
    nv_cur = nv_ref[i]
    nv_next = jnp.where(i + 1 < nblk, nv_ref[jnp.minimum(i + 1, nblk - 1)], 0)

    def for_real_groups(n_real, fn):
        for k in range(M // COPY_GROUP):
            @pl.when(k * COPY_GROUP < n_real)
            def _group():
                fn(range(k * COPY_GROUP, (k + 1) * COPY_GROUP))

    def gather_rows(idx_ref, buf_slot, n_real):
        def start(rows):
            for r in rows:
                tok_row = pl.multiple_of(idx_ref[0, 0, r] * ROW_TILE, ROW_TILE)
                pltpu.make_async_copy(hf_hbm.at[pl.ds(tok_row, ROW_TILE)],
                                      xbuf.at[buf_slot, pl.ds(r * ROW_TILE, ROW_TILE)],
                                      gsem.at[buf_slot]).start(priority=r % DMA_PRIORITIES)
        for_real_groups(n_real, start)

    def scatter_rows(idx_ref, buf_slot, n_real):
        def start(rows):
            for r in rows:
                dst_row = pl.multiple_of(idx_ref[0, 0, r] * ROW_TILE, ROW_TILE)
                pltpu.make_async_copy(obuf.at[buf_slot, pl.ds(r * ROW_TILE, ROW_TILE)],
                                      out_hbm.at[pl.ds(dst_row, ROW_TILE)],
                                      ssem.at[buf_slot]).start(priority=r % DMA_PRIORITIES)
        for_real_groups(n_real, start)

    def wait_rows(buf, buf_slot, sem, n_real):
        span = COPY_GROUP * ROW_TILE
        for_real_groups(n_real, lambda rows: pltpu.make_async_copy(
            hf_hbm.at[pl.ds(0, span)], buf.at[buf_slot, pl.ds(0, span)], sem.at[buf_slot]).wait())

    @pl.when(nv_cur > 0)
    def _compute():
        @pl.when(i == 0)
        def _first_gather():
            gather_rows(src_ref, 0, nv_cur)
            obuf[1] = jnp.zeros(obuf.shape[1:], F32)
            spare = pltpu.make_async_copy(
                obuf.at[1], out_hbm.at[pl.ds(out_hbm.shape[0] - M * ROW_TILE, M * ROW_TILE)], ssem.at[1])
            spare.start()
            spare.wait()

        wait_rows(xbuf, slot, gsem, nv_cur)

        @pl.when(i >= 2)
        def _reuse_obuf():
            wait_rows(obuf, slot, ssem, nv_ref[jnp.maximum(i - 2, 0)])

        gather_rows(srcn_ref, 1 - slot, nv_next)
        xb = _from_token_tiles(xbuf.at[slot], M).astype(BF16)
        a = jnp.dot(xb, wg_ref[0].astype(BF16), preferred_element_type=F32)
        u = jnp.dot(xb, wu_ref[0].astype(BF16), preferred_element_type=F32)
        hmid = (a * jax.nn.sigmoid(a)) * u
        _to_token_tiles(obuf.at[slot], jnp.dot(hmid.astype(BF16), wd_ref[0].astype(BF16),
                                               preferred_element_type=F32))
        scatter_rows(dst_ref, slot, nv_cur)

        @pl.when(nv_next == 0)
        def _drain():
            wait_rows(obuf, slot, ssem, nv_cur)

            @pl.when(i >= 1)
            def _drain_prev():
                wait_rows(obuf, 1 - slot, ssem, nv_ref[jnp.maximum(i - 1, 0)])


def _moe_call(block_expert, n_real, src_tok, dst_row, hf, w_g, w_u, w_d):
    T = hf.shape[0] // ROW_TILE
    M = DISPATCH_BLOCK
    nblk = block_expert.shape[0]
    D, DE = w_g.shape[1], w_g.shape[2]
    assert D == ROW_TILE * LANES and hf.shape[1] == LANES and M % COPY_GROUP == 0
    idx_spec = lambda f: pl.BlockSpec((1, 1, M), f, memory_space=pltpu.SMEM)
    cur = lambda i, be, nu: (i, 0, 0)
    nxt = lambda i, be, nu: (jnp.minimum(i + 1, nblk - 1), 0, 0)
    wspec = lambda a, b: pl.BlockSpec((1, a, b), lambda i, be, nu: (be[i], 0, 0))
    grid_spec = pltpu.PrefetchScalarGridSpec(
        num_scalar_prefetch=2,
        grid=(nblk,),
        in_specs=[idx_spec(cur), idx_spec(nxt), idx_spec(cur),
                  pl.BlockSpec(memory_space=pl.ANY),
                  wspec(D, DE), wspec(D, DE), wspec(DE, D)],
        out_specs=pl.BlockSpec(memory_space=pl.ANY),
        scratch_shapes=[pltpu.VMEM((2, M * ROW_TILE, LANES), F32), pltpu.VMEM((2, M * ROW_TILE, LANES), F32),
                        pltpu.SemaphoreType.DMA((2,)), pltpu.SemaphoreType.DMA((2,))],
    )
    src3 = src_tok.reshape(nblk, 1, M)
    return pl.pallas_call(
        _moe_kernel,
        grid_spec=grid_spec,
        out_shape=jax.ShapeDtypeStruct(((EXPERT_TOPK * T + M) * ROW_TILE, LANES), F32),
        compiler_params=pltpu.CompilerParams(
            dimension_semantics=("arbitrary",), vmem_limit_bytes=VMEM_LIMIT),
        name="moe",
    )(block_expert, n_real, src3, src3, dst_row.reshape(nblk, 1, M), hf, w_g, w_u, w_d)


def _ple_kernel(x1_ref, m0_ref, m1_ref, wt_ref, p_ref, pg_ref, wgate_ref, wproj_ref, pog_ref, out_ref):
    proj = _rmsnorm(jnp.dot(p_ref[...].astype(BF16), wproj_ref[...], preferred_element_type=F32),
                    pog_ref[...])
    tm = x1_ref.shape[0]
    moe = (_from_token_tiles(m0_ref, tm) * wt_ref[:, 0:1] + _from_token_tiles(m1_ref, tm) * wt_ref[:, 1:2])
    x2 = x1_ref[...] + moe
    hn = _rmsnorm(x2, pg_ref[...]).astype(BF16)
    gate = jax.nn.sigmoid(jnp.dot(hn, wgate_ref[...], preferred_element_type=F32))
    out_ref[...] = x2 + gate * proj


def _ple_call(moe_rows, x1, wt_cols, p2d, pg, wgate, wproj, pog):
    T, D = x1.shape
    tm = PLE_TILE
    tok = lambda w: pl.BlockSpec((tm, w), lambda i: (i, 0))
    full = lambda a: pl.BlockSpec(a.shape, lambda i: (0, 0))
    first = pl.BlockSpec((tm * ROW_TILE, LANES), lambda i: (i, 0))
    second = pl.BlockSpec((tm * ROW_TILE, LANES), lambda i: (T // tm + i, 0))
    return pl.pallas_call(
        _ple_kernel,
        grid=(T // tm,),
        in_specs=[tok(D), first, second, tok(EXPERT_TOPK), tok(p2d.shape[1]),
                  full(pg), full(wgate), full(wproj), full(pog)],
        out_specs=tok(D),
        out_shape=jax.ShapeDtypeStruct((T, D), F32),
        compiler_params=pltpu.CompilerParams(
            dimension_semantics=("arbitrary",), vmem_limit_bytes=VMEM_LIMIT),
        name="ple",
    )(x1, moe_rows, moe_rows, wt_cols, p2d, pg, wgate, wproj, pog)


def _dispatch_plan(eid, T):
    A = EXPERT_TOPK * T
    M = DISPATCH_BLOCK
    nblk = A // M + N_EXPERTS
    assert A <= PAD_MARK
    e_flat = eid.reshape(A)
    experts = jnp.arange(N_EXPERTS, dtype=I32)
    counts = jnp.sum((e_flat[:, None] == experts[None, :]).astype(I32), axis=0)
    padded = (counts + M - 1) // M * M
    n_used = jnp.sum(padded) // M
    row = jnp.arange(M, dtype=I32)[None, :]
    pad_keys = jnp.where(row < (padded - counts)[:, None], 2 * experts[:, None] + 1, 2 * N_EXPERTS)
    keys = jnp.concatenate([2 * e_flat, pad_keys.reshape(-1)])
    vals = jnp.concatenate([jnp.arange(A, dtype=I32), jnp.full((N_EXPERTS * M,), PAD_MARK, I32)])
    packed = jnp.sort(keys * (PAD_MARK + 1) + vals).reshape(nblk, M)
    a_s = packed & PAD_MARK
    blk = jnp.arange(nblk, dtype=I32)
    be = jnp.minimum(packed[:, 0] // (2 * (PAD_MARK + 1)), N_EXPERTS - 1)
    be = jnp.where(blk < n_used, be, be[jnp.maximum(n_used - 1, 0)])
    valid = a_s != PAD_MARK
    src_tok = jnp.where(valid, jnp.where(a_s >= T, a_s - T, a_s), 0)
    dst_row = jnp.where(valid, a_s, A + row)
    return be, jnp.sum(valid.astype(I32), axis=1), src_tok, dst_row


def _layer(x2d, p2d, rel_bias, attn_norm, w_in, swa_q_norm, swa_k_norm, swa_sinks, moba_q_norm,
           moba_k_norm, swa_out_norm, moba_out_norm, w_out, ffn_norm, w_rg, b_rg, w_re, b_re,
           w_g, w_u, w_d, ple_norm, w_ple_gate, w_ple_proj, ple_out_norm, B, S):
    T, D = x2d.shape
    row = lambda v: v.reshape(1, -1).astype(F32)
    head_gain = jnp.concatenate([
        jnp.tile(swa_q_norm, SWA_Q_HEADS), jnp.tile(swa_k_norm, SWA_KV_HEADS),
        jnp.tile(moba_q_norm, MOBA_Q_HEADS), jnp.tile(moba_k_norm, MOBA_KV_HEADS)])
    w_in_b = w_in.astype(BF16)
    qa_w, kv_w = SWA_Q_HEADS * HEAD_DIM, SWA_KV_HEADS * HEAD_DIM
    qb_w = MOBA_Q_HEADS * HEAD_DIM
    c_va = qa_w + kv_w
    c_qb = c_va + kv_w
    c_vb = c_qb + qb_w + kv_w
    w_qk = jnp.concatenate([w_in_b[:, :c_va], w_in_b[:, c_qb:c_vb]], axis=1)
    w_vt = jnp.concatenate([w_in_b[:, c_va:c_qb], w_in_b[:, c_vb:]], axis=1).T
    qa, ka, qb, kb, vat, vbt = _qkv_call(x2d, row(attn_norm), w_qk, w_vt, row(head_gain))

    tbl_flat = rel_bias.astype(F32).reshape(-1)
    oa = _swa_call(tbl_flat, swa_sinks.astype(F32), qa, ka, vat, B, S)
    ob = _moba_call(tbl_flat, qb, kb, vbt, B, S)

    pad_rows = lambda n: jnp.zeros((n, D), F32)
    wr = jnp.concatenate([w_rg.T, pad_rows(EXPERT_ROW0 - N_GROUPS), w_re.T,
                          pad_rows(ROUTER_ROWS - EXPERT_ROW0 - N_EXPERTS)], axis=0)
    wr_hi = wr.astype(BF16)
    wr_lo = (wr - wr_hi.astype(F32)).astype(BF16)
    rb = jnp.concatenate([b_rg, jnp.zeros((EXPERT_ROW0 - N_GROUPS,), F32), b_re,
                          jnp.zeros((ROUTER_ROWS - EXPERT_ROW0 - N_EXPERTS,), F32)]).reshape(ROUTER_ROWS, 1)
    wo = w_out.astype(BF16)
    na_w = SWA_Q_HEADS * HEAD_DIM
    x1, hf, eid, wts = _outproj_call(oa, ob, x2d, row(swa_out_norm), row(moba_out_norm),
                                     wo[:na_w], wo[na_w:], row(ffn_norm), wr_hi, wr_lo, rb)

    be, n_real, src_tok, dst_row = _dispatch_plan(eid, T)
    moe_rows = _moe_call(be, n_real, src_tok, dst_row, hf, w_g, w_u, w_d)
    return _ple_call(moe_rows, x1, wts.T, p2d, row(ple_norm), w_ple_gate.astype(BF16),
                     w_ple_proj.astype(BF16), row(ple_out_norm))


def kernel(x, p, rel_bias, attn_norm, w_in, swa_q_norm, swa_k_norm, swa_sinks, moba_q_norm, moba_k_norm,
           swa_out_norm, moba_out_norm, w_out, ffn_norm, w_router_group, b_router_group, w_router_expert,
           b_router_expert, w_exp_gate, w_exp_up, w_exp_down, ple_norm, w_ple_gate, w_ple_proj, ple_out_norm):
    B, S, D = x.shape
    x2d = x.reshape(B * S, D)
    for i in range(p.shape[0]):
        x2d = _layer(x2d, p[i].reshape(B * S, -1), rel_bias, attn_norm[i], w_in[i], swa_q_norm[i],
                     swa_k_norm[i], swa_sinks[i], moba_q_norm[i], moba_k_norm[i], swa_out_norm[i],
                     moba_out_norm[i], w_out[i], ffn_norm[i], w_router_group[i], b_router_group[i],
                     w_router_expert[i], b_router_expert[i], w_exp_gate[i], w_exp_up[i], w_exp_down[i],
                     ple_norm[i], w_ple_gate[i], w_ple_proj[i], ple_out_norm[i], B, S)
    return x2d.reshape(B, S, D)
```

```python
import math

import numpy as np
import jax
import jax.numpy as jnp
from jax import lax
from jax.experimental import pallas as pl
from jax.experimental.pallas import tpu as pltpu

F32 = jnp.float32
BF16 = jnp.bfloat16
I32 = jnp.int32

HEAD_DIM = 64
SWA_Q_HEADS = 8
SWA_KV_HEADS = 2
SWA_WINDOW = 128
MOBA_Q_HEADS = 8
MOBA_KV_HEADS = 2
MOBA_BLOCK = 256
MOBA_TOPK = 3
N_HEADS = SWA_Q_HEADS + MOBA_Q_HEADS
REL_BUCKETS = 32
REL_MAX_DIST = 128
N_GROUPS = 4
EXPERTS_PER_GROUP = 8
N_EXPERTS = N_GROUPS * EXPERTS_PER_GROUP
EXPERT_TOPK = 2
DISPATCH_BLOCK = 512
PAD_MARK = 0xFFFF
EPS = 1e-6
NEG = -1e30
LOG2E = math.log2(math.e)

DMA_PRIORITIES = 2
LANES = 128
ROW_TILE = 8
BF16_SUBLANES = 16
GROUP_HEADS = 4
GROUP_W = GROUP_HEADS * HEAD_DIM
ROUTER_ROWS = 128
EXPERT_ROW0 = 8
TOKEN_TILE = 1024
PLE_TILE = 1024
VMEM_LIMIT = 52 * 1024 * 1024


def _dot_nt(a, b):
    return lax.dot_general(a, b, (((1,), (1,)), ((), ())), preferred_element_type=F32)


def _rmsnorm(x, g):
    ms = jnp.mean(x * x, axis=-1, keepdims=True)
    return x * lax.rsqrt(ms + EPS) * g


def _to_token_tiles(ref, x):
    m = x.shape[0]
    for j in range(x.shape[1] // LANES):
        ref[pl.ds(j, m, stride=ROW_TILE), :] = x[:, LANES * j:LANES * (j + 1)]


def _from_token_tiles(ref, m):
    return jnp.concatenate([ref[pl.ds(j, m, stride=ROW_TILE), :] for j in range(ref.shape[0] // m)], axis=1)


def _rel_bucket_np(dist):
    n = np.maximum(dist, 0)
    exact = REL_BUCKETS // 2
    nf = np.maximum(n, 1).astype(np.float32)
    large = exact + (np.log(nf / exact) / math.log(REL_MAX_DIST / exact)
                     * (REL_BUCKETS - exact)).astype(np.int32)
    return np.where(n < exact, n, np.minimum(large, REL_BUCKETS - 1)).astype(np.int32)


def _band_buckets(block):
    qi = np.arange(block)[:, None]
    kj = np.arange(2 * block)[None, :]
    return _rel_bucket_np(qi + block - kj)


def _bias_from_buckets(bkt, tbl_ref, head):
    acc = jnp.zeros(bkt.shape, F32)
    for j in range(REL_BUCKETS):
        acc = jnp.where(bkt == j, tbl_ref[j * N_HEADS + head] * LOG2E, acc)
    return acc


def _qkv_kernel(x_ref, g_ref, w_ref, wvt_ref, hg_ref, qa_ref, ka_ref, qb_ref, kb_ref, vat_ref, vbt_ref):
    h = _rmsnorm(x_ref[...], g_ref[...]).astype(BF16)
    acc = jnp.dot(h, w_ref[...], preferred_element_type=F32)
    lo = lax.broadcasted_iota(I32, (1, LANES), 1) < HEAD_DIM

    def head_normed(c, scale):
        blk = acc[:, LANES * c:LANES * (c + 1)]
        sq = blk * blk
        s_lo = jnp.sum(jnp.where(lo, sq, 0.0), axis=-1, keepdims=True)
        s_hi = jnp.sum(jnp.where(lo, 0.0, sq), axis=-1, keepdims=True)
        inv = jnp.where(lo, lax.rsqrt(s_lo / HEAD_DIM + EPS), lax.rsqrt(s_hi / HEAD_DIM + EPS))
        return blk * inv * hg_ref[:, LANES * c:LANES * (c + 1)] * scale

    scale = HEAD_DIM ** -0.5 * LOG2E
    for c in range(4):
        qa_ref[:, LANES * c:LANES * (c + 1)] = head_normed(c, scale).astype(BF16)
        qb_ref[:, LANES * c:LANES * (c + 1)] = head_normed(5 + c, scale).astype(BF16)
    ka_ref[...] = head_normed(4, 1.0).astype(BF16)
    kb_ref[...] = head_normed(9, 1.0).astype(BF16)
    vt = _dot_nt(wvt_ref[...], h).astype(BF16)
    for j in range(vat_ref.shape[0]):
        vat_ref[j] = vt[:LANES, SWA_WINDOW * j:SWA_WINDOW * (j + 1)]
    for j in range(vbt_ref.shape[0]):
        vbt_ref[j] = vt[LANES:, MOBA_BLOCK * j:MOBA_BLOCK * (j + 1)]


def _qkv_call(x2d, attn_g, w_qk, w_vt, head_gain):
    T, D = x2d.shape
    tm = TOKEN_TILE
    tok = lambda w: pl.BlockSpec((tm, w), lambda i: (i, 0))
    full = lambda a: pl.BlockSpec(a.shape, lambda i: (0, 0))
    out_w = (SWA_Q_HEADS * HEAD_DIM, LANES, MOBA_Q_HEADS * HEAD_DIM, LANES)
    slabs = lambda blk: pl.BlockSpec((tm // blk, LANES, blk), lambda i: (i, 0, 0))
    slab_shape = lambda blk: jax.ShapeDtypeStruct((T // blk, LANES, blk), BF16)
    return pl.pallas_call(
        _qkv_kernel,
        grid=(T // tm,),
        in_specs=[tok(D), full(attn_g), full(w_qk), full(w_vt), full(head_gain)],
        out_specs=[tok(w) for w in out_w] + [slabs(SWA_WINDOW), slabs(MOBA_BLOCK)],
        out_shape=[jax.ShapeDtypeStruct((T, w), BF16) for w in out_w]
        + [slab_shape(SWA_WINDOW), slab_shape(MOBA_BLOCK)],
        compiler_params=pltpu.CompilerParams(
            dimension_semantics=("arbitrary",), vmem_limit_bytes=VMEM_LIMIT),
        name="qkv",
    )(x2d, attn_g, w_qk, w_vt, head_gain)


def _padded_heads(qblk, heads):
    zeros = jnp.zeros((qblk.shape[0], HEAD_DIM), qblk.dtype)
    pieces = []
    for h in range(heads):
        piece = qblk[:, HEAD_DIM * h:HEAD_DIM * (h + 1)]
        pieces.append(jnp.concatenate([piece, zeros] if h < GROUP_HEADS else [zeros, piece], axis=1))
    return jnp.concatenate(pieces, axis=0)


def _swa_kernel(tbl_ref, sink_ref, bkt_ref, q_ref, k_ref, vt_ref, o_ref, bias_scr):
    W = SWA_WINDOW
    H = SWA_Q_HEADS
    S = q_ref.shape[0]

    @pl.when(pl.program_id(0) == 0)
    def _init_bias():
        bkt = bkt_ref[...]
        for h in range(H):
            bias_scr[:, W * h:W * (h + 1)] = _bias_from_buckets(bkt, tbl_ref, h)

    key_i = lax.broadcasted_iota(I32, (2 * W, H * W), 0)
    qry_i = lax.broadcasted_iota(I32, (2 * W, H * W), 1) & (W - 1)
    dist = qry_i + W - key_i
    band = (dist >= 0) & (dist < W)
    own_part = key_i >= W
    head_of_lane = lax.broadcasted_iota(I32, (1, H * W), 1) // W
    sinks = jnp.zeros((1, H * W), F32)
    for h in range(H):
        sinks = jnp.where(head_of_lane == h, sink_ref[h] * LOG2E, sinks)

    def body(i, carry):
        r0 = pl.multiple_of(i * W, W)
        prev = jnp.maximum(i - 1, 0)
        p0 = pl.multiple_of(prev * W, W)
        q8 = _padded_heads(q_ref[pl.ds(r0, W), :], H)
        kband = jnp.concatenate([k_ref[pl.ds(p0, W), :], k_ref[pl.ds(r0, W), :]], axis=0)
        mask = band & (own_part | (i > 0))
        s = jnp.where(mask, _dot_nt(kband, q8) + bias_scr[...], NEG)
        m = jnp.maximum(jnp.max(s, axis=0, keepdims=True), sinks)
        e = jnp.exp2(s - m)
        den = jnp.sum(e, axis=0, keepdims=True) + jnp.exp2(sinks - m)
        vtband = jnp.concatenate([vt_ref[prev], vt_ref[i]], axis=1)
        ot = jnp.dot(vtband, e.astype(BF16), preferred_element_type=F32) / den
        for pr in range(H // 2):
            f0 = HEAD_DIM * (2 * pr // GROUP_HEADS)
            pair = ot[f0:f0 + HEAD_DIM, 2 * W * pr:2 * W * (pr + 1)]
            o_ref[pl.ds(r0, W), LANES * pr:LANES * (pr + 1)] = jnp.concatenate(
                [pair[:, :W], pair[:, W:]], axis=0).T
        return carry

    lax.fori_loop(0, S // W, body, 0)


def _swa_call(tbl_flat, sinks, qa, ka, vat, B, S):
    W = SWA_WINDOW
    bkt = jnp.asarray(np.ascontiguousarray(_band_buckets(W).T))
    smem = pl.BlockSpec(memory_space=pltpu.SMEM)
    return pl.pallas_call(
        _swa_kernel,
        grid=(B,),
        in_specs=[smem, smem,
                  pl.BlockSpec(bkt.shape, lambda b: (0, 0)),
                  pl.BlockSpec((S, qa.shape[1]), lambda b: (b, 0)),
                  pl.BlockSpec((S, LANES), lambda b: (b, 0)),
                  pl.BlockSpec((S // W, LANES, W), lambda b: (b, 0, 0))],
        out_specs=pl.BlockSpec((S, qa.shape[1]), lambda b: (b, 0)),
        out_shape=jax.ShapeDtypeStruct(qa.shape, F32),
        scratch_shapes=[pltpu.VMEM((2 * W, SWA_Q_HEADS * W), F32)],
        compiler_params=pltpu.CompilerParams(
            dimension_semantics=("arbitrary",), vmem_limit_bytes=VMEM_LIMIT),
        name="swa",
    )(tbl_flat, sinks, bkt, qa, ka, vat)


def _moba_kernel(tbl_ref, bkt_ref, q_ref, k_ref, vt_ref, o_ref,
                 bias_scr, km_scr, q8_scr, sel_scr, m_scr, l_scr, acc_scr):
    BS = MOBA_BLOCK
    H = MOBA_Q_HEADS
    S = q_ref.shape[0]
    NBK = S // BS
    NQ = H * BS
    GQ = GROUP_HEADS * BS

    @pl.when(pl.program_id(0) == 0)
    def _init_bias():
        bkt = bkt_ref[...]
        for h in range(H):
            band = _bias_from_buckets(bkt, tbl_ref, SWA_Q_HEADS + h)
            far = tbl_ref[(REL_BUCKETS - 1) * N_HEADS + SWA_Q_HEADS + h] * LOG2E
            bias_scr[0, :, BS * h:BS * (h + 1)] = jnp.full((BS, BS), far, F32)
            bias_scr[1, :, BS * h:BS * (h + 1)] = band[:BS]
            bias_scr[2, :, BS * h:BS * (h + 1)] = band[BS:]

    blk_row = lax.broadcasted_iota(I32, (BF16_SUBLANES, 1), 0)
    kmean = jnp.sum(k_ref[...].astype(F32).reshape(NBK, BS, LANES), axis=1) / BS
    kmean = jnp.concatenate([kmean, jnp.zeros((BF16_SUBLANES - NBK, LANES), F32)], axis=0)
    km_hi = kmean.astype(BF16)
    km_scr[0] = km_hi
    km_scr[1] = (kmean - km_hi.astype(F32)).astype(BF16)

    def scores(n, case):
        k0 = pl.multiple_of(n * BS, BS)
        return _dot_nt(k_ref[pl.ds(k0, BS), :], q8_scr[...]) + bias_scr[case]

    ones_rows = jnp.ones((BF16_SUBLANES, BS), BF16)

    def weighted_values(n, e):
        vt_ext = jnp.concatenate([vt_ref[n], ones_rows], axis=0)
        pv = jnp.dot(vt_ext, e.astype(BF16), preferred_element_type=F32)
        vals = jnp.concatenate([pv[:HEAD_DIM, :GQ], pv[HEAD_DIM:LANES, GQ:]], axis=1)
        return vals, pv[LANES:LANES + 1, :]

    def qblock(mi, carry):
        r0 = pl.multiple_of(mi * BS, BS)
        q8 = _padded_heads(q_ref[pl.ds(r0, BS), :], H)
        q8_scr[...] = q8
        gate = _dot_nt(km_scr[0], q8) + _dot_nt(km_scr[1], q8)
        past = blk_row < mi
        gm = jnp.where(past, gate, NEG)
        cnt = jnp.zeros(gm.shape, F32)
        for n in range(NBK - 1):
            col = gm[n:n + 1, :]
            beats = (gm > col) | ((gm == col) & (blk_row < n))
            c = jnp.sum(jnp.where(beats, 1.0, 0.0), axis=0, keepdims=True)
            cnt = jnp.where(blk_row == n, c, cnt)
        sel_scr[...] = jnp.where(past & (cnt < MOBA_TOPK), 1.0, 0.0)

        key_i = lax.broadcasted_iota(I32, (BS, NQ), 0)
        qry_i = lax.broadcasted_iota(I32, (BS, NQ), 1) & (BS - 1)
        s = jnp.where(key_i <= qry_i, scores(mi, 2), NEG)
        m0 = jnp.max(s, axis=0, keepdims=True)
        e = jnp.exp2(s - m0)
        m_scr[...] = m0
        acc_scr[...], l_scr[...] = weighted_values(mi, e)

        def kvblock(j, c2):
            n = mi - j
            s = jnp.where(sel_scr[pl.ds(n, 1), :] > 0.5, scores(n, jnp.where(j == 1, 1, 0)), NEG)
            m_prev = m_scr[...]
            m_new = jnp.maximum(m_prev, jnp.max(s, axis=0, keepdims=True))
            alpha = jnp.exp2(m_prev - m_new)
            e = jnp.exp2(s - m_new)
            vals, row_sum = weighted_values(n, e)
            l_scr[...] = alpha * l_scr[...] + row_sum
            acc_scr[...] = alpha * acc_scr[...] + vals
            m_scr[...] = m_new
            return c2

        lax.fori_loop(1, mi + 1, kvblock, 0)
        ot = acc_scr[...] / l_scr[...]
        for pr in range(H // 2):
            pair = ot[:, 2 * BS * pr:2 * BS * (pr + 1)]
            o_ref[pl.ds(r0, BS), LANES * pr:LANES * (pr + 1)] = jnp.concatenate(
                [pair[:, :BS], pair[:, BS:]], axis=0).T
        return carry

    lax.fori_loop(0, NBK, qblock, 0)


def _moba_call(tbl_flat, qb, kb, vbt, B, S):
    BS = MOBA_BLOCK
    NQ = MOBA_Q_HEADS * BS
    NBK = S // BS
    bkt = jnp.asarray(np.ascontiguousarray(_band_buckets(BS).T))
    smem = pl.BlockSpec(memory_space=pltpu.SMEM)
    return pl.pallas_call(
        _moba_kernel,
        grid=(B,),
        in_specs=[smem,
                  pl.BlockSpec(bkt.shape, lambda b: (0, 0)),
                  pl.BlockSpec((S, qb.shape[1]), lambda b: (b, 0)),
                  pl.BlockSpec((S, LANES), lambda b: (b, 0)),
                  pl.BlockSpec((NBK, LANES, BS), lambda b: (b, 0, 0))],
        out_specs=pl.BlockSpec((S, qb.shape[1]), lambda b: (b, 0)),
        out_shape=jax.ShapeDtypeStruct(qb.shape, F32),
        scratch_shapes=[pltpu.VMEM((3, BS, NQ), F32),
                        pltpu.VMEM((2, BF16_SUBLANES, LANES), BF16),
                        pltpu.VMEM((NQ, LANES), BF16),
                        pltpu.VMEM((BF16_SUBLANES, NQ), F32),
                        pltpu.VMEM((1, NQ), F32),
                        pltpu.VMEM((1, NQ), F32),
                        pltpu.VMEM((HEAD_DIM, NQ), F32)],
        compiler_params=pltpu.CompilerParams(
            dimension_semantics=("arbitrary",), vmem_limit_bytes=VMEM_LIMIT),
        name="moba",
    )(tbl_flat, bkt, qb, kb, vbt)


def _outproj_kernel(oa_ref, ob_ref, x_ref, ga_ref, gb_ref, woa_ref, wob_ref, fg_ref,
                    wrh_ref, wrl_ref, rb_ref, x1_ref, hf_ref, eid_ref, wt_ref):
    na = _rmsnorm(oa_ref[...], ga_ref[...]).astype(BF16)
    nb = _rmsnorm(ob_ref[...], gb_ref[...]).astype(BF16)
    y = (jnp.dot(na, woa_ref[...], preferred_element_type=F32)
         + jnp.dot(nb, wob_ref[...], preferred_element_type=F32))
    x1 = x_ref[...] + y
    x1_ref[...] = x1
    hf = _rmsnorm(x1, fg_ref[...])
    _to_token_tiles(hf_ref, hf)

    hf_hi = hf.astype(BF16)
    hf_lo = (hf - hf_hi.astype(F32)).astype(BF16)
    lt = (_dot_nt(wrh_ref[...], hf_hi) + _dot_nt(wrh_ref[...], hf_lo)
          + _dot_nt(wrl_ref[...], hf_hi) + rb_ref[...])
    gl = [lt[j:j + 1, :] for j in range(N_GROUPS)]
    best = gl[0]
    gsel = jnp.zeros(best.shape, I32)
    for j in range(1, N_GROUPS):
        better = gl[j] > best
        gsel = jnp.where(better, j, gsel)
        best = jnp.where(better, gl[j], best)
    gsum = jnp.zeros(best.shape, F32)
    for j in range(N_GROUPS):
        gsum = gsum + jnp.exp(gl[j] - best)
    gw = 1.0 / gsum
    E = EXPERTS_PER_GROUP
    es = lt[EXPERT_ROW0:EXPERT_ROW0 + E, :]
    for j in range(1, N_GROUPS):
        es = jnp.where(gsel == j, lt[EXPERT_ROW0 + E * j:EXPERT_ROW0 + E * (j + 1), :], es)
    row = lax.broadcasted_iota(I32, es.shape, 0)
    v1 = jnp.max(es, axis=0, keepdims=True)
    i1 = jnp.min(jnp.where(es == v1, row, E), axis=0, keepdims=True)
    es2 = jnp.where(row == i1, -jnp.inf, es)
    v2 = jnp.max(es2, axis=0, keepdims=True)
    i2 = jnp.min(jnp.where(es2 == v2, row, E), axis=0, keepdims=True)
    e2 = jnp.exp(v2 - v1)
    den = 1.0 + e2
    eid_ref[...] = jnp.concatenate([gsel * E + i1, gsel * E + i2], axis=0)
    wt_ref[...] = jnp.concatenate([gw * (1.0 / den), gw * (e2 / den)], axis=0)


def _outproj_call(oa, ob, x2d, ga, gb, woa, wob, fg, wrh, wrl, rb):
    T, D = x2d.shape
    tm = TOKEN_TILE
    tok = lambda w: pl.BlockSpec((tm, w), lambda i: (i, 0))
    full = lambda a: pl.BlockSpec(a.shape, lambda i: (0, 0))
    col = pl.BlockSpec((EXPERT_TOPK, tm), lambda i: (0, i))
    return pl.pallas_call(
        _outproj_kernel,
        grid=(T // tm,),
        in_specs=[tok(oa.shape[1]), tok(ob.shape[1]), tok(D), full(ga), full(gb), full(woa), full(wob),
                  full(fg), full(wrh), full(wrl), full(rb)],
        out_specs=[tok(D), pl.BlockSpec((tm * ROW_TILE, LANES), lambda i: (i, 0)), col, col],
        out_shape=[jax.ShapeDtypeStruct((T, D), F32), jax.ShapeDtypeStruct((T * ROW_TILE, LANES), F32),
                   jax.ShapeDtypeStruct((EXPERT_TOPK, T), I32), jax.ShapeDtypeStruct((EXPERT_TOPK, T), F32)],
        compiler_params=pltpu.CompilerParams(
            dimension_semantics=("arbitrary",), vmem_limit_bytes=VMEM_LIMIT),
        name="outproj",
    )(oa, ob, x2d, ga, gb, woa, wob, fg, wrh, wrl, rb)


COPY_GROUP = 32


def _moe_kernel(be_ref, nv_ref, src_ref, srcn_ref, dst_ref, hf_hbm, wg_ref, wu_ref, wd_ref, out_hbm,
                xbuf, obuf, gsem, ssem):
    i = pl.program_id(0)
    nblk = pl.num_programs(0)
    M = xbuf.shape[1] // ROW_TILE
    slot = i % 2
    nv_cur = nv_ref[i]
    nv_next = jnp.where(i + 1 < nblk, nv_ref[jnp.minimum(i + 1, nblk - 1)], 0)

    def for_real_groups(n_real, fn):
        for k in range(M // COPY_GROUP):
            @pl.when(k * COPY_GROUP < n_real)
            def _group():
                fn(range(k * COPY_GROUP, (k + 1) * COPY_GROUP))

    def gather_rows(idx_ref, buf_slot, n_real):
        def start(rows):
            for r in rows:
                tok_row = pl.multiple_of(idx_ref[0, 0, r] * ROW_TILE, ROW_TILE)
                pltpu.make_async_copy(hf_hbm.at[pl.ds(tok_row, ROW_TILE)],
                                      xbuf.at[buf_slot, pl.ds(r * ROW_TILE, ROW_TILE)],
                                      gsem.at[buf_slot]).start(priority=r % DMA_PRIORITIES)
        for_real_groups(n_real, start)

    def scatter_rows(idx_ref, buf_slot, n_real):
        def start(rows):
            for r in rows:
                dst_row = pl.multiple_of(idx_ref[0, 0, r] * ROW_TILE, ROW_TILE)
                pltpu.make_async_copy(obuf.at[buf_slot, pl.ds(r * ROW_TILE, ROW_TILE)],
                                      out_hbm.at[pl.ds(dst_row, ROW_TILE)],
                                      ssem.at[buf_slot]).start(priority=r % DMA_PRIORITIES)
        for_real_groups(n_real, start)

    def wait_rows(buf, buf_slot, sem, n_real):
        span = COPY_GROUP * ROW_TILE
        for_real_groups(n_real, lambda rows: pltpu.make_async_copy(
            hf_hbm.at[pl.ds(0, span)], buf.at[buf_slot, pl.ds(0, span)], sem.at[buf_slot]).wait())

    @pl.when(nv_cur > 0)
    def _compute():
        @pl.when(i == 0)
        def _first_gather():
            gather_rows(src_ref, 0, nv_cur)
            obuf[1] = jnp.zeros(obuf.shape[1:], F32)
            spare = pltpu.make_async_copy(
                obuf.at[1], out_hbm.at[pl.ds(out_hbm.shape[0] - M * ROW_TILE, M * ROW_TILE)], ssem.at[1])
            spare.start()
            spare.wait()

        wait_rows(xbuf, slot, gsem, nv_cur)

        @pl.when(i >= 2)
        def _reuse_obuf():
            wait_rows(obuf, slot, ssem, nv_ref[jnp.maximum(i - 2, 0)])

        gather_rows(srcn_ref, 1 - slot, nv_next)
        xb = _from_token_tiles(xbuf.at[slot], M).astype(BF16)
        a = jnp.dot(xb, wg_ref[0].astype(BF16), preferred_element_type=F32)
        u = jnp.dot(xb, wu_ref[0].astype(BF16), preferred_element_type=F32)
        hmid = (a * jax.nn.sigmoid(a)) * u
        _to_token_tiles(obuf.at[slot], jnp.dot(hmid.astype(BF16), wd_ref[0].astype(BF16),
                                               preferred_element_type=F32))
        scatter_rows(dst_ref, slot, nv_cur)

        @pl.when(nv_next == 0)
        def _drain():
            wait_rows(obuf, slot, ssem, nv_cur)

            @pl.when(i >= 1)
            def _drain_prev():
                wait_rows(obuf, 1 - slot, ssem, nv_ref[jnp.maximum(i - 1, 0)])


def _moe_call(block_expert, n_real, src_tok, dst_row, hf, w_g, w_u, w_d):
    T = hf.shape[0] // ROW_TILE
    M = DISPATCH_BLOCK
    nblk = block_expert.shape[0]
    D, DE = w_g.shape[1], w_g.shape[2]
    assert D == ROW_TILE * LANES and hf.shape[1] == LANES and M % COPY_GROUP == 0
    idx_spec = lambda f: pl.BlockSpec((1, 1, M), f, memory_space=pltpu.SMEM)
    cur = lambda i, be, nu: (i, 0, 0)
    nxt = lambda i, be, nu: (jnp.minimum(i + 1, nblk - 1), 0, 0)
    wspec = lambda a, b: pl.BlockSpec((1, a, b), lambda i, be, nu: (be[i], 0, 0))
    grid_spec = pltpu.PrefetchScalarGridSpec(
        num_scalar_prefetch=2,
        grid=(nblk,),
        in_specs=[idx_spec(cur), idx_spec(nxt), idx_spec(cur),
                  pl.BlockSpec(memory_space=pl.ANY),
                  wspec(D, DE), wspec(D, DE), wspec(DE, D)],
        out_specs=pl.BlockSpec(memory_space=pl.ANY),
        scratch_shapes=[pltpu.VMEM((2, M * ROW_TILE, LANES), F32), pltpu.VMEM((2, M * ROW_TILE, LANES), F32),
                        pltpu.SemaphoreType.DMA((2,)), pltpu.SemaphoreType.DMA((2,))],
    )
    src3 = src_tok.reshape(nblk, 1, M)
    return pl.pallas_call(
        _moe_kernel,
        grid_spec=grid_spec,
        out_shape=jax.ShapeDtypeStruct(((EXPERT_TOPK * T + M) * ROW_TILE, LANES), F32),
        compiler_params=pltpu.CompilerParams(
            dimension_semantics=("arbitrary",), vmem_limit_bytes=VMEM_LIMIT),
        name="moe",
    )(block_expert, n_real, src3, src3, dst_row.reshape(nblk, 1, M), hf, w_g, w_u, w_d)


def _ple_kernel(x1_ref, m0_ref, m1_ref, wt_ref, p_ref, pg_ref, wgate_ref, wproj_ref, pog_ref, out_ref):
    proj = _rmsnorm(jnp.dot(p_ref[...].astype(BF16), wproj_ref[...], preferred_element_type=F32),
                    pog_ref[...])
    tm = x1_ref.shape[0]
    moe = (_from_token_tiles(m0_ref, tm) * wt_ref[:, 0:1] + _from_token_tiles(m1_ref, tm) * wt_ref[:, 1:2])
    x2 = x1_ref[...] + moe
    hn = _rmsnorm(x2, pg_ref[...]).astype(BF16)
    gate = jax.nn.sigmoid(jnp.dot(hn, wgate_ref[...], preferred_element_type=F32))
    out_ref[...] = x2 + gate * proj


def _ple_call(moe_rows, x1, wt_cols, p2d, pg, wgate, wproj, pog):
    T, D = x1.shape
    tm = PLE_TILE
    tok = lambda w: pl.BlockSpec((tm, w), lambda i: (i, 0))
    full = lambda a: pl.BlockSpec(a.shape, lambda i: (0, 0))
    first = pl.BlockSpec((tm * ROW_TILE, LANES), lambda i: (i, 0))
    second = pl.BlockSpec((tm * ROW_TILE, LANES), lambda i: (T // tm + i, 0))
    return pl.pallas_call(
        _ple_kernel,
        grid=(T // tm,),
        in_specs=[tok(D), first, second, tok(EXPERT_TOPK), tok(p2d.shape[1]),
                  full(pg), full(wgate), full(wproj), full(pog)],
        out_specs=tok(D),
        out_shape=jax.ShapeDtypeStruct((T, D), F32),
        compiler_params=pltpu.CompilerParams(
            dimension_semantics=("arbitrary",), vmem_limit_bytes=VMEM_LIMIT),
        name="ple",
    )(x1, moe_rows, moe_rows, wt_cols, p2d, pg, wgate, wproj, pog)


def _dispatch_plan(eid, T):
    A = EXPERT_TOPK * T
    M = DISPATCH_BLOCK
    nblk = A // M + N_EXPERTS
    assert A <= PAD_MARK
    e_flat = eid.reshape(A)
    experts = jnp.arange(N_EXPERTS, dtype=I32)
    counts = jnp.sum((e_flat[:, None] == experts[None, :]).astype(I32), axis=0)
    padded = (counts + M - 1) // M * M
    n_used = jnp.sum(padded) // M
    row = jnp.arange(M, dtype=I32)[None, :]
    pad_keys = jnp.where(row < (padded - counts)[:, None], 2 * experts[:, None] + 1, 2 * N_EXPERTS)
    keys = jnp.concatenate([2 * e_flat, pad_keys.reshape(-1)])
    vals = jnp.concatenate([jnp.arange(A, dtype=I32), jnp.full((N_EXPERTS * M,), PAD_MARK, I32)])
    packed = jnp.sort(keys * (PAD_MARK + 1) + vals).reshape(nblk, M)
    a_s = packed & PAD_MARK
    blk = jnp.arange(nblk, dtype=I32)
    be = jnp.minimum(packed[:, 0] // (2 * (PAD_MARK + 1)), N_EXPERTS - 1)
    be = jnp.where(blk < n_used, be, be[jnp.maximum(n_used - 1, 0)])
    valid = a_s != PAD_MARK
    src_tok = jnp.where(valid, jnp.where(a_s >= T, a_s - T, a_s), 0)
    dst_row = jnp.where(valid, a_s, A + row)
    return be, jnp.sum(valid.astype(I32), axis=1), src_tok, dst_row


def _layer(x2d, p2d, rel_bias, attn_norm, w_in, swa_q_norm, swa_k_norm, swa_sinks, moba_q_norm,
           moba_k_norm, swa_out_norm, moba_out_norm, w_out, ffn_norm, w_rg, b_rg, w_re, b_re,
           w_g, w_u, w_d, ple_norm, w_ple_gate, w_ple_proj, ple_out_norm, B, S):
    T, D = x2d.shape
    row = lambda v: v.reshape(1, -1).astype(F32)
    head_gain = jnp.concatenate([
        jnp.tile(swa_q_norm, SWA_Q_HEADS), jnp.tile(swa_k_norm, SWA_KV_HEADS),
        jnp.tile(moba_q_norm, MOBA_Q_HEADS), jnp.tile(moba_k_norm, MOBA_KV_HEADS)])
    w_in_b = w_in.astype(BF16)
    qa_w, kv_w = SWA_Q_HEADS * HEAD_DIM, SWA_KV_HEADS * HEAD_DIM
    qb_w = MOBA_Q_HEADS * HEAD_DIM
    c_va = qa_w + kv_w
    c_qb = c_va + kv_w
    c_vb = c_qb + qb_w + kv_w
    w_qk = jnp.concatenate([w_in_b[:, :c_va], w_in_b[:, c_qb:c_vb]], axis=1)
    w_vt = jnp.concatenate([w_in_b[:, c_va:c_qb], w_in_b[:, c_vb:]], axis=1).T
    qa, ka, qb, kb, vat, vbt = _qkv_call(x2d, row(attn_norm), w_qk, w_vt, row(head_gain))

    tbl_flat = rel_bias.astype(F32).reshape(-1)
    oa = _swa_call(tbl_flat, swa_sinks.astype(F32), qa, ka, vat, B, S)
    ob = _moba_call(tbl_flat, qb, kb, vbt, B, S)

    pad_rows = lambda n: jnp.zeros((n, D), F32)
    wr = jnp.concatenate([w_rg.T, pad_rows(EXPERT_ROW0 - N_GROUPS), w_re.T,
                          pad_rows(ROUTER_ROWS - EXPERT_ROW0 - N_EXPERTS)], axis=0)
    wr_hi = wr.astype(BF16)
    wr_lo = (wr - wr_hi.astype(F32)).astype(BF16)
    rb = jnp.concatenate([b_rg, jnp.zeros((EXPERT_ROW0 - N_GROUPS,), F32), b_re,
                          jnp.zeros((ROUTER_ROWS - EXPERT_ROW0 - N_EXPERTS,), F32)]).reshape(ROUTER_ROWS, 1)
    wo = w_out.astype(BF16)
    na_w = SWA_Q_HEADS * HEAD_DIM
    x1, hf, eid, wts = _outproj_call(oa, ob, x2d, row(swa_out_norm), row(moba_out_norm),
                                     wo[:na_w], wo[na_w:], row(ffn_norm), wr_hi, wr_lo, rb)

    be, n_real, src_tok, dst_row = _dispatch_plan(eid, T)
    moe_rows = _moe_call(be, n_real, src_tok, dst_row, hf, w_g, w_u, w_d)
    return _ple_call(moe_rows, x1, wts.T, p2d, row(ple_norm), w_ple_gate.astype(BF16),
                     w_ple_proj.astype(BF16), row(ple_out_norm))


def kernel(x, p, rel_bias, attn_norm, w_in, swa_q_norm, swa_k_norm, swa_sinks, moba_q_norm, moba_k_norm,
           swa_out_norm, moba_out_norm, w_out, ffn_norm, w_router_group, b_router_group, w_router_expert,
           b_router_expert, w_exp_gate, w_exp_up, w_exp_down, ple_norm, w_ple_gate, w_ple_proj, ple_out_norm):
    B, S, D = x.shape
    x2d = x.reshape(B * S, D)
    for i in range(p.shape[0]):
        x2d = _layer(x2d, p[i].reshape(B * S, -1), rel_bias, attn_norm[i], w_in[i], swa_q_norm[i],
                     swa_k_norm[i], swa_sinks[i], moba_q_norm[i], moba_k_norm[i], swa_out_norm[i],
                     moba_out_norm[i], w_out[i], ffn_norm[i], w_router_group[i], b_router_group[i],
                     w_router_expert[i], b_router_expert[i], w_exp_gate[i], w_exp_up[i], w_exp_down[i],
                     ple_norm[i], w_ple_gate[i], w_ple_proj[i], ple_out_norm[i], B, S)
    return x2d.reshape(B, S, D)
```

```python
import math

import numpy as np
import jax
import jax.numpy as jnp
from jax import lax
from jax.experimental import pallas as pl
from jax.experimental.pallas import tpu as pltpu

F32 = jnp.float32
BF16 = jnp.bfloat16
I32 = jnp.int32

HEAD_DIM = 64
SWA_Q_HEADS = 8
SWA_KV_HEADS = 2
SWA_WINDOW = 128
MOBA_Q_HEADS = 8
MOBA_KV_HEADS = 2
MOBA_BLOCK = 256
MOBA_TOPK = 3
N_HEADS = SWA_Q_HEADS + MOBA_Q_HEADS
REL_BUCKETS = 32
REL_MAX_DIST = 128
N_GROUPS = 4
EXPERTS_PER_GROUP = 8
N_EXPERTS = N_GROUPS * EXPERTS_PER_GROUP
EXPERT_TOPK = 2
DISPATCH_BLOCK = 512
PAD_MARK = 0xFFFF
EPS = 1e-6
NEG = -1e30
LOG2E = math.log2(math.e)

DMA_PRIORITIES = 2
LANES = 128
ROW_TILE = 8
BF16_SUBLANES = 16
GROUP_HEADS = 4
GROUP_W = GROUP_HEADS * HEAD_DIM
ROUTER_ROWS = 128
EXPERT_ROW0 = 8
TOKEN_TILE = 1024
PLE_TILE = 1024
VMEM_LIMIT = 52 * 1024 * 1024


def _dot_nt(a, b):
    return lax.dot_general(a, b, (((1,), (1,)), ((), ())), preferred_element_type=F32)


def _rmsnorm(x, g):
    ms = jnp.mean(x * x, axis=-1, keepdims=True)
    return x * lax.rsqrt(ms + EPS) * g


def _to_token_tiles(ref, x):
    m = x.shape[0]
    for j in range(x.shape[1] // LANES):
        ref[pl.ds(j, m, stride=ROW_TILE), :] = x[:, LANES * j:LANES * (j + 1)]


def _from_token_tiles(ref, m):
    return jnp.concatenate([ref[pl.ds(j, m, stride=ROW_TILE), :] for j in range(ref.shape[0] // m)], axis=1)


def _rel_bucket_np(dist):
    n = np.maximum(dist, 0)
    exact = REL_BUCKETS // 2
    nf = np.maximum(n, 1).astype(np.float32)
    large = exact + (np.log(nf / exact) / math.log(REL_MAX_DIST / exact)
                     * (REL_BUCKETS - exact)).astype(np.int32)
    return np.where(n < exact, n, np.minimum(large, REL_BUCKETS - 1)).astype(np.int32)


def _band_buckets(block):
    qi = np.arange(block)[:, None]
    kj = np.arange(2 * block)[None, :]
    return _rel_bucket_np(qi + block - kj)


def _bias_from_buckets(bkt, tbl_ref, head):
    acc = jnp.zeros(bkt.shape, F32)
    for j in range(REL_BUCKETS):
        acc = jnp.where(bkt == j, tbl_ref[j * N_HEADS + head] * LOG2E, acc)
    return acc


def _qkv_kernel(x_ref, g_ref, w_ref, wvt_ref, hg_ref, qa_ref, ka_ref, qb_ref, kb_ref, vat_ref, vbt_ref):
    h = _rmsnorm(x_ref[...], g_ref[...]).astype(BF16)
    acc = jnp.dot(h, w_ref[...], preferred_element_type=F32)
    lo = lax.broadcasted_iota(I32, (1, LANES), 1) < HEAD_DIM

    def head_normed(c, scale):
        blk = acc[:, LANES * c:LANES * (c + 1)]
        sq = blk * blk
        s_lo = jnp.sum(jnp.where(lo, sq, 0.0), axis=-1, keepdims=True)
        s_hi = jnp.sum(jnp.where(lo, 0.0, sq), axis=-1, keepdims=True)
        inv = jnp.where(lo, lax.rsqrt(s_lo / HEAD_DIM + EPS), lax.rsqrt(s_hi / HEAD_DIM + EPS))
        return blk * inv * hg_ref[:, LANES * c:LANES * (c + 1)] * scale

    scale = HEAD_DIM ** -0.5 * LOG2E
    for c in range(4):
        qa_ref[:, LANES * c:LANES * (c + 1)] = head_normed(c, scale).astype(BF16)
        qb_ref[:, LANES * c:LANES * (c + 1)] = head_normed(5 + c, scale).astype(BF16)
    ka_ref[...] = head_normed(4, 1.0).astype(BF16)
    kb_ref[...] = head_normed(9, 1.0).astype(BF16)
    vt = _dot_nt(wvt_ref[...], h).astype(BF16)
    for j in range(vat_ref.shape[0]):
        vat_ref[j] = vt[:LANES, SWA_WINDOW * j:SWA_WINDOW * (j + 1)]
    for j in range(vbt_ref.shape[0]):
        vbt_ref[j] = vt[LANES:, MOBA_BLOCK * j:MOBA_BLOCK * (j + 1)]


def _qkv_call(x2d, attn_g, w_qk, w_vt, head_gain):
    T, D = x2d.shape
    tm = TOKEN_TILE
    tok = lambda w: pl.BlockSpec((tm, w), lambda i: (i, 0))
    full = lambda a: pl.BlockSpec(a.shape, lambda i: (0, 0))
    out_w = (SWA_Q_HEADS * HEAD_DIM, LANES, MOBA_Q_HEADS * HEAD_DIM, LANES)
    slabs = lambda blk: pl.BlockSpec((tm // blk, LANES, blk), lambda i: (i, 0, 0))
    slab_shape = lambda blk: jax.ShapeDtypeStruct((T // blk, LANES, blk), BF16)
    return pl.pallas_call(
        _qkv_kernel,
        grid=(T // tm,),
        in_specs=[tok(D), full(attn_g), full(w_qk), full(w_vt), full(head_gain)],
        out_specs=[tok(w) for w in out_w] + [slabs(SWA_WINDOW), slabs(MOBA_BLOCK)],
        out_shape=[jax.ShapeDtypeStruct((T, w), BF16) for w in out_w]
        + [slab_shape(SWA_WINDOW), slab_shape(MOBA_BLOCK)],
        compiler_params=pltpu.CompilerParams(
            dimension_semantics=("arbitrary",), vmem_limit_bytes=VMEM_LIMIT),
        name="qkv",
    )(x2d, attn_g, w_qk, w_vt, head_gain)


def _padded_heads(qblk, heads):
    zeros = jnp.zeros((qblk.shape[0], HEAD_DIM), qblk.dtype)
    pieces = []
    for h in range(heads):
        piece = qblk[:, HEAD_DIM * h:HEAD_DIM * (h + 1)]
        pieces.append(jnp.concatenate([piece, zeros] if h < GROUP_HEADS else [zeros, piece], axis=1))
    return jnp.concatenate(pieces, axis=0)


def _swa_kernel(tbl_ref, sink_ref, bkt_ref, q_ref, k_ref, vt_ref, o_ref, bias_scr):
    W = SWA_WINDOW
    H = SWA_Q_HEADS
    S = q_ref.shape[0]

    @pl.when(pl.program_id(0) == 0)
    def _init_bias():
        bkt = bkt_ref[...]
        for h in range(H):
            bias_scr[:, W * h:W * (h + 1)] = _bias_from_buckets(bkt, tbl_ref, h)

    key_i = lax.broadcasted_iota(I32, (2 * W, H * W), 0)
    qry_i = lax.broadcasted_iota(I32, (2 * W, H * W), 1) & (W - 1)
    dist = qry_i + W - key_i
    band = (dist >= 0) & (dist < W)
    own_part = key_i >= W
    head_of_lane = lax.broadcasted_iota(I32, (1, H * W), 1) // W
    sinks = jnp.zeros((1, H * W), F32)
    for h in range(H):
        sinks = jnp.where(head_of_lane == h, sink_ref[h] * LOG2E, sinks)

    def body(i, carry):
        r0 = pl.multiple_of(i * W, W)
        prev = jnp.maximum(i - 1, 0)
        p0 = pl.multiple_of(prev * W, W)
        q8 = _padded_heads(q_ref[pl.ds(r0, W), :], H)
        kband = jnp.concatenate([k_ref[pl.ds(p0, W), :], k_ref[pl.ds(r0, W), :]], axis=0)
        mask = band & (own_part | (i > 0))
        s = jnp.where(mask, _dot_nt(kband, q8) + bias_scr[...], NEG)
        m = jnp.maximum(jnp.max(s, axis=0, keepdims=True), sinks)
        e = jnp.exp2(s - m)
        den = jnp.sum(e, axis=0, keepdims=True) + jnp.exp2(sinks - m)
        vtband = jnp.concatenate([vt_ref[prev], vt_ref[i]], axis=1)
        ot = jnp.dot(vtband, e.astype(BF16), preferred_element_type=F32) / den
        for pr in range(H // 2):
            f0 = HEAD_DIM * (2 * pr // GROUP_HEADS)
            pair = ot[f0:f0 + HEAD_DIM, 2 * W * pr:2 * W * (pr + 1)]
            o_ref[pl.ds(r0, W), LANES * pr:LANES * (pr + 1)] = jnp.concatenate(
                [pair[:, :W], pair[:, W:]], axis=0).T
        return carry

    lax.fori_loop(0, S // W, body, 0)


def _swa_call(tbl_flat, sinks, qa, ka, vat, B, S):
    W = SWA_WINDOW
    bkt = jnp.asarray(np.ascontiguousarray(_band_buckets(W).T))
    smem = pl.BlockSpec(memory_space=pltpu.SMEM)
    return pl.pallas_call(
        _swa_kernel,
        grid=(B,),
        in_specs=[smem, smem,
                  pl.BlockSpec(bkt.shape, lambda b: (0, 0)),
                  pl.BlockSpec((S, qa.shape[1]), lambda b: (b, 0)),
                  pl.BlockSpec((S, LANES), lambda b: (b, 0)),
                  pl.BlockSpec((S // W, LANES, W), lambda b: (b, 0, 0))],
        out_specs=pl.BlockSpec((S, qa.shape[1]), lambda b: (b, 0)),
        out_shape=jax.ShapeDtypeStruct(qa.shape, F32),
        scratch_shapes=[pltpu.VMEM((2 * W, SWA_Q_HEADS * W), F32)],
        compiler_params=pltpu.CompilerParams(
            dimension_semantics=("arbitrary",), vmem_limit_bytes=VMEM_LIMIT),
        name="swa",
    )(tbl_flat, sinks, bkt, qa, ka, vat)


def _moba_kernel(tbl_ref, bkt_ref, q_ref, k_ref, vt_ref, o_ref,
                 bias_scr, km_scr, q8_scr, sel_scr, m_scr, l_scr, acc_scr):
    BS = MOBA_BLOCK
    H = MOBA_Q_HEADS
    S = q_ref.shape[0]
    NBK = S // BS
    NQ = H * BS
    GQ = GROUP_HEADS * BS

    @pl.when(pl.program_id(0) == 0)
    def _init_bias():
        bkt = bkt_ref[...]
        for h in range(H):
            band = _bias_from_buckets(bkt, tbl_ref, SWA_Q_HEADS + h)
            bias_scr[0, :, BS * h:BS * (h + 1)] = band[:BS]
            bias_scr[1, :, BS * h:BS * (h + 1)] = band[BS:]

    blk_row = lax.broadcasted_iota(I32, (BF16_SUBLANES, 1), 0)
    kmean = jnp.sum(k_ref[...].astype(F32).reshape(NBK, BS, LANES), axis=1) / BS
    kmean = jnp.concatenate([kmean, jnp.zeros((BF16_SUBLANES - NBK, LANES), F32)], axis=0)
    km_hi = kmean.astype(BF16)
    km_scr[0] = km_hi
    km_scr[1] = (kmean - km_hi.astype(F32)).astype(BF16)

    head_of_lane = lax.broadcasted_iota(I32, (1, NQ), 1) // BS
    far_bias = jnp.zeros((1, NQ), F32)
    for h in range(H):
        far_bias = jnp.where(head_of_lane == h,
                             tbl_ref[(REL_BUCKETS - 1) * N_HEADS + SWA_Q_HEADS + h] * LOG2E, far_bias)

    def qk(n):
        k0 = pl.multiple_of(n * BS, BS)
        return _dot_nt(k_ref[pl.ds(k0, BS), :], q8_scr[...])

    ones_rows = jnp.ones((BF16_SUBLANES, BS), BF16)

    def weighted_values(n, e):
        vt_ext = jnp.concatenate([vt_ref[n], ones_rows], axis=0)
        pv = jnp.dot(vt_ext, e.astype(BF16), preferred_element_type=F32)
        vals = jnp.concatenate([pv[:HEAD_DIM, :GQ], pv[HEAD_DIM:LANES, GQ:]], axis=1)
        return vals, pv[LANES:LANES + 1, :]

    def qblock(mi, carry):
        r0 = pl.multiple_of(mi * BS, BS)
        q8 = _padded_heads(q_ref[pl.ds(r0, BS), :], H)
        q8_scr[...] = q8
        gate = _dot_nt(km_scr[0], q8) + _dot_nt(km_scr[1], q8)
        past = blk_row < mi
        gm = jnp.where(past, gate, NEG)
        cnt = jnp.zeros(gm.shape, F32)
        for n in range(NBK - 1):
            col = gm[n:n + 1, :]
            beats = (gm > col) | ((gm == col) & (blk_row < n))
            c = jnp.sum(jnp.where(beats, 1.0, 0.0), axis=0, keepdims=True)
            cnt = jnp.where(blk_row == n, c, cnt)
        sel_scr[...] = jnp.where(past & (cnt < MOBA_TOPK), 1.0, 0.0)

        key_i = lax.broadcasted_iota(I32, (BS, NQ), 0)
        qry_i = lax.broadcasted_iota(I32, (BS, NQ), 1) & (BS - 1)
        s = jnp.where(key_i <= qry_i, qk(mi) + bias_scr[1], NEG)
        m0 = jnp.max(s, axis=0, keepdims=True)
        e = jnp.exp2(s - m0)
        m_scr[...] = m0
        acc_scr[...], l_scr[...] = weighted_values(mi, e)

        def online_update(n, s):
            m_prev = m_scr[...]
            m_new = jnp.maximum(m_prev, jnp.max(s, axis=0, keepdims=True))
            alpha = jnp.exp2(m_prev - m_new)
            e = jnp.exp2(s - m_new)
            vals, row_sum = weighted_values(n, e)
            l_scr[...] = alpha * l_scr[...] + row_sum
            acc_scr[...] = alpha * acc_scr[...] + vals
            m_scr[...] = m_new

        @pl.when(mi >= 1)
        def _previous_block():
            n = mi - 1
            online_update(n, jnp.where(sel_scr[pl.ds(n, 1), :] > 0.5, qk(n) + bias_scr[0], NEG))

        def far_block(j, c2):
            n = mi - j
            online_update(n, qk(n) + jnp.where(sel_scr[pl.ds(n, 1), :] > 0.5, far_bias, NEG))
            return c2

        lax.fori_loop(2, mi + 1, far_block, 0)
        ot = acc_scr[...] / l_scr[...]
        for pr in range(H // 2):
            pair = ot[:, 2 * BS * pr:2 * BS * (pr + 1)]
            o_ref[pl.ds(r0, BS), LANES * pr:LANES * (pr + 1)] = jnp.concatenate(
                [pair[:, :BS], pair[:, BS:]], axis=0).T
        return carry

    lax.fori_loop(0, NBK, qblock, 0)


def _moba_call(tbl_flat, qb, kb, vbt, B, S):
    BS = MOBA_BLOCK
    NQ = MOBA_Q_HEADS * BS
    NBK = S // BS
    bkt = jnp.asarray(np.ascontiguousarray(_band_buckets(BS).T))
    smem = pl.BlockSpec(memory_space=pltpu.SMEM)
    return pl.pallas_call(
        _moba_kernel,
        grid=(B,),
        in_specs=[smem,
                  pl.BlockSpec(bkt.shape, lambda b: (0, 0)),
                  pl.BlockSpec((S, qb.shape[1]), lambda b: (b, 0)),
                  pl.BlockSpec((S, LANES), lambda b: (b, 0)),
                  pl.BlockSpec((NBK, LANES, BS), lambda b: (b, 0, 0))],
        out_specs=pl.BlockSpec((S, qb.shape[1]), lambda b: (b, 0)),
        out_shape=jax.ShapeDtypeStruct(qb.shape, F32),
        scratch_shapes=[pltpu.VMEM((2, BS, NQ), F32),
                        pltpu.VMEM((2, BF16_SUBLANES, LANES), BF16),
                        pltpu.VMEM((NQ, LANES), BF16),
                        pltpu.VMEM((BF16_SUBLANES, NQ), F32),
                        pltpu.VMEM((1, NQ), F32),
                        pltpu.VMEM((1, NQ), F32),
                        pltpu.VMEM((HEAD_DIM, NQ), F32)],
        compiler_params=pltpu.CompilerParams(
            dimension_semantics=("arbitrary",), vmem_limit_bytes=VMEM_LIMIT),
        name="moba",
    )(tbl_flat, bkt, qb, kb, vbt)


def _outproj_kernel(oa_ref, ob_ref, x_ref, ga_ref, gb_ref, woa_ref, wob_ref, fg_ref,
                    wrh_ref, wrl_ref, rb_ref, x1_ref, hf_ref, eid_ref, wt_ref):
    na = _rmsnorm(oa_ref[...], ga_ref[...]).astype(BF16)
    nb = _rmsnorm(ob_ref[...], gb_ref[...]).astype(BF16)
    y = (jnp.dot(na, woa_ref[...], preferred_element_type=F32)
         + jnp.dot(nb, wob_ref[...], preferred_element_type=F32))
    x1 = x_ref[...] + y
    x1_ref[...] = x1
    hf = _rmsnorm(x1, fg_ref[...])
    _to_token_tiles(hf_ref, hf)

    hf_hi = hf.astype(BF16)
    hf_lo = (hf - hf_hi.astype(F32)).astype(BF16)
    lt = (_dot_nt(wrh_ref[...], hf_hi) + _dot_nt(wrh_ref[...], hf_lo)
          + _dot_nt(wrl_ref[...], hf_hi) + rb_ref[...])
    gl = [lt[j:j + 1, :] for j in range(N_GROUPS)]
    best = gl[0]
    gsel = jnp.zeros(best.shape, I32)
    for j in range(1, N_GROUPS):
        better = gl[j] > best
        gsel = jnp.where(better, j, gsel)
        best = jnp.where(better, gl[j], best)
    gsum = jnp.zeros(best.shape, F32)
    for j in range(N_GROUPS):
        gsum = gsum + jnp.exp(gl[j] - best)
    gw = 1.0 / gsum
    E = EXPERTS_PER_GROUP
    es = lt[EXPERT_ROW0:EXPERT_ROW0 + E, :]
    for j in range(1, N_GROUPS):
        es = jnp.where(gsel == j, lt[EXPERT_ROW0 + E * j:EXPERT_ROW0 + E * (j + 1), :], es)
    row = lax.broadcasted_iota(I32, es.shape, 0)
    v1 = jnp.max(es, axis=0, keepdims=True)
    i1 = jnp.min(jnp.where(es == v1, row, E), axis=0, keepdims=True)
    es2 = jnp.where(row == i1, -jnp.inf, es)
    v2 = jnp.max(es2, axis=0, keepdims=True)
    i2 = jnp.min(jnp.where(es2 == v2, row, E), axis=0, keepdims=True)
    e2 = jnp.exp(v2 - v1)
    den = 1.0 + e2
    eid_ref[...] = jnp.concatenate([gsel * E + i1, gsel * E + i2], axis=0)
    wt_ref[...] = jnp.concatenate([gw * (1.0 / den), gw * (e2 / den)], axis=0)


def _outproj_call(oa, ob, x2d, ga, gb, woa, wob, fg, wrh, wrl, rb):
    T, D = x2d.shape
    tm = TOKEN_TILE
    tok = lambda w: pl.BlockSpec((tm, w), lambda i: (i, 0))
    full = lambda a: pl.BlockSpec(a.shape, lambda i: (0, 0))
    col = pl.BlockSpec((EXPERT_TOPK, tm), lambda i: (0, i))
    return pl.pallas_call(
        _outproj_kernel,
        grid=(T // tm,),
        in_specs=[tok(oa.shape[1]), tok(ob.shape[1]), tok(D), full(ga), full(gb), full(woa), full(wob),
                  full(fg), full(wrh), full(wrl), full(rb)],
        out_specs=[tok(D), pl.BlockSpec((tm * ROW_TILE, LANES), lambda i: (i, 0)), col, col],
        out_shape=[jax.ShapeDtypeStruct((T, D), F32), jax.ShapeDtypeStruct((T * ROW_TILE, LANES), F32),
                   jax.ShapeDtypeStruct((EXPERT_TOPK, T), I32), jax.ShapeDtypeStruct((EXPERT_TOPK, T), F32)],
        compiler_params=pltpu.CompilerParams(
            dimension_semantics=("arbitrary",), vmem_limit_bytes=VMEM_LIMIT),
        name="outproj",
    )(oa, ob, x2d, ga, gb, woa, wob, fg, wrh, wrl, rb)


COPY_GROUP = 32


def _moe_kernel(be_ref, nv_ref, src_ref, srcn_ref, dst_ref, hf_hbm, wg_ref, wu_ref, wd_ref, out_hbm,
                xbuf, obuf, gsem, ssem):
    i = pl.program_id(0)
    nblk = pl.num_programs(0)
    M = xbuf.shape[1] // ROW_TILE
    slot = i % 2
    nv_cur = nv_ref[i]
    nv_next = jnp.where(i + 1 < nblk, nv_ref[jnp.minimum(i + 1, nblk - 1)], 0)

    def for_real_groups(n_real, fn):
        for k in range(M // COPY_GROUP):
            @pl.when(k * COPY_GROUP < n_real)
            def _group():
                fn(range(k * COPY_GROUP, (k + 1) * COPY_GROUP))

    def gather_rows(idx_ref, buf_slot, n_real):
        def start(rows):
            for r in rows:
                tok_row = pl.multiple_of(idx_ref[0, 0, r] * ROW_TILE, ROW_TILE)
                pltpu.make_async_copy(hf_hbm.at[pl.ds(tok_row, ROW_TILE)],
                                      xbuf.at[buf_slot, pl.ds(r * ROW_TILE, ROW_TILE)],
                                      gsem.at[buf_slot]).start(priority=r % DMA_PRIORITIES)
        for_real_groups(n_real, start)

    def scatter_rows(idx_ref, buf_slot, n_real):
        def start(rows):
            for r in rows:
                dst_row = pl.multiple_of(idx_ref[0, 0, r] * ROW_TILE, ROW_TILE)
                pltpu.make_async_copy(obuf.at[buf_slot, pl.ds(r * ROW_TILE, ROW_TILE)],
                                      out_hbm.at[pl.ds(dst_row, ROW_TILE)],
                                      ssem.at[buf_slot]).start(priority=r % DMA_PRIORITIES)
        for_real_groups(n_real, start)

    def wait_rows(buf, buf_slot, sem, n_real):
        span = COPY_GROUP * ROW_TILE
        for_real_groups(n_real, lambda rows: pltpu.make_async_copy(
            hf_hbm.at[pl.ds(0, span)], buf.at[buf_slot, pl.ds(0, span)], sem.at[buf_slot]).wait())

    @pl.when(nv_cur > 0)
    def _compute():
        @pl.when(i == 0)
        def _first_gather():
            gather_rows(src_ref, 0, nv_cur)
            obuf[1] = jnp.zeros(obuf.shape[1:], F32)
            spare = pltpu.make_async_copy(
                obuf.at[1], out_hbm.at[pl.ds(out_hbm.shape[0] - M * ROW_TILE, M * ROW_TILE)], ssem.at[1])
            spare.start()
            spare.wait()

        wait_rows(xbuf, slot, gsem, nv_cur)

        @pl.when(i >= 2)
        def _reuse_obuf():
            wait_rows(obuf, slot, ssem, nv_ref[jnp.maximum(i - 2, 0)])

        gather_rows(srcn_ref, 1 - slot, nv_next)
        xb = _from_token_tiles(xbuf.at[slot], M).astype(BF16)
        a = jnp.dot(xb, wg_ref[0].astype(BF16), preferred_element_type=F32)
        u = jnp.dot(xb, wu_ref[0].astype(BF16), preferred_element_type=F32)
        hmid = (a * jax.nn.sigmoid(a)) * u
        _to_token_tiles(obuf.at[slot], jnp.dot(hmid.astype(BF16), wd_ref[0].astype(BF16),
                                               preferred_element_type=F32))
        scatter_rows(dst_ref, slot, nv_cur)

        @pl.when(nv_next == 0)
        def _drain():
            wait_rows(obuf, slot, ssem, nv_cur)

            @pl.when(i >= 1)
            def _drain_prev():
                wait_rows(obuf, 1 - slot, ssem, nv_ref[jnp.maximum(i - 1, 0)])


def _moe_call(block_expert, n_real, src_tok, dst_row, hf, w_g, w_u, w_d):
    T = hf.shape[0] // ROW_TILE
    M = DISPATCH_BLOCK
    nblk = block_expert.shape[0]
    D, DE = w_g.shape[1], w_g.shape[2]
    assert D == ROW_TILE * LANES and hf.shape[1] == LANES and M % COPY_GROUP == 0
    idx_spec = lambda f: pl.BlockSpec((1, 1, M), f, memory_space=pltpu.SMEM)
    cur = lambda i, be, nu: (i, 0, 0)
    nxt = lambda i, be, nu: (jnp.minimum(i + 1, nblk - 1), 0, 0)
    wspec = lambda a, b: pl.BlockSpec((1, a, b), lambda i, be, nu: (be[i], 0, 0))
    grid_spec = pltpu.PrefetchScalarGridSpec(
        num_scalar_prefetch=2,
        grid=(nblk,),
        in_specs=[idx_spec(cur), idx_spec(nxt), idx_spec(cur),
                  pl.BlockSpec(memory_space=pl.ANY),
                  wspec(D, DE), wspec(D, DE), wspec(DE, D)],
        out_specs=pl.BlockSpec(memory_space=pl.ANY),
        scratch_shapes=[pltpu.VMEM((2, M * ROW_TILE, LANES), F32), pltpu.VMEM((2, M * ROW_TILE, LANES), F32),
                        pltpu.SemaphoreType.DMA((2,)), pltpu.SemaphoreType.DMA((2,))],
    )
    src3 = src_tok.reshape(nblk, 1, M)
    return pl.pallas_call(
        _moe_kernel,
        grid_spec=grid_spec,
        out_shape=jax.ShapeDtypeStruct(((EXPERT_TOPK * T + M) * ROW_TILE, LANES), F32),
        compiler_params=pltpu.CompilerParams(
            dimension_semantics=("arbitrary",), vmem_limit_bytes=VMEM_LIMIT),
        name="moe",
    )(block_expert, n_real, src3, src3, dst_row.reshape(nblk, 1, M), hf, w_g, w_u, w_d)


def _ple_kernel(x1_ref, m0_ref, m1_ref, wt_ref, p_ref, pg_ref, wgate_ref, wproj_ref, pog_ref, out_ref):
    proj = _rmsnorm(jnp.dot(p_ref[...].astype(BF16), wproj_ref[...], preferred_element_type=F32),
                    pog_ref[...])
    tm = x1_ref.shape[0]
    moe = (_from_token_tiles(m0_ref, tm) * wt_ref[:, 0:1] + _from_token_tiles(m1_ref, tm) * wt_ref[:, 1:2])
    x2 = x1_ref[...] + moe
    hn = _rmsnorm(x2, pg_ref[...]).astype(BF16)
    gate = jax.nn.sigmoid(jnp.dot(hn, wgate_ref[...], preferred_element_type=F32))
    out_ref[...] = x2 + gate * proj


def _ple_call(moe_rows, x1, wt_cols, p2d, pg, wgate, wproj, pog):
    T, D = x1.shape
    tm = PLE_TILE
    tok = lambda w: pl.BlockSpec((tm, w), lambda i: (i, 0))
    full = lambda a: pl.BlockSpec(a.shape, lambda i: (0, 0))
    first = pl.BlockSpec((tm * ROW_TILE, LANES), lambda i: (i, 0))
    second = pl.BlockSpec((tm * ROW_TILE, LANES), lambda i: (T // tm + i, 0))
    return pl.pallas_call(
        _ple_kernel,
        grid=(T // tm,),
        in_specs=[tok(D), first, second, tok(EXPERT_TOPK), tok(p2d.shape[1]),
                  full(pg), full(wgate), full(wproj), full(pog)],
        out_specs=tok(D),
        out_shape=jax.ShapeDtypeStruct((T, D), F32),
        compiler_params=pltpu.CompilerParams(
            dimension_semantics=("arbitrary",), vmem_limit_bytes=VMEM_LIMIT),
        name="ple",
    )(x1, moe_rows, moe_rows, wt_cols, p2d, pg, wgate, wproj, pog)


def _dispatch_plan(eid, T):
    A = EXPERT_TOPK * T
    M = DISPATCH_BLOCK
    nblk = A // M + N_EXPERTS
    assert A <= PAD_MARK
    e_flat = eid.reshape(A)
    experts = jnp.arange(N_EXPERTS, dtype=I32)
    counts = jnp.sum((e_flat[:, None] == experts[None, :]).astype(I32), axis=0)
    padded = (counts + M - 1) // M * M
    n_used = jnp.sum(padded) // M
    row = jnp.arange(M, dtype=I32)[None, :]
    pad_keys = jnp.where(row < (padded - counts)[:, None], 2 * experts[:, None] + 1, 2 * N_EXPERTS)
    keys = jnp.concatenate([2 * e_flat, pad_keys.reshape(-1)])
    vals = jnp.concatenate([jnp.arange(A, dtype=I32), jnp.full((N_EXPERTS * M,), PAD_MARK, I32)])
    packed = jnp.sort(keys * (PAD_MARK + 1) + vals).reshape(nblk, M)
    a_s = packed & PAD_MARK
    blk = jnp.arange(nblk, dtype=I32)
    be = jnp.minimum(packed[:, 0] // (2 * (PAD_MARK + 1)), N_EXPERTS - 1)
    be = jnp.where(blk < n_used, be, be[jnp.maximum(n_used - 1, 0)])
    valid = a_s != PAD_MARK
    src_tok = jnp.where(valid, jnp.where(a_s >= T, a_s - T, a_s), 0)
    dst_row = jnp.where(valid, a_s, A + row)
    return be, jnp.sum(valid.astype(I32), axis=1), src_tok, dst_row


def _layer(x2d, p2d, rel_bias, attn_norm, w_in, swa_q_norm, swa_k_norm, swa_sinks, moba_q_norm,
           moba_k_norm, swa_out_norm, moba_out_norm, w_out, ffn_norm, w_rg, b_rg, w_re, b_re,
           w_g, w_u, w_d, ple_norm, w_ple_gate, w_ple_proj, ple_out_norm, B, S):
    T, D = x2d.shape
    row = lambda v: v.reshape(1, -1).astype(F32)
    head_gain = jnp.concatenate([
        jnp.tile(swa_q_norm, SWA_Q_HEADS), jnp.tile(swa_k_norm, SWA_KV_HEADS),
        jnp.tile(moba_q_norm, MOBA_Q_HEADS), jnp.tile(moba_k_norm, MOBA_KV_HEADS)])
    w_in_b = w_in.astype(BF16)
    qa_w, kv_w = SWA_Q_HEADS * HEAD_DIM, SWA_KV_HEADS * HEAD_DIM
    qb_w = MOBA_Q_HEADS * HEAD_DIM
    c_va = qa_w + kv_w
    c_qb = c_va + kv_w
    c_vb = c_qb + qb_w + kv_w
    w_qk = jnp.concatenate([w_in_b[:, :c_va], w_in_b[:, c_qb:c_vb]], axis=1)
    w_vt = jnp.concatenate([w_in_b[:, c_va:c_qb], w_in_b[:, c_vb:]], axis=1).T
    qa, ka, qb, kb, vat, vbt = _qkv_call(x2d, row(attn_norm), w_qk, w_vt, row(head_gain))

    tbl_flat = rel_bias.astype(F32).reshape(-1)
    oa = _swa_call(tbl_flat, swa_sinks.astype(F32), qa, ka, vat, B, S)
    ob = _moba_call(tbl_flat, qb, kb, vbt, B, S)

    pad_rows = lambda n: jnp.zeros((n, D), F32)
    wr = jnp.concatenate([w_rg.T, pad_rows(EXPERT_ROW0 - N_GROUPS), w_re.T,
                          pad_rows(ROUTER_ROWS - EXPERT_ROW0 - N_EXPERTS)], axis=0)
    wr_hi = wr.astype(BF16)
    wr_lo = (wr - wr_hi.astype(F32)).astype(BF16)
    rb = jnp.concatenate([b_rg, jnp.zeros((EXPERT_ROW0 - N_GROUPS,), F32), b_re,
                          jnp.zeros((ROUTER_ROWS - EXPERT_ROW0 - N_EXPERTS,), F32)]).reshape(ROUTER_ROWS, 1)
    wo = w_out.astype(BF16)
    na_w = SWA_Q_HEADS * HEAD_DIM
    x1, hf, eid, wts = _outproj_call(oa, ob, x2d, row(swa_out_norm), row(moba_out_norm),
                                     wo[:na_w], wo[na_w:], row(ffn_norm), wr_hi, wr_lo, rb)

    be, n_real, src_tok, dst_row = _dispatch_plan(eid, T)
    moe_rows = _moe_call(be, n_real, src_tok, dst_row, hf, w_g, w_u, w_d)
    return _ple_call(moe_rows, x1, wts.T, p2d, row(ple_norm), w_ple_gate.astype(BF16),
                     w_ple_proj.astype(BF16), row(ple_out_norm))


def kernel(x, p, rel_bias, attn_norm, w_in, swa_q_norm, swa_k_norm, swa_sinks, moba_q_norm, moba_k_norm,
           swa_out_norm, moba_out_norm, w_out, ffn_norm, w_router_group, b_router_group, w_router_expert,
           b_router_expert, w_exp_gate, w_exp_up, w_exp_down, ple_norm, w_ple_gate, w_ple_proj, ple_out_norm):
    B, S, D = x.shape
    x2d = x.reshape(B * S, D)
    for i in range(p.shape[0]):
        x2d = _layer(x2d, p[i].reshape(B * S, -1), rel_bias, attn_norm[i], w_in[i], swa_q_norm[i],
                     swa_k_norm[i], swa_sinks[i], moba_q_norm[i], moba_k_norm[i], swa_out_norm[i],
                     moba_out_norm[i], w_out[i], ffn_norm[i], w_router_group[i], b_router_group[i],
                     w_router_expert[i], b_router_expert[i], w_exp_gate[i], w_exp_up[i], w_exp_down[i],
                     ple_norm[i], w_ple_gate[i], w_ple_proj[i], ple_out_norm[i], B, S)
    return x2d.reshape(B, S, D)
```

```python
import math

import numpy as np
import jax
import jax.numpy as jnp
from jax import lax
from jax.experimental import pallas as pl
from jax.experimental.pallas import tpu as pltpu

F32 = jnp.float32
BF16 = jnp.bfloat16
I32 = jnp.int32

HEAD_DIM = 64
SWA_Q_HEADS = 8
SWA_KV_HEADS = 2
SWA_WINDOW = 128
MOBA_Q_HEADS = 8
MOBA_KV_HEADS = 2
MOBA_BLOCK = 256
MOBA_TOPK = 3
N_HEADS = SWA_Q_HEADS + MOBA_Q_HEADS
REL_BUCKETS = 32
REL_MAX_DIST = 128
N_GROUPS = 4
EXPERTS_PER_GROUP = 8
N_EXPERTS = N_GROUPS * EXPERTS_PER_GROUP
EXPERT_TOPK = 2
DISPATCH_BLOCK = 512
PAD_MARK = 0xFFFF
EPS = 1e-6
NEG = -1e30
LOG2E = math.log2(math.e)

DMA_PRIORITIES = 2
LANES = 128
ROW_TILE = 8
BF16_SUBLANES = 16
GROUP_HEADS = 4
GROUP_W = GROUP_HEADS * HEAD_DIM
ROUTER_ROWS = 128
EXPERT_ROW0 = 8
TOKEN_TILE = 1024
PLE_TILE = 1024
VMEM_LIMIT = 52 * 1024 * 1024


def _dot_nt(a, b):
    return lax.dot_general(a, b, (((1,), (1,)), ((), ())), preferred_element_type=F32)


def _rmsnorm(x, g):
    ms = jnp.mean(x * x, axis=-1, keepdims=True)
    return x * lax.rsqrt(ms + EPS) * g


def _to_token_tiles(ref, x):
    m = x.shape[0]
    for j in range(x.shape[1] // LANES):
        ref[pl.ds(j, m, stride=ROW_TILE), :] = x[:, LANES * j:LANES * (j + 1)]


def _from_token_tiles(ref, m):
    return jnp.concatenate([ref[pl.ds(j, m, stride=ROW_TILE), :] for j in range(ref.shape[0] // m)], axis=1)


def _rel_bucket_np(dist):
    n = np.maximum(dist, 0)
    exact = REL_BUCKETS // 2
    nf = np.maximum(n, 1).astype(np.float32)
    large = exact + (np.log(nf / exact) / math.log(REL_MAX_DIST / exact)
                     * (REL_BUCKETS - exact)).astype(np.int32)
    return np.where(n < exact, n, np.minimum(large, REL_BUCKETS - 1)).astype(np.int32)


def _band_buckets(block):
    qi = np.arange(block)[:, None]
    kj = np.arange(2 * block)[None, :]
    return _rel_bucket_np(qi + block - kj)


def _bias_from_buckets(bkt, tbl_ref, head):
    acc = jnp.zeros(bkt.shape, F32)
    for j in range(REL_BUCKETS):
        acc = jnp.where(bkt == j, tbl_ref[j * N_HEADS + head] * LOG2E, acc)
    return acc


def _qkv_kernel(x_ref, g_ref, w_ref, wvt_ref, hg_ref, qa_ref, ka_ref, qb_ref, kb_ref, vat_ref, vbt_ref):
    h = _rmsnorm(x_ref[...], g_ref[...]).astype(BF16)
    acc = jnp.dot(h, w_ref[...], preferred_element_type=F32)
    lo = lax.broadcasted_iota(I32, (1, LANES), 1) < HEAD_DIM

    def head_normed(c, scale):
        blk = acc[:, LANES * c:LANES * (c + 1)]
        sq = blk * blk
        s_lo = jnp.sum(jnp.where(lo, sq, 0.0), axis=-1, keepdims=True)
        s_hi = jnp.sum(jnp.where(lo, 0.0, sq), axis=-1, keepdims=True)
        inv = jnp.where(lo, lax.rsqrt(s_lo / HEAD_DIM + EPS), lax.rsqrt(s_hi / HEAD_DIM + EPS))
        return blk * inv * hg_ref[:, LANES * c:LANES * (c + 1)] * scale

    scale = HEAD_DIM ** -0.5 * LOG2E
    for c in range(4):
        qa_ref[:, LANES * c:LANES * (c + 1)] = head_normed(c, scale).astype(BF16)
        qb_ref[:, LANES * c:LANES * (c + 1)] = head_normed(5 + c, scale).astype(BF16)
    ka_ref[...] = head_normed(4, 1.0).astype(BF16)
    kb_ref[...] = head_normed(9, 1.0).astype(BF16)
    vt = _dot_nt(wvt_ref[...], h).astype(BF16)
    for j in range(vat_ref.shape[0]):
        vat_ref[j] = vt[:LANES, SWA_WINDOW * j:SWA_WINDOW * (j + 1)]
    for j in range(vbt_ref.shape[0]):
        vbt_ref[j] = vt[LANES:, MOBA_BLOCK * j:MOBA_BLOCK * (j + 1)]


def _qkv_call(x2d, attn_g, w_qk, w_vt, head_gain):
    T, D = x2d.shape
    tm = TOKEN_TILE
    tok = lambda w: pl.BlockSpec((tm, w), lambda i: (i, 0))
    full = lambda a: pl.BlockSpec(a.shape, lambda i: (0, 0))
    out_w = (SWA_Q_HEADS * HEAD_DIM, LANES, MOBA_Q_HEADS * HEAD_DIM, LANES)
    slabs = lambda blk: pl.BlockSpec((tm // blk, LANES, blk), lambda i: (i, 0, 0))
    slab_shape = lambda blk: jax.ShapeDtypeStruct((T // blk, LANES, blk), BF16)
    return pl.pallas_call(
        _qkv_kernel,
        grid=(T // tm,),
        in_specs=[tok(D), full(attn_g), full(w_qk), full(w_vt), full(head_gain)],
        out_specs=[tok(w) for w in out_w] + [slabs(SWA_WINDOW), slabs(MOBA_BLOCK)],
        out_shape=[jax.ShapeDtypeStruct((T, w), BF16) for w in out_w]
        + [slab_shape(SWA_WINDOW), slab_shape(MOBA_BLOCK)],
        compiler_params=pltpu.CompilerParams(
            dimension_semantics=("arbitrary",), vmem_limit_bytes=VMEM_LIMIT),
        name="qkv",
    )(x2d, attn_g, w_qk, w_vt, head_gain)


def _padded_heads(qblk, heads):
    zeros = jnp.zeros((qblk.shape[0], HEAD_DIM), qblk.dtype)
    pieces = []
    for h in range(heads):
        piece = qblk[:, HEAD_DIM * h:HEAD_DIM * (h + 1)]
        pieces.append(jnp.concatenate([piece, zeros] if h < GROUP_HEADS else [zeros, piece], axis=1))
    return jnp.concatenate(pieces, axis=0)


def _swa_kernel(tbl_ref, sink_ref, bkt_ref, q_ref, k_ref, vt_ref, o_ref, bias_scr):
    W = SWA_WINDOW
    QB = 2 * W
    NK = W + QB
    H = SWA_Q_HEADS
    S = q_ref.shape[0]

    @pl.when(pl.program_id(0) == 0)
    def _init_bias():
        bkt = bkt_ref[...]
        for h in range(H):
            bias_scr[:, QB * h:QB * (h + 1)] = _bias_from_buckets(bkt, tbl_ref, h)

    key_i = lax.broadcasted_iota(I32, (NK, H * QB), 0)
    qry_i = lax.broadcasted_iota(I32, (NK, H * QB), 1) & (QB - 1)
    dist = qry_i + W - key_i
    band = (dist >= 0) & (dist < W)
    real_key = key_i >= W
    head_of_lane = lax.broadcasted_iota(I32, (1, H * QB), 1) // QB
    sinks = jnp.zeros((1, H * QB), F32)
    for h in range(H):
        sinks = jnp.where(head_of_lane == h, sink_ref[h] * LOG2E, sinks)

    def body(i, carry):
        r0 = pl.multiple_of(i * QB, QB)
        p0 = pl.multiple_of(jnp.maximum(r0 - W, 0), W)
        q8 = _padded_heads(q_ref[pl.ds(r0, QB), :], H)
        kband = jnp.concatenate([k_ref[pl.ds(p0, W), :], k_ref[pl.ds(r0, QB), :]], axis=0)
        mask = band & (real_key | (i > 0))
        s = jnp.where(mask, _dot_nt(kband, q8) + bias_scr[...], NEG)
        m = jnp.maximum(jnp.max(s, axis=0, keepdims=True), sinks)
        e = jnp.exp2(s - m)
        den = jnp.sum(e, axis=0, keepdims=True) + jnp.exp2(sinks - m)
        j = 2 * i
        vtband = jnp.concatenate([vt_ref[jnp.maximum(j - 1, 0)], vt_ref[j], vt_ref[j + 1]], axis=1)
        ot = jnp.dot(vtband, e.astype(BF16), preferred_element_type=F32) / den
        for pr in range(H // 2):
            f0 = HEAD_DIM * (2 * pr // GROUP_HEADS)
            pair = ot[f0:f0 + HEAD_DIM, 2 * QB * pr:2 * QB * (pr + 1)]
            o_ref[pl.ds(r0, QB), LANES * pr:LANES * (pr + 1)] = jnp.concatenate(
                [pair[:, :QB], pair[:, QB:]], axis=0).T
        return carry

    lax.fori_loop(0, S // QB, body, 0)


def _swa_call(tbl_flat, sinks, qa, ka, vat, B, S):
    W = SWA_WINDOW
    QB, NK = 2 * W, 3 * W
    bkt = jnp.asarray(_rel_bucket_np(np.arange(QB)[None, :] + W - np.arange(NK)[:, None]))
    smem = pl.BlockSpec(memory_space=pltpu.SMEM)
    return pl.pallas_call(
        _swa_kernel,
        grid=(B,),
        in_specs=[smem, smem,
                  pl.BlockSpec(bkt.shape, lambda b: (0, 0)),
                  pl.BlockSpec((S, qa.shape[1]), lambda b: (b, 0)),
                  pl.BlockSpec((S, LANES), lambda b: (b, 0)),
                  pl.BlockSpec((S // W, LANES, W), lambda b: (b, 0, 0))],
        out_specs=pl.BlockSpec((S, qa.shape[1]), lambda b: (b, 0)),
        out_shape=jax.ShapeDtypeStruct(qa.shape, F32),
        scratch_shapes=[pltpu.VMEM((NK, SWA_Q_HEADS * QB), F32)],
        compiler_params=pltpu.CompilerParams(
            dimension_semantics=("arbitrary",), vmem_limit_bytes=VMEM_LIMIT),
        name="swa",
    )(tbl_flat, sinks, bkt, qa, ka, vat)


def _moba_kernel(tbl_ref, bkt_ref, q_ref, k_ref, vt_ref, o_ref,
                 bias_scr, km_scr, q8_scr, sel_scr, m_scr, l_scr, acc_scr):
    BS = MOBA_BLOCK
    H = MOBA_Q_HEADS
    S = q_ref.shape[0]
    NBK = S // BS
    NQ = H * BS
    GQ = GROUP_HEADS * BS

    @pl.when(pl.program_id(0) == 0)
    def _init_bias():
        bkt = bkt_ref[...]
        for h in range(H):
            band = _bias_from_buckets(bkt, tbl_ref, SWA_Q_HEADS + h)
            far = tbl_ref[(REL_BUCKETS - 1) * N_HEADS + SWA_Q_HEADS + h] * LOG2E
            bias_scr[0, :, BS * h:BS * (h + 1)] = jnp.full((BS, BS), far, F32)
            bias_scr[1, :, BS * h:BS * (h + 1)] = band[:BS]
            bias_scr[2, :, BS * h:BS * (h + 1)] = band[BS:]

    blk_row = lax.broadcasted_iota(I32, (BF16_SUBLANES, 1), 0)
    kmean = jnp.sum(k_ref[...].astype(F32).reshape(NBK, BS, LANES), axis=1) / BS
    kmean = jnp.concatenate([kmean, jnp.zeros((BF16_SUBLANES - NBK, LANES), F32)], axis=0)
    km_hi = kmean.astype(BF16)
    km_scr[0] = km_hi
    km_scr[1] = (kmean - km_hi.astype(F32)).astype(BF16)

    def scores(n, case):
        k0 = pl.multiple_of(n * BS, BS)
        return _dot_nt(k_ref[pl.ds(k0, BS), :], q8_scr[...]) + bias_scr[case]

    ones_rows = jnp.ones((BF16_SUBLANES, BS), BF16)

    def weighted_values(n, e):
        vt_ext = jnp.concatenate([vt_ref[n], ones_rows], axis=0)
        pv = jnp.dot(vt_ext, e.astype(BF16), preferred_element_type=F32)
        vals = jnp.concatenate([pv[:HEAD_DIM, :GQ], pv[HEAD_DIM:LANES, GQ:]], axis=1)
        return vals, pv[LANES:LANES + 1, :]

    def qblock(mi, carry):
        r0 = pl.multiple_of(mi * BS, BS)
        q8 = _padded_heads(q_ref[pl.ds(r0, BS), :], H)
        q8_scr[...] = q8
        gate = _dot_nt(km_scr[0], q8) + _dot_nt(km_scr[1], q8)
        past = blk_row < mi
        gm = jnp.where(past, gate, NEG)
        cnt = jnp.zeros(gm.shape, F32)
        for n in range(NBK - 1):
            col = gm[n:n + 1, :]
            beats = (gm > col) | ((gm == col) & (blk_row < n))
            c = jnp.sum(jnp.where(beats, 1.0, 0.0), axis=0, keepdims=True)
            cnt = jnp.where(blk_row == n, c, cnt)
        sel_scr[...] = jnp.where(past & (cnt < MOBA_TOPK), 1.0, 0.0)

        key_i = lax.broadcasted_iota(I32, (BS, NQ), 0)
        qry_i = lax.broadcasted_iota(I32, (BS, NQ), 1) & (BS - 1)
        s = jnp.where(key_i <= qry_i, scores(mi, 2), NEG)
        m0 = jnp.max(s, axis=0, keepdims=True)
        e = jnp.exp2(s - m0)
        m_scr[...] = m0
        acc_scr[...], l_scr[...] = weighted_values(mi, e)

        def kvblock(j, c2):
            n = mi - j
            s = jnp.where(sel_scr[pl.ds(n, 1), :] > 0.5, scores(n, jnp.where(j == 1, 1, 0)), NEG)
            m_prev = m_scr[...]
            m_new = jnp.maximum(m_prev, jnp.max(s, axis=0, keepdims=True))
            alpha = jnp.exp2(m_prev - m_new)
            e = jnp.exp2(s - m_new)
            vals, row_sum = weighted_values(n, e)
            l_scr[...] = alpha * l_scr[...] + row_sum
            acc_scr[...] = alpha * acc_scr[...] + vals
            m_scr[...] = m_new
            return c2

        lax.fori_loop(1, mi + 1, kvblock, 0)
        ot = acc_scr[...] / l_scr[...]
        for pr in range(H // 2):
            pair = ot[:, 2 * BS * pr:2 * BS * (pr + 1)]
            o_ref[pl.ds(r0, BS), LANES * pr:LANES * (pr + 1)] = jnp.concatenate(
                [pair[:, :BS], pair[:, BS:]], axis=0).T
        return carry

    lax.fori_loop(0, NBK, qblock, 0)


def _moba_call(tbl_flat, qb, kb, vbt, B, S):
    BS = MOBA_BLOCK
    NQ = MOBA_Q_HEADS * BS
    NBK = S // BS
    bkt = jnp.asarray(np.ascontiguousarray(_band_buckets(BS).T))
    smem = pl.BlockSpec(memory_space=pltpu.SMEM)
    return pl.pallas_call(
        _moba_kernel,
        grid=(B,),
        in_specs=[smem,
                  pl.BlockSpec(bkt.shape, lambda b: (0, 0)),
                  pl.BlockSpec((S, qb.shape[1]), lambda b: (b, 0)),
                  pl.BlockSpec((S, LANES), lambda b: (b, 0)),
                  pl.BlockSpec((NBK, LANES, BS), lambda b: (b, 0, 0))],
        out_specs=pl.BlockSpec((S, qb.shape[1]), lambda b: (b, 0)),
        out_shape=jax.ShapeDtypeStruct(qb.shape, F32),
        scratch_shapes=[pltpu.VMEM((3, BS, NQ), F32),
                        pltpu.VMEM((2, BF16_SUBLANES, LANES), BF16),
                        pltpu.VMEM((NQ, LANES), BF16),
                        pltpu.VMEM((BF16_SUBLANES, NQ), F32),
                        pltpu.VMEM((1, NQ), F32),
                        pltpu.VMEM((1, NQ), F32),
                        pltpu.VMEM((HEAD_DIM, NQ), F32)],
        compiler_params=pltpu.CompilerParams(
            dimension_semantics=("arbitrary",), vmem_limit_bytes=VMEM_LIMIT),
        name="moba",
    )(tbl_flat, bkt, qb, kb, vbt)


def _outproj_kernel(oa_ref, ob_ref, x_ref, ga_ref, gb_ref, woa_ref, wob_ref, fg_ref,
                    wrh_ref, wrl_ref, rb_ref, x1_ref, hf_ref, eid_ref, wt_ref):
    na = _rmsnorm(oa_ref[...], ga_ref[...]).astype(BF16)
    nb = _rmsnorm(ob_ref[...], gb_ref[...]).astype(BF16)
    y = (jnp.dot(na, woa_ref[...], preferred_element_type=F32)
         + jnp.dot(nb, wob_ref[...], preferred_element_type=F32))
    x1 = x_ref[...] + y
    x1_ref[...] = x1
    hf = _rmsnorm(x1, fg_ref[...])
    _to_token_tiles(hf_ref, hf)

    hf_hi = hf.astype(BF16)
    hf_lo = (hf - hf_hi.astype(F32)).astype(BF16)
    lt = (_dot_nt(wrh_ref[...], hf_hi) + _dot_nt(wrh_ref[...], hf_lo)
          + _dot_nt(wrl_ref[...], hf_hi) + rb_ref[...])
    gl = [lt[j:j + 1, :] for j in range(N_GROUPS)]
    best = gl[0]
    gsel = jnp.zeros(best.shape, I32)
    for j in range(1, N_GROUPS):
        better = gl[j] > best
        gsel = jnp.where(better, j, gsel)
        best = jnp.where(better, gl[j], best)
    gsum = jnp.zeros(best.shape, F32)
    for j in range(N_GROUPS):
        gsum = gsum + jnp.exp(gl[j] - best)
    gw = 1.0 / gsum
    E = EXPERTS_PER_GROUP
    es = lt[EXPERT_ROW0:EXPERT_ROW0 + E, :]
    for j in range(1, N_GROUPS):
        es = jnp.where(gsel == j, lt[EXPERT_ROW0 + E * j:EXPERT_ROW0 + E * (j + 1), :], es)
    row = lax.broadcasted_iota(I32, es.shape, 0)
    v1 = jnp.max(es, axis=0, keepdims=True)
    i1 = jnp.min(jnp.where(es == v1, row, E), axis=0, keepdims=True)
    es2 = jnp.where(row == i1, -jnp.inf, es)
    v2 = jnp.max(es2, axis=0, keepdims=True)
    i2 = jnp.min(jnp.where(es2 == v2, row, E), axis=0, keepdims=True)
    e2 = jnp.exp(v2 - v1)
    den = 1.0 + e2
    eid_ref[...] = jnp.concatenate([gsel * E + i1, gsel * E + i2], axis=0)
    wt_ref[...] = jnp.concatenate([gw * (1.0 / den), gw * (e2 / den)], axis=0)


def _outproj_call(oa, ob, x2d, ga, gb, woa, wob, fg, wrh, wrl, rb):
    T, D = x2d.shape
    tm = TOKEN_TILE
    tok = lambda w: pl.BlockSpec((tm, w), lambda i: (i, 0))
    full = lambda a: pl.BlockSpec(a.shape, lambda i: (0, 0))
    col = pl.BlockSpec((EXPERT_TOPK, tm), lambda i: (0, i))
    return pl.pallas_call(
        _outproj_kernel,
        grid=(T // tm,),
        in_specs=[tok(oa.shape[1]), tok(ob.shape[1]), tok(D), full(ga), full(gb), full(woa), full(wob),
                  full(fg), full(wrh), full(wrl), full(rb)],
        out_specs=[tok(D), pl.BlockSpec((tm * ROW_TILE, LANES), lambda i: (i, 0)), col, col],
        out_shape=[jax.ShapeDtypeStruct((T, D), F32), jax.ShapeDtypeStruct((T * ROW_TILE, LANES), F32),
                   jax.ShapeDtypeStruct((EXPERT_TOPK, T), I32), jax.ShapeDtypeStruct((EXPERT_TOPK, T), F32)],
        compiler_params=pltpu.CompilerParams(
            dimension_semantics=("arbitrary",), vmem_limit_bytes=VMEM_LIMIT),
        name="outproj",
    )(oa, ob, x2d, ga, gb, woa, wob, fg, wrh, wrl, rb)


COPY_GROUP = 32


def _moe_kernel(be_ref, nv_ref, src_ref, srcn_ref, dst_ref, hf_hbm, wg_ref, wu_ref, wd_ref, out_hbm,
                xbuf, obuf, gsem, ssem):
    i = pl.program_id(0)
    nblk = pl.num_programs(0)
    M = xbuf.shape[1] // ROW_TILE
    slot = i % 2
    nv_cur = nv_ref[i]
    nv_next = jnp.where(i + 1 < nblk, nv_ref[jnp.minimum(i + 1, nblk - 1)], 0)

    def for_real_groups(n_real, fn):
        for k in range(M // COPY_GROUP):
            @pl.when(k * COPY_GROUP < n_real)
            def _group():
                fn(range(k * COPY_GROUP, (k + 1) * COPY_GROUP))

    def gather_rows(idx_ref, buf_slot, n_real):
        def start(rows):
            for r in rows:
                tok_row = pl.multiple_of(idx_ref[0, 0, r] * ROW_TILE, ROW_TILE)
                pltpu.make_async_copy(hf_hbm.at[pl.ds(tok_row, ROW_TILE)],
                                      xbuf.at[buf_slot, pl.ds(r * ROW_TILE, ROW_TILE)],
                                      gsem.at[buf_slot]).start(priority=r % DMA_PRIORITIES)
        for_real_groups(n_real, start)

    def scatter_rows(idx_ref, buf_slot, n_real):
        def start(rows):
            for r in rows:
                dst_row = pl.multiple_of(idx_ref[0, 0, r] * ROW_TILE, ROW_TILE)
                pltpu.make_async_copy(obuf.at[buf_slot, pl.ds(r * ROW_TILE, ROW_TILE)],
                                      out_hbm.at[pl.ds(dst_row, ROW_TILE)],
                                      ssem.at[buf_slot]).start(priority=r % DMA_PRIORITIES)
        for_real_groups(n_real, start)

    def wait_rows(buf, buf_slot, sem, n_real):
        span = COPY_GROUP * ROW_TILE
        for_real_groups(n_real, lambda rows: pltpu.make_async_copy(
            hf_hbm.at[pl.ds(0, span)], buf.at[buf_slot, pl.ds(0, span)], sem.at[buf_slot]).wait())

    @pl.when(nv_cur > 0)
    def _compute():
        @pl.when(i == 0)
        def _first_gather():
            gather_rows(src_ref, 0, nv_cur)
            obuf[1] = jnp.zeros(obuf.shape[1:], F32)
            spare = pltpu.make_async_copy(
                obuf.at[1], out_hbm.at[pl.ds(out_hbm.shape[0] - M * ROW_TILE, M * ROW_TILE)], ssem.at[1])
            spare.start()
            spare.wait()

        wait_rows(xbuf, slot, gsem, nv_cur)

        @pl.when(i >= 2)
        def _reuse_obuf():
            wait_rows(obuf, slot, ssem, nv_ref[jnp.maximum(i - 2, 0)])

        gather_rows(srcn_ref, 1 - slot, nv_next)
        xb = _from_token_tiles(xbuf.at[slot], M).astype(BF16)
        a = jnp.dot(xb, wg_ref[0].astype(BF16), preferred_element_type=F32)
        u = jnp.dot(xb, wu_ref[0].astype(BF16), preferred_element_type=F32)
        hmid = (a * jax.nn.sigmoid(a)) * u
        _to_token_tiles(obuf.at[slot], jnp.dot(hmid.astype(BF16), wd_ref[0].astype(BF16),
                                               preferred_element_type=F32))
        scatter_rows(dst_ref, slot, nv_cur)

        @pl.when(nv_next == 0)
        def _drain():
            wait_rows(obuf, slot, ssem, nv_cur)

            @pl.when(i >= 1)
            def _drain_prev():
                wait_rows(obuf, 1 - slot, ssem, nv_ref[jnp.maximum(i - 1, 0)])


def _moe_call(block_expert, n_real, src_tok, dst_row, hf, w_g, w_u, w_d):
    T = hf.shape[0] // ROW_TILE
    M = DISPATCH_BLOCK
    nblk = block_expert.shape[0]
    D, DE = w_g.shape[1], w_g.shape[2]
    assert D == ROW_TILE * LANES and hf.shape[1] == LANES and M % COPY_GROUP == 0
    idx_spec = lambda f: pl.BlockSpec((1, 1, M), f, memory_space=pltpu.SMEM)
    cur = lambda i, be, nu: (i, 0, 0)
    nxt = lambda i, be, nu: (jnp.minimum(i + 1, nblk - 1), 0, 0)
    wspec = lambda a, b: pl.BlockSpec((1, a, b), lambda i, be, nu: (be[i], 0, 0))
    grid_spec = pltpu.PrefetchScalarGridSpec(
        num_scalar_prefetch=2,
        grid=(nblk,),
        in_specs=[idx_spec(cur), idx_spec(nxt), idx_spec(cur),
                  pl.BlockSpec(memory_space=pl.ANY),
                  wspec(D, DE), wspec(D, DE), wspec(DE, D)],
        out_specs=pl.BlockSpec(memory_space=pl.ANY),
        scratch_shapes=[pltpu.VMEM((2, M * ROW_TILE, LANES), F32), pltpu.VMEM((2, M * ROW_TILE, LANES), F32),
                        pltpu.SemaphoreType.DMA((2,)), pltpu.SemaphoreType.DMA((2,))],
    )
    src3 = src_tok.reshape(nblk, 1, M)
    return pl.pallas_call(
        _moe_kernel,
        grid_spec=grid_spec,
        out_shape=jax.ShapeDtypeStruct(((EXPERT_TOPK * T + M) * ROW_TILE, LANES), F32),
        compiler_params=pltpu.CompilerParams(
            dimension_semantics=("arbitrary",), vmem_limit_bytes=VMEM_LIMIT),
        name="moe",
    )(block_expert, n_real, src3, src3, dst_row.reshape(nblk, 1, M), hf, w_g, w_u, w_d)


def _ple_kernel(x1_ref, m0_ref, m1_ref, wt_ref, p_ref, pg_ref, wgate_ref, wproj_ref, pog_ref, out_ref):
    proj = _rmsnorm(jnp.dot(p_ref[...].astype(BF16), wproj_ref[...], preferred_element_type=F32),
                    pog_ref[...])
    tm = x1_ref.shape[0]
    moe = (_from_token_tiles(m0_ref, tm) * wt_ref[:, 0:1] + _from_token_tiles(m1_ref, tm) * wt_ref[:, 1:2])
    x2 = x1_ref[...] + moe
    hn = _rmsnorm(x2, pg_ref[...]).astype(BF16)
    gate = jax.nn.sigmoid(jnp.dot(hn, wgate_ref[...], preferred_element_type=F32))
    out_ref[...] = x2 + gate * proj


def _ple_call(moe_rows, x1, wt_cols, p2d, pg, wgate, wproj, pog):
    T, D = x1.shape
    tm = PLE_TILE
    tok = lambda w: pl.BlockSpec((tm, w), lambda i: (i, 0))
    full = lambda a: pl.BlockSpec(a.shape, lambda i: (0, 0))
    first = pl.BlockSpec((tm * ROW_TILE, LANES), lambda i: (i, 0))
    second = pl.BlockSpec((tm * ROW_TILE, LANES), lambda i: (T // tm + i, 0))
    return pl.pallas_call(
        _ple_kernel,
        grid=(T // tm,),
        in_specs=[tok(D), first, second, tok(EXPERT_TOPK), tok(p2d.shape[1]),
                  full(pg), full(wgate), full(wproj), full(pog)],
        out_specs=tok(D),
        out_shape=jax.ShapeDtypeStruct((T, D), F32),
        compiler_params=pltpu.CompilerParams(
            dimension_semantics=("arbitrary",), vmem_limit_bytes=VMEM_LIMIT),
        name="ple",
    )(x1, moe_rows, moe_rows, wt_cols, p2d, pg, wgate, wproj, pog)


def _dispatch_plan(eid, T):
    A = EXPERT_TOPK * T
    M = DISPATCH_BLOCK
    nblk = A // M + N_EXPERTS
    assert A <= PAD_MARK
    e_flat = eid.reshape(A)
    experts = jnp.arange(N_EXPERTS, dtype=I32)
    counts = jnp.sum((e_flat[:, None] == experts[None, :]).astype(I32), axis=0)
    padded = (counts + M - 1) // M * M
    n_used = jnp.sum(padded) // M
    row = jnp.arange(M, dtype=I32)[None, :]
    pad_keys = jnp.where(row < (padded - counts)[:, None], 2 * experts[:, None] + 1, 2 * N_EXPERTS)
    keys = jnp.concatenate([2 * e_flat, pad_keys.reshape(-1)])
    vals = jnp.concatenate([jnp.arange(A, dtype=I32), jnp.full((N_EXPERTS * M,), PAD_MARK, I32)])
    packed = jnp.sort(keys * (PAD_MARK + 1) + vals).reshape(nblk, M)
    a_s = packed & PAD_MARK
    blk = jnp.arange(nblk, dtype=I32)
    be = jnp.minimum(packed[:, 0] // (2 * (PAD_MARK + 1)), N_EXPERTS - 1)
    be = jnp.where(blk < n_used, be, be[jnp.maximum(n_used - 1, 0)])
    valid = a_s != PAD_MARK
    src_tok = jnp.where(valid, jnp.where(a_s >= T, a_s - T, a_s), 0)
    dst_row = jnp.where(valid, a_s, A + row)
    return be, jnp.sum(valid.astype(I32), axis=1), src_tok, dst_row


def _layer(x2d, p2d, rel_bias, attn_norm, w_in, swa_q_norm, swa_k_norm, swa_sinks, moba_q_norm,
           moba_k_norm, swa_out_norm, moba_out_norm, w_out, ffn_norm, w_rg, b_rg, w_re, b_re,
           w_g, w_u, w_d, ple_norm, w_ple_gate, w_ple_proj, ple_out_norm, B, S):
    T, D = x2d.shape
    row = lambda v: v.reshape(1, -1).astype(F32)
    head_gain = jnp.concatenate([
        jnp.tile(swa_q_norm, SWA_Q_HEADS), jnp.tile(swa_k_norm, SWA_KV_HEADS),
        jnp.tile(moba_q_norm, MOBA_Q_HEADS), jnp.tile(moba_k_norm, MOBA_KV_HEADS)])
    w_in_b = w_in.astype(BF16)
    qa_w, kv_w = SWA_Q_HEADS * HEAD_DIM, SWA_KV_HEADS * HEAD_DIM
    qb_w = MOBA_Q_HEADS * HEAD_DIM
    c_va = qa_w + kv_w
    c_qb = c_va + kv_w
    c_vb = c_qb + qb_w + kv_w
    w_qk = jnp.concatenate([w_in_b[:, :c_va], w_in_b[:, c_qb:c_vb]], axis=1)
    w_vt = jnp.concatenate([w_in_b[:, c_va:c_qb], w_in_b[:, c_vb:]], axis=1).T
    qa, ka, qb, kb, vat, vbt = _qkv_call(x2d, row(attn_norm), w_qk, w_vt, row(head_gain))

    tbl_flat = rel_bias.astype(F32).reshape(-1)
    oa = _swa_call(tbl_flat, swa_sinks.astype(F32), qa, ka, vat, B, S)
    ob = _moba_call(tbl_flat, qb, kb, vbt, B, S)

    pad_rows = lambda n: jnp.zeros((n, D), F32)
    wr = jnp.concatenate([w_rg.T, pad_rows(EXPERT_ROW0 - N_GROUPS), w_re.T,
                          pad_rows(ROUTER_ROWS - EXPERT_ROW0 - N_EXPERTS)], axis=0)
    wr_hi = wr.astype(BF16)
    wr_lo = (wr - wr_hi.astype(F32)).astype(BF16)
    rb = jnp.concatenate([b_rg, jnp.zeros((EXPERT_ROW0 - N_GROUPS,), F32), b_re,
                          jnp.zeros((ROUTER_ROWS - EXPERT_ROW0 - N_EXPERTS,), F32)]).reshape(ROUTER_ROWS, 1)
    wo = w_out.astype(BF16)
    na_w = SWA_Q_HEADS * HEAD_DIM
    x1, hf, eid, wts = _outproj_call(oa, ob, x2d, row(swa_out_norm), row(moba_out_norm),
                                     wo[:na_w], wo[na_w:], row(ffn_norm), wr_hi, wr_lo, rb)

    be, n_real, src_tok, dst_row = _dispatch_plan(eid, T)
    moe_rows = _moe_call(be, n_real, src_tok, dst_row, hf, w_g, w_u, w_d)
    return _ple_call(moe_rows, x1, wts.T, p2d, row(ple_norm), w_ple_gate.astype(BF16),
                     w_ple_proj.astype(BF16), row(ple_out_norm))


def kernel(x, p, rel_bias, attn_norm, w_in, swa_q_norm, swa_k_norm, swa_sinks, moba_q_norm, moba_k_norm,
           swa_out_norm, moba_out_norm, w_out, ffn_norm, w_router_group, b_router_group, w_router_expert,
           b_router_expert, w_exp_gate, w_exp_up, w_exp_down, ple_norm, w_ple_gate, w_ple_proj, ple_out_norm):
    B, S, D = x.shape
    x2d = x.reshape(B * S, D)
    for i in range(p.shape[0]):
        x2d = _layer(x2d, p[i].reshape(B * S, -1), rel_bias, attn_norm[i], w_in[i], swa_q_norm[i],
                     swa_k_norm[i], swa_sinks[i], moba_q_norm[i], moba_k_norm[i], swa_out_norm[i],
                     moba_out_norm[i], w_out[i], ffn_norm[i], w_router_group[i], b_router_group[i],
                     w_router_expert[i], b_router_expert[i], w_exp_gate[i], w_exp_up[i], w_exp_down[i],
                     ple_norm[i], w_ple_gate[i], w_ple_proj[i], ple_out_norm[i], B, S)
    return x2d.reshape(B, S, D)
```

```python
import math

import numpy as np
import jax
import jax.numpy as jnp
from jax import lax
from jax.experimental import pallas as pl
from jax.experimental.pallas import tpu as pltpu

F32 = jnp.float32
BF16 = jnp.bfloat16
I32 = jnp.int32

HEAD_DIM = 64
SWA_Q_HEADS = 8
SWA_KV_HEADS = 2
SWA_WINDOW = 128
MOBA_Q_HEADS = 8
MOBA_KV_HEADS = 2
MOBA_BLOCK = 256
MOBA_TOPK = 3
N_HEADS = SWA_Q_HEADS + MOBA_Q_HEADS
REL_BUCKETS = 32
REL_MAX_DIST = 128
N_GROUPS = 4
EXPERTS_PER_GROUP = 8
N_EXPERTS = N_GROUPS * EXPERTS_PER_GROUP
EXPERT_TOPK = 2
DISPATCH_BLOCK = 512
PAD_MARK = 0xFFFF
EPS = 1e-6
NEG = -1e30
LOG2E = math.log2(math.e)

DMA_PRIORITIES = 2
LANES = 128
ROW_TILE = 8
BF16_SUBLANES = 16
GROUP_HEADS = 4
GROUP_W = GROUP_HEADS * HEAD_DIM
ROUTER_ROWS = 128
EXPERT_ROW0 = 8
TOKEN_TILE = 1024
PLE_TILE = 1024
VMEM_LIMIT = 52 * 1024 * 1024


def _dot_nt(a, b):
    return lax.dot_general(a, b, (((1,), (1,)), ((), ())), preferred_element_type=F32)


def _rmsnorm(x, g):
    ms = jnp.mean(x * x, axis=-1, keepdims=True)
    return x * lax.rsqrt(ms + EPS) * g


def _to_token_tiles(ref, x):
    m = x.shape[0]
    for j in range(x.shape[1] // LANES):
        ref[pl.ds(j, m, stride=ROW_TILE), :] = x[:, LANES * j:LANES * (j + 1)]


def _from_token_tiles(ref, m):
    return jnp.concatenate([ref[pl.ds(j, m, stride=ROW_TILE), :] for j in range(ref.shape[0] // m)], axis=1)


def _rel_bucket_np(dist):
    n = np.maximum(dist, 0)
    exact = REL_BUCKETS // 2
    nf = np.maximum(n, 1).astype(np.float32)
    large = exact + (np.log(nf / exact) / math.log(REL_MAX_DIST / exact)
                     * (REL_BUCKETS - exact)).astype(np.int32)
    return np.where(n < exact, n, np.minimum(large, REL_BUCKETS - 1)).astype(np.int32)


def _band_buckets(block):
    qi = np.arange(block)[:, None]
    kj = np.arange(2 * block)[None, :]
    return _rel_bucket_np(qi + block - kj)


def _bias_from_buckets(bkt, tbl_ref, head):
    acc = jnp.zeros(bkt.shape, F32)
    for j in range(REL_BUCKETS):
        acc = jnp.where(bkt == j, tbl_ref[j * N_HEADS + head] * LOG2E, acc)
    return acc


def _qkv_kernel(x_ref, g_ref, w_ref, wvt_ref, hg_ref, qa_ref, ka_ref, qb_ref, kb_ref, vat_ref, vbt_ref):
    h = _rmsnorm(x_ref[...], g_ref[...]).astype(BF16)
    acc = jnp.dot(h, w_ref[...], preferred_element_type=F32)
    lo = lax.broadcasted_iota(I32, (1, LANES), 1) < HEAD_DIM

    def head_normed(c, scale):
        blk = acc[:, LANES * c:LANES * (c + 1)]
        sq = blk * blk
        s_lo = jnp.sum(jnp.where(lo, sq, 0.0), axis=-1, keepdims=True)
        s_hi = jnp.sum(jnp.where(lo, 0.0, sq), axis=-1, keepdims=True)
        inv = jnp.where(lo, lax.rsqrt(s_lo / HEAD_DIM + EPS), lax.rsqrt(s_hi / HEAD_DIM + EPS))
        return blk * inv * hg_ref[:, LANES * c:LANES * (c + 1)] * scale

    scale = HEAD_DIM ** -0.5 * LOG2E
    for c in range(4):
        qa_ref[:, LANES * c:LANES * (c + 1)] = head_normed(c, scale).astype(BF16)
        qb_ref[:, LANES * c:LANES * (c + 1)] = head_normed(5 + c, scale).astype(BF16)
    ka_ref[...] = head_normed(4, 1.0).astype(BF16)
    kb_ref[...] = head_normed(9, 1.0).astype(BF16)
    vt = _dot_nt(wvt_ref[...], h).astype(BF16)
    for j in range(vat_ref.shape[0]):
        vat_ref[j] = vt[:LANES, SWA_WINDOW * j:SWA_WINDOW * (j + 1)]
    for j in range(vbt_ref.shape[0]):
        vbt_ref[j] = vt[LANES:, MOBA_BLOCK * j:MOBA_BLOCK * (j + 1)]


def _qkv_call(x2d, attn_g, w_qk, w_vt, head_gain):
    T, D = x2d.shape
    tm = TOKEN_TILE
    tok = lambda w: pl.BlockSpec((tm, w), lambda i: (i, 0))
    full = lambda a: pl.BlockSpec(a.shape, lambda i: (0, 0))
    out_w = (SWA_Q_HEADS * HEAD_DIM, LANES, MOBA_Q_HEADS * HEAD_DIM, LANES)
    slabs = lambda blk: pl.BlockSpec((tm // blk, LANES, blk), lambda i: (i, 0, 0))
    slab_shape = lambda blk: jax.ShapeDtypeStruct((T // blk, LANES, blk), BF16)
    return pl.pallas_call(
        _qkv_kernel,
        grid=(T // tm,),
        in_specs=[tok(D), full(attn_g), full(w_qk), full(w_vt), full(head_gain)],
        out_specs=[tok(w) for w in out_w] + [slabs(SWA_WINDOW), slabs(MOBA_BLOCK)],
        out_shape=[jax.ShapeDtypeStruct((T, w), BF16) for w in out_w]
        + [slab_shape(SWA_WINDOW), slab_shape(MOBA_BLOCK)],
        compiler_params=pltpu.CompilerParams(
            dimension_semantics=("arbitrary",), vmem_limit_bytes=VMEM_LIMIT),
        name="qkv",
    )(x2d, attn_g, w_qk, w_vt, head_gain)


def _padded_heads(qblk, heads):
    zeros = jnp.zeros((qblk.shape[0], HEAD_DIM), qblk.dtype)
    pieces = []
    for h in range(heads):
        piece = qblk[:, HEAD_DIM * h:HEAD_DIM * (h + 1)]
        pieces.append(jnp.concatenate([piece, zeros] if h < GROUP_HEADS else [zeros, piece], axis=1))
    return jnp.concatenate(pieces, axis=0)


def _swa_kernel(tbl_ref, sink_ref, bkt_ref, q_ref, k_ref, vt_ref, o_ref, bias_scr):
    W = SWA_WINDOW
    QB = 2 * W
    NK = W + QB
    H = SWA_Q_HEADS
    S = q_ref.shape[0]

    @pl.when(pl.program_id(0) == 0)
    def _init_bias():
        bkt = bkt_ref[...]
        for h in range(H):
            bias_scr[:, QB * h:QB * (h + 1)] = _bias_from_buckets(bkt, tbl_ref, h)

    key_i = lax.broadcasted_iota(I32, (NK, H * QB), 0)
    qry_i = lax.broadcasted_iota(I32, (NK, H * QB), 1) & (QB - 1)
    dist = qry_i + W - key_i
    band = (dist >= 0) & (dist < W)
    real_key = key_i >= W
    head_of_lane = lax.broadcasted_iota(I32, (1, H * QB), 1) // QB
    sinks = jnp.zeros((1, H * QB), F32)
    for h in range(H):
        sinks = jnp.where(head_of_lane == h, sink_ref[h] * LOG2E, sinks)

    def body(i, carry):
        r0 = pl.multiple_of(i * QB, QB)
        p0 = pl.multiple_of(jnp.maximum(r0 - W, 0), W)
        q8 = _padded_heads(q_ref[pl.ds(r0, QB), :], H)
        kband = jnp.concatenate([k_ref[pl.ds(p0, W), :], k_ref[pl.ds(r0, QB), :]], axis=0)
        mask = band & (real_key | (i > 0))
        s = jnp.where(mask, _dot_nt(kband, q8) + bias_scr[...], NEG)
        m = jnp.maximum(jnp.max(s, axis=0, keepdims=True), sinks)
        e = jnp.exp2(s - m)
        den = jnp.sum(e, axis=0, keepdims=True) + jnp.exp2(sinks - m)
        j = 2 * i
        vtband = jnp.concatenate([vt_ref[jnp.maximum(j - 1, 0)], vt_ref[j], vt_ref[j + 1]], axis=1)
        ot = jnp.dot(vtband, e.astype(BF16), preferred_element_type=F32) / den
        for pr in range(H // 2):
            f0 = HEAD_DIM * (2 * pr // GROUP_HEADS)
            pair = ot[f0:f0 + HEAD_DIM, 2 * QB * pr:2 * QB * (pr + 1)]
            o_ref[pl.ds(r0, QB), LANES * pr:LANES * (pr + 1)] = jnp.concatenate(
                [pair[:, :QB], pair[:, QB:]], axis=0).T
        return carry

    lax.fori_loop(0, S // QB, body, 0)


def _swa_call(tbl_flat, sinks, qa, ka, vat, B, S):
    W = SWA_WINDOW
    QB, NK = 2 * W, 3 * W
    bkt = jnp.asarray(_rel_bucket_np(np.arange(QB)[None, :] + W - np.arange(NK)[:, None]))
    smem = pl.BlockSpec(memory_space=pltpu.SMEM)
    return pl.pallas_call(
        _swa_kernel,
        grid=(B,),
        in_specs=[smem, smem,
                  pl.BlockSpec(bkt.shape, lambda b: (0, 0)),
                  pl.BlockSpec((S, qa.shape[1]), lambda b: (b, 0)),
                  pl.BlockSpec((S, LANES), lambda b: (b, 0)),
                  pl.BlockSpec((S // W, LANES, W), lambda b: (b, 0, 0))],
        out_specs=pl.BlockSpec((S, qa.shape[1]), lambda b: (b, 0)),
        out_shape=jax.ShapeDtypeStruct(qa.shape, F32),
        scratch_shapes=[pltpu.VMEM((NK, SWA_Q_HEADS * QB), F32)],
        compiler_params=pltpu.CompilerParams(
            dimension_semantics=("arbitrary",), vmem_limit_bytes=VMEM_LIMIT),
        name="swa",
    )(tbl_flat, sinks, bkt, qa, ka, vat)


def _moba_kernel(tbl_ref, bkt_ref, q_ref, k_ref, vt_ref, o_ref,
                 bias_scr, km_scr, q8_scr, sel_scr, m_scr, l_scr, acc_scr):
    BS = MOBA_BLOCK
    H = MOBA_Q_HEADS
    S = q_ref.shape[0]
    NBK = S // BS
    NQ = H * BS
    GQ = GROUP_HEADS * BS

    @pl.when(pl.program_id(0) == 0)
    def _init_bias():
        bkt = bkt_ref[...]
        for h in range(H):
            band = _bias_from_buckets(bkt, tbl_ref, SWA_Q_HEADS + h)
            far = tbl_ref[(REL_BUCKETS - 1) * N_HEADS + SWA_Q_HEADS + h] * LOG2E
            bias_scr[0, :, BS * h:BS * (h + 1)] = jnp.full((BS, BS), far, F32)
            bias_scr[1, :, BS * h:BS * (h + 1)] = band[:BS]
            bias_scr[2, :, BS * h:BS * (h + 1)] = band[BS:]

    blk_row = lax.broadcasted_iota(I32, (BF16_SUBLANES, 1), 0)
    kmean = jnp.sum(k_ref[...].astype(F32).reshape(NBK, BS, LANES), axis=1) / BS
    kmean = jnp.concatenate([kmean, jnp.zeros((BF16_SUBLANES - NBK, LANES), F32)], axis=0)
    km_hi = kmean.astype(BF16)
    km_scr[0] = km_hi
    km_scr[1] = (kmean - km_hi.astype(F32)).astype(BF16)

    def scores(n, case):
        k0 = pl.multiple_of(n * BS, BS)
        return _dot_nt(k_ref[pl.ds(k0, BS), :], q8_scr[...]) + bias_scr[case]

    ones_rows = jnp.ones((BF16_SUBLANES, BS), BF16)

    def weighted_values(n, e):
        vt_ext = jnp.concatenate([vt_ref[n], ones_rows], axis=0)
        pv = jnp.dot(vt_ext, e.astype(BF16), preferred_element_type=F32)
        vals = jnp.concatenate([pv[:HEAD_DIM, :GQ], pv[HEAD_DIM:LANES, GQ:]], axis=1)
        return vals, pv[LANES:LANES + 1, :]

    def qblock(mi, carry):
        r0 = pl.multiple_of(mi * BS, BS)
        q8 = _padded_heads(q_ref[pl.ds(r0, BS), :], H)
        q8_scr[...] = q8
        gate = _dot_nt(km_scr[0], q8) + _dot_nt(km_scr[1], q8)
        past = blk_row < mi
        gm = jnp.where(past, gate, NEG)
        cnt = jnp.zeros(gm.shape, F32)
        for n in range(NBK - 1):
            col = gm[n:n + 1, :]
            beats = (gm > col) | ((gm == col) & (blk_row < n))
            c = jnp.sum(jnp.where(beats, 1.0, 0.0), axis=0, keepdims=True)
            cnt = jnp.where(blk_row == n, c, cnt)
        sel_scr[...] = jnp.where(past & (cnt < MOBA_TOPK), 1.0, 0.0)

        key_i = lax.broadcasted_iota(I32, (BS, NQ), 0)
        qry_i = lax.broadcasted_iota(I32, (BS, NQ), 1) & (BS - 1)
        s = jnp.where(key_i <= qry_i, scores(mi, 2), NEG)
        m0 = jnp.max(s, axis=0, keepdims=True)
        e = jnp.exp2(s - m0)
        m_scr[...] = m0
        acc_scr[...], l_scr[...] = weighted_values(mi, e)

        def kvblock(j, c2):
            n = mi - j
            s = jnp.where(sel_scr[pl.ds(n, 1), :] > 0.5, scores(n, jnp.where(j == 1, 1, 0)), NEG)
            m_prev = m_scr[...]
            m_new = jnp.maximum(m_prev, jnp.max(s, axis=0, keepdims=True))
            alpha = jnp.exp2(m_prev - m_new)
            e = jnp.exp2(s - m_new)
            vals, row_sum = weighted_values(n, e)
            l_scr[...] = alpha * l_scr[...] + row_sum
            acc_scr[...] = alpha * acc_scr[...] + vals
            m_scr[...] = m_new
            return c2

        lax.fori_loop(1, mi + 1, kvblock, 0)
        ot = acc_scr[...] / l_scr[...]
        for pr in range(H // 2):
            pair = ot[:, 2 * BS * pr:2 * BS * (pr + 1)]
            o_ref[pl.ds(r0, BS), LANES * pr:LANES * (pr + 1)] = jnp.concatenate(
                [pair[:, :BS], pair[:, BS:]], axis=0).T
        return carry

    lax.fori_loop(0, NBK, qblock, 0)


def _moba_call(tbl_flat, qb, kb, vbt, B, S):
    BS = MOBA_BLOCK
    NQ = MOBA_Q_HEADS * BS
    NBK = S // BS
    bkt = jnp.asarray(np.ascontiguousarray(_band_buckets(BS).T))
    smem = pl.BlockSpec(memory_space=pltpu.SMEM)
    return pl.pallas_call(
        _moba_kernel,
        grid=(B,),
        in_specs=[smem,
                  pl.BlockSpec(bkt.shape, lambda b: (0, 0)),
                  pl.BlockSpec((S, qb.shape[1]), lambda b: (b, 0)),
                  pl.BlockSpec((S, LANES), lambda b: (b, 0)),
                  pl.BlockSpec((NBK, LANES, BS), lambda b: (b, 0, 0))],
        out_specs=pl.BlockSpec((S, qb.shape[1]), lambda b: (b, 0)),
        out_shape=jax.ShapeDtypeStruct(qb.shape, F32),
        scratch_shapes=[pltpu.VMEM((3, BS, NQ), F32),
                        pltpu.VMEM((2, BF16_SUBLANES, LANES), BF16),
                        pltpu.VMEM((NQ, LANES), BF16),
                        pltpu.VMEM((BF16_SUBLANES, NQ), F32),
                        pltpu.VMEM((1, NQ), F32),
                        pltpu.VMEM((1, NQ), F32),
                        pltpu.VMEM((HEAD_DIM, NQ), F32)],
        compiler_params=pltpu.CompilerParams(
            dimension_semantics=("arbitrary",), vmem_limit_bytes=VMEM_LIMIT),
        name="moba",
    )(tbl_flat, bkt, qb, kb, vbt)


def _outproj_kernel(oa_ref, ob_ref, x_ref, ga_ref, gb_ref, woa_ref, wob_ref, fg_ref,
                    wrh_ref, wrl_ref, rb_ref, x1_ref, hf_ref, eid_ref, wt_ref):
    na = _rmsnorm(oa_ref[...], ga_ref[...]).astype(BF16)
    nb = _rmsnorm(ob_ref[...], gb_ref[...]).astype(BF16)
    y = (jnp.dot(na, woa_ref[...], preferred_element_type=F32)
         + jnp.dot(nb, wob_ref[...], preferred_element_type=F32))
    x1 = x_ref[...] + y
    x1_ref[...] = x1
    hf = _rmsnorm(x1, fg_ref[...])
    _to_token_tiles(hf_ref, hf)

    hf_hi = hf.astype(BF16)
    hf_lo = (hf - hf_hi.astype(F32)).astype(BF16)
    lt = (_dot_nt(wrh_ref[...], hf_hi) + _dot_nt(wrh_ref[...], hf_lo)
          + _dot_nt(wrl_ref[...], hf_hi) + rb_ref[...])
    gl = [lt[j:j + 1, :] for j in range(N_GROUPS)]
    best = gl[0]
    gsel = jnp.zeros(best.shape, I32)
    for j in range(1, N_GROUPS):
        better = gl[j] > best
        gsel = jnp.where(better, j, gsel)
        best = jnp.where(better, gl[j], best)
    gsum = jnp.zeros(best.shape, F32)
    for j in range(N_GROUPS):
        gsum = gsum + jnp.exp(gl[j] - best)
    gw = 1.0 / gsum
    E = EXPERTS_PER_GROUP
    es = lt[EXPERT_ROW0:EXPERT_ROW0 + E, :]
    for j in range(1, N_GROUPS):
        es = jnp.where(gsel == j, lt[EXPERT_ROW0 + E * j:EXPERT_ROW0 + E * (j + 1), :], es)
    row = lax.broadcasted_iota(I32, es.shape, 0)
    v1 = jnp.max(es, axis=0, keepdims=True)
    i1 = jnp.min(jnp.where(es == v1, row, E), axis=0, keepdims=True)
    es2 = jnp.where(row == i1, -jnp.inf, es)
    v2 = jnp.max(es2, axis=0, keepdims=True)
    i2 = jnp.min(jnp.where(es2 == v2, row, E), axis=0, keepdims=True)
    e2 = jnp.exp(v2 - v1)
    den = 1.0 + e2
    eid_ref[...] = jnp.concatenate([gsel * E + i1, gsel * E + i2], axis=0)
    wt_ref[...] = jnp.concatenate([gw * (1.0 / den), gw * (e2 / den)], axis=0)


def _outproj_call(oa, ob, x2d, ga, gb, woa, wob, fg, wrh, wrl, rb):
    T, D = x2d.shape
    tm = TOKEN_TILE
    tok = lambda w: pl.BlockSpec((tm, w), lambda i: (i, 0))
    full = lambda a: pl.BlockSpec(a.shape, lambda i: (0, 0))
    col = pl.BlockSpec((EXPERT_TOPK, tm), lambda i: (0, i))
    return pl.pallas_call(
        _outproj_kernel,
        grid=(T // tm,),
        in_specs=[tok(oa.shape[1]), tok(ob.shape[1]), tok(D), full(ga), full(gb), full(woa), full(wob),
                  full(fg), full(wrh), full(wrl), full(rb)],
        out_specs=[tok(D), pl.BlockSpec((tm * ROW_TILE, LANES), lambda i: (i, 0)), col, col],
        out_shape=[jax.ShapeDtypeStruct((T, D), F32), jax.ShapeDtypeStruct((T * ROW_TILE, LANES), F32),
                   jax.ShapeDtypeStruct((EXPERT_TOPK, T), I32), jax.ShapeDtypeStruct((EXPERT_TOPK, T), F32)],
        compiler_params=pltpu.CompilerParams(
            dimension_semantics=("arbitrary",), vmem_limit_bytes=VMEM_LIMIT),
        name="outproj",
    )(oa, ob, x2d, ga, gb, woa, wob, fg, wrh, wrl, rb)


COPY_GROUP = 64


def _moe_kernel(be_ref, nv_ref, src_ref, srcn_ref, dst_ref, hf_hbm, wg_ref, wu_ref, wd_ref, out_hbm,
                xbuf, obuf, gsem, ssem):
    i = pl.program_id(0)
    nblk = pl.num_programs(0)
    M = xbuf.shape[1] // ROW_TILE
    slot = i % 2
    nv_cur = nv_ref[i]
    nv_next = jnp.where(i + 1 < nblk, nv_ref[jnp.minimum(i + 1, nblk - 1)], 0)

    def for_real_groups(n_real, fn):
        for k in range(M // COPY_GROUP):
            @pl.when(k * COPY_GROUP < n_real)
            def _group():
                fn(range(k * COPY_GROUP, (k + 1) * COPY_GROUP))

    def gather_rows(idx_ref, buf_slot, n_real):
        def start(rows):
            for r in rows:
                tok_row = pl.multiple_of(idx_ref[0, 0, r] * ROW_TILE, ROW_TILE)
                pltpu.make_async_copy(hf_hbm.at[pl.ds(tok_row, ROW_TILE)],
                                      xbuf.at[buf_slot, pl.ds(r * ROW_TILE, ROW_TILE)],
                                      gsem.at[buf_slot]).start(priority=r % DMA_PRIORITIES)
        for_real_groups(n_real, start)

    def scatter_rows(idx_ref, buf_slot, n_real):
        def start(rows):
            for r in rows:
                dst_row = pl.multiple_of(idx_ref[0, 0, r] * ROW_TILE, ROW_TILE)
                pltpu.make_async_copy(obuf.at[buf_slot, pl.ds(r * ROW_TILE, ROW_TILE)],
                                      out_hbm.at[pl.ds(dst_row, ROW_TILE)],
                                      ssem.at[buf_slot]).start(priority=r % DMA_PRIORITIES)
        for_real_groups(n_real, start)

    def wait_rows(buf, buf_slot, sem, n_real):
        span = COPY_GROUP * ROW_TILE
        for_real_groups(n_real, lambda rows: pltpu.make_async_copy(
            hf_hbm.at[pl.ds(0, span)], buf.at[buf_slot, pl.ds(0, span)], sem.at[buf_slot]).wait())

    @pl.when(nv_cur > 0)
    def _compute():
        @pl.when(i == 0)
        def _first_gather():
            gather_rows(src_ref, 0, nv_cur)
            obuf[1] = jnp.zeros(obuf.shape[1:], F32)
            spare = pltpu.make_async_copy(
                obuf.at[1], out_hbm.at[pl.ds(out_hbm.shape[0] - M * ROW_TILE, M * ROW_TILE)], ssem.at[1])
            spare.start()
            spare.wait()

        wait_rows(xbuf, slot, gsem, nv_cur)

        @pl.when(i >= 2)
        def _reuse_obuf():
            wait_rows(obuf, slot, ssem, nv_ref[jnp.maximum(i - 2, 0)])

        gather_rows(srcn_ref, 1 - slot, nv_next)
        xb = _from_token_tiles(xbuf.at[slot], M).astype(BF16)
        a = jnp.dot(xb, wg_ref[0].astype(BF16), preferred_element_type=F32)
        u = jnp.dot(xb, wu_ref[0].astype(BF16), preferred_element_type=F32)
        hmid = (a * jax.nn.sigmoid(a)) * u
        _to_token_tiles(obuf.at[slot], jnp.dot(hmid.astype(BF16), wd_ref[0].astype(BF16),
                                               preferred_element_type=F32))
        scatter_rows(dst_ref, slot, nv_cur)

        @pl.when(nv_next == 0)
        def _drain():
            wait_rows(obuf, slot, ssem, nv_cur)

            @pl.when(i >= 1)
            def _drain_prev():
                wait_rows(obuf, 1 - slot, ssem, nv_ref[jnp.maximum(i - 1, 0)])


def _moe_call(block_expert, n_real, src_tok, dst_row, hf, w_g, w_u, w_d):
    T = hf.shape[0] // ROW_TILE
    M = DISPATCH_BLOCK
    nblk = block_expert.shape[0]
    D, DE = w_g.shape[1], w_g.shape[2]
    assert D == ROW_TILE * LANES and hf.shape[1] == LANES and M % COPY_GROUP == 0
    idx_spec = lambda f: pl.BlockSpec((1, 1, M), f, memory_space=pltpu.SMEM)
    cur = lambda i, be, nu: (i, 0, 0)
    nxt = lambda i, be, nu: (jnp.minimum(i + 1, nblk - 1), 0, 0)
    wspec = lambda a, b: pl.BlockSpec((1, a, b), lambda i, be, nu: (be[i], 0, 0))
    grid_spec = pltpu.PrefetchScalarGridSpec(
        num_scalar_prefetch=2,
        grid=(nblk,),
        in_specs=[idx_spec(cur), idx_spec(nxt), idx_spec(cur),
                  pl.BlockSpec(memory_space=pl.ANY),
                  wspec(D, DE), wspec(D, DE), wspec(DE, D)],
        out_specs=pl.BlockSpec(memory_space=pl.ANY),
        scratch_shapes=[pltpu.VMEM((2, M * ROW_TILE, LANES), F32), pltpu.VMEM((2, M * ROW_TILE, LANES), F32),
                        pltpu.SemaphoreType.DMA((2,)), pltpu.SemaphoreType.DMA((2,))],
    )
    src3 = src_tok.reshape(nblk, 1, M)
    return pl.pallas_call(
        _moe_kernel,
        grid_spec=grid_spec,
        out_shape=jax.ShapeDtypeStruct(((EXPERT_TOPK * T + M) * ROW_TILE, LANES), F32),
        compiler_params=pltpu.CompilerParams(
            dimension_semantics=("arbitrary",), vmem_limit_bytes=VMEM_LIMIT),
        name="moe",
    )(block_expert, n_real, src3, src3, dst_row.reshape(nblk, 1, M), hf, w_g, w_u, w_d)


def _ple_kernel(x1_ref, m0_ref, m1_ref, wt_ref, p_ref, pg_ref, wgate_ref, wproj_ref, pog_ref, out_ref):
    proj = _rmsnorm(jnp.dot(p_ref[...].astype(BF16), wproj_ref[...], preferred_element_type=F32),
                    pog_ref[...])
    tm = x1_ref.shape[0]
    moe = (_from_token_tiles(m0_ref, tm) * wt_ref[:, 0:1] + _from_token_tiles(m1_ref, tm) * wt_ref[:, 1:2])
    x2 = x1_ref[...] + moe
    hn = _rmsnorm(x2, pg_ref[...]).astype(BF16)
    gate = jax.nn.sigmoid(jnp.dot(hn, wgate_ref[...], preferred_element_type=F32))
    out_ref[...] = x2 + gate * proj


def _ple_call(moe_rows, x1, wt_cols, p2d, pg, wgate, wproj, pog):
    T, D = x1.shape
    tm = PLE_TILE
    tok = lambda w: pl.BlockSpec((tm, w), lambda i: (i, 0))
    full = lambda a: pl.BlockSpec(a.shape, lambda i: (0, 0))
    first = pl.BlockSpec((tm * ROW_TILE, LANES), lambda i: (i, 0))
    second = pl.BlockSpec((tm * ROW_TILE, LANES), lambda i: (T // tm + i, 0))
    return pl.pallas_call(
        _ple_kernel,
        grid=(T // tm,),
        in_specs=[tok(D), first, second, tok(EXPERT_TOPK), tok(p2d.shape[1]),
                  full(pg), full(wgate), full(wproj), full(pog)],
        out_specs=tok(D),
        out_shape=jax.ShapeDtypeStruct((T, D), F32),
        compiler_params=pltpu.CompilerParams(
            dimension_semantics=("arbitrary",), vmem_limit_bytes=VMEM_LIMIT),
        name="ple",
    )(x1, moe_rows, moe_rows, wt_cols, p2d, pg, wgate, wproj, pog)


def _dispatch_plan(eid, T):
    A = EXPERT_TOPK * T
    M = DISPATCH_BLOCK
    nblk = A // M + N_EXPERTS
    assert A <= PAD_MARK
    e_flat = eid.reshape(A)
    experts = jnp.arange(N_EXPERTS, dtype=I32)
    counts = jnp.sum((e_flat[:, None] == experts[None, :]).astype(I32), axis=0)
    padded = (counts + M - 1) // M * M
    n_used = jnp.sum(padded) // M
    row = jnp.arange(M, dtype=I32)[None, :]
    pad_keys = jnp.where(row < (padded - counts)[:, None], 2 * experts[:, None] + 1, 2 * N_EXPERTS)
    keys = jnp.concatenate([2 * e_flat, pad_keys.reshape(-1)])
    vals = jnp.concatenate([jnp.arange(A, dtype=I32), jnp.full((N_EXPERTS * M,), PAD_MARK, I32)])
    packed = jnp.sort(keys * (PAD_MARK + 1) + vals).reshape(nblk, M)
    a_s = packed & PAD_MARK
    blk = jnp.arange(nblk, dtype=I32)
    be = jnp.minimum(packed[:, 0] // (2 * (PAD_MARK + 1)), N_EXPERTS - 1)
    be = jnp.where(blk < n_used, be, be[jnp.maximum(n_used - 1, 0)])
    valid = a_s != PAD_MARK
    src_tok = jnp.where(valid, jnp.where(a_s >= T, a_s - T, a_s), 0)
    dst_row = jnp.where(valid, a_s, A + row)
    return be, jnp.sum(valid.astype(I32), axis=1), src_tok, dst_row


def _layer(x2d, p2d, rel_bias, attn_norm, w_in, swa_q_norm, swa_k_norm, swa_sinks, moba_q_norm,
           moba_k_norm, swa_out_norm, moba_out_norm, w_out, ffn_norm, w_rg, b_rg, w_re, b_re,
           w_g, w_u, w_d, ple_norm, w_ple_gate, w_ple_proj, ple_out_norm, B, S):
    T, D = x2d.shape
    row = lambda v: v.reshape(1, -1).astype(F32)
    head_gain = jnp.concatenate([
        jnp.tile(swa_q_norm, SWA_Q_HEADS), jnp.tile(swa_k_norm, SWA_KV_HEADS),
        jnp.tile(moba_q_norm, MOBA_Q_HEADS), jnp.tile(moba_k_norm, MOBA_KV_HEADS)])
    w_in_b = w_in.astype(BF16)
    qa_w, kv_w = SWA_Q_HEADS * HEAD_DIM, SWA_KV_HEADS * HEAD_DIM
    qb_w = MOBA_Q_HEADS * HEAD_DIM
    c_va = qa_w + kv_w
    c_qb = c_va + kv_w
    c_vb = c_qb + qb_w + kv_w
    w_qk = jnp.concatenate([w_in_b[:, :c_va], w_in_b[:, c_qb:c_vb]], axis=1)
    w_vt = jnp.concatenate([w_in_b[:, c_va:c_qb], w_in_b[:, c_vb:]], axis=1).T
    qa, ka, qb, kb, vat, vbt = _qkv_call(x2d, row(attn_norm), w_qk, w_vt, row(head_gain))

    tbl_flat = rel_bias.astype(F32).reshape(-1)
    oa = _swa_call(tbl_flat, swa_sinks.astype(F32), qa, ka, vat, B, S)
    ob = _moba_call(tbl_flat, qb, kb, vbt, B, S)

    pad_rows = lambda n: jnp.zeros((n, D), F32)
    wr = jnp.concatenate([w_rg.T, pad_rows(EXPERT_ROW0 - N_GROUPS), w_re.T,
                          pad_rows(ROUTER_ROWS - EXPERT_ROW0 - N_EXPERTS)], axis=0)
    wr_hi = wr.astype(BF16)
    wr_lo = (wr - wr_hi.astype(F32)).astype(BF16)
    rb = jnp.concatenate([b_rg, jnp.zeros((EXPERT_ROW0 - N_GROUPS,), F32), b_re,
                          jnp.zeros((ROUTER_ROWS - EXPERT_ROW0 - N_EXPERTS,), F32)]).reshape(ROUTER_ROWS, 1)
    wo = w_out.astype(BF16)
    na_w = SWA_Q_HEADS * HEAD_DIM
    x1, hf, eid, wts = _outproj_call(oa, ob, x2d, row(swa_out_norm), row(moba_out_norm),
                                     wo[:na_w], wo[na_w:], row(ffn_norm), wr_hi, wr_lo, rb)

    be, n_real, src_tok, dst_row = _dispatch_plan(eid, T)
    moe_rows = _moe_call(be, n_real, src_tok, dst_row, hf, w_g, w_u, w_d)
    return _ple_call(moe_rows, x1, wts.T, p2d, row(ple_norm), w_ple_gate.astype(BF16),
                     w_ple_proj.astype(BF16), row(ple_out_norm))


def kernel(x, p, rel_bias, attn_norm, w_in, swa_q_norm, swa_k_norm, swa_sinks, moba_q_norm, moba_k_norm,
           swa_out_norm, moba_out_norm, w_out, ffn_norm, w_router_group, b_router_group, w_router_expert,
           b_router_expert, w_exp_gate, w_exp_up, w_exp_down, ple_norm, w_ple_gate, w_ple_proj, ple_out_norm):
    B, S, D = x.shape
    x2d = x.reshape(B * S, D)
    for i in range(p.shape[0]):
        x2d = _layer(x2d, p[i].reshape(B * S, -1), rel_bias, attn_norm[i], w_in[i], swa_q_norm[i],
                     swa_k_norm[i], swa_sinks[i], moba_q_norm[i], moba_k_norm[i], swa_out_norm[i],
                     moba_out_norm[i], w_out[i], ffn_norm[i], w_router_group[i], b_router_group[i],
                     w_router_expert[i], b_router_expert[i], w_exp_gate[i], w_exp_up[i], w_exp_down[i],
                     ple_norm[i], w_ple_gate[i], w_ple_proj[i], ple_out_norm[i], B, S)
    return x2d.reshape(B, S, D)
```

```python
import math

import numpy as np
import jax
import jax.numpy as jnp
from jax import lax
from jax.experimental import pallas as pl
from jax.experimental.pallas import tpu as pltpu

F32 = jnp.float32
BF16 = jnp.bfloat16
I32 = jnp.int32

HEAD_DIM = 64
SWA_Q_HEADS = 8
SWA_KV_HEADS = 2
SWA_WINDOW = 128
MOBA_Q_HEADS = 8
MOBA_KV_HEADS = 2
MOBA_BLOCK = 256
MOBA_TOPK = 3
N_HEADS = SWA_Q_HEADS + MOBA_Q_HEADS
REL_BUCKETS = 32
REL_MAX_DIST = 128
N_GROUPS = 4
EXPERTS_PER_GROUP = 8
N_EXPERTS = N_GROUPS * EXPERTS_PER_GROUP
EXPERT_TOPK = 2
DISPATCH_BLOCK = 512
PAD_MARK = 0xFFFF
EPS = 1e-6
NEG = -1e30
LOG2E = math.log2(math.e)

DMA_PRIORITIES = 2
LANES = 128
ROW_TILE = 8
BF16_SUBLANES = 16
GROUP_HEADS = 4
GROUP_W = GROUP_HEADS * HEAD_DIM
ROUTER_ROWS = 128
EXPERT_ROW0 = 8
TOKEN_TILE = 1024
PLE_TILE = 1024
VMEM_LIMIT = 52 * 1024 * 1024


def _dot_nt(a, b):
    return lax.dot_general(a, b, (((1,), (1,)), ((), ())), preferred_element_type=F32)


def _rmsnorm(x, g):
    ms = jnp.mean(x * x, axis=-1, keepdims=True)
    return x * lax.rsqrt(ms + EPS) * g


def _to_token_tiles(ref, x):
    m = x.shape[0]
    for j in range(x.shape[1] // LANES):
        ref[pl.ds(j, m, stride=ROW_TILE), :] = x[:, LANES * j:LANES * (j + 1)]


def _from_token_tiles(ref, m):
    return jnp.concatenate([ref[pl.ds(j, m, stride=ROW_TILE), :] for j in range(ROW_TILE)], axis=1)


def _rel_bucket_np(dist):
    n = np.maximum(dist, 0)
    exact = REL_BUCKETS // 2
    nf = np.maximum(n, 1).astype(np.float32)
    large = exact + (np.log(nf / exact) / math.log(REL_MAX_DIST / exact)
                     * (REL_BUCKETS - exact)).astype(np.int32)
    return np.where(n < exact, n, np.minimum(large, REL_BUCKETS - 1)).astype(np.int32)


def _band_buckets(block):
    qi = np.arange(block)[:, None]
    kj = np.arange(2 * block)[None, :]
    return _rel_bucket_np(qi + block - kj)


def _bias_from_buckets(bkt, tbl_ref, head):
    acc = jnp.zeros(bkt.shape, F32)
    for j in range(REL_BUCKETS):
        acc = jnp.where(bkt == j, tbl_ref[j * N_HEADS + head] * LOG2E, acc)
    return acc


def _qkv_kernel(x_ref, g_ref, w_ref, wvt_ref, hg_ref, qa_ref, ka_ref, qb_ref, kb_ref, vat_ref, vbt_ref):
    h = _rmsnorm(x_ref[...], g_ref[...]).astype(BF16)
    acc = jnp.dot(h, w_ref[...], preferred_element_type=F32)
    lo = lax.broadcasted_iota(I32, (1, LANES), 1) < HEAD_DIM

    def head_normed(c, scale):
        blk = acc[:, LANES * c:LANES * (c + 1)]
        sq = blk * blk
        s_lo = jnp.sum(jnp.where(lo, sq, 0.0), axis=-1, keepdims=True)
        s_hi = jnp.sum(jnp.where(lo, 0.0, sq), axis=-1, keepdims=True)
        inv = jnp.where(lo, lax.rsqrt(s_lo / HEAD_DIM + EPS), lax.rsqrt(s_hi / HEAD_DIM + EPS))
        return blk * inv * hg_ref[:, LANES * c:LANES * (c + 1)] * scale

    scale = HEAD_DIM ** -0.5 * LOG2E
    for c in range(4):
        qa_ref[:, LANES * c:LANES * (c + 1)] = head_normed(c, scale).astype(BF16)
        qb_ref[:, LANES * c:LANES * (c + 1)] = head_normed(5 + c, scale).astype(BF16)
    ka_ref[...] = head_normed(4, 1.0).astype(BF16)
    kb_ref[...] = head_normed(9, 1.0).astype(BF16)
    vt = _dot_nt(wvt_ref[...], h).astype(BF16)
    for j in range(vat_ref.shape[0]):
        vat_ref[j] = vt[:LANES, SWA_WINDOW * j:SWA_WINDOW * (j + 1)]
    for j in range(vbt_ref.shape[0]):
        vbt_ref[j] = vt[LANES:, MOBA_BLOCK * j:MOBA_BLOCK * (j + 1)]


def _qkv_call(x2d, attn_g, w_qk, w_vt, head_gain):
    T, D = x2d.shape
    tm = TOKEN_TILE
    tok = lambda w: pl.BlockSpec((tm, w), lambda i: (i, 0))
    full = lambda a: pl.BlockSpec(a.shape, lambda i: (0, 0))
    out_w = (SWA_Q_HEADS * HEAD_DIM, LANES, MOBA_Q_HEADS * HEAD_DIM, LANES)
    slabs = lambda blk: pl.BlockSpec((tm // blk, LANES, blk), lambda i: (i, 0, 0))
    slab_shape = lambda blk: jax.ShapeDtypeStruct((T // blk, LANES, blk), BF16)
    return pl.pallas_call(
        _qkv_kernel,
        grid=(T // tm,),
        in_specs=[tok(D), full(attn_g), full(w_qk), full(w_vt), full(head_gain)],
        out_specs=[tok(w) for w in out_w] + [slabs(SWA_WINDOW), slabs(MOBA_BLOCK)],
        out_shape=[jax.ShapeDtypeStruct((T, w), BF16) for w in out_w]
        + [slab_shape(SWA_WINDOW), slab_shape(MOBA_BLOCK)],
        compiler_params=pltpu.CompilerParams(
            dimension_semantics=("arbitrary",), vmem_limit_bytes=VMEM_LIMIT),
        name="qkv",
    )(x2d, attn_g, w_qk, w_vt, head_gain)


def _padded_heads(qblk, heads):
    zeros = jnp.zeros((qblk.shape[0], HEAD_DIM), qblk.dtype)
    pieces = []
    for h in range(heads):
        piece = qblk[:, HEAD_DIM * h:HEAD_DIM * (h + 1)]
        pieces.append(jnp.concatenate([piece, zeros] if h < GROUP_HEADS else [zeros, piece], axis=1))
    return jnp.concatenate(pieces, axis=0)


def _swa_kernel(tbl_ref, sink_ref, bkt_ref, q_ref, k_ref, vt_ref, o_ref, bias_scr):
    W = SWA_WINDOW
    QB = 2 * W
    NK = W + QB
    H = SWA_Q_HEADS
    S = q_ref.shape[0]

    @pl.when(pl.program_id(0) == 0)
    def _init_bias():
        bkt = bkt_ref[...]
        for h in range(H):
            bias_scr[:, QB * h:QB * (h + 1)] = _bias_from_buckets(bkt, tbl_ref, h)

    key_i = lax.broadcasted_iota(I32, (NK, H * QB), 0)
    qry_i = lax.broadcasted_iota(I32, (NK, H * QB), 1) & (QB - 1)
    dist = qry_i + W - key_i
    band = (dist >= 0) & (dist < W)
    real_key = key_i >= W
    head_of_lane = lax.broadcasted_iota(I32, (1, H * QB), 1) // QB
    sinks = jnp.zeros((1, H * QB), F32)
    for h in range(H):
        sinks = jnp.where(head_of_lane == h, sink_ref[h] * LOG2E, sinks)

    def body(i, carry):
        r0 = pl.multiple_of(i * QB, QB)
        p0 = pl.multiple_of(jnp.maximum(r0 - W, 0), W)
        q8 = _padded_heads(q_ref[pl.ds(r0, QB), :], H)
        kband = jnp.concatenate([k_ref[pl.ds(p0, W), :], k_ref[pl.ds(r0, QB), :]], axis=0)
        mask = band & (real_key | (i > 0))
        s = jnp.where(mask, _dot_nt(kband, q8) + bias_scr[...], NEG)
        m = jnp.maximum(jnp.max(s, axis=0, keepdims=True), sinks)
        e = jnp.exp2(s - m)
        den = jnp.sum(e, axis=0, keepdims=True) + jnp.exp2(sinks - m)
        j = 2 * i
        vtband = jnp.concatenate([vt_ref[jnp.maximum(j - 1, 0)], vt_ref[j], vt_ref[j + 1]], axis=1)
        ot = jnp.dot(vtband, e.astype(BF16), preferred_element_type=F32) / den
        for pr in range(H // 2):
            f0 = HEAD_DIM * (2 * pr // GROUP_HEADS)
            pair = ot[f0:f0 + HEAD_DIM, 2 * QB * pr:2 * QB * (pr + 1)]
            o_ref[pl.ds(r0, QB), LANES * pr:LANES * (pr + 1)] = jnp.concatenate(
                [pair[:, :QB], pair[:, QB:]], axis=0).T
        return carry

    lax.fori_loop(0, S // QB, body, 0)


def _swa_call(tbl_flat, sinks, qa, ka, vat, B, S):
    W = SWA_WINDOW
    QB, NK = 2 * W, 3 * W
    bkt = jnp.asarray(_rel_bucket_np(np.arange(QB)[None, :] + W - np.arange(NK)[:, None]))
    smem = pl.BlockSpec(memory_space=pltpu.SMEM)
    return pl.pallas_call(
        _swa_kernel,
        grid=(B,),
        in_specs=[smem, smem,
                  pl.BlockSpec(bkt.shape, lambda b: (0, 0)),
                  pl.BlockSpec((S, qa.shape[1]), lambda b: (b, 0)),
                  pl.BlockSpec((S, LANES), lambda b: (b, 0)),
                  pl.BlockSpec((S // W, LANES, W), lambda b: (b, 0, 0))],
        out_specs=pl.BlockSpec((S, qa.shape[1]), lambda b: (b, 0)),
        out_shape=jax.ShapeDtypeStruct(qa.shape, F32),
        scratch_shapes=[pltpu.VMEM((NK, SWA_Q_HEADS * QB), F32)],
        compiler_params=pltpu.CompilerParams(
            dimension_semantics=("arbitrary",), vmem_limit_bytes=VMEM_LIMIT),
        name="swa",
    )(tbl_flat, sinks, bkt, qa, ka, vat)


def _moba_kernel(tbl_ref, bkt_ref, q_ref, k_ref, vt_ref, o_ref,
                 bias_scr, km_scr, q8_scr, sel_scr, m_scr, l_scr, acc_scr):
    BS = MOBA_BLOCK
    H = MOBA_Q_HEADS
    S = q_ref.shape[0]
    NBK = S // BS
    NQ = H * BS
    GQ = GROUP_HEADS * BS

    @pl.when(pl.program_id(0) == 0)
    def _init_bias():
        bkt = bkt_ref[...]
        for h in range(H):
            band = _bias_from_buckets(bkt, tbl_ref, SWA_Q_HEADS + h)
            far = tbl_ref[(REL_BUCKETS - 1) * N_HEADS + SWA_Q_HEADS + h] * LOG2E
            bias_scr[0, :, BS * h:BS * (h + 1)] = jnp.full((BS, BS), far, F32)
            bias_scr[1, :, BS * h:BS * (h + 1)] = band[:BS]
            bias_scr[2, :, BS * h:BS * (h + 1)] = band[BS:]

    blk_row = lax.broadcasted_iota(I32, (BF16_SUBLANES, 1), 0)
    kmean = jnp.sum(k_ref[...].astype(F32).reshape(NBK, BS, LANES), axis=1) / BS
    kmean = jnp.concatenate([kmean, jnp.zeros((BF16_SUBLANES - NBK, LANES), F32)], axis=0)
    km_hi = kmean.astype(BF16)
    km_scr[0] = km_hi
    km_scr[1] = (kmean - km_hi.astype(F32)).astype(BF16)

    def scores(n, case):
        k0 = pl.multiple_of(n * BS, BS)
        return _dot_nt(k_ref[pl.ds(k0, BS), :], q8_scr[...]) + bias_scr[case]

    ones_rows = jnp.ones((BF16_SUBLANES, BS), BF16)

    def weighted_values(n, e):
        vt_ext = jnp.concatenate([vt_ref[n], ones_rows], axis=0)
        pv = jnp.dot(vt_ext, e.astype(BF16), preferred_element_type=F32)
        vals = jnp.concatenate([pv[:HEAD_DIM, :GQ], pv[HEAD_DIM:LANES, GQ:]], axis=1)
        return vals, pv[LANES:LANES + 1, :]

    def qblock(mi, carry):
        r0 = pl.multiple_of(mi * BS, BS)
        q8 = _padded_heads(q_ref[pl.ds(r0, BS), :], H)
        q8_scr[...] = q8
        gate = _dot_nt(km_scr[0], q8) + _dot_nt(km_scr[1], q8)
        past = blk_row < mi
        gm = jnp.where(past, gate, NEG)
        cnt = jnp.zeros(gm.shape, F32)
        for n in range(NBK - 1):
            col = gm[n:n + 1, :]
            beats = (gm > col) | ((gm == col) & (blk_row < n))
            c = jnp.sum(jnp.where(beats, 1.0, 0.0), axis=0, keepdims=True)
            cnt = jnp.where(blk_row == n, c, cnt)
        sel_scr[...] = jnp.where(past & (cnt < MOBA_TOPK), 1.0, 0.0)

        key_i = lax.broadcasted_iota(I32, (BS, NQ), 0)
        qry_i = lax.broadcasted_iota(I32, (BS, NQ), 1) & (BS - 1)
        s = jnp.where(key_i <= qry_i, scores(mi, 2), NEG)
        m0 = jnp.max(s, axis=0, keepdims=True)
        e = jnp.exp2(s - m0)
        m_scr[...] = m0
        acc_scr[...], l_scr[...] = weighted_values(mi, e)

        def kvblock(j, c2):
            n = mi - j
            s = jnp.where(sel_scr[pl.ds(n, 1), :] > 0.5, scores(n, jnp.where(j == 1, 1, 0)), NEG)
            m_prev = m_scr[...]
            m_new = jnp.maximum(m_prev, jnp.max(s, axis=0, keepdims=True))
            alpha = jnp.exp2(m_prev - m_new)
            e = jnp.exp2(s - m_new)
            vals, row_sum = weighted_values(n, e)
            l_scr[...] = alpha * l_scr[...] + row_sum
            acc_scr[...] = alpha * acc_scr[...] + vals
            m_scr[...] = m_new
            return c2

        lax.fori_loop(1, mi + 1, kvblock, 0)
        ot = acc_scr[...] / l_scr[...]
        for pr in range(H // 2):
            pair = ot[:, 2 * BS * pr:2 * BS * (pr + 1)]
            o_ref[pl.ds(r0, BS), LANES * pr:LANES * (pr + 1)] = jnp.concatenate(
                [pair[:, :BS], pair[:, BS:]], axis=0).T
        return carry

    lax.fori_loop(0, NBK, qblock, 0)


def _moba_call(tbl_flat, qb, kb, vbt, B, S):
    BS = MOBA_BLOCK
    NQ = MOBA_Q_HEADS * BS
    NBK = S // BS
    bkt = jnp.asarray(np.ascontiguousarray(_band_buckets(BS).T))
    smem = pl.BlockSpec(memory_space=pltpu.SMEM)
    return pl.pallas_call(
        _moba_kernel,
        grid=(B,),
        in_specs=[smem,
                  pl.BlockSpec(bkt.shape, lambda b: (0, 0)),
                  pl.BlockSpec((S, qb.shape[1]), lambda b: (b, 0)),
                  pl.BlockSpec((S, LANES), lambda b: (b, 0)),
                  pl.BlockSpec((NBK, LANES, BS), lambda b: (b, 0, 0))],
        out_specs=pl.BlockSpec((S, qb.shape[1]), lambda b: (b, 0)),
        out_shape=jax.ShapeDtypeStruct(qb.shape, F32),
        scratch_shapes=[pltpu.VMEM((3, BS, NQ), F32),
                        pltpu.VMEM((2, BF16_SUBLANES, LANES), BF16),
                        pltpu.VMEM((NQ, LANES), BF16),
                        pltpu.VMEM((BF16_SUBLANES, NQ), F32),
                        pltpu.VMEM((1, NQ), F32),
                        pltpu.VMEM((1, NQ), F32),
                        pltpu.VMEM((HEAD_DIM, NQ), F32)],
        compiler_params=pltpu.CompilerParams(
            dimension_semantics=("arbitrary",), vmem_limit_bytes=VMEM_LIMIT),
        name="moba",
    )(tbl_flat, bkt, qb, kb, vbt)


def _outproj_kernel(oa_ref, ob_ref, x_ref, ga_ref, gb_ref, woa_ref, wob_ref, fg_ref,
                    wrh_ref, wrl_ref, rb_ref, x1_ref, hf_ref, eid_ref, wt_ref):
    na = _rmsnorm(oa_ref[...], ga_ref[...]).astype(BF16)
    nb = _rmsnorm(ob_ref[...], gb_ref[...]).astype(BF16)
    y = (jnp.dot(na, woa_ref[...], preferred_element_type=F32)
         + jnp.dot(nb, wob_ref[...], preferred_element_type=F32))
    x1 = x_ref[...] + y
    x1_ref[...] = x1
    hf = _rmsnorm(x1, fg_ref[...])
    _to_token_tiles(hf_ref, hf)

    hf_hi = hf.astype(BF16)
    hf_lo = (hf - hf_hi.astype(F32)).astype(BF16)
    lt = (_dot_nt(wrh_ref[...], hf_hi) + _dot_nt(wrh_ref[...], hf_lo)
          + _dot_nt(wrl_ref[...], hf_hi) + rb_ref[...])
    gl = [lt[j:j + 1, :] for j in range(N_GROUPS)]
    best = gl[0]
    gsel = jnp.zeros(best.shape, I32)
    for j in range(1, N_GROUPS):
        better = gl[j] > best
        gsel = jnp.where(better, j, gsel)
        best = jnp.where(better, gl[j], best)
    gsum = jnp.zeros(best.shape, F32)
    for j in range(N_GROUPS):
        gsum = gsum + jnp.exp(gl[j] - best)
    gw = 1.0 / gsum
    E = EXPERTS_PER_GROUP
    es = lt[EXPERT_ROW0:EXPERT_ROW0 + E, :]
    for j in range(1, N_GROUPS):
        es = jnp.where(gsel == j, lt[EXPERT_ROW0 + E * j:EXPERT_ROW0 + E * (j + 1), :], es)
    row = lax.broadcasted_iota(I32, es.shape, 0)
    v1 = jnp.max(es, axis=0, keepdims=True)
    i1 = jnp.min(jnp.where(es == v1, row, E), axis=0, keepdims=True)
    es2 = jnp.where(row == i1, -jnp.inf, es)
    v2 = jnp.max(es2, axis=0, keepdims=True)
    i2 = jnp.min(jnp.where(es2 == v2, row, E), axis=0, keepdims=True)
    e2 = jnp.exp(v2 - v1)
    den = 1.0 + e2
    eid_ref[...] = jnp.concatenate([gsel * E + i1, gsel * E + i2], axis=0)
    wt_ref[...] = jnp.concatenate([gw * (1.0 / den), gw * (e2 / den)], axis=0)


def _outproj_call(oa, ob, x2d, ga, gb, woa, wob, fg, wrh, wrl, rb):
    T, D = x2d.shape
    tm = TOKEN_TILE
    tok = lambda w: pl.BlockSpec((tm, w), lambda i: (i, 0))
    full = lambda a: pl.BlockSpec(a.shape, lambda i: (0, 0))
    col = pl.BlockSpec((EXPERT_TOPK, tm), lambda i: (0, i))
    return pl.pallas_call(
        _outproj_kernel,
        grid=(T // tm,),
        in_specs=[tok(oa.shape[1]), tok(ob.shape[1]), tok(D), full(ga), full(gb), full(woa), full(wob),
                  full(fg), full(wrh), full(wrl), full(rb)],
        out_specs=[tok(D), pl.BlockSpec((tm * ROW_TILE, LANES), lambda i: (i, 0)), col, col],
        out_shape=[jax.ShapeDtypeStruct((T, D), F32), jax.ShapeDtypeStruct((T * ROW_TILE, LANES), F32),
                   jax.ShapeDtypeStruct((EXPERT_TOPK, T), I32), jax.ShapeDtypeStruct((EXPERT_TOPK, T), F32)],
        compiler_params=pltpu.CompilerParams(
            dimension_semantics=("arbitrary",), vmem_limit_bytes=VMEM_LIMIT),
        name="outproj",
    )(oa, ob, x2d, ga, gb, woa, wob, fg, wrh, wrl, rb)


COPY_GROUP = 32


def _moe_kernel(be_ref, nv_ref, src_ref, srcn_ref, dst_ref, hf_hbm, wg_ref, wu_ref, wd_ref, out_hbm,
                xbuf, obuf, gsem, ssem):
    i = pl.program_id(0)
    nblk = pl.num_programs(0)
    M = xbuf.shape[1] // ROW_TILE
    slot = i % 2
    nv_cur = nv_ref[i]
    nv_next = jnp.where(i + 1 < nblk, nv_ref[jnp.minimum(i + 1, nblk - 1)], 0)

    def for_real_groups(n_real, fn):
        for k in range(M // COPY_GROUP):
            @pl.when(k * COPY_GROUP < n_real)
            def _group():
                fn(range(k * COPY_GROUP, (k + 1) * COPY_GROUP))

    def gather_rows(idx_ref, buf_slot, n_real):
        def start(rows):
            for r in rows:
                tok_row = pl.multiple_of(idx_ref[0, 0, r] * ROW_TILE, ROW_TILE)
                pltpu.make_async_copy(hf_hbm.at[pl.ds(tok_row, ROW_TILE)],
                                      xbuf.at[buf_slot, pl.ds(r * ROW_TILE, ROW_TILE)],
                                      gsem.at[buf_slot]).start(priority=r % DMA_PRIORITIES)
        for_real_groups(n_real, start)

    def scatter_rows(idx_ref, buf_slot, n_real):
        def start(rows):
            for r in rows:
                dst_row = pl.multiple_of(idx_ref[0, 0, r] * ROW_TILE, ROW_TILE)
                pltpu.make_async_copy(obuf.at[buf_slot, pl.ds(r * ROW_TILE, ROW_TILE)],
                                      out_hbm.at[pl.ds(dst_row, ROW_TILE)],
                                      ssem.at[buf_slot]).start(priority=r % DMA_PRIORITIES)
        for_real_groups(n_real, start)

    def wait_rows(buf, buf_slot, sem, n_real):
        span = COPY_GROUP * ROW_TILE
        for_real_groups(n_real, lambda rows: pltpu.make_async_copy(
            hf_hbm.at[pl.ds(0, span)], buf.at[buf_slot, pl.ds(0, span)], sem.at[buf_slot]).wait())

    @pl.when(nv_cur > 0)
    def _compute():
        @pl.when(i == 0)
        def _first_gather():
            gather_rows(src_ref, 0, nv_cur)
            obuf[1] = jnp.zeros(obuf.shape[1:], F32)
            spare = pltpu.make_async_copy(
                obuf.at[1], out_hbm.at[pl.ds(out_hbm.shape[0] - M * ROW_TILE, M * ROW_TILE)], ssem.at[1])
            spare.start()
            spare.wait()

        wait_rows(xbuf, slot, gsem, nv_cur)

        @pl.when(i >= 2)
        def _reuse_obuf():
            wait_rows(obuf, slot, ssem, nv_ref[jnp.maximum(i - 2, 0)])

        gather_rows(srcn_ref, 1 - slot, nv_next)

        def expert_mlp(rows):
            xb = _from_token_tiles(xbuf.at[slot], rows).astype(BF16)
            a = jnp.dot(xb, wg_ref[0].astype(BF16), preferred_element_type=F32)
            u = jnp.dot(xb, wu_ref[0].astype(BF16), preferred_element_type=F32)
            hmid = (a * jax.nn.sigmoid(a)) * u
            _to_token_tiles(obuf.at[slot], jnp.dot(hmid.astype(BF16), wd_ref[0].astype(BF16),
                                                   preferred_element_type=F32))

        @pl.when(nv_cur > M // 2)
        def _whole_block():
            expert_mlp(M)

        @pl.when(nv_cur <= M // 2)
        def _half_block():
            expert_mlp(M // 2)

        scatter_rows(dst_ref, slot, nv_cur)

        @pl.when(nv_next == 0)
        def _drain():
            wait_rows(obuf, slot, ssem, nv_cur)

            @pl.when(i >= 1)
            def _drain_prev():
                wait_rows(obuf, 1 - slot, ssem, nv_ref[jnp.maximum(i - 1, 0)])


def _moe_call(block_expert, n_real, src_tok, dst_row, hf, w_g, w_u, w_d):
    T = hf.shape[0] // ROW_TILE
    M = DISPATCH_BLOCK
    nblk = block_expert.shape[0]
    D, DE = w_g.shape[1], w_g.shape[2]
    assert D == ROW_TILE * LANES and hf.shape[1] == LANES and (M // 2) % COPY_GROUP == 0
    idx_spec = lambda f: pl.BlockSpec((1, 1, M), f, memory_space=pltpu.SMEM)
    cur = lambda i, be, nu: (i, 0, 0)
    nxt = lambda i, be, nu: (jnp.minimum(i + 1, nblk - 1), 0, 0)
    wspec = lambda a, b: pl.BlockSpec((1, a, b), lambda i, be, nu: (be[i], 0, 0))
    grid_spec = pltpu.PrefetchScalarGridSpec(
        num_scalar_prefetch=2,
        grid=(nblk,),
        in_specs=[idx_spec(cur), idx_spec(nxt), idx_spec(cur),
                  pl.BlockSpec(memory_space=pl.ANY),
                  wspec(D, DE), wspec(D, DE), wspec(DE, D)],
        out_specs=pl.BlockSpec(memory_space=pl.ANY),
        scratch_shapes=[pltpu.VMEM((2, M * ROW_TILE, LANES), F32), pltpu.VMEM((2, M * ROW_TILE, LANES), F32),
                        pltpu.SemaphoreType.DMA((2,)), pltpu.SemaphoreType.DMA((2,))],
    )
    src3 = src_tok.reshape(nblk, 1, M)
    return pl.pallas_call(
        _moe_kernel,
        grid_spec=grid_spec,
        out_shape=jax.ShapeDtypeStruct(((EXPERT_TOPK * T + M) * ROW_TILE, LANES), F32),
        compiler_params=pltpu.CompilerParams(
            dimension_semantics=("arbitrary",), vmem_limit_bytes=VMEM_LIMIT),
        name="moe",
    )(block_expert, n_real, src3, src3, dst_row.reshape(nblk, 1, M), hf, w_g, w_u, w_d)


def _ple_kernel(x1_ref, m0_ref, m1_ref, wt_ref, p_ref, pg_ref, wgate_ref, wproj_ref, pog_ref, out_ref):
    proj = _rmsnorm(jnp.dot(p_ref[...].astype(BF16), wproj_ref[...], preferred_element_type=F32),
                    pog_ref[...])
    tm = x1_ref.shape[0]
    moe = (_from_token_tiles(m0_ref, tm) * wt_ref[:, 0:1] + _from_token_tiles(m1_ref, tm) * wt_ref[:, 1:2])
    x2 = x1_ref[...] + moe
    hn = _rmsnorm(x2, pg_ref[...]).astype(BF16)
    gate = jax.nn.sigmoid(jnp.dot(hn, wgate_ref[...], preferred_element_type=F32))
    out_ref[...] = x2 + gate * proj


def _ple_call(moe_rows, x1, wt_cols, p2d, pg, wgate, wproj, pog):
    T, D = x1.shape
    tm = PLE_TILE
    tok = lambda w: pl.BlockSpec((tm, w), lambda i: (i, 0))
    full = lambda a: pl.BlockSpec(a.shape, lambda i: (0, 0))
    first = pl.BlockSpec((tm * ROW_TILE, LANES), lambda i: (i, 0))
    second = pl.BlockSpec((tm * ROW_TILE, LANES), lambda i: (T // tm + i, 0))
    return pl.pallas_call(
        _ple_kernel,
        grid=(T // tm,),
        in_specs=[tok(D), first, second, tok(EXPERT_TOPK), tok(p2d.shape[1]),
                  full(pg), full(wgate), full(wproj), full(pog)],
        out_specs=tok(D),
        out_shape=jax.ShapeDtypeStruct((T, D), F32),
        compiler_params=pltpu.CompilerParams(
            dimension_semantics=("arbitrary",), vmem_limit_bytes=VMEM_LIMIT),
        name="ple",
    )(x1, moe_rows, moe_rows, wt_cols, p2d, pg, wgate, wproj, pog)


def _dispatch_plan(eid, T):
    A = EXPERT_TOPK * T
    M = DISPATCH_BLOCK
    nblk = A // M + N_EXPERTS
    assert A <= PAD_MARK
    e_flat = eid.reshape(A)
    experts = jnp.arange(N_EXPERTS, dtype=I32)
    counts = jnp.sum((e_flat[:, None] == experts[None, :]).astype(I32), axis=0)
    padded = (counts + M - 1) // M * M
    n_used = jnp.sum(padded) // M
    row = jnp.arange(M, dtype=I32)[None, :]
    pad_keys = jnp.where(row < (padded - counts)[:, None], 2 * experts[:, None] + 1, 2 * N_EXPERTS)
    keys = jnp.concatenate([2 * e_flat, pad_keys.reshape(-1)])
    vals = jnp.concatenate([jnp.arange(A, dtype=I32), jnp.full((N_EXPERTS * M,), PAD_MARK, I32)])
    packed = jnp.sort(keys * (PAD_MARK + 1) + vals).reshape(nblk, M)
    a_s = packed & PAD_MARK
    blk = jnp.arange(nblk, dtype=I32)
    be = jnp.minimum(packed[:, 0] // (2 * (PAD_MARK + 1)), N_EXPERTS - 1)
    be = jnp.where(blk < n_used, be, be[jnp.maximum(n_used - 1, 0)])
    valid = a_s != PAD_MARK
    src_tok = jnp.where(valid, jnp.where(a_s >= T, a_s - T, a_s), 0)
    dst_row = jnp.where(valid, a_s, A + row)
    return be, jnp.sum(valid.astype(I32), axis=1), src_tok, dst_row


def _layer(x2d, p2d, rel_bias, attn_norm, w_in, swa_q_norm, swa_k_norm, swa_sinks, moba_q_norm,
           moba_k_norm, swa_out_norm, moba_out_norm, w_out, ffn_norm, w_rg, b_rg, w_re, b_re,
           w_g, w_u, w_d, ple_norm, w_ple_gate, w_ple_proj, ple_out_norm, B, S):
    T, D = x2d.shape
    row = lambda v: v.reshape(1, -1).astype(F32)
    head_gain = jnp.concatenate([
        jnp.tile(swa_q_norm, SWA_Q_HEADS), jnp.tile(swa_k_norm, SWA_KV_HEADS),
        jnp.tile(moba_q_norm, MOBA_Q_HEADS), jnp.tile(moba_k_norm, MOBA_KV_HEADS)])
    w_in_b = w_in.astype(BF16)
    qa_w, kv_w = SWA_Q_HEADS * HEAD_DIM, SWA_KV_HEADS * HEAD_DIM
    qb_w = MOBA_Q_HEADS * HEAD_DIM
    c_va = qa_w + kv_w
    c_qb = c_va + kv_w
    c_vb = c_qb + qb_w + kv_w
    w_qk = jnp.concatenate([w_in_b[:, :c_va], w_in_b[:, c_qb:c_vb]], axis=1)
    w_vt = jnp.concatenate([w_in_b[:, c_va:c_qb], w_in_b[:, c_vb:]], axis=1).T
    qa, ka, qb, kb, vat, vbt = _qkv_call(x2d, row(attn_norm), w_qk, w_vt, row(head_gain))

    tbl_flat = rel_bias.astype(F32).reshape(-1)
    oa = _swa_call(tbl_flat, swa_sinks.astype(F32), qa, ka, vat, B, S)
    ob = _moba_call(tbl_flat, qb, kb, vbt, B, S)

    pad_rows = lambda n: jnp.zeros((n, D), F32)
    wr = jnp.concatenate([w_rg.T, pad_rows(EXPERT_ROW0 - N_GROUPS), w_re.T,
                          pad_rows(ROUTER_ROWS - EXPERT_ROW0 - N_EXPERTS)], axis=0)
    wr_hi = wr.astype(BF16)
    wr_lo = (wr - wr_hi.astype(F32)).astype(BF16)
    rb = jnp.concatenate([b_rg, jnp.zeros((EXPERT_ROW0 - N_GROUPS,), F32), b_re,
                          jnp.zeros((ROUTER_ROWS - EXPERT_ROW0 - N_EXPERTS,), F32)]).reshape(ROUTER_ROWS, 1)
    wo = w_out.astype(BF16)
    na_w = SWA_Q_HEADS * HEAD_DIM
    x1, hf, eid, wts = _outproj_call(oa, ob, x2d, row(swa_out_norm), row(moba_out_norm),
                                     wo[:na_w], wo[na_w:], row(ffn_norm), wr_hi, wr_lo, rb)

    be, n_real, src_tok, dst_row = _dispatch_plan(eid, T)
    moe_rows = _moe_call(be, n_real, src_tok, dst_row, hf, w_g, w_u, w_d)
    return _ple_call(moe_rows, x1, wts.T, p2d, row(ple_norm), w_ple_gate.astype(BF16),
                     w_ple_proj.astype(BF16), row(ple_out_norm))


def kernel(x, p, rel_bias, attn_norm, w_in, swa_q_norm, swa_k_norm, swa_sinks, moba_q_norm, moba_k_norm,
           swa_out_norm, moba_out_norm, w_out, ffn_norm, w_router_group, b_router_group, w_router_expert,
           b_router_expert, w_exp_gate, w_exp_up, w_exp_down, ple_norm, w_ple_gate, w_ple_proj, ple_out_norm):
    B, S, D = x.shape
    x2d = x.reshape(B * S, D)
    for i in range(p.shape[0]):
        x2d = _layer(x2d, p[i].reshape(B * S, -1), rel_bias, attn_norm[i], w_in[i], swa_q_norm[i],
                     swa_k_norm[i], swa_sinks[i], moba_q_norm[i], moba_k_norm[i], swa_out_norm[i],
                     moba_out_norm[i], w_out[i], ffn_norm[i], w_router_group[i], b_router_group[i],
                     w_router_expert[i], b_router_expert[i], w_exp_gate[i], w_exp_up[i], w_exp_down[i],
                     ple_norm[i], w_ple_gate[i], w_ple_proj[i], ple_out_norm[i], B, S)
    return x2d.reshape(B, S, D)
```

```python
import math

import numpy as np
import jax
import jax.numpy as jnp
from jax import lax
from jax.experimental import pallas as pl
from jax.experimental.pallas import tpu as pltpu

F32 = jnp.float32
BF16 = jnp.bfloat16
I32 = jnp.int32

HEAD_DIM = 64
SWA_Q_HEADS = 8
SWA_KV_HEADS = 2
SWA_WINDOW = 128
MOBA_Q_HEADS = 8
MOBA_KV_HEADS = 2
MOBA_BLOCK = 256
MOBA_TOPK = 3
N_HEADS = SWA_Q_HEADS + MOBA_Q_HEADS
REL_BUCKETS = 32
REL_MAX_DIST = 128
N_GROUPS = 4
EXPERTS_PER_GROUP = 8
N_EXPERTS = N_GROUPS * EXPERTS_PER_GROUP
EXPERT_TOPK = 2
DISPATCH_BLOCK = 512
PAD_MARK = 0xFFFF
EPS = 1e-6
NEG = -1e30
LOG2E = math.log2(math.e)

DMA_PRIORITIES = 2
LANES = 128
ROW_TILE = 8
BF16_SUBLANES = 16
GROUP_HEADS = 4
GROUP_W = GROUP_HEADS * HEAD_DIM
ROUTER_ROWS = 128
EXPERT_ROW0 = 8
TOKEN_TILE = 1024
PLE_TILE = 1024
VMEM_LIMIT = 52 * 1024 * 1024


def _dot_nt(a, b):
    return lax.dot_general(a, b, (((1,), (1,)), ((), ())), preferred_element_type=F32)


def _rmsnorm(x, g):
    ms = jnp.mean(x * x, axis=-1, keepdims=True)
    return x * lax.rsqrt(ms + EPS) * g


def _to_token_tiles(ref, x):
    m = x.shape[0]
    for j in range(x.shape[1] // LANES):
        ref[pl.ds(j, m, stride=ROW_TILE), :] = x[:, LANES * j:LANES * (j + 1)]


def _from_token_tiles(ref, m):
    return jnp.concatenate([ref[pl.ds(j, m, stride=ROW_TILE), :] for j in range(ref.shape[0] // m)], axis=1)


def _rel_bucket_np(dist):
    n = np.maximum(dist, 0)
    exact = REL_BUCKETS // 2
    nf = np.maximum(n, 1).astype(np.float32)
    large = exact + (np.log(nf / exact) / math.log(REL_MAX_DIST / exact)
                     * (REL_BUCKETS - exact)).astype(np.int32)
    return np.where(n < exact, n, np.minimum(large, REL_BUCKETS - 1)).astype(np.int32)


def _band_buckets(block):
    qi = np.arange(block)[:, None]
    kj = np.arange(2 * block)[None, :]
    return _rel_bucket_np(qi + block - kj)


def _bias_from_buckets(bkt, tbl_ref, head):
    acc = jnp.zeros(bkt.shape, F32)
    for j in range(REL_BUCKETS):
        acc = jnp.where(bkt == j, tbl_ref[j * N_HEADS + head] * LOG2E, acc)
    return acc


def _qkv_kernel(x_ref, g_ref, w_ref, wvt_ref, hg_ref, qa_ref, ka_ref, qb_ref, kb_ref, vat_ref, vbt_ref):
    h = _rmsnorm(x_ref[...], g_ref[...]).astype(BF16)
    acc = jnp.dot(h, w_ref[...], preferred_element_type=F32)
    lo = lax.broadcasted_iota(I32, (1, LANES), 1) < HEAD_DIM

    def head_normed(c, scale):
        blk = acc[:, LANES * c:LANES * (c + 1)]
        sq = blk * blk
        s_lo = jnp.sum(jnp.where(lo, sq, 0.0), axis=-1, keepdims=True)
        s_hi = jnp.sum(jnp.where(lo, 0.0, sq), axis=-1, keepdims=True)
        inv = jnp.where(lo, lax.rsqrt(s_lo / HEAD_DIM + EPS), lax.rsqrt(s_hi / HEAD_DIM + EPS))
        return blk * inv * hg_ref[:, LANES * c:LANES * (c + 1)] * scale

    scale = HEAD_DIM ** -0.5 * LOG2E
    for c in range(4):
        qa_ref[:, LANES * c:LANES * (c + 1)] = head_normed(c, scale).astype(BF16)
        qb_ref[:, LANES * c:LANES * (c + 1)] = head_normed(5 + c, scale).astype(BF16)
    ka_ref[...] = head_normed(4, 1.0).astype(BF16)
    kb_ref[...] = head_normed(9, 1.0).astype(BF16)
    vt = _dot_nt(wvt_ref[...], h).astype(BF16)
    for j in range(vat_ref.shape[0]):
        vat_ref[j] = vt[:LANES, SWA_WINDOW * j:SWA_WINDOW * (j + 1)]
    for j in range(vbt_ref.shape[0]):
        vbt_ref[j] = vt[LANES:, MOBA_BLOCK * j:MOBA_BLOCK * (j + 1)]


def _qkv_call(x2d, attn_g, w_qk, w_vt, head_gain):
    T, D = x2d.shape
    tm = TOKEN_TILE
    tok = lambda w: pl.BlockSpec((tm, w), lambda i: (i, 0))
    full = lambda a: pl.BlockSpec(a.shape, lambda i: (0, 0))
    out_w = (SWA_Q_HEADS * HEAD_DIM, LANES, MOBA_Q_HEADS * HEAD_DIM, LANES)
    slabs = lambda blk: pl.BlockSpec((tm // blk, LANES, blk), lambda i: (i, 0, 0))
    slab_shape = lambda blk: jax.ShapeDtypeStruct((T // blk, LANES, blk), BF16)
    return pl.pallas_call(
        _qkv_kernel,
        grid=(T // tm,),
        in_specs=[tok(D), full(attn_g), full(w_qk), full(w_vt), full(head_gain)],
        out_specs=[tok(w) for w in out_w] + [slabs(SWA_WINDOW), slabs(MOBA_BLOCK)],
        out_shape=[jax.ShapeDtypeStruct((T, w), BF16) for w in out_w]
        + [slab_shape(SWA_WINDOW), slab_shape(MOBA_BLOCK)],
        compiler_params=pltpu.CompilerParams(
            dimension_semantics=("arbitrary",), vmem_limit_bytes=VMEM_LIMIT),
        name="qkv",
    )(x2d, attn_g, w_qk, w_vt, head_gain)


def _padded_heads(qblk, heads):
    zeros = jnp.zeros((qblk.shape[0], HEAD_DIM), qblk.dtype)
    pieces = []
    for h in range(heads):
        piece = qblk[:, HEAD_DIM * h:HEAD_DIM * (h + 1)]
        pieces.append(jnp.concatenate([piece, zeros] if h < GROUP_HEADS else [zeros, piece], axis=1))
    return jnp.concatenate(pieces, axis=0)


def _swa_kernel(tbl_ref, sink_ref, bkt_ref, q_ref, k_ref, vt_ref, o_ref, bias_scr):
    W = SWA_WINDOW
    QB = 2 * W
    NK = W + QB
    H = SWA_Q_HEADS
    S = q_ref.shape[0]

    @pl.when(pl.program_id(0) == 0)
    def _init_bias():
        bkt = bkt_ref[...]
        for h in range(H):
            bias_scr[:, QB * h:QB * (h + 1)] = _bias_from_buckets(bkt, tbl_ref, h)

    key_i = lax.broadcasted_iota(I32, (NK, H * QB), 0)
    qry_i = lax.broadcasted_iota(I32, (NK, H * QB), 1) & (QB - 1)
    dist = qry_i + W - key_i
    band = (dist >= 0) & (dist < W)
    real_key = key_i >= W
    head_of_lane = lax.broadcasted_iota(I32, (1, H * QB), 1) // QB
    sinks = jnp.zeros((1, H * QB), F32)
    for h in range(H):
        sinks = jnp.where(head_of_lane == h, sink_ref[h] * LOG2E, sinks)

    def body(i, carry):
        r0 = pl.multiple_of(i * QB, QB)
        p0 = pl.multiple_of(jnp.maximum(r0 - W, 0), W)
        q8 = _padded_heads(q_ref[pl.ds(r0, QB), :], H)
        kband = jnp.concatenate([k_ref[pl.ds(p0, W), :], k_ref[pl.ds(r0, QB), :]], axis=0)
        mask = band & (real_key | (i > 0))
        s = jnp.where(mask, _dot_nt(kband, q8) + bias_scr[...], NEG)
        m = jnp.maximum(jnp.max(s, axis=0, keepdims=True), sinks)
        e = jnp.exp2(s - m)
        den = jnp.sum(e, axis=0, keepdims=True) + jnp.exp2(sinks - m)
        j = 2 * i
        vtband = jnp.concatenate([vt_ref[jnp.maximum(j - 1, 0)], vt_ref[j], vt_ref[j + 1]], axis=1)
        ot = jnp.dot(vtband, e.astype(BF16), preferred_element_type=F32) / den
        for pr in range(H // 2):
            f0 = HEAD_DIM * (2 * pr // GROUP_HEADS)
            pair = ot[f0:f0 + HEAD_DIM, 2 * QB * pr:2 * QB * (pr + 1)]
            o_ref[pl.ds(r0, QB), LANES * pr:LANES * (pr + 1)] = jnp.concatenate(
                [pair[:, :QB], pair[:, QB:]], axis=0).T
        return carry

    lax.fori_loop(0, S // QB, body, 0)


def _swa_call(tbl_flat, sinks, qa, ka, vat, B, S):
    W = SWA_WINDOW
    QB, NK = 2 * W, 3 * W
    bkt = jnp.asarray(_rel_bucket_np(np.arange(QB)[None, :] + W - np.arange(NK)[:, None]))
    smem = pl.BlockSpec(memory_space=pltpu.SMEM)
    return pl.pallas_call(
        _swa_kernel,
        grid=(B,),
        in_specs=[smem, smem,
                  pl.BlockSpec(bkt.shape, lambda b: (0, 0)),
                  pl.BlockSpec((S, qa.shape[1]), lambda b: (b, 0)),
                  pl.BlockSpec((S, LANES), lambda b: (b, 0)),
                  pl.BlockSpec((S // W, LANES, W), lambda b: (b, 0, 0))],
        out_specs=pl.BlockSpec((S, qa.shape[1]), lambda b: (b, 0)),
        out_shape=jax.ShapeDtypeStruct(qa.shape, F32),
        scratch_shapes=[pltpu.VMEM((NK, SWA_Q_HEADS * QB), F32)],
        compiler_params=pltpu.CompilerParams(
            dimension_semantics=("arbitrary",), vmem_limit_bytes=VMEM_LIMIT),
        name="swa",
    )(tbl_flat, sinks, bkt, qa, ka, vat)


def _moba_kernel(tbl_ref, bkt_ref, q_ref, k_ref, vt_ref, o_ref,
                 bias_scr, km_scr, q8_scr, sel_scr, m_scr, l_scr, acc_scr):
    BS = MOBA_BLOCK
    H = MOBA_Q_HEADS
    S = q_ref.shape[0]
    NBK = S // BS
    NQ = H * BS
    GQ = GROUP_HEADS * BS

    @pl.when(pl.program_id(0) == 0)
    def _init_bias():
        bkt = bkt_ref[...]
        for h in range(H):
            band = _bias_from_buckets(bkt, tbl_ref, SWA_Q_HEADS + h)
            far = tbl_ref[(REL_BUCKETS - 1) * N_HEADS + SWA_Q_HEADS + h] * LOG2E
            bias_scr[0, :, BS * h:BS * (h + 1)] = jnp.full((BS, BS), far, F32)
            bias_scr[1, :, BS * h:BS * (h + 1)] = band[:BS]
            bias_scr[2, :, BS * h:BS * (h + 1)] = band[BS:]
        key_i = lax.broadcasted_iota(I32, (BS, NQ), 0)
        qry_i = lax.broadcasted_iota(I32, (BS, NQ), 1) & (BS - 1)
        bias_scr[3] = jnp.where(key_i <= qry_i, 1.0, 0.0)

    blk_row = lax.broadcasted_iota(I32, (NBK, 1), 0)
    kmean = jnp.sum(k_ref[...].astype(F32).reshape(NBK, BS, LANES), axis=1) / BS
    kmean = jnp.concatenate([kmean, jnp.zeros((BF16_SUBLANES - NBK, LANES), F32)], axis=0)
    km_hi = kmean.astype(BF16)
    km_scr[0] = km_hi
    km_scr[1] = (kmean - km_hi.astype(F32)).astype(BF16)

    def scores(n, case):
        k0 = pl.multiple_of(n * BS, BS)
        return _dot_nt(k_ref[pl.ds(k0, BS), :], q8_scr[...]) + bias_scr[case]

    ones_rows = jnp.ones((BF16_SUBLANES, BS), BF16)

    def weighted_values(n, e):
        vt_ext = jnp.concatenate([vt_ref[n], ones_rows], axis=0)
        pv = jnp.dot(vt_ext, e.astype(BF16), preferred_element_type=F32)
        vals = jnp.concatenate([pv[:HEAD_DIM, :GQ], pv[HEAD_DIM:LANES, GQ:]], axis=1)
        return vals, pv[LANES:LANES + 1, :]

    def qblock(mi, carry):
        r0 = pl.multiple_of(mi * BS, BS)
        q8 = _padded_heads(q_ref[pl.ds(r0, BS), :], H)
        q8_scr[...] = q8
        gate = (_dot_nt(km_scr[0], q8) + _dot_nt(km_scr[1], q8))[:NBK]
        past = blk_row < mi
        gm = jnp.where(past, gate, NEG)
        cnt = jnp.zeros(gm.shape, F32)
        for n in range(NBK - 1):
            col = gm[n:n + 1, :]
            beats = (gm > col) | ((gm == col) & (blk_row < n))
            c = jnp.sum(jnp.where(beats, 1.0, 0.0), axis=0, keepdims=True)
            cnt = jnp.where(blk_row == n, c, cnt)
        sel_scr[...] = jnp.where(past & (cnt < MOBA_TOPK), 1.0, 0.0)

        s = jnp.where(bias_scr[3] > 0.5, scores(mi, 2), NEG)
        m0 = jnp.max(s, axis=0, keepdims=True)
        e = jnp.exp2(s - m0)
        m_scr[...] = m0
        acc_scr[...], l_scr[...] = weighted_values(mi, e)

        def kvblock(j, c2):
            n = mi - j
            s = jnp.where(sel_scr[pl.ds(n, 1), :] > 0.5, scores(n, jnp.where(j == 1, 1, 0)), NEG)
            m_prev = m_scr[...]
            m_new = jnp.maximum(m_prev, jnp.max(s, axis=0, keepdims=True))
            alpha = jnp.exp2(m_prev - m_new)
            e = jnp.exp2(s - m_new)
            vals, row_sum = weighted_values(n, e)
            l_scr[...] = alpha * l_scr[...] + row_sum
            acc_scr[...] = alpha * acc_scr[...] + vals
            m_scr[...] = m_new
            return c2

        lax.fori_loop(1, mi + 1, kvblock, 0)
        ot = acc_scr[...] / l_scr[...]
        for pr in range(H // 2):
            pair = ot[:, 2 * BS * pr:2 * BS * (pr + 1)]
            o_ref[pl.ds(r0, BS), LANES * pr:LANES * (pr + 1)] = jnp.concatenate(
                [pair[:, :BS], pair[:, BS:]], axis=0).T
        return carry

    lax.fori_loop(0, NBK, qblock, 0)


def _moba_call(tbl_flat, qb, kb, vbt, B, S):
    BS = MOBA_BLOCK
    NQ = MOBA_Q_HEADS * BS
    NBK = S // BS
    bkt = jnp.asarray(np.ascontiguousarray(_band_buckets(BS).T))
    smem = pl.BlockSpec(memory_space=pltpu.SMEM)
    return pl.pallas_call(
        _moba_kernel,
        grid=(B,),
        in_specs=[smem,
                  pl.BlockSpec(bkt.shape, lambda b: (0, 0)),
                  pl.BlockSpec((S, qb.shape[1]), lambda b: (b, 0)),
                  pl.BlockSpec((S, LANES), lambda b: (b, 0)),
                  pl.BlockSpec((NBK, LANES, BS), lambda b: (b, 0, 0))],
        out_specs=pl.BlockSpec((S, qb.shape[1]), lambda b: (b, 0)),
        out_shape=jax.ShapeDtypeStruct(qb.shape, F32),
        scratch_shapes=[pltpu.VMEM((4, BS, NQ), F32),
                        pltpu.VMEM((2, BF16_SUBLANES, LANES), BF16),
                        pltpu.VMEM((NQ, LANES), BF16),
                        pltpu.VMEM((NBK, NQ), F32),
                        pltpu.VMEM((1, NQ), F32),
                        pltpu.VMEM((1, NQ), F32),
                        pltpu.VMEM((HEAD_DIM, NQ), F32)],
        compiler_params=pltpu.CompilerParams(
            dimension_semantics=("arbitrary",), vmem_limit_bytes=VMEM_LIMIT),
        name="moba",
    )(tbl_flat, bkt, qb, kb, vbt)


def _outproj_kernel(oa_ref, ob_ref, x_ref, ga_ref, gb_ref, woa_ref, wob_ref, fg_ref,
                    wrh_ref, wrl_ref, rb_ref, x1_ref, hf_ref, eid_ref, wt_ref):
    na = _rmsnorm(oa_ref[...], ga_ref[...]).astype(BF16)
    nb = _rmsnorm(ob_ref[...], gb_ref[...]).astype(BF16)
    y = (jnp.dot(na, woa_ref[...], preferred_element_type=F32)
         + jnp.dot(nb, wob_ref[...], preferred_element_type=F32))
    x1 = x_ref[...] + y
    x1_ref[...] = x1
    hf = _rmsnorm(x1, fg_ref[...])
    _to_token_tiles(hf_ref, hf)

    hf_hi = hf.astype(BF16)
    hf_lo = (hf - hf_hi.astype(F32)).astype(BF16)
    lt = (_dot_nt(wrh_ref[...], hf_hi) + _dot_nt(wrh_ref[...], hf_lo)
          + _dot_nt(wrl_ref[...], hf_hi) + rb_ref[...])
    gl = [lt[j:j + 1, :] for j in range(N_GROUPS)]
    best = gl[0]
    gsel = jnp.zeros(best.shape, I32)
    for j in range(1, N_GROUPS):
        better = gl[j] > best
        gsel = jnp.where(better, j, gsel)
        best = jnp.where(better, gl[j], best)
    gsum = jnp.zeros(best.shape, F32)
    for j in range(N_GROUPS):
        gsum = gsum + jnp.exp(gl[j] - best)
    gw = 1.0 / gsum
    E = EXPERTS_PER_GROUP
    es = lt[EXPERT_ROW0:EXPERT_ROW0 + E, :]
    for j in range(1, N_GROUPS):
        es = jnp.where(gsel == j, lt[EXPERT_ROW0 + E * j:EXPERT_ROW0 + E * (j + 1), :], es)
    row = lax.broadcasted_iota(I32, es.shape, 0)
    v1 = jnp.max(es, axis=0, keepdims=True)
    i1 = jnp.min(jnp.where(es == v1, row, E), axis=0, keepdims=True)
    es2 = jnp.where(row == i1, -jnp.inf, es)
    v2 = jnp.max(es2, axis=0, keepdims=True)
    i2 = jnp.min(jnp.where(es2 == v2, row, E), axis=0, keepdims=True)
    e2 = jnp.exp(v2 - v1)
    den = 1.0 + e2
    eid_ref[...] = jnp.concatenate([gsel * E + i1, gsel * E + i2], axis=0)
    wt_ref[...] = jnp.concatenate([gw * (1.0 / den), gw * (e2 / den)], axis=0)


def _outproj_call(oa, ob, x2d, ga, gb, woa, wob, fg, wrh, wrl, rb):
    T, D = x2d.shape
    tm = TOKEN_TILE
    tok = lambda w: pl.BlockSpec((tm, w), lambda i: (i, 0))
    full = lambda a: pl.BlockSpec(a.shape, lambda i: (0, 0))
    col = pl.BlockSpec((EXPERT_TOPK, tm), lambda i: (0, i))
    return pl.pallas_call(
        _outproj_kernel,
        grid=(T // tm,),
        in_specs=[tok(oa.shape[1]), tok(ob.shape[1]), tok(D), full(ga), full(gb), full(woa), full(wob),
                  full(fg), full(wrh), full(wrl), full(rb)],
        out_specs=[tok(D), pl.BlockSpec((tm * ROW_TILE, LANES), lambda i: (i, 0)), col, col],
        out_shape=[jax.ShapeDtypeStruct((T, D), F32), jax.ShapeDtypeStruct((T * ROW_TILE, LANES), F32),
                   jax.ShapeDtypeStruct((EXPERT_TOPK, T), I32), jax.ShapeDtypeStruct((EXPERT_TOPK, T), F32)],
        compiler_params=pltpu.CompilerParams(
            dimension_semantics=("arbitrary",), vmem_limit_bytes=VMEM_LIMIT),
        name="outproj",
    )(oa, ob, x2d, ga, gb, woa, wob, fg, wrh, wrl, rb)


COPY_GROUP = 32


def _moe_kernel(be_ref, nv_ref, src_ref, srcn_ref, dst_ref, hf_hbm, wg_ref, wu_ref, wd_ref, out_hbm,
                xbuf, obuf, gsem, ssem):
    i = pl.program_id(0)
    nblk = pl.num_programs(0)
    M = xbuf.shape[1] // ROW_TILE
    slot = i % 2
    nv_cur = nv_ref[i]
    nv_next = jnp.where(i + 1 < nblk, nv_ref[jnp.minimum(i + 1, nblk - 1)], 0)

    def for_real_groups(n_real, fn):
        for k in range(M // COPY_GROUP):
            @pl.when(k * COPY_GROUP < n_real)
            def _group():
                fn(range(k * COPY_GROUP, (k + 1) * COPY_GROUP))

    def gather_rows(idx_ref, buf_slot, n_real):
        def start(rows):
            for r in rows:
                tok_row = pl.multiple_of(idx_ref[0, 0, r] * ROW_TILE, ROW_TILE)
                pltpu.make_async_copy(hf_hbm.at[pl.ds(tok_row, ROW_TILE)],
                                      xbuf.at[buf_slot, pl.ds(r * ROW_TILE, ROW_TILE)],
                                      gsem.at[buf_slot]).start(priority=r % DMA_PRIORITIES)
        for_real_groups(n_real, start)

    def scatter_rows(idx_ref, buf_slot, n_real):
        def start(rows):
            for r in rows:
                dst_row = pl.multiple_of(idx_ref[0, 0, r] * ROW_TILE, ROW_TILE)
                pltpu.make_async_copy(obuf.at[buf_slot, pl.ds(r * ROW_TILE, ROW_TILE)],
                                      out_hbm.at[pl.ds(dst_row, ROW_TILE)],
                                      ssem.at[buf_slot]).start(priority=r % DMA_PRIORITIES)
        for_real_groups(n_real, start)

    def wait_rows(buf, buf_slot, sem, n_real):
        span = COPY_GROUP * ROW_TILE
        for_real_groups(n_real, lambda rows: pltpu.make_async_copy(
            hf_hbm.at[pl.ds(0, span)], buf.at[buf_slot, pl.ds(0, span)], sem.at[buf_slot]).wait())

    @pl.when(nv_cur > 0)
    def _compute():
        @pl.when(i == 0)
        def _first_gather():
            gather_rows(src_ref, 0, nv_cur)
            obuf[1] = jnp.zeros(obuf.shape[1:], F32)
            spare = pltpu.make_async_copy(
                obuf.at[1], out_hbm.at[pl.ds(out_hbm.shape[0] - M * ROW_TILE, M * ROW_TILE)], ssem.at[1])
            spare.start()
            spare.wait()

        wait_rows(xbuf, slot, gsem, nv_cur)

        @pl.when(i >= 2)
        def _reuse_obuf():
            wait_rows(obuf, slot, ssem, nv_ref[jnp.maximum(i - 2, 0)])

        gather_rows(srcn_ref, 1 - slot, nv_next)
        xb = _from_token_tiles(xbuf.at[slot], M).astype(BF16)
        a = jnp.dot(xb, wg_ref[0].astype(BF16), preferred_element_type=F32)
        u = jnp.dot(xb, wu_ref[0].astype(BF16), preferred_element_type=F32)
        hmid = (a * jax.nn.sigmoid(a)) * u
        _to_token_tiles(obuf.at[slot], jnp.dot(hmid.astype(BF16), wd_ref[0].astype(BF16),
                                               preferred_element_type=F32))
        scatter_rows(dst_ref, slot, nv_cur)

        @pl.when(nv_next == 0)
        def _drain():
            wait_rows(obuf, slot, ssem, nv_cur)

            @pl.when(i >= 1)
            def _drain_prev():
                wait_rows(obuf, 1 - slot, ssem, nv_ref[jnp.maximum(i - 1, 0)])


def _moe_call(block_expert, n_real, src_tok, dst_row, hf, w_g, w_u, w_d):
    T = hf.shape[0] // ROW_TILE
    M = DISPATCH_BLOCK
    nblk = block_expert.shape[0]
    D, DE = w_g.shape[1], w_g.shape[2]
    assert D == ROW_TILE * LANES and hf.shape[1] == LANES and M % COPY_GROUP == 0
    idx_spec = lambda f: pl.BlockSpec((1, 1, M), f, memory_space=pltpu.SMEM)
    cur = lambda i, be, nu: (i, 0, 0)
    nxt = lambda i, be, nu: (jnp.minimum(i + 1, nblk - 1), 0, 0)
    wspec = lambda a, b: pl.BlockSpec((1, a, b), lambda i, be, nu: (be[i], 0, 0))
    grid_spec = pltpu.PrefetchScalarGridSpec(
        num_scalar_prefetch=2,
        grid=(nblk,),
        in_specs=[idx_spec(cur), idx_spec(nxt), idx_spec(cur),
                  pl.BlockSpec(memory_space=pl.ANY),
                  wspec(D, DE), wspec(D, DE), wspec(DE, D)],
        out_specs=pl.BlockSpec(memory_space=pl.ANY),
        scratch_shapes=[pltpu.VMEM((2, M * ROW_TILE, LANES), F32), pltpu.VMEM((2, M * ROW_TILE, LANES), F32),
                        pltpu.SemaphoreType.DMA((2,)), pltpu.SemaphoreType.DMA((2,))],
    )
    src3 = src_tok.reshape(nblk, 1, M)
    return pl.pallas_call(
        _moe_kernel,
        grid_spec=grid_spec,
        out_shape=jax.ShapeDtypeStruct(((EXPERT_TOPK * T + M) * ROW_TILE, LANES), F32),
        compiler_params=pltpu.CompilerParams(
            dimension_semantics=("arbitrary",), vmem_limit_bytes=VMEM_LIMIT),
        name="moe",
    )(block_expert, n_real, src3, src3, dst_row.reshape(nblk, 1, M), hf, w_g, w_u, w_d)


def _ple_kernel(x1_ref, m0_ref, m1_ref, wt_ref, p_ref, pg_ref, wgate_ref, wproj_ref, pog_ref, out_ref):
    proj = _rmsnorm(jnp.dot(p_ref[...].astype(BF16), wproj_ref[...], preferred_element_type=F32),
                    pog_ref[...])
    tm = x1_ref.shape[0]
    moe = (_from_token_tiles(m0_ref, tm) * wt_ref[:, 0:1] + _from_token_tiles(m1_ref, tm) * wt_ref[:, 1:2])
    x2 = x1_ref[...] + moe
    hn = _rmsnorm(x2, pg_ref[...]).astype(BF16)
    gate = jax.nn.sigmoid(jnp.dot(hn, wgate_ref[...], preferred_element_type=F32))
    out_ref[...] = x2 + gate * proj


def _ple_call(moe_rows, x1, wt_cols, p2d, pg, wgate, wproj, pog):
    T, D = x1.shape
    tm = PLE_TILE
    tok = lambda w: pl.BlockSpec((tm, w), lambda i: (i, 0))
    full = lambda a: pl.BlockSpec(a.shape, lambda i: (0, 0))
    first = pl.BlockSpec((tm * ROW_TILE, LANES), lambda i: (i, 0))
    second = pl.BlockSpec((tm * ROW_TILE, LANES), lambda i: (T // tm + i, 0))
    return pl.pallas_call(
        _ple_kernel,
        grid=(T // tm,),
        in_specs=[tok(D), first, second, tok(EXPERT_TOPK), tok(p2d.shape[1]),
                  full(pg), full(wgate), full(wproj), full(pog)],
        out_specs=tok(D),
        out_shape=jax.ShapeDtypeStruct((T, D), F32),
        compiler_params=pltpu.CompilerParams(
            dimension_semantics=("arbitrary",), vmem_limit_bytes=VMEM_LIMIT),
        name="ple",
    )(x1, moe_rows, moe_rows, wt_cols, p2d, pg, wgate, wproj, pog)


def _dispatch_plan(eid, T):
    A = EXPERT_TOPK * T
    M = DISPATCH_BLOCK
    nblk = A // M + N_EXPERTS
    assert A <= PAD_MARK
    e_flat = eid.reshape(A)
    experts = jnp.arange(N_EXPERTS, dtype=I32)
    counts = jnp.sum((e_flat[:, None] == experts[None, :]).astype(I32), axis=0)
    padded = (counts + M - 1) // M * M
    n_used = jnp.sum(padded) // M
    row = jnp.arange(M, dtype=I32)[None, :]
    pad_keys = jnp.where(row < (padded - counts)[:, None], 2 * experts[:, None] + 1, 2 * N_EXPERTS)
    keys = jnp.concatenate([2 * e_flat, pad_keys.reshape(-1)])
    vals = jnp.concatenate([jnp.arange(A, dtype=I32), jnp.full((N_EXPERTS * M,), PAD_MARK, I32)])
    packed = jnp.sort(keys * (PAD_MARK + 1) + vals).reshape(nblk, M)
    a_s = packed & PAD_MARK
    blk = jnp.arange(nblk, dtype=I32)
    be = jnp.minimum(packed[:, 0] // (2 * (PAD_MARK + 1)), N_EXPERTS - 1)
    be = jnp.where(blk < n_used, be, be[jnp.maximum(n_used - 1, 0)])
    valid = a_s != PAD_MARK
    src_tok = jnp.where(valid, jnp.where(a_s >= T, a_s - T, a_s), 0)
    dst_row = jnp.where(valid, a_s, A + row)
    return be, jnp.sum(valid.astype(I32), axis=1), src_tok, dst_row


def _layer(x2d, p2d, rel_bias, attn_norm, w_in, swa_q_norm, swa_k_norm, swa_sinks, moba_q_norm,
           moba_k_norm, swa_out_norm, moba_out_norm, w_out, ffn_norm, w_rg, b_rg, w_re, b_re,
           w_g, w_u, w_d, ple_norm, w_ple_gate, w_ple_proj, ple_out_norm, B, S):
    T, D = x2d.shape
    row = lambda v: v.reshape(1, -1).astype(F32)
    head_gain = jnp.concatenate([
        jnp.tile(swa_q_norm, SWA_Q_HEADS), jnp.tile(swa_k_norm, SWA_KV_HEADS),
        jnp.tile(moba_q_norm, MOBA_Q_HEADS), jnp.tile(moba_k_norm, MOBA_KV_HEADS)])
    w_in_b = w_in.astype(BF16)
    qa_w, kv_w = SWA_Q_HEADS * HEAD_DIM, SWA_KV_HEADS * HEAD_DIM
    qb_w = MOBA_Q_HEADS * HEAD_DIM
    c_va = qa_w + kv_w
    c_qb = c_va + kv_w
    c_vb = c_qb + qb_w + kv_w
    w_qk = jnp.concatenate([w_in_b[:, :c_va], w_in_b[:, c_qb:c_vb]], axis=1)
    w_vt = jnp.concatenate([w_in_b[:, c_va:c_qb], w_in_b[:, c_vb:]], axis=1).T
    qa, ka, qb, kb, vat, vbt = _qkv_call(x2d, row(attn_norm), w_qk, w_vt, row(head_gain))

    tbl_flat = rel_bias.astype(F32).reshape(-1)
    oa = _swa_call(tbl_flat, swa_sinks.astype(F32), qa, ka, vat, B, S)
    ob = _moba_call(tbl_flat, qb, kb, vbt, B, S)

    pad_rows = lambda n: jnp.zeros((n, D), F32)
    wr = jnp.concatenate([w_rg.T, pad_rows(EXPERT_ROW0 - N_GROUPS), w_re.T,
                          pad_rows(ROUTER_ROWS - EXPERT_ROW0 - N_EXPERTS)], axis=0)
    wr_hi = wr.astype(BF16)
    wr_lo = (wr - wr_hi.astype(F32)).astype(BF16)
    rb = jnp.concatenate([b_rg, jnp.zeros((EXPERT_ROW0 - N_GROUPS,), F32), b_re,
                          jnp.zeros((ROUTER_ROWS - EXPERT_ROW0 - N_EXPERTS,), F32)]).reshape(ROUTER_ROWS, 1)
    wo = w_out.astype(BF16)
    na_w = SWA_Q_HEADS * HEAD_DIM
    x1, hf, eid, wts = _outproj_call(oa, ob, x2d, row(swa_out_norm), row(moba_out_norm),
                                     wo[:na_w], wo[na_w:], row(ffn_norm), wr_hi, wr_lo, rb)

    be, n_real, src_tok, dst_row = _dispatch_plan(eid, T)
    moe_rows = _moe_call(be, n_real, src_tok, dst_row, hf, w_g, w_u, w_d)
    return _ple_call(moe_rows, x1, wts.T, p2d, row(ple_norm), w_ple_gate.astype(BF16),
                     w_ple_proj.astype(BF16), row(ple_out_norm))


def kernel(x, p, rel_bias, attn_norm, w_in, swa_q_norm, swa_k_norm, swa_sinks, moba_q_norm, moba_k_norm,
           swa_out_norm, moba_out_norm, w_out, ffn_norm, w_router_group, b_router_group, w_router_expert,
           b_router_expert, w_exp_gate, w_exp_up, w_exp_down, ple_norm, w_ple_gate, w_ple_proj, ple_out_norm):
    B, S, D = x.shape
    x2d = x.reshape(B * S, D)
    for i in range(p.shape[0]):
        x2d = _layer(x2d, p[i].reshape(B * S, -1), rel_bias, attn_norm[i], w_in[i], swa_q_norm[i],
                     swa_k_norm[i], swa_sinks[i], moba_q_norm[i], moba_k_norm[i], swa_out_norm[i],
                     moba_out_norm[i], w_out[i], ffn_norm[i], w_router_group[i], b_router_group[i],
                     w_router_expert[i], b_router_expert[i], w_exp_gate[i], w_exp_up[i], w_exp_down[i],
                     ple_norm[i], w_ple_gate[i], w_ple_proj[i], ple_out_norm[i], B, S)
    return x2d.reshape(B, S, D)
```

```python
import math

import numpy as np
import jax
import jax.numpy as jnp
from jax import lax
from jax.experimental import pallas as pl
from jax.experimental.pallas import tpu as pltpu

F32 = jnp.float32
BF16 = jnp.bfloat16
I32 = jnp.int32

HEAD_DIM = 64
SWA_Q_HEADS = 8
SWA_KV_HEADS = 2
SWA_WINDOW = 128
MOBA_Q_HEADS = 8
MOBA_KV_HEADS = 2
MOBA_BLOCK = 256
MOBA_TOPK = 3
N_HEADS = SWA_Q_HEADS + MOBA_Q_HEADS
REL_BUCKETS = 32
REL_MAX_DIST = 128
N_GROUPS = 4
EXPERTS_PER_GROUP = 8
N_EXPERTS = N_GROUPS * EXPERTS_PER_GROUP
EXPERT_TOPK = 2
DISPATCH_BLOCK = 512
PAD_MARK = 0xFFFF
EPS = 1e-6
NEG = -1e30
LOG2E = math.log2(math.e)

DMA_PRIORITIES = 1
LANES = 128
ROW_TILE = 8
BF16_SUBLANES = 16
GROUP_HEADS = 4
GROUP_W = GROUP_HEADS * HEAD_DIM
ROUTER_ROWS = 128
EXPERT_ROW0 = 8
TOKEN_TILE = 1024
PLE_TILE = 1024
VMEM_LIMIT = 52 * 1024 * 1024


def _dot_nt(a, b):
    return lax.dot_general(a, b, (((1,), (1,)), ((), ())), preferred_element_type=F32)


def _rmsnorm(x, g):
    ms = jnp.mean(x * x, axis=-1, keepdims=True)
    return x * lax.rsqrt(ms + EPS) * g


def _to_token_tiles(ref, x):
    m = x.shape[0]
    for j in range(x.shape[1] // LANES):
        ref[pl.ds(j, m, stride=ROW_TILE), :] = x[:, LANES * j:LANES * (j + 1)]


def _from_token_tiles(ref, m):
    return jnp.concatenate([ref[pl.ds(j, m, stride=ROW_TILE), :] for j in range(ref.shape[0] // m)], axis=1)


def _rel_bucket_np(dist):
    n = np.maximum(dist, 0)
    exact = REL_BUCKETS // 2
    nf = np.maximum(n, 1).astype(np.float32)
    large = exact + (np.log(nf / exact) / math.log(REL_MAX_DIST / exact)
                     * (REL_BUCKETS - exact)).astype(np.int32)
    return np.where(n < exact, n, np.minimum(large, REL_BUCKETS - 1)).astype(np.int32)


def _band_buckets(block):
    qi = np.arange(block)[:, None]
    kj = np.arange(2 * block)[None, :]
    return _rel_bucket_np(qi + block - kj)


def _bias_from_buckets(bkt, tbl_ref, head):
    acc = jnp.zeros(bkt.shape, F32)
    for j in range(REL_BUCKETS):
        acc = jnp.where(bkt == j, tbl_ref[j * N_HEADS + head] * LOG2E, acc)
    return acc


def _qkv_kernel(x_ref, g_ref, w_ref, wvt_ref, hg_ref, qa_ref, ka_ref, qb_ref, kb_ref, vat_ref, vbt_ref):
    h = _rmsnorm(x_ref[...], g_ref[...]).astype(BF16)
    acc = jnp.dot(h, w_ref[...], preferred_element_type=F32)
    lo = lax.broadcasted_iota(I32, (1, LANES), 1) < HEAD_DIM

    def head_normed(c, scale):
        blk = acc[:, LANES * c:LANES * (c + 1)]
        sq = blk * blk
        s_lo = jnp.sum(jnp.where(lo, sq, 0.0), axis=-1, keepdims=True)
        s_hi = jnp.sum(jnp.where(lo, 0.0, sq), axis=-1, keepdims=True)
        inv = jnp.where(lo, lax.rsqrt(s_lo / HEAD_DIM + EPS), lax.rsqrt(s_hi / HEAD_DIM + EPS))
        return blk * inv * hg_ref[:, LANES * c:LANES * (c + 1)] * scale

    scale = HEAD_DIM ** -0.5 * LOG2E
    for c in range(4):
        qa_ref[:, LANES * c:LANES * (c + 1)] = head_normed(c, scale).astype(BF16)
        qb_ref[:, LANES * c:LANES * (c + 1)] = head_normed(5 + c, scale).astype(BF16)
    ka_ref[...] = head_normed(4, 1.0).astype(BF16)
    kb_ref[...] = head_normed(9, 1.0).astype(BF16)
    vt = _dot_nt(wvt_ref[...], h).astype(BF16)
    for j in range(vat_ref.shape[0]):
        vat_ref[j] = vt[:LANES, SWA_WINDOW * j:SWA_WINDOW * (j + 1)]
    for j in range(vbt_ref.shape[0]):
        vbt_ref[j] = vt[LANES:, MOBA_BLOCK * j:MOBA_BLOCK * (j + 1)]


def _qkv_call(x2d, attn_g, w_qk, w_vt, head_gain):
    T, D = x2d.shape
    tm = TOKEN_TILE
    tok = lambda w: pl.BlockSpec((tm, w), lambda i: (i, 0))
    full = lambda a: pl.BlockSpec(a.shape, lambda i: (0, 0))
    out_w = (SWA_Q_HEADS * HEAD_DIM, LANES, MOBA_Q_HEADS * HEAD_DIM, LANES)
    slabs = lambda blk: pl.BlockSpec((tm // blk, LANES, blk), lambda i: (i, 0, 0))
    slab_shape = lambda blk: jax.ShapeDtypeStruct((T // blk, LANES, blk), BF16)
    return pl.pallas_call(
        _qkv_kernel,
        grid=(T // tm,),
        in_specs=[tok(D), full(attn_g), full(w_qk), full(w_vt), full(head_gain)],
        out_specs=[tok(w) for w in out_w] + [slabs(SWA_WINDOW), slabs(MOBA_BLOCK)],
        out_shape=[jax.ShapeDtypeStruct((T, w), BF16) for w in out_w]
        + [slab_shape(SWA_WINDOW), slab_shape(MOBA_BLOCK)],
        compiler_params=pltpu.CompilerParams(
            dimension_semantics=("arbitrary",), vmem_limit_bytes=VMEM_LIMIT),
        name="qkv",
    )(x2d, attn_g, w_qk, w_vt, head_gain)


def _padded_heads(qblk, heads):
    zeros = jnp.zeros((qblk.shape[0], HEAD_DIM), qblk.dtype)
    pieces = []
    for h in range(heads):
        piece = qblk[:, HEAD_DIM * h:HEAD_DIM * (h + 1)]
        pieces.append(jnp.concatenate([piece, zeros] if h < GROUP_HEADS else [zeros, piece], axis=1))
    return jnp.concatenate(pieces, axis=0)


def _swa_kernel(tbl_ref, sink_ref, bkt_ref, q_ref, k_ref, vt_ref, o_ref, bias_scr):
    W = SWA_WINDOW
    QB = 2 * W
    NK = W + QB
    H = SWA_Q_HEADS
    S = q_ref.shape[0]

    @pl.when(pl.program_id(0) == 0)
    def _init_bias():
        bkt = bkt_ref[...]
        for h in range(H):
            bias_scr[:, QB * h:QB * (h + 1)] = _bias_from_buckets(bkt, tbl_ref, h)

    key_i = lax.broadcasted_iota(I32, (NK, H * QB), 0)
    qry_i = lax.broadcasted_iota(I32, (NK, H * QB), 1) & (QB - 1)
    dist = qry_i + W - key_i
    band = (dist >= 0) & (dist < W)
    real_key = key_i >= W
    head_of_lane = lax.broadcasted_iota(I32, (1, H * QB), 1) // QB
    sinks = jnp.zeros((1, H * QB), F32)
    for h in range(H):
        sinks = jnp.where(head_of_lane == h, sink_ref[h] * LOG2E, sinks)
    ones_rows = jnp.ones((BF16_SUBLANES, NK), BF16)

    def body(i, carry):
        r0 = pl.multiple_of(i * QB, QB)
        p0 = pl.multiple_of(jnp.maximum(r0 - W, 0), W)
        q8 = _padded_heads(q_ref[pl.ds(r0, QB), :], H)
        kband = jnp.concatenate([k_ref[pl.ds(p0, W), :], k_ref[pl.ds(r0, QB), :]], axis=0)
        mask = band & (real_key | (i > 0))
        s = jnp.where(mask, _dot_nt(kband, q8) + bias_scr[...], NEG)
        m = jnp.maximum(jnp.max(s, axis=0, keepdims=True), sinks)
        e = jnp.exp2(s - m)
        j = 2 * i
        vtband = jnp.concatenate([vt_ref[jnp.maximum(j - 1, 0)], vt_ref[j], vt_ref[j + 1]], axis=1)
        pv = jnp.dot(jnp.concatenate([vtband, ones_rows], axis=0), e.astype(BF16), preferred_element_type=F32)
        ot = pv[:LANES, :] / (pv[LANES:LANES + 1, :] + jnp.exp2(sinks - m))
        for pr in range(H // 2):
            f0 = HEAD_DIM * (2 * pr // GROUP_HEADS)
            pair = ot[f0:f0 + HEAD_DIM, 2 * QB * pr:2 * QB * (pr + 1)]
            o_ref[pl.ds(r0, QB), LANES * pr:LANES * (pr + 1)] = jnp.concatenate(
                [pair[:, :QB], pair[:, QB:]], axis=0).T
        return carry

    lax.fori_loop(0, S // QB, body, 0)


def _swa_call(tbl_flat, sinks, qa, ka, vat, B, S):
    W = SWA_WINDOW
    QB, NK = 2 * W, 3 * W
    bkt = jnp.asarray(_rel_bucket_np(np.arange(QB)[None, :] + W - np.arange(NK)[:, None]))
    smem = pl.BlockSpec(memory_space=pltpu.SMEM)
    return pl.pallas_call(
        _swa_kernel,
        grid=(B,),
        in_specs=[smem, smem,
                  pl.BlockSpec(bkt.shape, lambda b: (0, 0)),
                  pl.BlockSpec((S, qa.shape[1]), lambda b: (b, 0)),
                  pl.BlockSpec((S, LANES), lambda b: (b, 0)),
                  pl.BlockSpec((S // W, LANES, W), lambda b: (b, 0, 0))],
        out_specs=pl.BlockSpec((S, qa.shape[1]), lambda b: (b, 0)),
        out_shape=jax.ShapeDtypeStruct(qa.shape, F32),
        scratch_shapes=[pltpu.VMEM((NK, SWA_Q_HEADS * QB), F32)],
        compiler_params=pltpu.CompilerParams(
            dimension_semantics=("arbitrary",), vmem_limit_bytes=VMEM_LIMIT),
        name="swa",
    )(tbl_flat, sinks, bkt, qa, ka, vat)


def _moba_kernel(tbl_ref, bkt_ref, q_ref, k_ref, vt_ref, o_ref,
                 bias_scr, km_scr, q8_scr, sel_scr, m_scr, l_scr, acc_scr):
    BS = MOBA_BLOCK
    H = MOBA_Q_HEADS
    S = q_ref.shape[0]
    NBK = S // BS
    NQ = H * BS
    GQ = GROUP_HEADS * BS

    @pl.when(pl.program_id(0) == 0)
    def _init_bias():
        bkt = bkt_ref[...]
        for h in range(H):
            band = _bias_from_buckets(bkt, tbl_ref, SWA_Q_HEADS + h)
            far = tbl_ref[(REL_BUCKETS - 1) * N_HEADS + SWA_Q_HEADS + h] * LOG2E
            bias_scr[0, :, BS * h:BS * (h + 1)] = jnp.full((BS, BS), far, F32)
            bias_scr[1, :, BS * h:BS * (h + 1)] = band[:BS]
            bias_scr[2, :, BS * h:BS * (h + 1)] = band[BS:]

    blk_row = lax.broadcasted_iota(I32, (NBK, 1), 0)
    kmean = jnp.sum(k_ref[...].astype(F32).reshape(NBK, BS, LANES), axis=1) / BS
    kmean = jnp.concatenate([kmean, jnp.zeros((BF16_SUBLANES - NBK, LANES), F32)], axis=0)
    km_hi = kmean.astype(BF16)
    km_scr[0] = km_hi
    km_scr[1] = (kmean - km_hi.astype(F32)).astype(BF16)

    def scores(n, case):
        k0 = pl.multiple_of(n * BS, BS)
        return _dot_nt(k_ref[pl.ds(k0, BS), :], q8_scr[...]) + bias_scr[case]

    ones_rows = jnp.ones((BF16_SUBLANES, BS), BF16)

    def weighted_values(n, e):
        vt_ext = jnp.concatenate([vt_ref[n], ones_rows], axis=0)
        pv = jnp.dot(vt_ext, e.astype(BF16), preferred_element_type=F32)
        vals = jnp.concatenate([pv[:HEAD_DIM, :GQ], pv[HEAD_DIM:LANES, GQ:]], axis=1)
        return vals, pv[LANES:LANES + 1, :]

    def qblock(mi, carry):
        r0 = pl.multiple_of(mi * BS, BS)
        q8 = _padded_heads(q_ref[pl.ds(r0, BS), :], H)
        q8_scr[...] = q8
        gate = (_dot_nt(km_scr[0], q8) + _dot_nt(km_scr[1], q8))[:NBK]
        past = blk_row < mi
        gm = jnp.where(past, gate, NEG)
        cnt = jnp.zeros(gm.shape, F32)
        for n in range(NBK - 1):
            col = gm[n:n + 1, :]
            beats = (gm > col) | ((gm == col) & (blk_row < n))
            c = jnp.sum(jnp.where(beats, 1.0, 0.0), axis=0, keepdims=True)
            cnt = jnp.where(blk_row == n, c, cnt)
        sel_scr[...] = jnp.where(past & (cnt < MOBA_TOPK), 1.0, 0.0)

        key_i = lax.broadcasted_iota(I32, (BS, NQ), 0)
        qry_i = lax.broadcasted_iota(I32, (BS, NQ), 1) & (BS - 1)
        s = jnp.where(key_i <= qry_i, scores(mi, 2), NEG)
        m0 = jnp.max(s, axis=0, keepdims=True)
        e = jnp.exp2(s - m0)
        m_scr[...] = m0
        acc_scr[...], l_scr[...] = weighted_values(mi, e)

        def kvblock(j, c2):
            n = mi - j
            s = jnp.where(sel_scr[pl.ds(n, 1), :] > 0.5, scores(n, jnp.where(j == 1, 1, 0)), NEG)
            m_prev = m_scr[...]
            m_new = jnp.maximum(m_prev, jnp.max(s, axis=0, keepdims=True))
            alpha = jnp.exp2(m_prev - m_new)
            e = jnp.exp2(s - m_new)
            vals, row_sum = weighted_values(n, e)
            l_scr[...] = alpha * l_scr[...] + row_sum
            acc_scr[...] = alpha * acc_scr[...] + vals
            m_scr[...] = m_new
            return c2

        lax.fori_loop(1, mi + 1, kvblock, 0)
        ot = acc_scr[...] / l_scr[...]
        for pr in range(H // 2):
            pair = ot[:, 2 * BS * pr:2 * BS * (pr + 1)]
            o_ref[pl.ds(r0, BS), LANES * pr:LANES * (pr + 1)] = jnp.concatenate(
                [pair[:, :BS], pair[:, BS:]], axis=0).T
        return carry

    lax.fori_loop(0, NBK, qblock, 0)


def _moba_call(tbl_flat, qb, kb, vbt, B, S):
    BS = MOBA_BLOCK
    NQ = MOBA_Q_HEADS * BS
    NBK = S // BS
    bkt = jnp.asarray(np.ascontiguousarray(_band_buckets(BS).T))
    smem = pl.BlockSpec(memory_space=pltpu.SMEM)
    return pl.pallas_call(
        _moba_kernel,
        grid=(B,),
        in_specs=[smem,
                  pl.BlockSpec(bkt.shape, lambda b: (0, 0)),
                  pl.BlockSpec((S, qb.shape[1]), lambda b: (b, 0)),
                  pl.BlockSpec((S, LANES), lambda b: (b, 0)),
                  pl.BlockSpec((NBK, LANES, BS), lambda b: (b, 0, 0))],
        out_specs=pl.BlockSpec((S, qb.shape[1]), lambda b: (b, 0)),
        out_shape=jax.ShapeDtypeStruct(qb.shape, F32),
        scratch_shapes=[pltpu.VMEM((3, BS, NQ), F32),
                        pltpu.VMEM((2, BF16_SUBLANES, LANES), BF16),
                        pltpu.VMEM((NQ, LANES), BF16),
                        pltpu.VMEM((NBK, NQ), F32),
                        pltpu.VMEM((1, NQ), F32),
                        pltpu.VMEM((1, NQ), F32),
                        pltpu.VMEM((HEAD_DIM, NQ), F32)],
        compiler_params=pltpu.CompilerParams(
            dimension_semantics=("arbitrary",), vmem_limit_bytes=VMEM_LIMIT),
        name="moba",
    )(tbl_flat, bkt, qb, kb, vbt)


def _outproj_kernel(oa_ref, ob_ref, x_ref, ga_ref, gb_ref, woa_ref, wob_ref, fg_ref,
                    wrh_ref, wrl_ref, rb_ref, x1_ref, hf_ref, eid_ref, wt_ref):
    na = _rmsnorm(oa_ref[...], ga_ref[...]).astype(BF16)
    nb = _rmsnorm(ob_ref[...], gb_ref[...]).astype(BF16)
    y = (jnp.dot(na, woa_ref[...], preferred_element_type=F32)
         + jnp.dot(nb, wob_ref[...], preferred_element_type=F32))
    x1 = x_ref[...] + y
    x1_ref[...] = x1
    hf = _rmsnorm(x1, fg_ref[...])
    _to_token_tiles(hf_ref, hf)

    hf_hi = hf.astype(BF16)
    hf_lo = (hf - hf_hi.astype(F32)).astype(BF16)
    lt = (_dot_nt(wrh_ref[...], hf_hi) + _dot_nt(wrh_ref[...], hf_lo)
          + _dot_nt(wrl_ref[...], hf_hi) + rb_ref[...])
    gl = [lt[j:j + 1, :] for j in range(N_GROUPS)]
    best = gl[0]
    gsel = jnp.zeros(best.shape, I32)
    for j in range(1, N_GROUPS):
        better = gl[j] > best
        gsel = jnp.where(better, j, gsel)
        best = jnp.where(better, gl[j], best)
    gsum = jnp.zeros(best.shape, F32)
    for j in range(N_GROUPS):
        gsum = gsum + jnp.exp(gl[j] - best)
    gw = 1.0 / gsum
    E = EXPERTS_PER_GROUP
    es = lt[EXPERT_ROW0:EXPERT_ROW0 + E, :]
    for j in range(1, N_GROUPS):
        es = jnp.where(gsel == j, lt[EXPERT_ROW0 + E * j:EXPERT_ROW0 + E * (j + 1), :], es)
    row = lax.broadcasted_iota(I32, es.shape, 0)
    v1 = jnp.max(es, axis=0, keepdims=True)
    i1 = jnp.min(jnp.where(es == v1, row, E), axis=0, keepdims=True)
    es2 = jnp.where(row == i1, -jnp.inf, es)
    v2 = jnp.max(es2, axis=0, keepdims=True)
    i2 = jnp.min(jnp.where(es2 == v2, row, E), axis=0, keepdims=True)
    e2 = jnp.exp(v2 - v1)
    den = 1.0 + e2
    eid_ref[...] = jnp.concatenate([gsel * E + i1, gsel * E + i2], axis=0)
    wt_ref[...] = jnp.concatenate([gw * (1.0 / den), gw * (e2 / den)], axis=0)


def _outproj_call(oa, ob, x2d, ga, gb, woa, wob, fg, wrh, wrl, rb):
    T, D = x2d.shape
    tm = TOKEN_TILE
    tok = lambda w: pl.BlockSpec((tm, w), lambda i: (i, 0))
    full = lambda a: pl.BlockSpec(a.shape, lambda i: (0, 0))
    col = pl.BlockSpec((EXPERT_TOPK, tm), lambda i: (0, i))
    return pl.pallas_call(
        _outproj_kernel,
        grid=(T // tm,),
        in_specs=[tok(oa.shape[1]), tok(ob.shape[1]), tok(D), full(ga), full(gb), full(woa), full(wob),
                  full(fg), full(wrh), full(wrl), full(rb)],
        out_specs=[tok(D), pl.BlockSpec((tm * ROW_TILE, LANES), lambda i: (i, 0)), col, col],
        out_shape=[jax.ShapeDtypeStruct((T, D), F32), jax.ShapeDtypeStruct((T * ROW_TILE, LANES), F32),
                   jax.ShapeDtypeStruct((EXPERT_TOPK, T), I32), jax.ShapeDtypeStruct((EXPERT_TOPK, T), F32)],
        compiler_params=pltpu.CompilerParams(
            dimension_semantics=("arbitrary",), vmem_limit_bytes=VMEM_LIMIT),
        name="outproj",
    )(oa, ob, x2d, ga, gb, woa, wob, fg, wrh, wrl, rb)


COPY_GROUP = 32


def _moe_kernel(be_ref, nv_ref, src_ref, srcn_ref, dst_ref, hf_hbm, wg_ref, wu_ref, wd_ref, out_hbm,
                xbuf, obuf, gsem, ssem):
    i = pl.program_id(0)
    nblk = pl.num_programs(0)
    M = xbuf.shape[1] // ROW_TILE
    slot = i % 2
    nv_cur = nv_ref[i]
    nv_next = jnp.where(i + 1 < nblk, nv_ref[jnp.minimum(i + 1, nblk - 1)], 0)

    def for_real_groups(n_real, fn):
        for k in range(M // COPY_GROUP):
            @pl.when(k * COPY_GROUP < n_real)
            def _group():
                fn(range(k * COPY_GROUP, (k + 1) * COPY_GROUP))

    def gather_rows(idx_ref, buf_slot, n_real):
        def start(rows):
            for r in rows:
                tok_row = pl.multiple_of(idx_ref[0, 0, r] * ROW_TILE, ROW_TILE)
                pltpu.make_async_copy(hf_hbm.at[pl.ds(tok_row, ROW_TILE)],
                                      xbuf.at[buf_slot, pl.ds(r * ROW_TILE, ROW_TILE)],
                                      gsem.at[buf_slot]).start(priority=r % DMA_PRIORITIES)
        for_real_groups(n_real, start)

    def scatter_rows(idx_ref, buf_slot, n_real):
        def start(rows):
            for r in rows:
                dst_row = pl.multiple_of(idx_ref[0, 0, r] * ROW_TILE, ROW_TILE)
                pltpu.make_async_copy(obuf.at[buf_slot, pl.ds(r * ROW_TILE, ROW_TILE)],
                                      out_hbm.at[pl.ds(dst_row, ROW_TILE)],
                                      ssem.at[buf_slot]).start(priority=r % DMA_PRIORITIES)
        for_real_groups(n_real, start)

    def wait_rows(buf, buf_slot, sem, n_real):
        span = COPY_GROUP * ROW_TILE
        for_real_groups(n_real, lambda rows: pltpu.make_async_copy(
            hf_hbm.at[pl.ds(0, span)], buf.at[buf_slot, pl.ds(0, span)], sem.at[buf_slot]).wait())

    @pl.when(nv_cur > 0)
    def _compute():
        @pl.when(i == 0)
        def _first_gather():
            gather_rows(src_ref, 0, nv_cur)
            obuf[1] = jnp.zeros(obuf.shape[1:], F32)
            spare = pltpu.make_async_copy(
                obuf.at[1], out_hbm.at[pl.ds(out_hbm.shape[0] - M * ROW_TILE, M * ROW_TILE)], ssem.at[1])
            spare.start()
            spare.wait()

        wait_rows(xbuf, slot, gsem, nv_cur)

        @pl.when(i >= 2)
        def _reuse_obuf():
            wait_rows(obuf, slot, ssem, nv_ref[jnp.maximum(i - 2, 0)])

        gather_rows(srcn_ref, 1 - slot, nv_next)
        xb = _from_token_tiles(xbuf.at[slot], M).astype(BF16)
        a = jnp.dot(xb, wg_ref[0].astype(BF16), preferred_element_type=F32)
        u = jnp.dot(xb, wu_ref[0].astype(BF16), preferred_element_type=F32)
        hmid = (a * jax.nn.sigmoid(a)) * u
        _to_token_tiles(obuf.at[slot], jnp.dot(hmid.astype(BF16), wd_ref[0].astype(BF16),
                                               preferred_element_type=F32))
        scatter_rows(dst_ref, slot, nv_cur)

        @pl.when(nv_next == 0)
        def _drain():
            wait_rows(obuf, slot, ssem, nv_cur)

            @pl.when(i >= 1)
            def _drain_prev():
                wait_rows(obuf, 1 - slot, ssem, nv_ref[jnp.maximum(i - 1, 0)])


def _moe_call(block_expert, n_real, src_tok, dst_row, hf, w_g, w_u, w_d):
    T = hf.shape[0] // ROW_TILE
    M = DISPATCH_BLOCK
    nblk = block_expert.shape[0]
    D, DE = w_g.shape[1], w_g.shape[2]
    assert D == ROW_TILE * LANES and hf.shape[1] == LANES and M % COPY_GROUP == 0
    idx_spec = lambda f: pl.BlockSpec((1, 1, M), f, memory_space=pltpu.SMEM)
    cur = lambda i, be, nu: (i, 0, 0)
    nxt = lambda i, be, nu: (jnp.minimum(i + 1, nblk - 1), 0, 0)
    wspec = lambda a, b: pl.BlockSpec((1, a, b), lambda i, be, nu: (be[i], 0, 0))
    grid_spec = pltpu.PrefetchScalarGridSpec(
        num_scalar_prefetch=2,
        grid=(nblk,),
        in_specs=[idx_spec(cur), idx_spec(nxt), idx_spec(cur),
                  pl.BlockSpec(memory_space=pl.ANY),
                  wspec(D, DE), wspec(D, DE), wspec(DE, D)],
        out_specs=pl.BlockSpec(memory_space=pl.ANY),
        scratch_shapes=[pltpu.VMEM((2, M * ROW_TILE, LANES), F32), pltpu.VMEM((2, M * ROW_TILE, LANES), F32),
                        pltpu.SemaphoreType.DMA((2,)), pltpu.SemaphoreType.DMA((2,))],
    )
    src3 = src_tok.reshape(nblk, 1, M)
    return pl.pallas_call(
        _moe_kernel,
        grid_spec=grid_spec,
        out_shape=jax.ShapeDtypeStruct(((EXPERT_TOPK * T + M) * ROW_TILE, LANES), F32),
        compiler_params=pltpu.CompilerParams(
            dimension_semantics=("arbitrary",), vmem_limit_bytes=VMEM_LIMIT),
        name="moe",
    )(block_expert, n_real, src3, src3, dst_row.reshape(nblk, 1, M), hf, w_g, w_u, w_d)


def _ple_kernel(x1_ref, m0_ref, m1_ref, wt_ref, p_ref, pg_ref, wgate_ref, wproj_ref, pog_ref, out_ref):
    proj = _rmsnorm(jnp.dot(p_ref[...].astype(BF16), wproj_ref[...], preferred_element_type=F32),
                    pog_ref[...])
    tm = x1_ref.shape[0]
    moe = (_from_token_tiles(m0_ref, tm) * wt_ref[:, 0:1] + _from_token_tiles(m1_ref, tm) * wt_ref[:, 1:2])
    x2 = x1_ref[...] + moe
    hn = _rmsnorm(x2, pg_ref[...]).astype(BF16)
    gate = jax.nn.sigmoid(jnp.dot(hn, wgate_ref[...], preferred_element_type=F32))
    out_ref[...] = x2 + gate * proj


def _ple_call(moe_rows, x1, wt_cols, p2d, pg, wgate, wproj, pog):
    T, D = x1.shape
    tm = PLE_TILE
    tok = lambda w: pl.BlockSpec((tm, w), lambda i: (i, 0))
    full = lambda a: pl.BlockSpec(a.shape, lambda i: (0, 0))
    first = pl.BlockSpec((tm * ROW_TILE, LANES), lambda i: (i, 0))
    second = pl.BlockSpec((tm * ROW_TILE, LANES), lambda i: (T // tm + i, 0))
    return pl.pallas_call(
        _ple_kernel,
        grid=(T // tm,),
        in_specs=[tok(D), first, second, tok(EXPERT_TOPK), tok(p2d.shape[1]),
                  full(pg), full(wgate), full(wproj), full(pog)],
        out_specs=tok(D),
        out_shape=jax.ShapeDtypeStruct((T, D), F32),
        compiler_params=pltpu.CompilerParams(
            dimension_semantics=("arbitrary",), vmem_limit_bytes=VMEM_LIMIT),
        name="ple",
    )(x1, moe_rows, moe_rows, wt_cols, p2d, pg, wgate, wproj, pog)


def _dispatch_plan(eid, T):
    A = EXPERT_TOPK * T
    M = DISPATCH_BLOCK
    nblk = A // M + N_EXPERTS
    assert A <= PAD_MARK
    e_flat = eid.reshape(A)
    experts = jnp.arange(N_EXPERTS, dtype=I32)
    counts = jnp.sum((e_flat[:, None] == experts[None, :]).astype(I32), axis=0)
    padded = (counts + M - 1) // M * M
    n_used = jnp.sum(padded) // M
    row = jnp.arange(M, dtype=I32)[None, :]
    pad_keys = jnp.where(row < (padded - counts)[:, None], 2 * experts[:, None] + 1, 2 * N_EXPERTS)
    keys = jnp.concatenate([2 * e_flat, pad_keys.reshape(-1)])
    vals = jnp.concatenate([jnp.arange(A, dtype=I32), jnp.full((N_EXPERTS * M,), PAD_MARK, I32)])
    packed = jnp.sort(keys * (PAD_MARK + 1) + vals).reshape(nblk, M)
    a_s = packed & PAD_MARK
    blk = jnp.arange(nblk, dtype=I32)
    be = jnp.minimum(packed[:, 0] // (2 * (PAD_MARK + 1)), N_EXPERTS - 1)
    be = jnp.where(blk < n_used, be, be[jnp.maximum(n_used - 1, 0)])
    valid = a_s != PAD_MARK
    src_tok = jnp.where(valid, jnp.where(a_s >= T, a_s - T, a_s), 0)
    dst_row = jnp.where(valid, a_s, A + row)
    return be, jnp.sum(valid.astype(I32), axis=1), src_tok, dst_row


def _layer(x2d, p2d, rel_bias, attn_norm, w_in, swa_q_norm, swa_k_norm, swa_sinks, moba_q_norm,
           moba_k_norm, swa_out_norm, moba_out_norm, w_out, ffn_norm, w_rg, b_rg, w_re, b_re,
           w_g, w_u, w_d, ple_norm, w_ple_gate, w_ple_proj, ple_out_norm, B, S):
    T, D = x2d.shape
    row = lambda v: v.reshape(1, -1).astype(F32)
    head_gain = jnp.concatenate([
        jnp.tile(swa_q_norm, SWA_Q_HEADS), jnp.tile(swa_k_norm, SWA_KV_HEADS),
        jnp.tile(moba_q_norm, MOBA_Q_HEADS), jnp.tile(moba_k_norm, MOBA_KV_HEADS)])
    w_in_b = w_in.astype(BF16)
    qa_w, kv_w = SWA_Q_HEADS * HEAD_DIM, SWA_KV_HEADS * HEAD_DIM
    qb_w = MOBA_Q_HEADS * HEAD_DIM
    c_va = qa_w + kv_w
    c_qb = c_va + kv_w
    c_vb = c_qb + qb_w + kv_w
    w_qk = jnp.concatenate([w_in_b[:, :c_va], w_in_b[:, c_qb:c_vb]], axis=1)
    w_vt = jnp.concatenate([w_in_b[:, c_va:c_qb], w_in_b[:, c_vb:]], axis=1).T
    qa, ka, qb, kb, vat, vbt = _qkv_call(x2d, row(attn_norm), w_qk, w_vt, row(head_gain))

    tbl_flat = rel_bias.astype(F32).reshape(-1)
    oa = _swa_call(tbl_flat, swa_sinks.astype(F32), qa, ka, vat, B, S)
    ob = _moba_call(tbl_flat, qb, kb, vbt, B, S)

    pad_rows = lambda n: jnp.zeros((n, D), F32)
    wr = jnp.concatenate([w_rg.T, pad_rows(EXPERT_ROW0 - N_GROUPS), w_re.T,
                          pad_rows(ROUTER_ROWS - EXPERT_ROW0 - N_EXPERTS)], axis=0)
    wr_hi = wr.astype(BF16)
    wr_lo = (wr - wr_hi.astype(F32)).astype(BF16)
    rb = jnp.concatenate([b_rg, jnp.zeros((EXPERT_ROW0 - N_GROUPS,), F32), b_re,
                          jnp.zeros((ROUTER_ROWS - EXPERT_ROW0 - N_EXPERTS,), F32)]).reshape(ROUTER_ROWS, 1)
    wo = w_out.astype(BF16)
    na_w = SWA_Q_HEADS * HEAD_DIM
    x1, hf, eid, wts = _outproj_call(oa, ob, x2d, row(swa_out_norm), row(moba_out_norm),
                                     wo[:na_w], wo[na_w:], row(ffn_norm), wr_hi, wr_lo, rb)

    be, n_real, src_tok, dst_row = _dispatch_plan(eid, T)
    moe_rows = _moe_call(be, n_real, src_tok, dst_row, hf, w_g, w_u, w_d)
    return _ple_call(moe_rows, x1, wts.T, p2d, row(ple_norm), w_ple_gate.astype(BF16),
                     w_ple_proj.astype(BF16), row(ple_out_norm))


def kernel(x, p, rel_bias, attn_norm, w_in, swa_q_norm, swa_k_norm, swa_sinks, moba_q_norm, moba_k_norm,
           swa_out_norm, moba_out_norm, w_out, ffn_norm, w_router_group, b_router_group, w_router_expert,
           b_router_expert, w_exp_gate, w_exp_up, w_exp_down, ple_norm, w_ple_gate, w_ple_proj, ple_out_norm):
    B, S, D = x.shape
    x2d = x.reshape(B * S, D)
    for i in range(p.shape[0]):
        x2d = _layer(x2d, p[i].reshape(B * S, -1), rel_bias, attn_norm[i], w_in[i], swa_q_norm[i],
                     swa_k_norm[i], swa_sinks[i], moba_q_norm[i], moba_k_norm[i], swa_out_norm[i],
                     moba_out_norm[i], w_out[i], ffn_norm[i], w_router_group[i], b_router_group[i],
                     w_router_expert[i], b_router_expert[i], w_exp_gate[i], w_exp_up[i], w_exp_down[i],
                     ple_norm[i], w_ple_gate[i], w_ple_proj[i], ple_out_norm[i], B, S)
    return x2d.reshape(B, S, D)
```

```python
import math

import numpy as np
import jax
import jax.numpy as jnp
from jax import lax
from jax.experimental import pallas as pl
from jax.experimental.pallas import tpu as pltpu

F32 = jnp.float32
BF16 = jnp.bfloat16
I32 = jnp.int32

HEAD_DIM = 64
SWA_Q_HEADS = 8
SWA_KV_HEADS = 2
SWA_WINDOW = 128
MOBA_Q_HEADS = 8
MOBA_KV_HEADS = 2
MOBA_BLOCK = 256
MOBA_TOPK = 3
N_HEADS = SWA_Q_HEADS + MOBA_Q_HEADS
REL_BUCKETS = 32
REL_MAX_DIST = 128
N_GROUPS = 4
EXPERTS_PER_GROUP = 8
N_EXPERTS = N_GROUPS * EXPERTS_PER_GROUP
EXPERT_TOPK = 2
DISPATCH_BLOCK = 512
PAD_MARK = 0xFFFF
EPS = 1e-6
NEG = -1e30
LOG2E = math.log2(math.e)

DMA_PRIORITIES = 2
LANES = 128
ROW_TILE = 8
BF16_SUBLANES = 16
GROUP_HEADS = 4
GROUP_W = GROUP_HEADS * HEAD_DIM
ROUTER_ROWS = 128
EXPERT_ROW0 = 8
TOKEN_TILE = 1024
PLE_TILE = 1024
VMEM_LIMIT = 52 * 1024 * 1024


def _dot_nt(a, b):
    return lax.dot_general(a, b, (((1,), (1,)), ((), ())), preferred_element_type=F32)


def _rmsnorm(x, g):
    ms = jnp.mean(x * x, axis=-1, keepdims=True)
    return x * lax.rsqrt(ms + EPS) * g


def _to_token_tiles(ref, x):
    m = x.shape[0]
    for j in range(x.shape[1] // LANES):
        ref[pl.ds(j, m, stride=ROW_TILE), :] = x[:, LANES * j:LANES * (j + 1)]


def _from_token_tiles(ref, m):
    return jnp.concatenate([ref[pl.ds(j, m, stride=ROW_TILE), :] for j in range(ref.shape[0] // m)], axis=1)


def _rel_bucket_np(dist):
    n = np.maximum(dist, 0)
    exact = REL_BUCKETS // 2
    nf = np.maximum(n, 1).astype(np.float32)
    large = exact + (np.log(nf / exact) / math.log(REL_MAX_DIST / exact)
                     * (REL_BUCKETS - exact)).astype(np.int32)
    return np.where(n < exact, n, np.minimum(large, REL_BUCKETS - 1)).astype(np.int32)


def _band_buckets(block):
    qi = np.arange(block)[:, None]
    kj = np.arange(2 * block)[None, :]
    return _rel_bucket_np(qi + block - kj)


def _bias_from_buckets(bkt, tbl_ref, head):
    acc = jnp.zeros(bkt.shape, F32)
    for j in range(REL_BUCKETS):
        acc = jnp.where(bkt == j, tbl_ref[j * N_HEADS + head] * LOG2E, acc)
    return acc


def _qkv_kernel(x_ref, g_ref, w_ref, wvt_ref, hg_ref, qa_ref, ka_ref, qb_ref, kb_ref, vat_ref, vbt_ref):
    h = _rmsnorm(x_ref[...], g_ref[...]).astype(BF16)
    acc = jnp.dot(h, w_ref[...], preferred_element_type=F32)
    lo = lax.broadcasted_iota(I32, (1, LANES), 1) < HEAD_DIM

    def head_normed(c, scale):
        blk = acc[:, LANES * c:LANES * (c + 1)]
        sq = blk * blk
        s_lo = jnp.sum(jnp.where(lo, sq, 0.0), axis=-1, keepdims=True)
        s_hi = jnp.sum(jnp.where(lo, 0.0, sq), axis=-1, keepdims=True)
        inv = jnp.where(lo, lax.rsqrt(s_lo / HEAD_DIM + EPS), lax.rsqrt(s_hi / HEAD_DIM + EPS))
        return blk * inv * hg_ref[:, LANES * c:LANES * (c + 1)] * scale

    scale = HEAD_DIM ** -0.5 * LOG2E
    for c in range(4):
        qa_ref[:, LANES * c:LANES * (c + 1)] = head_normed(c, scale).astype(BF16)
        qb_ref[:, LANES * c:LANES * (c + 1)] = head_normed(5 + c, scale).astype(BF16)
    ka_ref[...] = head_normed(4, 1.0).astype(BF16)
    kb_ref[...] = head_normed(9, 1.0).astype(BF16)
    vt = _dot_nt(wvt_ref[...], h).astype(BF16)
    for j in range(vat_ref.shape[0]):
        vat_ref[j] = vt[:LANES, SWA_WINDOW * j:SWA_WINDOW * (j + 1)]
    for j in range(vbt_ref.shape[0]):
        vbt_ref[j] = vt[LANES:, MOBA_BLOCK * j:MOBA_BLOCK * (j + 1)]


def _qkv_call(x2d, attn_g, w_qk, w_vt, head_gain):
    T, D = x2d.shape
    tm = TOKEN_TILE
    tok = lambda w: pl.BlockSpec((tm, w), lambda i: (i, 0))
    full = lambda a: pl.BlockSpec(a.shape, lambda i: (0, 0))
    out_w = (SWA_Q_HEADS * HEAD_DIM, LANES, MOBA_Q_HEADS * HEAD_DIM, LANES)
    slabs = lambda blk: pl.BlockSpec((tm // blk, LANES, blk), lambda i: (i, 0, 0))
    slab_shape = lambda blk: jax.ShapeDtypeStruct((T // blk, LANES, blk), BF16)
    return pl.pallas_call(
        _qkv_kernel,
        grid=(T // tm,),
        in_specs=[tok(D), full(attn_g), full(w_qk), full(w_vt), full(head_gain)],
        out_specs=[tok(w) for w in out_w] + [slabs(SWA_WINDOW), slabs(MOBA_BLOCK)],
        out_shape=[jax.ShapeDtypeStruct((T, w), BF16) for w in out_w]
        + [slab_shape(SWA_WINDOW), slab_shape(MOBA_BLOCK)],
        compiler_params=pltpu.CompilerParams(
            dimension_semantics=("arbitrary",), vmem_limit_bytes=VMEM_LIMIT),
        name="qkv",
    )(x2d, attn_g, w_qk, w_vt, head_gain)


def _padded_heads(qblk, heads):
    zeros = jnp.zeros((qblk.shape[0], HEAD_DIM), qblk.dtype)
    pieces = []
    for h in range(heads):
        piece = qblk[:, HEAD_DIM * h:HEAD_DIM * (h + 1)]
        pieces.append(jnp.concatenate([piece, zeros] if h < GROUP_HEADS else [zeros, piece], axis=1))
    return jnp.concatenate(pieces, axis=0)


def _swa_kernel(tbl_ref, sink_ref, bkt_ref, q_ref, k_ref, vt_ref, o_ref, bias_scr):
    W = SWA_WINDOW
    QB = 2 * W
    NK = W + QB
    H = SWA_Q_HEADS
    S = q_ref.shape[0]

    @pl.when(pl.program_id(0) == 0)
    def _init_bias():
        bkt = bkt_ref[...]
        for h in range(H):
            bias_scr[:, QB * h:QB * (h + 1)] = _bias_from_buckets(bkt, tbl_ref, h)

    key_i = lax.broadcasted_iota(I32, (NK, H * QB), 0)
    qry_i = lax.broadcasted_iota(I32, (NK, H * QB), 1) & (QB - 1)
    dist = qry_i + W - key_i
    band = (dist >= 0) & (dist < W)
    real_key = key_i >= W
    head_of_lane = lax.broadcasted_iota(I32, (1, H * QB), 1) // QB
    sinks = jnp.zeros((1, H * QB), F32)
    for h in range(H):
        sinks = jnp.where(head_of_lane == h, sink_ref[h] * LOG2E, sinks)
    ones_rows = jnp.ones((BF16_SUBLANES, NK), BF16)

    def body(i, carry):
        r0 = pl.multiple_of(i * QB, QB)
        p0 = pl.multiple_of(jnp.maximum(r0 - W, 0), W)
        q8 = _padded_heads(q_ref[pl.ds(r0, QB), :], H)
        kband = jnp.concatenate([k_ref[pl.ds(p0, W), :], k_ref[pl.ds(r0, QB), :]], axis=0)
        mask = band & (real_key | (i > 0))
        s = jnp.where(mask, _dot_nt(kband, q8) + bias_scr[...], NEG)
        m = jnp.maximum(jnp.max(s, axis=0, keepdims=True), sinks)
        e = jnp.exp2(s - m)
        j = 2 * i
        vtband = jnp.concatenate([vt_ref[jnp.maximum(j - 1, 0)], vt_ref[j], vt_ref[j + 1]], axis=1)
        pv = jnp.dot(jnp.concatenate([vtband, ones_rows], axis=0), e.astype(BF16), preferred_element_type=F32)
        ot = pv[:LANES, :] / (pv[LANES:LANES + 1, :] + jnp.exp2(sinks - m))
        for pr in range(H // 2):
            f0 = HEAD_DIM * (2 * pr // GROUP_HEADS)
            pair = ot[f0:f0 + HEAD_DIM, 2 * QB * pr:2 * QB * (pr + 1)]
            o_ref[pl.ds(r0, QB), LANES * pr:LANES * (pr + 1)] = jnp.concatenate(
                [pair[:, :QB], pair[:, QB:]], axis=0).T
        return carry

    lax.fori_loop(0, S // QB, body, 0)


def _swa_call(tbl_flat, sinks, qa, ka, vat, B, S):
    W = SWA_WINDOW
    QB, NK = 2 * W, 3 * W
    bkt = jnp.asarray(_rel_bucket_np(np.arange(QB)[None, :] + W - np.arange(NK)[:, None]))
    smem = pl.BlockSpec(memory_space=pltpu.SMEM)
    return pl.pallas_call(
        _swa_kernel,
        grid=(B,),
        in_specs=[smem, smem,
                  pl.BlockSpec(bkt.shape, lambda b: (0, 0)),
                  pl.BlockSpec((S, qa.shape[1]), lambda b: (b, 0)),
                  pl.BlockSpec((S, LANES), lambda b: (b, 0)),
                  pl.BlockSpec((S // W, LANES, W), lambda b: (b, 0, 0))],
        out_specs=pl.BlockSpec((S, qa.shape[1]), lambda b: (b, 0)),
        out_shape=jax.ShapeDtypeStruct(qa.shape, F32),
        scratch_shapes=[pltpu.VMEM((NK, SWA_Q_HEADS * QB), F32)],
        compiler_params=pltpu.CompilerParams(
            dimension_semantics=("arbitrary",), vmem_limit_bytes=VMEM_LIMIT),
        name="swa",
    )(tbl_flat, sinks, bkt, qa, ka, vat)


def _moba_kernel(tbl_ref, bkt_ref, q_ref, k_ref, vt_ref, o_ref,
                 bias_scr, km_scr, q8_scr, sel_scr, m_scr, l_scr, acc_scr):
    BS = MOBA_BLOCK
    H = MOBA_Q_HEADS
    S = q_ref.shape[0]
    NBK = S // BS
    NQ = H * BS
    GQ = GROUP_HEADS * BS

    @pl.when(pl.program_id(0) == 0)
    def _init_bias():
        bkt = bkt_ref[...]
        for h in range(H):
            band = _bias_from_buckets(bkt, tbl_ref, SWA_Q_HEADS + h)
            far = tbl_ref[(REL_BUCKETS - 1) * N_HEADS + SWA_Q_HEADS + h] * LOG2E
            bias_scr[0, :, BS * h:BS * (h + 1)] = jnp.full((BS, BS), far, F32)
            bias_scr[1, :, BS * h:BS * (h + 1)] = band[:BS]
            bias_scr[2, :, BS * h:BS * (h + 1)] = band[BS:]

    blk_row = lax.broadcasted_iota(I32, (NBK, 1), 0)
    kmean = jnp.sum(k_ref[...].astype(F32).reshape(NBK, BS, LANES), axis=1) / BS
    kmean = jnp.concatenate([kmean, jnp.zeros((BF16_SUBLANES - NBK, LANES), F32)], axis=0)
    km_hi = kmean.astype(BF16)
    km_scr[0] = km_hi
    km_scr[1] = (kmean - km_hi.astype(F32)).astype(BF16)

    def scores(n, case):
        k0 = pl.multiple_of(n * BS, BS)
        return _dot_nt(k_ref[pl.ds(k0, BS), :], q8_scr[...]) + bias_scr[case]

    ones_rows = jnp.ones((BF16_SUBLANES, BS), BF16)

    def weighted_values(n, e):
        vt_ext = jnp.concatenate([vt_ref[n], ones_rows], axis=0)
        pv = jnp.dot(vt_ext, e.astype(BF16), preferred_element_type=F32)
        vals = jnp.concatenate([pv[:HEAD_DIM, :GQ], pv[HEAD_DIM:LANES, GQ:]], axis=1)
        return vals, pv[LANES:LANES + 1, :]

    def qblock(mi, carry):
        r0 = pl.multiple_of(mi * BS, BS)
        q8 = _padded_heads(q_ref[pl.ds(r0, BS), :], H)
        q8_scr[...] = q8
        gate = (_dot_nt(km_scr[0], q8) + _dot_nt(km_scr[1], q8))[:NBK]
        past = blk_row < mi
        gm = jnp.where(past, gate, NEG)
        cnt = jnp.zeros(gm.shape, F32)
        for n in range(NBK - 1):
            col = gm[n:n + 1, :]
            beats = (gm > col) | ((gm == col) & (blk_row < n))
            c = jnp.sum(jnp.where(beats, 1.0, 0.0), axis=0, keepdims=True)
            cnt = jnp.where(blk_row == n, c, cnt)
        sel_scr[...] = jnp.where(past & (cnt < MOBA_TOPK), 1.0, 0.0)

        key_i = lax.broadcasted_iota(I32, (BS, NQ), 0)
        qry_i = lax.broadcasted_iota(I32, (BS, NQ), 1) & (BS - 1)
        s = jnp.where(key_i <= qry_i, scores(mi, 2), NEG)
        m0 = jnp.max(s, axis=0, keepdims=True)
        e = jnp.exp2(s - m0)
        m_scr[...] = m0
        acc_scr[...], l_scr[...] = weighted_values(mi, e)

        def kvblock(j, c2):
            n = mi - j
            s = jnp.where(sel_scr[pl.ds(n, 1), :] > 0.5, scores(n, jnp.where(j == 1, 1, 0)), NEG)
            m_prev = m_scr[...]
            m_new = jnp.maximum(m_prev, jnp.max(s, axis=0, keepdims=True))
            alpha = jnp.exp2(m_prev - m_new)
            e = jnp.exp2(s - m_new)
            vals, row_sum = weighted_values(n, e)
            l_scr[...] = alpha * l_scr[...] + row_sum
            acc_scr[...] = alpha * acc_scr[...] + vals
            m_scr[...] = m_new
            return c2

        lax.fori_loop(1, mi + 1, kvblock, 0)
        ot = acc_scr[...] / l_scr[...]
        for pr in range(H // 2):
            pair = ot[:, 2 * BS * pr:2 * BS * (pr + 1)]
            o_ref[pl.ds(r0, BS), LANES * pr:LANES * (pr + 1)] = jnp.concatenate(
                [pair[:, :BS], pair[:, BS:]], axis=0).T
        return carry

    lax.fori_loop(0, NBK, qblock, 0)


def _moba_call(tbl_flat, qb, kb, vbt, B, S):
    BS = MOBA_BLOCK
    NQ = MOBA_Q_HEADS * BS
    NBK = S // BS
    bkt = jnp.asarray(np.ascontiguousarray(_band_buckets(BS).T))
    smem = pl.BlockSpec(memory_space=pltpu.SMEM)
    return pl.pallas_call(
        _moba_kernel,
        grid=(B,),
        in_specs=[smem,
                  pl.BlockSpec(bkt.shape, lambda b: (0, 0)),
                  pl.BlockSpec((S, qb.shape[1]), lambda b: (b, 0)),
                  pl.BlockSpec((S, LANES), lambda b: (b, 0)),
                  pl.BlockSpec((NBK, LANES, BS), lambda b: (b, 0, 0))],
        out_specs=pl.BlockSpec((S, qb.shape[1]), lambda b: (b, 0)),
        out_shape=jax.ShapeDtypeStruct(qb.shape, F32),
        scratch_shapes=[pltpu.VMEM((3, BS, NQ), F32),
                        pltpu.VMEM((2, BF16_SUBLANES, LANES), BF16),
                        pltpu.VMEM((NQ, LANES), BF16),
                        pltpu.VMEM((NBK, NQ), F32),
                        pltpu.VMEM((1, NQ), F32),
                        pltpu.VMEM((1, NQ), F32),
                        pltpu.VMEM((HEAD_DIM, NQ), F32)],
        compiler_params=pltpu.CompilerParams(
            dimension_semantics=("arbitrary",), vmem_limit_bytes=VMEM_LIMIT),
        name="moba",
    )(tbl_flat, bkt, qb, kb, vbt)


def _outproj_kernel(oa_ref, ob_ref, x_ref, ga_ref, gb_ref, woa_ref, wob_ref, fg_ref,
                    wrh_ref, wrl_ref, rb_ref, x1_ref, hf_ref, eid_ref, wt_ref):
    na = _rmsnorm(oa_ref[...], ga_ref[...]).astype(BF16)
    nb = _rmsnorm(ob_ref[...], gb_ref[...]).astype(BF16)
    y = (jnp.dot(na, woa_ref[...], preferred_element_type=F32)
         + jnp.dot(nb, wob_ref[...], preferred_element_type=F32))
    x1 = x_ref[...] + y
    x1_ref[...] = x1
    hf = _rmsnorm(x1, fg_ref[...])
    _to_token_tiles(hf_ref, hf)

    hf_hi = hf.astype(BF16)
    hf_lo = (hf - hf_hi.astype(F32)).astype(BF16)
    lt = (_dot_nt(wrh_ref[...], hf_hi) + _dot_nt(wrh_ref[...], hf_lo)
          + _dot_nt(wrl_ref[...], hf_hi) + rb_ref[...])
    gl = [lt[j:j + 1, :] for j in range(N_GROUPS)]
    best = gl[0]
    gsel = jnp.zeros(best.shape, I32)
    for j in range(1, N_GROUPS):
        better = gl[j] > best
        gsel = jnp.where(better, j, gsel)
        best = jnp.where(better, gl[j], best)
    gsum = jnp.zeros(best.shape, F32)
    for j in range(N_GROUPS):
        gsum = gsum + jnp.exp(gl[j] - best)
    gw = 1.0 / gsum
    E = EXPERTS_PER_GROUP
    es = lt[EXPERT_ROW0:EXPERT_ROW0 + E, :]
    for j in range(1, N_GROUPS):
        es = jnp.where(gsel == j, lt[EXPERT_ROW0 + E * j:EXPERT_ROW0 + E * (j + 1), :], es)
    row = lax.broadcasted_iota(I32, es.shape, 0)
    v1 = jnp.max(es, axis=0, keepdims=True)
    i1 = jnp.min(jnp.where(es == v1, row, E), axis=0, keepdims=True)
    es2 = jnp.where(row == i1, -jnp.inf, es)
    v2 = jnp.max(es2, axis=0, keepdims=True)
    i2 = jnp.min(jnp.where(es2 == v2, row, E), axis=0, keepdims=True)
    e2 = jnp.exp(v2 - v1)
    den = 1.0 + e2
    eid_ref[...] = jnp.concatenate([gsel * E + i1, gsel * E + i2], axis=0)
    wt_ref[...] = jnp.concatenate([gw * (1.0 / den), gw * (e2 / den)], axis=0)


def _outproj_call(oa, ob, x2d, ga, gb, woa, wob, fg, wrh, wrl, rb):
    T, D = x2d.shape
    tm = TOKEN_TILE
    tok = lambda w: pl.BlockSpec((tm, w), lambda i: (i, 0))
    full = lambda a: pl.BlockSpec(a.shape, lambda i: (0, 0))
    col = pl.BlockSpec((EXPERT_TOPK, tm), lambda i: (0, i))
    return pl.pallas_call(
        _outproj_kernel,
        grid=(T // tm,),
        in_specs=[tok(oa.shape[1]), tok(ob.shape[1]), tok(D), full(ga), full(gb), full(woa), full(wob),
                  full(fg), full(wrh), full(wrl), full(rb)],
        out_specs=[tok(D), pl.BlockSpec((tm * ROW_TILE, LANES), lambda i: (i, 0)), col, col],
        out_shape=[jax.ShapeDtypeStruct((T, D), F32), jax.ShapeDtypeStruct((T * ROW_TILE, LANES), F32),
                   jax.ShapeDtypeStruct((EXPERT_TOPK, T), I32), jax.ShapeDtypeStruct((EXPERT_TOPK, T), F32)],
        compiler_params=pltpu.CompilerParams(
            dimension_semantics=("arbitrary",), vmem_limit_bytes=VMEM_LIMIT),
        name="outproj",
    )(oa, ob, x2d, ga, gb, woa, wob, fg, wrh, wrl, rb)


COPY_GROUP = 32


def _moe_kernel(be_ref, nv_ref, src_ref, srcn_ref, dst_ref, hf_hbm, wg_ref, wu_ref, wd_ref, out_hbm,
                xbuf, obuf, gsem, ssem):
    i = pl.program_id(0)
    nblk = pl.num_programs(0)
    M = xbuf.shape[1] // ROW_TILE
    slot = i % 2
    nv_cur = nv_ref[i]
    nv_next = jnp.where(i + 1 < nblk, nv_ref[jnp.minimum(i + 1, nblk - 1)], 0)

    def for_real_groups(n_real, fn):
        for k in range(M // COPY_GROUP):
            @pl.when(k * COPY_GROUP < n_real)
            def _group():
                fn(range(k * COPY_GROUP, (k + 1) * COPY_GROUP))

    def gather_rows(idx_ref, buf_slot, n_real):
        def start(rows):
            for r in rows:
                tok_row = pl.multiple_of(idx_ref[0, 0, r] * ROW_TILE, ROW_TILE)
                pltpu.make_async_copy(hf_hbm.at[pl.ds(tok_row, ROW_TILE)],
                                      xbuf.at[buf_slot, pl.ds(r * ROW_TILE, ROW_TILE)],
                                      gsem.at[buf_slot]).start(priority=0)
        for_real_groups(n_real, start)

    def scatter_rows(idx_ref, buf_slot, n_real):
        def start(rows):
            for r in rows:
                dst_row = pl.multiple_of(idx_ref[0, 0, r] * ROW_TILE, ROW_TILE)
                pltpu.make_async_copy(obuf.at[buf_slot, pl.ds(r * ROW_TILE, ROW_TILE)],
                                      out_hbm.at[pl.ds(dst_row, ROW_TILE)],
                                      ssem.at[buf_slot]).start(priority=DMA_PRIORITIES - 1)
        for_real_groups(n_real, start)

    def wait_rows(buf, buf_slot, sem, n_real):
        span = COPY_GROUP * ROW_TILE
        for_real_groups(n_real, lambda rows: pltpu.make_async_copy(
            hf_hbm.at[pl.ds(0, span)], buf.at[buf_slot, pl.ds(0, span)], sem.at[buf_slot]).wait())

    @pl.when(nv_cur > 0)
    def _compute():
        @pl.when(i == 0)
        def _first_gather():
            gather_rows(src_ref, 0, nv_cur)
            obuf[1] = jnp.zeros(obuf.shape[1:], F32)
            spare = pltpu.make_async_copy(
                obuf.at[1], out_hbm.at[pl.ds(out_hbm.shape[0] - M * ROW_TILE, M * ROW_TILE)], ssem.at[1])
            spare.start()
            spare.wait()

        wait_rows(xbuf, slot, gsem, nv_cur)

        @pl.when(i >= 2)
        def _reuse_obuf():
            wait_rows(obuf, slot, ssem, nv_ref[jnp.maximum(i - 2, 0)])

        gather_rows(srcn_ref, 1 - slot, nv_next)
        xb = _from_token_tiles(xbuf.at[slot], M).astype(BF16)
        a = jnp.dot(xb, wg_ref[0].astype(BF16), preferred_element_type=F32)
        u = jnp.dot(xb, wu_ref[0].astype(BF16), preferred_element_type=F32)
        hmid = (a * jax.nn.sigmoid(a)) * u
        _to_token_tiles(obuf.at[slot], jnp.dot(hmid.astype(BF16), wd_ref[0].astype(BF16),
                                               preferred_element_type=F32))
        scatter_rows(dst_ref, slot, nv_cur)

        @pl.when(nv_next == 0)
        def _drain():
            wait_rows(obuf, slot, ssem, nv_cur)

            @pl.when(i >= 1)
            def _drain_prev():
                wait_rows(obuf, 1 - slot, ssem, nv_ref[jnp.maximum(i - 1, 0)])


def _moe_call(block_expert, n_real, src_tok, dst_row, hf, w_g, w_u, w_d):
    T = hf.shape[0] // ROW_TILE
    M = DISPATCH_BLOCK
    nblk = block_expert.shape[0]
    D, DE = w_g.shape[1], w_g.shape[2]
    assert D == ROW_TILE * LANES and hf.shape[1] == LANES and M % COPY_GROUP == 0
    idx_spec = lambda f: pl.BlockSpec((1, 1, M), f, memory_space=pltpu.SMEM)
    cur = lambda i, be, nu: (i, 0, 0)
    nxt = lambda i, be, nu: (jnp.minimum(i + 1, nblk - 1), 0, 0)
    wspec = lambda a, b: pl.BlockSpec((1, a, b), lambda i, be, nu: (be[i], 0, 0))
    grid_spec = pltpu.PrefetchScalarGridSpec(
        num_scalar_prefetch=2,
        grid=(nblk,),
        in_specs=[idx_spec(cur), idx_spec(nxt), idx_spec(cur),
                  pl.BlockSpec(memory_space=pl.ANY),
                  wspec(D, DE), wspec(D, DE), wspec(DE, D)],
        out_specs=pl.BlockSpec(memory_space=pl.ANY),
        scratch_shapes=[pltpu.VMEM((2, M * ROW_TILE, LANES), F32), pltpu.VMEM((2, M * ROW_TILE, LANES), F32),
                        pltpu.SemaphoreType.DMA((2,)), pltpu.SemaphoreType.DMA((2,))],
    )
    src3 = src_tok.reshape(nblk, 1, M)
    return pl.pallas_call(
        _moe_kernel,
        grid_spec=grid_spec,
        out_shape=jax.ShapeDtypeStruct(((EXPERT_TOPK * T + M) * ROW_TILE, LANES), F32),
        compiler_params=pltpu.CompilerParams(
            dimension_semantics=("arbitrary",), vmem_limit_bytes=VMEM_LIMIT),
        name="moe",
    )(block_expert, n_real, src3, src3, dst_row.reshape(nblk, 1, M), hf, w_g, w_u, w_d)


def _ple_kernel(x1_ref, m0_ref, m1_ref, wt_ref, p_ref, pg_ref, wgate_ref, wproj_ref, pog_ref, out_ref):
    proj = _rmsnorm(jnp.dot(p_ref[...].astype(BF16), wproj_ref[...], preferred_element_type=F32),
                    pog_ref[...])
    tm = x1_ref.shape[0]
    moe = (_from_token_tiles(m0_ref, tm) * wt_ref[:, 0:1] + _from_token_tiles(m1_ref, tm) * wt_ref[:, 1:2])
    x2 = x1_ref[...] + moe
    hn = _rmsnorm(x2, pg_ref[...]).astype(BF16)
    gate = jax.nn.sigmoid(jnp.dot(hn, wgate_ref[...], preferred_element_type=F32))
    out_ref[...] = x2 + gate * proj


def _ple_call(moe_rows, x1, wt_cols, p2d, pg, wgate, wproj, pog):
    T, D = x1.shape
    tm = PLE_TILE
    tok = lambda w: pl.BlockSpec((tm, w), lambda i: (i, 0))
    full = lambda a: pl.BlockSpec(a.shape, lambda i: (0, 0))
    first = pl.BlockSpec((tm * ROW_TILE, LANES), lambda i: (i, 0))
    second = pl.BlockSpec((tm * ROW_TILE, LANES), lambda i: (T // tm + i, 0))
    return pl.pallas_call(
        _ple_kernel,
        grid=(T // tm,),
        in_specs=[tok(D), first, second, tok(EXPERT_TOPK), tok(p2d.shape[1]),
                  full(pg), full(wgate), full(wproj), full(pog)],
        out_specs=tok(D),
        out_shape=jax.ShapeDtypeStruct((T, D), F32),
        compiler_params=pltpu.CompilerParams(
            dimension_semantics=("arbitrary",), vmem_limit_bytes=VMEM_LIMIT),
        name="ple",
    )(x1, moe_rows, moe_rows, wt_cols, p2d, pg, wgate, wproj, pog)


def _dispatch_plan(eid, T):
    A = EXPERT_TOPK * T
    M = DISPATCH_BLOCK
    nblk = A // M + N_EXPERTS
    assert A <= PAD_MARK
    e_flat = eid.reshape(A)
    experts = jnp.arange(N_EXPERTS, dtype=I32)
    counts = jnp.sum((e_flat[:, None] == experts[None, :]).astype(I32), axis=0)
    padded = (counts + M - 1) // M * M
    n_used = jnp.sum(padded) // M
    row = jnp.arange(M, dtype=I32)[None, :]
    pad_keys = jnp.where(row < (padded - counts)[:, None], 2 * experts[:, None] + 1, 2 * N_EXPERTS)
    keys = jnp.concatenate([2 * e_flat, pad_keys.reshape(-1)])
    vals = jnp.concatenate([jnp.arange(A, dtype=I32), jnp.full((N_EXPERTS * M,), PAD_MARK, I32)])
    packed = jnp.sort(keys * (PAD_MARK + 1) + vals).reshape(nblk, M)
    a_s = packed & PAD_MARK
    blk = jnp.arange(nblk, dtype=I32)
    be = jnp.minimum(packed[:, 0] // (2 * (PAD_MARK + 1)), N_EXPERTS - 1)
    be = jnp.where(blk < n_used, be, be[jnp.maximum(n_used - 1, 0)])
    valid = a_s != PAD_MARK
    src_tok = jnp.where(valid, jnp.where(a_s >= T, a_s - T, a_s), 0)
    dst_row = jnp.where(valid, a_s, A + row)
    return be, jnp.sum(valid.astype(I32), axis=1), src_tok, dst_row


def _layer(x2d, p2d, rel_bias, attn_norm, w_in, swa_q_norm, swa_k_norm, swa_sinks, moba_q_norm,
           moba_k_norm, swa_out_norm, moba_out_norm, w_out, ffn_norm, w_rg, b_rg, w_re, b_re,
           w_g, w_u, w_d, ple_norm, w_ple_gate, w_ple_proj, ple_out_norm, B, S):
    T, D = x2d.shape
    row = lambda v: v.reshape(1, -1).astype(F32)
    head_gain = jnp.concatenate([
        jnp.tile(swa_q_norm, SWA_Q_HEADS), jnp.tile(swa_k_norm, SWA_KV_HEADS),
        jnp.tile(moba_q_norm, MOBA_Q_HEADS), jnp.tile(moba_k_norm, MOBA_KV_HEADS)])
    w_in_b = w_in.astype(BF16)
    qa_w, kv_w = SWA_Q_HEADS * HEAD_DIM, SWA_KV_HEADS * HEAD_DIM
    qb_w = MOBA_Q_HEADS * HEAD_DIM
    c_va = qa_w + kv_w
    c_qb = c_va + kv_w
    c_vb = c_qb + qb_w + kv_w
    w_qk = jnp.concatenate([w_in_b[:, :c_va], w_in_b[:, c_qb:c_vb]], axis=1)
    w_vt = jnp.concatenate([w_in_b[:, c_va:c_qb], w_in_b[:, c_vb:]], axis=1).T
    qa, ka, qb, kb, vat, vbt = _qkv_call(x2d, row(attn_norm), w_qk, w_vt, row(head_gain))

    tbl_flat = rel_bias.astype(F32).reshape(-1)
    oa = _swa_call(tbl_flat, swa_sinks.astype(F32), qa, ka, vat, B, S)
    ob = _moba_call(tbl_flat, qb, kb, vbt, B, S)

    pad_rows = lambda n: jnp.zeros((n, D), F32)
    wr = jnp.concatenate([w_rg.T, pad_rows(EXPERT_ROW0 - N_GROUPS), w_re.T,
                          pad_rows(ROUTER_ROWS - EXPERT_ROW0 - N_EXPERTS)], axis=0)
    wr_hi = wr.astype(BF16)
    wr_lo = (wr - wr_hi.astype(F32)).astype(BF16)
    rb = jnp.concatenate([b_rg, jnp.zeros((EXPERT_ROW0 - N_GROUPS,), F32), b_re,
                          jnp.zeros((ROUTER_ROWS - EXPERT_ROW0 - N_EXPERTS,), F32)]).reshape(ROUTER_ROWS, 1)
    wo = w_out.astype(BF16)
    na_w = SWA_Q_HEADS * HEAD_DIM
    x1, hf, eid, wts = _outproj_call(oa, ob, x2d, row(swa_out_norm), row(moba_out_norm),
                                     wo[:na_w], wo[na_w:], row(ffn_norm), wr_hi, wr_lo, rb)

    be, n_real, src_tok, dst_row = _dispatch_plan(eid, T)
    moe_rows = _moe_call(be, n_real, src_tok, dst_row, hf, w_g, w_u, w_d)
    return _ple_call(moe_rows, x1, wts.T, p2d, row(ple_norm), w_ple_gate.astype(BF16),
                     w_ple_proj.astype(BF16), row(ple_out_norm))


def kernel(x, p, rel_bias, attn_norm, w_in, swa_q_norm, swa_k_norm, swa_sinks, moba_q_norm, moba_k_norm,
           swa_out_norm, moba_out_norm, w_out, ffn_norm, w_router_group, b_router_group, w_router_expert,
           b_router_expert, w_exp_gate, w_exp_up, w_exp_down, ple_norm, w_ple_gate, w_ple_proj, ple_out_norm):
    B, S, D = x.shape
    x2d = x.reshape(B * S, D)
    for i in range(p.shape[0]):
        x2d = _layer(x2d, p[i].reshape(B * S, -1), rel_bias, attn_norm[i], w_in[i], swa_q_norm[i],
                     swa_k_norm[i], swa_sinks[i], moba_q_norm[i], moba_k_norm[i], swa_out_norm[i],
                     moba_out_norm[i], w_out[i], ffn_norm[i], w_router_group[i], b_router_group[i],
                     w_router_expert[i], b_router_expert[i], w_exp_gate[i], w_exp_up[i], w_exp_down[i],
                     ple_norm[i], w_ple_gate[i], w_ple_proj[i], ple_out_norm[i], B, S)
    return x2d.reshape(B, S, D)
```
